```python
import jax, jax.numpy as jnp
from jax import lax
import numpy as np

D_MODEL = 1024
BATCH = 1
SEQ = 16384
DEPTH = 4

RET_HEADS = 4
RET_DK = 64
RET_DV = 128
RET_CHUNK = 128
RET_THETA = 10000.0
DSA_HEADS = 8
DSA_KV_HEADS = 2
DSA_DH = 64
DSA_ROT = DSA_DH // 4
ROPE_THETA = 500000.0
IDX_HEADS = 4
IDX_DH = 64
IDX_ROT = IDX_DH // 4
TOPK_MAX = 256
Q_BLOCK = 128
GLA_HEADS = 4
GLA_DK = 64
GLA_DV = 128
GLA_RANK = 16
GLA_TAU = 16.0
GLA_CHUNK = 64

N_BRANCH = 3
RMS_EPS = 1e-6
RET_W = RET_HEADS * RET_DV
DSA_W = DSA_HEADS * DSA_DH
GLA_W = GLA_HEADS * GLA_DV

IN_SPLITS = (
    ("ret_q", RET_HEADS * RET_DK), ("ret_k", RET_HEADS * RET_DK),
    ("ret_v", RET_W), ("ret_g", RET_W),
    ("dsa_q", DSA_W), ("dsa_k", DSA_KV_HEADS * DSA_DH), ("dsa_v", DSA_KV_HEADS * DSA_DH),
    ("dsa_g", DSA_W),
    ("idx_q", IDX_HEADS * IDX_DH), ("idx_k", IDX_DH), ("idx_w", IDX_HEADS),
    ("gla_q", GLA_HEADS * GLA_DK), ("gla_k", GLA_HEADS * GLA_DK),
    ("gla_v", GLA_W), ("gla_g", GLA_W), ("gla_a", GLA_RANK),
    ("merge", N_BRANCH * D_MODEL),
)
IN_WIDTH = sum(w for _, w in IN_SPLITS)

kernel_name = "hybrid_retention_dsa_gla_gated_merge"


def rms_norm(x, gain=None):
    xf = x.astype(jnp.float32)
    y = xf * lax.rsqrt(jnp.mean(xf * xf, axis=-1, keepdims=True) + RMS_EPS)
    if gain is not None:
        y = y * gain.astype(jnp.float32)
    return y.astype(x.dtype)


def split_proj(p):
    bounds = np.cumsum([w for _, w in IN_SPLITS])[:-1].tolist()
    parts = jnp.split(p, bounds, axis=-1)
    return {name: t for (name, _), t in zip(IN_SPLITS, parts)}


def rope(x, pos, rot_dim, theta):
    half = rot_dim // 2
    freqs = theta ** (-jnp.arange(half, dtype=jnp.float32) * 2.0 / rot_dim)
    ang = pos.astype(jnp.float32)[..., None] * freqs
    cos = jnp.cos(ang)[:, :, None, :]
    sin = jnp.sin(ang)[:, :, None, :]
    xf = x.astype(jnp.float32)
    x1 = xf[..., :half]
    x2 = xf[..., half:rot_dim]
    out = jnp.concatenate([x1 * cos - x2 * sin, x2 * cos + x1 * sin, xf[..., rot_dim:]], axis=-1)
    return out.astype(x.dtype)


def retention(q, k, v, pos):
    dt = v.dtype
    B, S, H, dk = q.shape
    dv = v.shape[-1]
    C = RET_CHUNK
    n = S // C
    q = rope(q, pos, dk, RET_THETA).astype(jnp.float32) * (dk ** -0.5)
    k = rope(k, pos, dk, RET_THETA).astype(jnp.float32)
    v = v.astype(jnp.float32)
    log_g = jnp.log1p(-jnp.exp2(-5.0 - jnp.arange(H, dtype=jnp.float32)))
    idx = jnp.arange(C, dtype=jnp.float32)
    rel = idx[:, None] - idx[None, :]
    decay = jnp.where(rel >= 0, jnp.exp(log_g[:, None, None] * jnp.maximum(rel, 0.0)), 0.0)
    qc = q.reshape(B, n, C, H, dk)
    kc = k.reshape(B, n, C, H, dk)
    vc = v.reshape(B, n, C, H, dv)
    scores = jnp.einsum('bnihd,bnjhd->bnhij', qc, kc) * decay
    o_intra = jnp.einsum('bnhij,bnjhe->bnihe', scores, vc)
    to_end = jnp.exp((C - 1 - idx)[:, None] * log_g[None, :])
    kv = jnp.einsum('bnjhd,bnjhe->bnhde', kc * to_end[None, None, :, :, None], vc)
    chunk_decay = jnp.exp(C * log_g)[None, :, None, None]

    def step(state, kv_n):
        return chunk_decay * state + kv_n, state

    _, prev = lax.scan(step, jnp.zeros((B, H, dk, dv), jnp.float32), jnp.moveaxis(kv, 1, 0))
    prev = jnp.moveaxis(prev, 0, 1)
    q_dec = qc * jnp.exp((idx + 1.0)[:, None] * log_g[None, :])[None, None, :, :, None]
    o_cross = jnp.einsum('bnihd,bnhde->bnihe', q_dec, prev)
    return (o_intra + o_cross).reshape(B, S, H, dv).astype(dt)


def gla(q, k, v, log_a):
    dt = v.dtype
    B, S, H, dk = q.shape
    dv = v.shape[-1]
    C = GLA_CHUNK
    n = S // C
    q = q.astype(jnp.float32) * (dk ** -0.5)
    k = k.astype(jnp.float32)
    v = v.astype(jnp.float32)
    log_a = log_a.astype(jnp.float32)
    causal = jnp.tril(jnp.ones((C, C), bool))

    def to_chunks(t):
        return jnp.moveaxis(t.reshape(B, n, C, H, t.shape[-1]), 1, 0)

    def step(state, inp):
        qn, kn, vn, an = inp
        bcum = jnp.cumsum(an, axis=1)
        diff = bcum[:, :, None] - bcum[:, None, :]
        w = jnp.exp(jnp.where(causal[None, :, :, None, None], diff, -jnp.inf))
        attn = jnp.einsum('bihd,bjhd,bijhd->bhij', qn, kn, w)
        o = (jnp.einsum('bhij,bjhe->bihe', attn, vn)
             + jnp.einsum('bihd,bhde->bihe', qn * jnp.exp(bcum), state))
        b_last = bcum[:, -1]
        new = (jnp.exp(b_last)[..., None] * state
               + jnp.einsum('bjhd,bjhe->bhde', kn * jnp.exp(b_last[:, None] - bcum), vn))
        return new, o

    _, o = lax.scan(step, jnp.zeros((B, H, dk, dv), jnp.float32),
                    (to_chunks(q), to_chunks(k), to_chunks(v), to_chunks(log_a)))
    return jnp.moveaxis(o, 0, 1).reshape(B, S, H, dv).astype(dt)


def dsa_attention(q, k, v, q_idx, k_idx, w_idx, pos):
    B, S, H, dh = q.shape
    KV = k.shape[2]
    G = H // KV
    topk = min(TOPK_MAX, S // 4)
    QB = Q_BLOCK
    nb = S // QB
    q = rope(q, pos, DSA_ROT, ROPE_THETA)
    k = rope(k, pos, DSA_ROT, ROPE_THETA)
    q_idx = rope(q_idx, pos, IDX_ROT, ROPE_THETA)
    k_idx_f = rope(k_idx[:, :, None], pos, IDX_ROT, ROPE_THETA)[:, :, 0].astype(jnp.float32)
    w_f = w_idx.astype(jnp.float32) * (IDX_HEADS ** -0.5)
    key_pos = jnp.arange(S)

    def blockify(t):
        return jnp.moveaxis(t.reshape(B, nb, QB, *t.shape[2:]), 1, 0)

    def one_block(inp):
        qb, qib, wb, tb = inp
        s = jnp.einsum('bqhd,bsd->bqhs', qib.astype(jnp.float32), k_idx_f) * (IDX_DH ** -0.5)
        score = jnp.einsum('bqh,bqhs->bqs', wb, jax.nn.relu(s))
        visible = key_pos[None, :] <= tb[:, None]
        score = jnp.where(visible[None], score, -jnp.inf)
        _, sel = lax.top_k(score, topk)
        valid = sel <= tb[None, :, None]
        ks = jax.vmap(lambda kk, ii: kk[ii])(k, sel)
        vs = jax.vmap(lambda vv, ii: vv[ii])(v, sel)
        qg = qb.reshape(B, QB, KV, G, dh)
        logits = jnp.einsum('bqngd,bqknd->bqngk', qg, ks).astype(jnp.float32) * (dh ** -0.5)
        logits = jnp.where(valid[:, :, None, None, :], logits, -jnp.inf)
        p = jax.nn.softmax(logits, axis=-1).astype(vs.dtype)
        o = jnp.einsum('bqngk,bqknd->bqngd', p, vs)
        return o.reshape(B, QB, H * dh)

    t_blocks = jnp.arange(S).reshape(nb, QB)
    out = lax.map(one_block, (blockify(q), blockify(q_idx), blockify(w_f), t_blocks))
    return jnp.moveaxis(out, 0, 1).reshape(B, S, H * dh)


def setup_inputs(seed: int = 0) -> dict:
    key = jax.random.key(seed)
    ks = jax.random.split(key, 16)
    D = D_MODEL
    nrm = jax.random.normal
    x = nrm(ks[0], (BATCH, SEQ, D), jnp.float32)
    c = nrm(ks[1], (BATCH, D), jnp.float32)
    positions = (jnp.arange(SEQ, dtype=jnp.int32)[None, :]
                 + jax.random.randint(ks[2], (BATCH, 1), 0, 4096, dtype=jnp.int32))
    ada_w = nrm(ks[3], (DEPTH, D, 3 * D), jnp.float32) * (0.1 * D ** -0.5)
    ada_b = nrm(ks[4], (DEPTH, 3 * D), jnp.float32) * 0.02
    pre_norm = 1.0 + 0.05 * nrm(ks[5], (DEPTH, D), jnp.float32)
    post_norm = 1.0 + 0.05 * nrm(ks[6], (DEPTH, D), jnp.float32)
    w_in = nrm(ks[7], (DEPTH, D, IN_WIDTH), jnp.float32) * (D ** -0.5)
    gla_w_lr = nrm(ks[8], (DEPTH, GLA_RANK, GLA_HEADS * GLA_DK), jnp.float32) * (GLA_RANK ** -0.5)
    gla_b_lr = 0.1 * nrm(ks[9], (DEPTH, GLA_HEADS * GLA_DK), jnp.float32)
    w_br_ret = nrm(ks[10], (DEPTH, RET_W, D), jnp.float32) * (RET_W ** -0.5)
    w_br_dsa = nrm(ks[11], (DEPTH, DSA_W, D), jnp.float32) * (DSA_W ** -0.5)
    w_br_gla = nrm(ks[12], (DEPTH, GLA_W, D), jnp.float32) * (GLA_W ** -0.5)
    w_out = nrm(ks[13], (DEPTH, D, D), jnp.float32) * (D ** -0.5)
    return {"x": x, "c": c, "positions": positions, "ada_w": ada_w, "ada_b": ada_b,
            "pre_norm": pre_norm, "post_norm": post_norm, "w_in": w_in,
            "gla_w_lr": gla_w_lr, "gla_b_lr": gla_b_lr, "w_br_ret": w_br_ret,
            "w_br_dsa": w_br_dsa, "w_br_gla": w_br_gla, "w_out": w_out}


def reference(x, c, positions, ada_w, ada_b, pre_norm, post_norm, w_in, gla_w_lr, gla_b_lr,
              w_br_ret, w_br_dsa, w_br_gla, w_out):
    B, S, D = x.shape
    c_act = jax.nn.silu(c)
    for l in range(DEPTH):
        mod = c_act @ ada_w[l] + ada_b[l]
        shift, scale, gate = jnp.split(mod, 3, axis=-1)
        h = rms_norm(x, pre_norm[l]) * (1.0 + scale[:, None]) + shift[:, None]
        p = split_proj(h @ w_in[l])
        ret = retention(p["ret_q"].reshape(B, S, RET_HEADS, RET_DK),
                        p["ret_k"].reshape(B, S, RET_HEADS, RET_DK),
                        p["ret_v"].reshape(B, S, RET_HEADS, RET_DV), positions)
        ret = rms_norm(ret).reshape(B, S, RET_W) * jax.nn.silu(p["ret_g"])
        dsa = dsa_attention(p["dsa_q"].reshape(B, S, DSA_HEADS, DSA_DH),
                            p["dsa_k"].reshape(B, S, DSA_KV_HEADS, DSA_DH),
                            p["dsa_v"].reshape(B, S, DSA_KV_HEADS, DSA_DH),
                            p["idx_q"].reshape(B, S, IDX_HEADS, IDX_DH),
                            p["idx_k"], p["idx_w"], positions)
        dsa = dsa * jax.nn.silu(p["dsa_g"])
        log_a = jax.nn.log_sigmoid((p["gla_a"] @ gla_w_lr[l] + gla_b_lr[l]).astype(jnp.float32)) / GLA_TAU
        gl = gla(p["gla_q"].reshape(B, S, GLA_HEADS, GLA_DK),
                 p["gla_k"].reshape(B, S, GLA_HEADS, GLA_DK),
                 p["gla_v"].reshape(B, S, GLA_HEADS, GLA_DV),
                 log_a.reshape(B, S, GLA_HEADS, GLA_DK))
        gl = rms_norm(gl).reshape(B, S, GLA_W) * jax.nn.silu(p["gla_g"])
        g = jax.nn.sigmoid(p["merge"]).reshape(B, S, N_BRANCH, D)
        y = (g[:, :, 0] * (ret @ w_br_ret[l])
             + g[:, :, 1] * (dsa @ w_br_dsa[l])
             + g[:, :, 2] * (gl @ w_br_gla[l]))
        y = y @ w_out[l]
        x = x + gate[:, None] * rms_norm(y, post_norm[l])
    return x
```

```python
import functools
import math

import jax
import jax.numpy as jnp
from jax import lax
from jax.experimental import pallas as pl
from jax.experimental.pallas import tpu as pltpu

D_MODEL = 1024
DEPTH = 4
RET_HEADS, RET_DK, RET_DV, RET_CHUNK, RET_THETA = 4, 64, 128, 128, 10000.0
DSA_HEADS, DSA_KV_HEADS, DSA_DH = 8, 2, 64
DSA_ROT = DSA_DH // 4
ROPE_THETA = 500000.0
IDX_HEADS, IDX_DH = 4, 64
TOPK_MAX = 256
GLA_HEADS, GLA_DK, GLA_DV, GLA_RANK, GLA_TAU, GLA_CHUNK = 4, 64, 128, 16, 16.0, 64
RMS_EPS = 1e-6
LANES = 128

_SRC = {}
_off = 0
for _name, _w in (("ret_q", 256), ("ret_k", 256), ("ret_v", 512), ("ret_g", 512),
                  ("dsa_q", 512), ("dsa_k", 128), ("dsa_v", 128), ("dsa_g", 512),
                  ("idx_q", 256), ("idx_k", 64), ("idx_w", 4),
                  ("gla_q", 256), ("gla_k", 256), ("gla_v", 512), ("gla_g", 512), ("gla_a", 16),
                  ("merge", 3072)):
    _SRC[_name] = (_off, _w)
    _off += _w
IN_WIDTH = _off

_PACK = (("merge", 3072), ("ret_q", 256), ("ret_k", 256), ("ret_v", 512), ("ret_g", 512),
         ("dsa_q", 512), ("dsa_g", 512), ("gla_v", 512), ("gla_g", 512),
         ("gla_q", 256), ("gla_k", 256), ("idx_q", 256),
         ("dsa_k", 128), ("dsa_v", 128), ("idx_kw", 128), ("gla_a", 128))
PCOL = {}
_off = 0
for _name, _w in _PACK:
    assert _off % min(_w, 1024) == 0
    PCOL[_name] = _off
    _off += _w
P_WIDTH = 8192
assert _off <= P_WIDTH

INT_MIN = -(2 ** 31)
NEG_INIT = -1e30
NEG_MASK = -2e30


def _cparams(n_axes, vmem_mb=48):
    return pltpu.CompilerParams(dimension_semantics=("arbitrary",) * n_axes,
                                vmem_limit_bytes=vmem_mb * 1024 * 1024)


def _bf(x):
    return x.astype(jnp.bfloat16)


def _dot(a, b):
    return jnp.dot(a, b, preferred_element_type=jnp.float32)


def _dot_nt(a, b):
    return lax.dot_general(a, b, (((1,), (1,)), ((), ())), preferred_element_type=jnp.float32)


def _dot_tn(a, b):
    return lax.dot_general(a, b, (((0,), (0,)), ((), ())), preferred_element_type=jnp.float32)


def _split3(x):
    hi = _bf(x)
    r1 = x - hi.astype(jnp.float32)
    mid = _bf(r1)
    lo = _bf(r1 - mid.astype(jnp.float32))
    return hi, mid, lo


def _silu(x):
    return x * (1.0 / (1.0 + jnp.exp(-x)))


def _sigmoid(x):
    return 1.0 / (1.0 + jnp.exp(-x))


def _mod_kernel(c_ref, w_ref, b_ref, o_ref):
    c = c_ref[...]
    ca = _silu(c)
    acc = None
    for t in _split3(ca):
        for u in _split3(w_ref[0]):
            part = _dot(t, u)
            acc = part if acc is None else acc + part
    o_ref[0] = acc + b_ref[0]


def _mod_call(c8, ada_w, ada_b3):
    depth, d, n = ada_w.shape
    tn = 1024
    return pl.pallas_call(
        _mod_kernel,
        grid=(depth, n // tn),
        in_specs=[pl.BlockSpec((8, d), lambda l, j: (0, 0)),
                  pl.BlockSpec((1, d, tn), lambda l, j: (l, 0, j)),
                  pl.BlockSpec((1, 1, tn), lambda l, j: (l, 0, j))],
        out_specs=pl.BlockSpec((1, 8, tn), lambda l, j: (l, 0, j)),
        out_shape=jax.ShapeDtypeStruct((depth, 8, n), jnp.float32),
        compiler_params=_cparams(2),
        name="adaln_mod",
    )(c8, ada_w, ada_b3)


def _tab_kernel(pos_ref, rf_ref, rs_ref, df_ref, ds_ref, rc_o, rsn_o, dc_o, dsn_o):
    pos = pos_ref[...].astype(jnp.float32)
    ang = pos * rf_ref[...]
    rc_o[...] = jnp.cos(ang)
    rsn_o[...] = jnp.sin(ang) * rs_ref[...]
    ang = pos * df_ref[...]
    dc_o[...] = jnp.cos(ang)
    dsn_o[...] = jnp.sin(ang) * ds_ref[...]


def _rope_rows():
    half = RET_DK // 2
    f = RET_THETA ** (-jnp.arange(half, dtype=jnp.float32) * 2.0 / RET_DK)
    rf = jnp.tile(jnp.concatenate([f, f]), RET_HEADS)[None, :]
    rs = jnp.tile(jnp.concatenate([-jnp.ones(half), jnp.ones(half)]), RET_HEADS)[None, :].astype(jnp.float32)
    half = DSA_ROT // 2
    f = ROPE_THETA ** (-jnp.arange(half, dtype=jnp.float32) * 2.0 / DSA_ROT)
    z = jnp.zeros(DSA_DH - DSA_ROT, jnp.float32)
    df = jnp.tile(jnp.concatenate([f, f, z]), 2)[None, :]
    ds = jnp.tile(jnp.concatenate([-jnp.ones(half), jnp.ones(half), z]), 2)[None, :].astype(jnp.float32)
    return rf, rs, df, ds


def _tab_call(pos_col):
    s = pos_col.shape[0]
    tm = min(1024, s)
    rf, rs, df, ds = _rope_rows()
    row = lambda w: pl.BlockSpec((1, w), lambda i: (0, 0))
    out = lambda w: pl.BlockSpec((tm, w), lambda i: (i, 0))
    return pl.pallas_call(
        _tab_kernel,
        grid=(s // tm,),
        in_specs=[pl.BlockSpec((tm, 1), lambda i: (i, 0)), row(256), row(256), row(128), row(128)],
        out_specs=[out(256), out(256), out(128), out(128)],
        out_shape=[jax.ShapeDtypeStruct((s, 256), jnp.float32), jax.ShapeDtypeStruct((s, 256), jnp.float32),
                   jax.ShapeDtypeStruct((s, 128), jnp.float32), jax.ShapeDtypeStruct((s, 128), jnp.float32)],
        compiler_params=_cparams(1),
        name="rope_tables",
    )(pos_col, rf, rs, df, ds)


def _swap_halves(x, half, period):
    w = x.shape[-1]
    lane = lax.broadcasted_iota(jnp.int32, x.shape, x.ndim - 1) & (period - 1)
    up = pltpu.roll(x, w - half, x.ndim - 1)
    dn = pltpu.roll(x, half, x.ndim - 1)
    return jnp.where(lane < half, up, dn)


def _rope(x, cos, sin_signed, half, period):
    return x * cos + _swap_halves(x, half, period) * sin_signed


def _proj_kernel(x_ref, pre_ref, sc_ref, sh_ref, w_ref, o_ref):
    x = x_ref[...]
    xn = x * lax.rsqrt(jnp.mean(x * x, axis=-1, keepdims=True) + RMS_EPS)
    h = xn * pre_ref[...] * (1.0 + sc_ref[...]) + sh_ref[...]
    o_ref[...] = _dot(_bf(h), w_ref[...])


def _proj_call(x2, pre, scale, shift, w_pack):
    s, d = x2.shape
    tm, tn = min(512, s), 2048
    vec = pl.BlockSpec((1, d), lambda j, i: (0, 0))
    return pl.pallas_call(
        _proj_kernel,
        grid=(P_WIDTH // tn, s // tm),
        in_specs=[pl.BlockSpec((tm, d), lambda j, i: (i, 0)), vec, vec, vec,
                  pl.BlockSpec((d, tn), lambda j, i: (0, j))],
        out_specs=pl.BlockSpec((tm, tn), lambda j, i: (i, j)),
        out_shape=jax.ShapeDtypeStruct((s, P_WIDTH), jnp.float32),
        compiler_params=_cparams(2),
        name="in_proj",
    )(x2, pre, scale, shift, w_pack)


def _pspec(tm, name, width):
    blk = PCOL[name] // width
    assert PCOL[name] % width == 0
    return pl.BlockSpec((tm, width), lambda i: (i, blk))


def _ret_log_g(h):
    return math.log1p(-(2.0 ** (-5.0 - h)))


def _ret_kernel(q_ref, k_ref, v_ref, g_ref, cos_ref, sin_ref, o_ref,
                state_ref, decay_ref, qdec_ref, kend_ref, *, chunks):
    C = RET_CHUNK

    @pl.when(pl.program_id(0) == 0)
    def _():
        state_ref[...] = jnp.zeros_like(state_ref)
        ii = lax.broadcasted_iota(jnp.int32, (C, C), 0)
        jj = lax.broadcasted_iota(jnp.int32, (C, C), 1)
        rel = (ii - jj).astype(jnp.float32)
        row = lax.broadcasted_iota(jnp.int32, (C, RET_DK), 0).astype(jnp.float32)
        for h in range(RET_HEADS):
            lg = _ret_log_g(h)
            decay_ref[h] = jnp.where(rel >= 0, jnp.exp(lg * jnp.maximum(rel, 0.0)), 0.0)
            qdec_ref[:, h * RET_DK:(h + 1) * RET_DK] = jnp.exp((row + 1.0) * lg)
            kend_ref[:, h * RET_DK:(h + 1) * RET_DK] = jnp.exp((C - 1.0 - row) * lg)

    for c in range(chunks):
        rows = slice(c * C, (c + 1) * C)
        cos, sin = cos_ref[rows, :], sin_ref[rows, :]
        q = _rope(q_ref[rows, :], cos, sin, RET_DK // 2, RET_DK) * (RET_DK ** -0.5)
        k = _rope(k_ref[rows, :], cos, sin, RET_DK // 2, RET_DK)
        qd = _bf(q * qdec_ref[...])
        kd = _bf(k * kend_ref[...])
        qb, kb = _bf(q), _bf(k)
        v = v_ref[rows, :]
        g = g_ref[rows, :]
        for h in range(RET_HEADS):
            dk = slice(h * RET_DK, (h + 1) * RET_DK)
            dv = slice(h * RET_DV, (h + 1) * RET_DV)
            vh = _bf(v[:, dv])
            scores = _dot_nt(qb[:, dk], kb[:, dk]) * decay_ref[h]
            st = state_ref[h]
            o = _dot(_bf(scores), vh) + _dot(qd[:, dk], _bf(st))
            state_ref[h] = math.exp(C * _ret_log_g(h)) * st + _dot_tn(kd[:, dk], vh)
            o = o * lax.rsqrt(jnp.mean(o * o, axis=-1, keepdims=True) + RMS_EPS)
            o_ref[rows, dv] = _bf(o * _silu(g[:, dv]))


def _ret_call(p, rcos, rsin):
    s = p.shape[0]
    chunks = 2
    tm = RET_CHUNK * chunks
    tab = pl.BlockSpec((tm, 256), lambda i: (i, 0))
    return pl.pallas_call(
        functools.partial(_ret_kernel, chunks=chunks),
        grid=(s // tm,),
        in_specs=[_pspec(tm, "ret_q", 256), _pspec(tm, "ret_k", 256), _pspec(tm, "ret_v", 512),
                  _pspec(tm, "ret_g", 512), tab, tab],
        out_specs=pl.BlockSpec((tm, 512), lambda i: (i, 0)),
        out_shape=jax.ShapeDtypeStruct((s, 512), jnp.bfloat16),
        scratch_shapes=[pltpu.VMEM((RET_HEADS, RET_DK, RET_DV), jnp.float32),
                        pltpu.VMEM((RET_HEADS, RET_CHUNK, RET_CHUNK), jnp.float32),
                        pltpu.VMEM((RET_CHUNK, RET_HEADS * RET_DK), jnp.float32),
                        pltpu.VMEM((RET_CHUNK, RET_HEADS * RET_DK), jnp.float32)],
        compiler_params=_cparams(1),
        name="retention",
    )(p, p, p, p, rcos, rsin)


def _gla_kernel(q_ref, k_ref, v_ref, g_ref, a_ref, wlr_ref, blr_ref, o_ref, state_ref, *, chunks):
    C = GLA_CHUNK

    @pl.when(pl.program_id(0) == 0)
    def _():
        state_ref[...] = jnp.zeros_like(state_ref)

    ii = lax.broadcasted_iota(jnp.int32, (C, C), 0)
    jj = lax.broadcasted_iota(jnp.int32, (C, C), 1)
    causal = jj <= ii
    tril = _bf(jnp.where(causal, 1.0, 0.0))
    wlr = wlr_ref[...]
    w_hi, w_mid, w_lo = _split3(wlr)

    for c in range(chunks):
        rows = slice(c * C, (c + 1) * C)
        a_hi, a_mid, a_lo = _split3(a_ref[rows, :])
        z = (_dot(a_hi, w_hi) + (_dot(a_hi, w_mid) + _dot(a_mid, w_hi))
             + (_dot(a_hi, w_lo) + _dot(a_mid, w_mid) + _dot(a_lo, w_hi))) + blr_ref[...]
        log_a = (jnp.minimum(z, 0.0) - jnp.log1p(jnp.exp(-jnp.abs(z)))) * (1.0 / GLA_TAU)
        l_hi, l_mid, l_lo = _split3(log_a)
        bcum = _dot(tril, l_hi) + _dot(tril, l_mid) + _dot(tril, l_lo)
        b_mid = bcum[C // 2 - 1:C // 2, :]
        b_last = bcum[C - 1:C, :]
        q = q_ref[rows, :] * (GLA_DK ** -0.5)
        k = k_ref[rows, :]
        qt = _bf(q * jnp.exp(bcum - b_mid))
        kt = _bf(k * jnp.exp(b_mid - bcum))
        qg = _bf(q * jnp.exp(bcum))
        kd = _bf(k * jnp.exp(b_last - bcum))
        e_last = jnp.exp(b_last)
        v = v_ref[rows, :]
        g = g_ref[rows, :]
        for h in range(GLA_HEADS):
            dk = slice(h * GLA_DK, (h + 1) * GLA_DK)
            dv = slice(h * GLA_DV, (h + 1) * GLA_DV)
            vh = _bf(v[:, dv])
            attn = jnp.where(causal, _dot_nt(qt[:, dk], kt[:, dk]), 0.0)
            st = state_ref[h]
            o = _dot(_bf(attn), vh) + _dot_nt(qg[:, dk], _bf(st))
            state_ref[h] = e_last[:, dk] * st + _dot_tn(vh, kd[:, dk])
            o = o * lax.rsqrt(jnp.mean(o * o, axis=-1, keepdims=True) + RMS_EPS)
            o_ref[rows, dv] = _bf(o * _silu(g[:, dv]))


def _gla_call(p, wlr_pad, blr):
    s = p.shape[0]
    chunks = 4
    tm = GLA_CHUNK * chunks
    return pl.pallas_call(
        functools.partial(_gla_kernel, chunks=chunks),
        grid=(s // tm,),
        in_specs=[_pspec(tm, "gla_q", 256), _pspec(tm, "gla_k", 256), _pspec(tm, "gla_v", 512),
                  _pspec(tm, "gla_g", 512), _pspec(tm, "gla_a", 128),
                  pl.BlockSpec((128, 256), lambda i: (0, 0)), pl.BlockSpec((1, 256), lambda i: (0, 0))],
        out_specs=pl.BlockSpec((tm, 512), lambda i: (i, 0)),
        out_shape=jax.ShapeDtypeStruct((s, 512), jnp.bfloat16),
        scratch_shapes=[pltpu.VMEM((GLA_HEADS, GLA_DV, GLA_DK), jnp.float32)],
        compiler_params=_cparams(1),
        name="gla",
    )(p, p, p, p, p, wlr_pad, blr)


def _place_head(pair, src_pos, dst_pos):
    lane = lax.broadcasted_iota(jnp.int32, pair.shape, 1)
    if src_pos != dst_pos:
        pair = pltpu.roll(pair, 64, 1)
    keep = (lane < 64) if dst_pos == 0 else (lane >= 64)
    return jnp.where(keep, pair, 0.0)


def _dprep_kernel(q_ref, k_ref, v_ref, iq_ref, ikw_ref, cos_ref, sin_ref,
                  qp_o, k_o, v_o, iqp_o, ik_o, w_o):
    cos, sin = cos_ref[...], sin_ref[...]
    half = DSA_ROT // 2
    cos4 = jnp.concatenate([cos] * 4, axis=1)
    sin4 = jnp.concatenate([sin] * 4, axis=1)
    q = _rope(q_ref[...], cos4, sin4, half, DSA_DH) * (DSA_DH ** -0.5)
    for h in range(DSA_HEADS):
        pair = q[:, (h // 2) * 128:(h // 2 + 1) * 128]
        qp_o[h] = _bf(_place_head(pair, h % 2, h // (DSA_HEADS // DSA_KV_HEADS)))
    k_o[...] = _bf(_rope(k_ref[...], cos, sin, half, DSA_DH))
    v_o[...] = _bf(v_ref[...])
    iq = _rope(iq_ref[...], cos4[:, :256], sin4[:, :256], half, IDX_DH)
    for h in range(IDX_HEADS):
        pair = iq[:, (h // 2) * 128:(h // 2 + 1) * 128]
        iqp_o[h] = _bf(_place_head(pair, h % 2, 0))
    ikw = ikw_ref[...]
    lane = lax.broadcasted_iota(jnp.int32, ikw.shape, 1)
    ik_o[...] = _bf(jnp.where(lane < IDX_DH, _rope(ikw, cos, sin, half, DSA_DH), 0.0))
    wscale = (IDX_HEADS ** -0.5) * (IDX_DH ** -0.5)
    w_o[...] = pltpu.roll(ikw, 128 - IDX_DH, 1) * wscale


def _dprep_call(p, dcos, dsin):
    s = p.shape[0]
    tm = min(512, s)
    tab = pl.BlockSpec((tm, 128), lambda i: (i, 0))
    return pl.pallas_call(
        _dprep_kernel,
        grid=(s // tm,),
        in_specs=[_pspec(tm, "dsa_q", 512), _pspec(tm, "dsa_k", 128), _pspec(tm, "dsa_v", 128),
                  _pspec(tm, "idx_q", 256), _pspec(tm, "idx_kw", 128), tab, tab],
        out_specs=[pl.BlockSpec((DSA_HEADS, tm, 128), lambda i: (0, i, 0)), tab, tab,
                   pl.BlockSpec((IDX_HEADS, tm, 128), lambda i: (0, i, 0)), tab, tab],
        out_shape=[jax.ShapeDtypeStruct((DSA_HEADS, s, 128), jnp.bfloat16),
                   jax.ShapeDtypeStruct((s, 128), jnp.bfloat16),
                   jax.ShapeDtypeStruct((s, 128), jnp.bfloat16),
                   jax.ShapeDtypeStruct((IDX_HEADS, s, 128), jnp.bfloat16),
                   jax.ShapeDtypeStruct((s, 128), jnp.bfloat16),
                   jax.ShapeDtypeStruct((s, 128), jnp.float32)],
        compiler_params=_cparams(1),
        name="dsa_prep",
    )(p, p, p, p, p, dcos, dsin)


DSA_QB = 256
DSA_TK = 256
SEL_ROWS = 64
SEL_LANES = 512


def _sortable_key(score):
    bits = lax.bitcast_convert_type(score, jnp.int32)
    sign = lax.shift_right_arithmetic(bits, 31)
    return (bits ^ (sign & 0x7FFFFFFF)) - sign


def _count_ge(keys_ref, r0, n_steps, cand_b, idx_from=None):
    def body(s, acc):
        c0 = pl.multiple_of(s * SEL_LANES, SEL_LANES)
        for u in range(SEL_LANES // LANES):
            kk = keys_ref[pl.ds(r0, SEL_ROWS), pl.ds(c0 + u * LANES, LANES)]
            if idx_from is None:
                acc = acc + jnp.where(kk >= cand_b, 1.0, 0.0)
            else:
                col = lax.broadcasted_iota(jnp.int32, kk.shape, 1) + (c0 + u * LANES)
                acc = acc + jnp.where(kk == cand_b, jnp.where(col >= idx_from, 1.0, 0.0), 0.0)
        return acc
    acc = lax.fori_loop(0, n_steps, body, jnp.zeros((SEL_ROWS, LANES), jnp.float32))
    return jnp.sum(acc, axis=-1, keepdims=True)


def _dsa_kernel(qp_ref, iqp_ref, w_ref, g_ref, k_ref, v_ref, ik_ref, o_ref,
                keys_ref, tau_ref, m_ref, l_ref, acc_ref, *, topk):
    QB, TK = DSA_QB, DSA_TK
    G = DSA_HEADS // DSA_KV_HEADS
    i = pl.program_id(0)
    n_tiles = i + 1
    row_id = lax.broadcasted_iota(jnp.int32, (QB, TK), 0)
    col_id = lax.broadcasted_iota(jnp.int32, (QB, TK), 1)

    iq = iqp_ref[...].reshape(IDX_HEADS * QB, LANES)
    wb = [jnp.broadcast_to(w_ref[:, h:h + 1], (QB, TK)) for h in range(IDX_HEADS)]

    def score_tile(j, diag):
        c0 = pl.multiple_of(j * TK, TK)
        s_all = _dot_nt(iq, ik_ref[pl.ds(c0, TK), :])
        score = None
        for h in range(IDX_HEADS):
            term = wb[h] * jnp.maximum(s_all[h * QB:(h + 1) * QB], 0.0)
            score = term if score is None else score + term
        key = _sortable_key(score)
        if diag:
            key = jnp.where(col_id <= row_id, key, INT_MIN)
        keys_ref[:, pl.ds(c0, TK)] = key

    def score_body(j, carry):
        score_tile(j, False)
        return carry
    lax.fori_loop(0, i, score_body, 0)
    score_tile(i, True)

    n_steps = (n_tiles * TK) // SEL_LANES + ((n_tiles * TK) % SEL_LANES != 0).astype(jnp.int32)
    @pl.when((n_tiles * TK) % SEL_LANES != 0)
    def _():
        c0 = pl.multiple_of(n_tiles * TK, TK)
        keys_ref[:, pl.ds(c0, TK)] = jnp.full((QB, TK), INT_MIN, jnp.int32)

    for rb in range(QB // SEL_ROWS):
        r0 = rb * SEL_ROWS

        def bit_body(b, carry):
            lo, cnt_lo = carry
            cand = lo + lax.shift_left(jnp.int32(1), 31 - b)
            cnt = _count_ge(keys_ref, r0, n_steps, jnp.broadcast_to(cand, (SEL_ROWS, LANES)))
            ok = cnt >= float(topk)
            return jnp.where(ok, cand, lo), jnp.where(ok, cnt, cnt_lo)

        lo0 = jnp.full((SEL_ROWS, 1), INT_MIN, jnp.int32)
        tau, cnt_tau = lax.fori_loop(0, 32, bit_body, (lo0, jnp.zeros((SEL_ROWS, 1), jnp.float32)))
        found = tau > INT_MIN
        excess = jnp.where(found, cnt_tau - float(topk), 0.0)
        tau_ref[pl.ds(r0, SEL_ROWS), :] = jnp.broadcast_to(jnp.where(found, tau, INT_MIN + 1), (SEL_ROWS, LANES))

        @pl.when(jnp.max(excess) > 0.0)
        def _():
            tau_b = jnp.broadcast_to(tau, (SEL_ROWS, LANES))

            def idx_body(b, sig):
                cand = sig + lax.shift_left(jnp.int32(1), 14 - b)
                cnt = _count_ge(keys_ref, r0, n_steps, tau_b,
                                idx_from=jnp.broadcast_to(cand, (SEL_ROWS, LANES)))
                return jnp.where(cnt >= excess, cand, sig)
            sig = lax.fori_loop(0, 15, idx_body, jnp.zeros((SEL_ROWS, 1), jnp.int32))
            drop_from = jnp.broadcast_to(jnp.where(excess > 0.0, sig, jnp.int32(2 ** 30)), (SEL_ROWS, LANES))

            def fix_body(s, carry):
                c0 = pl.multiple_of(s * LANES, LANES)
                kk = keys_ref[pl.ds(r0, SEL_ROWS), pl.ds(c0, LANES)]
                col = lax.broadcasted_iota(jnp.int32, kk.shape, 1) + c0
                dropped = jnp.where(col >= drop_from, kk - 1, kk)
                keys_ref[pl.ds(r0, SEL_ROWS), pl.ds(c0, LANES)] = jnp.where(kk == tau_b, dropped, kk)
                return carry
            lax.fori_loop(0, n_steps * (SEL_LANES // LANES), fix_body, 0)

    m_ref[...] = jnp.full(m_ref.shape, NEG_INIT, jnp.float32)
    l_ref[...] = jnp.zeros(l_ref.shape, jnp.float32)
    acc_ref[...] = jnp.zeros(acc_ref.shape, jnp.float32)
    tau_full = jnp.broadcast_to(tau_ref[:, 0:1], (QB, TK))

    def attn_body(j, carry):
        c0 = pl.multiple_of(j * TK, TK)
        sel = keys_ref[:, pl.ds(c0, TK)] >= tau_full
        kt = k_ref[pl.ds(c0, TK), :]
        vt = v_ref[pl.ds(c0, TK), :]
        for n in range(DSA_KV_HEADS):
            qn = qp_ref[n * G:(n + 1) * G].reshape(G * QB, LANES)
            logits = _dot_nt(qn, kt)
            ps = []
            for gq in range(G):
                h = n * G + gq
                lm = jnp.where(sel, logits[gq * QB:(gq + 1) * QB], NEG_MASK)
                m_old = m_ref[h]
                m_new = jnp.maximum(m_old, jnp.max(lm, axis=-1, keepdims=True))
                p = jnp.exp(lm - m_new[:, 0:1])
                alpha = jnp.exp(m_old - m_new)
                l_ref[h] = alpha * l_ref[h] + jnp.sum(p, axis=-1, keepdims=True)
                m_ref[h] = m_new
                acc_ref[h] = alpha * acc_ref[h]
                ps.append(_bf(p))
            pv = _dot(jnp.concatenate(ps, axis=0), vt)
            for gq in range(G):
                h = n * G + gq
                acc_ref[h] = acc_ref[h] + pv[gq * QB:(gq + 1) * QB]
        return carry
    lax.fori_loop(0, n_tiles, attn_body, 0)

    lane = lax.broadcasted_iota(jnp.int32, (QB, LANES), 1)
    g = g_ref[...]
    for pair in range(DSA_HEADS // 2):
        halves = []
        for pos in range(2):
            h = 2 * pair + pos
            n = h // G
            o = acc_ref[h] * (1.0 / l_ref[h])
            if n != pos:
                o = pltpu.roll(o, 64, 1)
            halves.append(o)
        o = jnp.where(lane < 64, halves[0], halves[1])
        sl = slice(pair * LANES, (pair + 1) * LANES)
        o_ref[:, sl] = _bf(o * _silu(g[:, sl]))


def _dsa_call(p, qp, kk, vv, iqp, ik, w4):
    s = p.shape[0]
    topk = min(TOPK_MAX, s // 4)
    QB = DSA_QB
    assert DSA_TK == QB and s % QB == 0 and s % SEL_LANES == 0
    full = lambda: pl.BlockSpec((s, LANES), lambda i: (0, 0))
    return pl.pallas_call(
        functools.partial(_dsa_kernel, topk=topk),
        grid=(s // QB,),
        in_specs=[pl.BlockSpec((DSA_HEADS, QB, LANES), lambda i: (0, i, 0)),
                  pl.BlockSpec((IDX_HEADS, QB, LANES), lambda i: (0, i, 0)),
                  pl.BlockSpec((QB, LANES), lambda i: (i, 0)),
                  _pspec(QB, "dsa_g", 512),
                  full(), full(), full()],
        out_specs=pl.BlockSpec((QB, 512), lambda i: (i, 0)),
        out_shape=jax.ShapeDtypeStruct((s, 512), jnp.bfloat16),
        scratch_shapes=[pltpu.VMEM((QB, s), jnp.int32),
                        pltpu.VMEM((QB, LANES), jnp.int32),
                        pltpu.VMEM((DSA_HEADS, QB, 1), jnp.float32),
                        pltpu.VMEM((DSA_HEADS, QB, 1), jnp.float32),
                        pltpu.VMEM((DSA_HEADS, QB, LANES), jnp.float32)],
        compiler_params=_cparams(1, vmem_mb=56),
        name="dsa",
    )(qp, iqp, w4, p, kk, vv, ik)


def _merge_kernel(x_ref, ret_ref, dsa_ref, gl_ref, m_ref, wr_ref, wd_ref, wg_ref, wo_ref,
                  post_ref, gate_ref, o_ref):
    d = D_MODEL
    y = (_sigmoid(m_ref[:, 0:d]) * _dot(ret_ref[...], wr_ref[...])
         + _sigmoid(m_ref[:, d:2 * d]) * _dot(dsa_ref[...], wd_ref[...])
         + _sigmoid(m_ref[:, 2 * d:3 * d]) * _dot(gl_ref[...], wg_ref[...]))
    y = _dot(_bf(y), wo_ref[...])
    yn = y * lax.rsqrt(jnp.mean(y * y, axis=-1, keepdims=True) + RMS_EPS) * post_ref[...]
    o_ref[...] = x_ref[...] + gate_ref[...] * yn


def _merge_call(x2, ret, dsa, gl, p, wr, wd, wg, wo, post, gate):
    s, d = x2.shape
    tm = min(512, s)
    rows = lambda w: pl.BlockSpec((tm, w), lambda i: (i, 0))
    whole = lambda a: pl.BlockSpec(a.shape, lambda i: (0, 0))
    return pl.pallas_call(
        _merge_kernel,
        grid=(s // tm,),
        in_specs=[rows(d), rows(512), rows(512), rows(512), _pspec(tm, "merge", 3072),
                  whole(wr), whole(wd), whole(wg), whole(wo), whole(post), whole(gate)],
        out_specs=rows(d),
        out_shape=jax.ShapeDtypeStruct((s, d), jnp.float32),
        compiler_params=_cparams(1),
        name="merge_out",
    )(x2, ret, dsa, gl, p, wr, wd, wg, wo, post, gate)


def _pack_w_in(w_in):
    depth, d, _ = w_in.shape
    zeros = lambda n: jnp.zeros((depth, d, n), w_in.dtype)
    src = lambda name: w_in[:, :, _SRC[name][0]:_SRC[name][0] + _SRC[name][1]]
    pieces, at = [], 0
    for name, width in _PACK:
        assert at == PCOL[name]
        if name == "idx_kw":
            cols = [src("idx_k"), src("idx_w"), zeros(width - IDX_DH - IDX_HEADS)]
        elif name == "gla_a":
            cols = [src("gla_a"), zeros(width - GLA_RANK)]
        else:
            cols = [src(name)]
        pieces += cols
        at += width
    pieces.append(zeros(P_WIDTH - at))
    return jnp.concatenate(pieces, axis=-1).astype(jnp.bfloat16)


def kernel(x, c, positions, ada_w, ada_b, pre_norm, post_norm, w_in, gla_w_lr, gla_b_lr,
           w_br_ret, w_br_dsa, w_br_gla, w_out):
    b, s, d = x.shape
    assert b == 1 and d == D_MODEL
    depth = ada_w.shape[0]
    x2 = x.reshape(s, d)
    mod = _mod_call(jnp.broadcast_to(c, (8, d)), ada_w, ada_b.reshape(depth, 1, 3 * d))[:, 0:1, :]
    rcos, rsin, dcos, dsin = _tab_call(positions.reshape(s, 1))
    w_pack = _pack_w_in(w_in)
    wlr_pad = jnp.pad(gla_w_lr, ((0, 0), (0, LANES - GLA_RANK), (0, 0)))
    for l in range(depth):
        shift, scale, gate = mod[l, :, 0:d], mod[l, :, d:2 * d], mod[l, :, 2 * d:3 * d]
        p = _proj_call(x2, pre_norm[l][None, :], scale, shift, w_pack[l])
        ret = _ret_call(p, rcos, rsin)
        gl = _gla_call(p, wlr_pad[l], gla_b_lr[l][None, :])
        qp, kk, vv, iqp, ik, w4 = _dprep_call(p, dcos, dsin)
        dsa = _dsa_call(p, qp, kk, vv, iqp, ik, w4)
        x2 = _merge_call(x2, ret, dsa, gl, p, _bf(w_br_ret[l]), _bf(w_br_dsa[l]), _bf(w_br_gla[l]),
                         _bf(w_out[l]), post_norm[l][None, :], gate)
    return x2.reshape(b, s, d)
```

```python
import functools
import math

import jax
import jax.numpy as jnp
from jax import lax
from jax.experimental import pallas as pl
from jax.experimental.pallas import tpu as pltpu

D_MODEL = 1024
DEPTH = 4
RET_HEADS, RET_DK, RET_DV, RET_CHUNK, RET_THETA = 4, 64, 128, 128, 10000.0
DSA_HEADS, DSA_KV_HEADS, DSA_DH = 8, 2, 64
DSA_ROT = DSA_DH // 4
ROPE_THETA = 500000.0
IDX_HEADS, IDX_DH = 4, 64
TOPK_MAX = 256
GLA_HEADS, GLA_DK, GLA_DV, GLA_RANK, GLA_TAU, GLA_CHUNK = 4, 64, 128, 16, 16.0, 64
RMS_EPS = 1e-6
LANES = 128

_SRC = {}
_off = 0
for _name, _w in (("ret_q", 256), ("ret_k", 256), ("ret_v", 512), ("ret_g", 512),
                  ("dsa_q", 512), ("dsa_k", 128), ("dsa_v", 128), ("dsa_g", 512),
                  ("idx_q", 256), ("idx_k", 64), ("idx_w", 4),
                  ("gla_q", 256), ("gla_k", 256), ("gla_v", 512), ("gla_g", 512), ("gla_a", 16),
                  ("merge", 3072)):
    _SRC[_name] = (_off, _w)
    _off += _w
IN_WIDTH = _off

_PACK = (("merge", 3072), ("ret_q", 256), ("ret_k", 256), ("ret_v", 512), ("ret_g", 512),
         ("dsa_q", 512), ("dsa_g", 512), ("gla_v", 512), ("gla_g", 512),
         ("gla_q", 256), ("gla_k", 256), ("idx_q", 256),
         ("dsa_k", 128), ("dsa_v", 128), ("idx_kw", 128), ("gla_a", 128))
PCOL = {}
_off = 0
for _name, _w in _PACK:
    assert _off % min(_w, 1024) == 0
    PCOL[_name] = _off
    _off += _w
P_WIDTH = 8192
assert _off <= P_WIDTH

LOG2E = math.log2(math.e)
INT_MIN = -(2 ** 31)
NEG_INIT = -1e30
NEG_MASK = -2e30


def _cparams(n_axes, vmem_mb=48):
    return pltpu.CompilerParams(dimension_semantics=("arbitrary",) * n_axes,
                                vmem_limit_bytes=vmem_mb * 1024 * 1024)


def _bf(x):
    return x.astype(jnp.bfloat16)


def _dot(a, b):
    return jnp.dot(a, b, preferred_element_type=jnp.float32)


def _dot_nt(a, b):
    return lax.dot_general(a, b, (((1,), (1,)), ((), ())), preferred_element_type=jnp.float32)


def _dot_tn(a, b):
    return lax.dot_general(a, b, (((0,), (0,)), ((), ())), preferred_element_type=jnp.float32)


def _split3(x):
    hi = _bf(x)
    r1 = x - hi.astype(jnp.float32)
    mid = _bf(r1)
    lo = _bf(r1 - mid.astype(jnp.float32))
    return hi, mid, lo


def _silu(x):
    return x * (1.0 / (1.0 + jnp.exp(-x)))


def _sigmoid(x):
    return 1.0 / (1.0 + jnp.exp(-x))


def _mod_kernel(c_ref, w_ref, b_ref, o_ref):
    c = c_ref[...]
    ca = _silu(c)
    acc = None
    for t in _split3(ca):
        for u in _split3(w_ref[0]):
            part = _dot(t, u)
            acc = part if acc is None else acc + part
    o_ref[0] = acc + b_ref[0]


def _mod_call(c8, ada_w, ada_b3):
    depth, d, n = ada_w.shape
    tn = 1024
    return pl.pallas_call(
        _mod_kernel,
        grid=(depth, n // tn),
        in_specs=[pl.BlockSpec((8, d), lambda l, j: (0, 0)),
                  pl.BlockSpec((1, d, tn), lambda l, j: (l, 0, j)),
                  pl.BlockSpec((1, 1, tn), lambda l, j: (l, 0, j))],
        out_specs=pl.BlockSpec((1, 8, tn), lambda l, j: (l, 0, j)),
        out_shape=jax.ShapeDtypeStruct((depth, 8, n), jnp.float32),
        compiler_params=_cparams(2),
        name="adaln_mod",
    )(c8, ada_w, ada_b3)


def _tab_kernel(pos_ref, rf_ref, rs_ref, df_ref, ds_ref, rc_o, rsn_o, dc_o, dsn_o):
    pos = pos_ref[...].astype(jnp.float32)
    ang = pos * rf_ref[...]
    rc_o[...] = jnp.cos(ang)
    rsn_o[...] = jnp.sin(ang) * rs_ref[...]
    ang = pos * df_ref[...]
    dc_o[...] = jnp.cos(ang)
    dsn_o[...] = jnp.sin(ang) * ds_ref[...]


def _rope_rows():
    half = RET_DK // 2
    f = RET_THETA ** (-jnp.arange(half, dtype=jnp.float32) * 2.0 / RET_DK)
    rf = jnp.tile(jnp.concatenate([f, f]), RET_HEADS)[None, :]
    rs = jnp.tile(jnp.concatenate([-jnp.ones(half), jnp.ones(half)]), RET_HEADS)[None, :].astype(jnp.float32)
    half = DSA_ROT // 2
    f = ROPE_THETA ** (-jnp.arange(half, dtype=jnp.float32) * 2.0 / DSA_ROT)
    z = jnp.zeros(DSA_DH - DSA_ROT, jnp.float32)
    df = jnp.tile(jnp.concatenate([f, f, z]), 2)[None, :]
    ds = jnp.tile(jnp.concatenate([-jnp.ones(half), jnp.ones(half), z]), 2)[None, :].astype(jnp.float32)
    return rf, rs, df, ds


def _tab_call(pos_col):
    s = pos_col.shape[0]
    tm = min(1024, s)
    rf, rs, df, ds = _rope_rows()
    row = lambda w: pl.BlockSpec((1, w), lambda i: (0, 0))
    out = lambda w: pl.BlockSpec((tm, w), lambda i: (i, 0))
    return pl.pallas_call(
        _tab_kernel,
        grid=(s // tm,),
        in_specs=[pl.BlockSpec((tm, 1), lambda i: (i, 0)), row(256), row(256), row(128), row(128)],
        out_specs=[out(256), out(256), out(128), out(128)],
        out_shape=[jax.ShapeDtypeStruct((s, 256), jnp.float32), jax.ShapeDtypeStruct((s, 256), jnp.float32),
                   jax.ShapeDtypeStruct((s, 128), jnp.float32), jax.ShapeDtypeStruct((s, 128), jnp.float32)],
        compiler_params=_cparams(1),
        name="rope_tables",
    )(pos_col, rf, rs, df, ds)


def _swap_halves(x, half, period):
    w = x.shape[-1]
    lane = lax.broadcasted_iota(jnp.int32, x.shape, x.ndim - 1) & (period - 1)
    up = pltpu.roll(x, w - half, x.ndim - 1)
    dn = pltpu.roll(x, half, x.ndim - 1)
    return jnp.where(lane < half, up, dn)


def _rope(x, cos, sin_signed, half, period):
    return x * cos + _swap_halves(x, half, period) * sin_signed


def _proj_kernel(x_ref, pre_ref, sc_ref, sh_ref, w_ref, o_ref):
    x = x_ref[...]
    xn = x * lax.rsqrt(jnp.mean(x * x, axis=-1, keepdims=True) + RMS_EPS)
    h = xn * pre_ref[...] * (1.0 + sc_ref[...]) + sh_ref[...]
    o_ref[...] = _dot(_bf(h), w_ref[...])


def _proj_call(x2, pre, scale, shift, w_pack):
    s, d = x2.shape
    tm, tn = min(512, s), 2048
    vec = pl.BlockSpec((1, d), lambda j, i: (0, 0))
    return pl.pallas_call(
        _proj_kernel,
        grid=(P_WIDTH // tn, s // tm),
        in_specs=[pl.BlockSpec((tm, d), lambda j, i: (i, 0)), vec, vec, vec,
                  pl.BlockSpec((d, tn), lambda j, i: (0, j))],
        out_specs=pl.BlockSpec((tm, tn), lambda j, i: (i, j)),
        out_shape=jax.ShapeDtypeStruct((s, P_WIDTH), jnp.float32),
        compiler_params=_cparams(2),
        name="in_proj",
    )(x2, pre, scale, shift, w_pack)


def _pspec(tm, name, width):
    blk = PCOL[name] // width
    assert PCOL[name] % width == 0
    return pl.BlockSpec((tm, width), lambda i: (i, blk))


def _ret_log_g(h):
    return math.log1p(-(2.0 ** (-5.0 - h)))


def _ret_kernel(q_ref, k_ref, v_ref, g_ref, cos_ref, sin_ref, o_ref,
                state_ref, decay_ref, qdec_ref, kend_ref, *, chunks):
    C = RET_CHUNK

    @pl.when(pl.program_id(0) == 0)
    def _():
        state_ref[...] = jnp.zeros_like(state_ref)
        ii = lax.broadcasted_iota(jnp.int32, (C, C), 0)
        jj = lax.broadcasted_iota(jnp.int32, (C, C), 1)
        rel = (ii - jj).astype(jnp.float32)
        row = lax.broadcasted_iota(jnp.int32, (C, RET_DK), 0).astype(jnp.float32)
        for h in range(RET_HEADS):
            lg = _ret_log_g(h)
            decay_ref[h] = jnp.where(rel >= 0, jnp.exp(lg * jnp.maximum(rel, 0.0)), 0.0)
            qdec_ref[:, h * RET_DK:(h + 1) * RET_DK] = jnp.exp((row + 1.0) * lg)
            kend_ref[:, h * RET_DK:(h + 1) * RET_DK] = jnp.exp((C - 1.0 - row) * lg)

    for c in range(chunks):
        rows = slice(c * C, (c + 1) * C)
        cos, sin = cos_ref[rows, :], sin_ref[rows, :]
        q = _rope(q_ref[rows, :], cos, sin, RET_DK // 2, RET_DK) * (RET_DK ** -0.5)
        k = _rope(k_ref[rows, :], cos, sin, RET_DK // 2, RET_DK)
        qd = _bf(q * qdec_ref[...])
        kd = _bf(k * kend_ref[...])
        qb, kb = _bf(q), _bf(k)
        v = v_ref[rows, :]
        g = g_ref[rows, :]
        for h in range(RET_HEADS):
            dk = slice(h * RET_DK, (h + 1) * RET_DK)
            dv = slice(h * RET_DV, (h + 1) * RET_DV)
            vh = _bf(v[:, dv])
            scores = _dot_nt(qb[:, dk], kb[:, dk]) * decay_ref[h]
            st = state_ref[h]
            o = _dot(_bf(scores), vh) + _dot(qd[:, dk], _bf(st))
            state_ref[h] = math.exp(C * _ret_log_g(h)) * st + _dot_tn(kd[:, dk], vh)
            o = o * lax.rsqrt(jnp.mean(o * o, axis=-1, keepdims=True) + RMS_EPS)
            o_ref[rows, dv] = _bf(o * _silu(g[:, dv]))


def _ret_call(p, rcos, rsin):
    s = p.shape[0]
    chunks = 2
    tm = RET_CHUNK * chunks
    tab = pl.BlockSpec((tm, 256), lambda i: (i, 0))
    return pl.pallas_call(
        functools.partial(_ret_kernel, chunks=chunks),
        grid=(s // tm,),
        in_specs=[_pspec(tm, "ret_q", 256), _pspec(tm, "ret_k", 256), _pspec(tm, "ret_v", 512),
                  _pspec(tm, "ret_g", 512), tab, tab],
        out_specs=pl.BlockSpec((tm, 512), lambda i: (i, 0)),
        out_shape=jax.ShapeDtypeStruct((s, 512), jnp.bfloat16),
        scratch_shapes=[pltpu.VMEM((RET_HEADS, RET_DK, RET_DV), jnp.float32),
                        pltpu.VMEM((RET_HEADS, RET_CHUNK, RET_CHUNK), jnp.float32),
                        pltpu.VMEM((RET_CHUNK, RET_HEADS * RET_DK), jnp.float32),
                        pltpu.VMEM((RET_CHUNK, RET_HEADS * RET_DK), jnp.float32)],
        compiler_params=_cparams(1),
        name="retention",
    )(p, p, p, p, rcos, rsin)


def _gla_kernel(q_ref, k_ref, v_ref, g_ref, a_ref, wlr_ref, blr_ref, o_ref, state_ref, *, chunks):
    C = GLA_CHUNK

    @pl.when(pl.program_id(0) == 0)
    def _():
        state_ref[...] = jnp.zeros_like(state_ref)

    ii = lax.broadcasted_iota(jnp.int32, (C, C), 0)
    jj = lax.broadcasted_iota(jnp.int32, (C, C), 1)
    causal = jj <= ii
    tril = _bf(jnp.where(causal, 1.0, 0.0))
    wlr = wlr_ref[...]
    w_hi, w_mid, w_lo = _split3(wlr)

    for c in range(chunks):
        rows = slice(c * C, (c + 1) * C)
        a_hi, a_mid, a_lo = _split3(a_ref[rows, :])
        z = (_dot(a_hi, w_hi) + (_dot(a_hi, w_mid) + _dot(a_mid, w_hi))
             + (_dot(a_hi, w_lo) + _dot(a_mid, w_mid) + _dot(a_lo, w_hi))) + blr_ref[...]
        log_a = (jnp.minimum(z, 0.0) - jnp.log1p(jnp.exp(-jnp.abs(z)))) * (1.0 / GLA_TAU)
        l_hi, l_mid, l_lo = _split3(log_a)
        bcum = _dot(tril, l_hi) + _dot(tril, l_mid) + _dot(tril, l_lo)
        b_mid = bcum[C // 2 - 1:C // 2, :]
        b_last = bcum[C - 1:C, :]
        q = q_ref[rows, :] * (GLA_DK ** -0.5)
        k = k_ref[rows, :]
        qt = _bf(q * jnp.exp(bcum - b_mid))
        kt = _bf(k * jnp.exp(b_mid - bcum))
        qg = _bf(q * jnp.exp(bcum))
        kd = _bf(k * jnp.exp(b_last - bcum))
        e_last = jnp.exp(b_last)
        v = v_ref[rows, :]
        g = g_ref[rows, :]
        for h in range(GLA_HEADS):
            dk = slice(h * GLA_DK, (h + 1) * GLA_DK)
            dv = slice(h * GLA_DV, (h + 1) * GLA_DV)
            vh = _bf(v[:, dv])
            attn = jnp.where(causal, _dot_nt(qt[:, dk], kt[:, dk]), 0.0)
            st = state_ref[h]
            o = _dot(_bf(attn), vh) + _dot_nt(qg[:, dk], _bf(st))
            state_ref[h] = e_last[:, dk] * st + _dot_tn(vh, kd[:, dk])
            o = o * lax.rsqrt(jnp.mean(o * o, axis=-1, keepdims=True) + RMS_EPS)
            o_ref[rows, dv] = _bf(o * _silu(g[:, dv]))


def _gla_call(p, wlr_pad, blr):
    s = p.shape[0]
    chunks = 4
    tm = GLA_CHUNK * chunks
    return pl.pallas_call(
        functools.partial(_gla_kernel, chunks=chunks),
        grid=(s // tm,),
        in_specs=[_pspec(tm, "gla_q", 256), _pspec(tm, "gla_k", 256), _pspec(tm, "gla_v", 512),
                  _pspec(tm, "gla_g", 512), _pspec(tm, "gla_a", 128),
                  pl.BlockSpec((128, 256), lambda i: (0, 0)), pl.BlockSpec((1, 256), lambda i: (0, 0))],
        out_specs=pl.BlockSpec((tm, 512), lambda i: (i, 0)),
        out_shape=jax.ShapeDtypeStruct((s, 512), jnp.bfloat16),
        scratch_shapes=[pltpu.VMEM((GLA_HEADS, GLA_DV, GLA_DK), jnp.float32)],
        compiler_params=_cparams(1),
        name="gla",
    )(p, p, p, p, p, wlr_pad, blr)


def _place_head(pair, src_pos, dst_pos):
    lane = lax.broadcasted_iota(jnp.int32, pair.shape, 1)
    if src_pos != dst_pos:
        pair = pltpu.roll(pair, 64, 1)
    keep = (lane < 64) if dst_pos == 0 else (lane >= 64)
    return jnp.where(keep, pair, 0.0)


def _dprep_kernel(q_ref, k_ref, v_ref, iq_ref, ikw_ref, cos_ref, sin_ref,
                  qp_o, k_o, v_o, iqp_o, ik_o, w_o):
    cos, sin = cos_ref[...], sin_ref[...]
    half = DSA_ROT // 2
    cos4 = jnp.concatenate([cos] * 4, axis=1)
    sin4 = jnp.concatenate([sin] * 4, axis=1)
    q = _rope(q_ref[...], cos4, sin4, half, DSA_DH) * (DSA_DH ** -0.5 * LOG2E)
    for h in range(DSA_HEADS):
        pair = q[:, (h // 2) * 128:(h // 2 + 1) * 128]
        qp_o[h] = _bf(_place_head(pair, h % 2, h // (DSA_HEADS // DSA_KV_HEADS)))
    k_o[...] = _bf(_rope(k_ref[...], cos, sin, half, DSA_DH))
    v = v_ref[...]
    lane = lax.broadcasted_iota(jnp.int32, v.shape, 1)
    v_o[0] = _bf(jnp.where(lane < DSA_DH, v, 1.0))
    v_o[1] = _bf(jnp.where(lane < DSA_DH, pltpu.roll(v, DSA_DH, 1), 1.0))
    iq = _rope(iq_ref[...], cos4[:, :256], sin4[:, :256], half, IDX_DH)
    for h in range(IDX_HEADS):
        pair = iq[:, (h // 2) * 128:(h // 2 + 1) * 128]
        iqp_o[h] = _bf(_place_head(pair, h % 2, 0))
    ikw = ikw_ref[...]
    lane = lax.broadcasted_iota(jnp.int32, ikw.shape, 1)
    ik_o[...] = _bf(jnp.where(lane < IDX_DH, _rope(ikw, cos, sin, half, DSA_DH), 0.0))
    wscale = (IDX_HEADS ** -0.5) * (IDX_DH ** -0.5)
    w_o[...] = pltpu.roll(ikw, 128 - IDX_DH, 1) * wscale


def _dprep_call(p, dcos, dsin):
    s = p.shape[0]
    tm = min(512, s)
    tab = pl.BlockSpec((tm, 128), lambda i: (i, 0))
    return pl.pallas_call(
        _dprep_kernel,
        grid=(s // tm,),
        in_specs=[_pspec(tm, "dsa_q", 512), _pspec(tm, "dsa_k", 128), _pspec(tm, "dsa_v", 128),
                  _pspec(tm, "idx_q", 256), _pspec(tm, "idx_kw", 128), tab, tab],
        out_specs=[pl.BlockSpec((DSA_HEADS, tm, 128), lambda i: (0, i, 0)), tab,
                   pl.BlockSpec((DSA_KV_HEADS, tm, 128), lambda i: (0, i, 0)),
                   pl.BlockSpec((IDX_HEADS, tm, 128), lambda i: (0, i, 0)), tab, tab],
        out_shape=[jax.ShapeDtypeStruct((DSA_HEADS, s, 128), jnp.bfloat16),
                   jax.ShapeDtypeStruct((s, 128), jnp.bfloat16),
                   jax.ShapeDtypeStruct((DSA_KV_HEADS, s, 128), jnp.bfloat16),
                   jax.ShapeDtypeStruct((IDX_HEADS, s, 128), jnp.bfloat16),
                   jax.ShapeDtypeStruct((s, 128), jnp.bfloat16),
                   jax.ShapeDtypeStruct((s, 128), jnp.float32)],
        compiler_params=_cparams(1),
        name="dsa_prep",
    )(p, p, p, p, p, dcos, dsin)


DSA_QB = 256
IDX_TK = 256
ATT_TK = 1024
SEL_ROWS = 128
SEL_LANES = 512


def _sortable_key(score):
    bits = lax.bitcast_convert_type(score, jnp.int32)
    sign = lax.shift_right_arithmetic(bits, 31)
    return (bits ^ (sign & 0x7FFFFFFF)) - sign


def _count_lanes(keys_ref, r0, n_steps, cand, idx_from=None):
    def body(s, acc):
        c0 = pl.multiple_of(s * SEL_LANES, SEL_LANES)
        for u in range(SEL_LANES // LANES):
            kk = keys_ref[pl.ds(r0, SEL_ROWS), pl.ds(c0 + u * LANES, LANES)]
            if idx_from is None:
                acc = acc + jnp.where(kk >= cand, 1.0, 0.0)
            else:
                col = lax.broadcasted_iota(jnp.int32, kk.shape, 1) + (c0 + u * LANES)
                acc = acc + jnp.where(kk == cand, jnp.where(col >= idx_from, 1.0, 0.0), 0.0)
        return acc
    return lax.fori_loop(0, n_steps, body, jnp.zeros((SEL_ROWS, LANES), jnp.float32))


def _row_total(lane_counts):
    ones = jnp.ones((LANES, LANES), jnp.bfloat16)
    return _dot(_bf(lane_counts), ones)


def _dsa_kernel(qp_ref, iqp_ref, w_ref, g_ref, k_ref, va_ref, ik_ref, o_ref,
                keys_ref, lo_ref, cnt_ref, part_ref, m_ref, acc_ref, *, topk):
    QB, TK = DSA_QB, IDX_TK
    G = DSA_HEADS // DSA_KV_HEADS
    i = pl.program_id(0)
    n_idx = i + 1
    n_att = (n_idx * IDX_TK + ATT_TK - 1) // ATT_TK
    n_sel = n_att * (ATT_TK // SEL_LANES)
    row_id = lax.broadcasted_iota(jnp.int32, (QB, TK), 0)
    col_id = lax.broadcasted_iota(jnp.int32, (QB, TK), 1)

    iq = iqp_ref[...].reshape(IDX_HEADS * QB, LANES)
    wb = [jnp.broadcast_to(w_ref[:, h:h + 1], (QB, TK)) for h in range(IDX_HEADS)]

    def score_tile(j, diag):
        c0 = pl.multiple_of(j * TK, TK)
        s_all = _dot_nt(iq, ik_ref[pl.ds(c0, TK), :])
        score = None
        for h in range(IDX_HEADS):
            term = wb[h] * jnp.maximum(s_all[h * QB:(h + 1) * QB], 0.0)
            score = term if score is None else score + term
        key = _sortable_key(score)
        if diag:
            key = jnp.where(col_id <= row_id, key, INT_MIN)
        keys_ref[:, pl.ds(c0, TK)] = key

    def score_body(j, carry):
        score_tile(j, False)
        return carry
    lax.fori_loop(0, i, score_body, 0)
    score_tile(i, True)

    def blank_body(j, carry):
        c0 = pl.multiple_of(j * TK, TK)
        keys_ref[:, pl.ds(c0, TK)] = jnp.full((QB, TK), INT_MIN, jnp.int32)
        return carry
    lax.fori_loop(n_idx, n_att * (ATT_TK // TK), blank_body, 0)

    lo_ref[...] = jnp.full(lo_ref.shape, INT_MIN, jnp.int32)
    cnt_ref[...] = jnp.zeros(cnt_ref.shape, jnp.float32)

    def bit_body(b, carry):
        bit = lax.shift_left(jnp.int32(1), 31 - b)
        for rb in range(QB // SEL_ROWS):
            rows = pl.ds(rb * SEL_ROWS, SEL_ROWS)
            part_ref[rows, :] = _count_lanes(keys_ref, rb * SEL_ROWS, n_sel, lo_ref[rows, :] + bit)
        cnt = _row_total(part_ref[...])
        lo = lo_ref[...]
        ok = cnt >= float(topk)
        lo_ref[...] = jnp.where(ok, lo + bit, lo)
        cnt_ref[...] = jnp.where(ok, cnt, cnt_ref[...])
        return carry
    lax.fori_loop(0, 32, bit_body, 0)

    tau = lo_ref[...]
    found = tau > INT_MIN
    excess = jnp.where(found, cnt_ref[...] - float(topk), 0.0)
    lo_ref[...] = jnp.where(found, tau, INT_MIN + 1)

    @pl.when(jnp.max(excess) > 0.0)
    def _():
        cnt_ref[...] = excess
        for rb in range(QB // SEL_ROWS):
            r0 = rb * SEL_ROWS
            rows = pl.ds(r0, SEL_ROWS)
            tau_b = lo_ref[rows, :]
            exc = cnt_ref[rows, :]

            def idx_body(b, sig):
                cand = sig + lax.shift_left(jnp.int32(1), 14 - b)
                cnt = _row_total(_count_lanes(keys_ref, r0, n_sel, tau_b, idx_from=cand))
                return jnp.where(cnt >= exc, cand, sig)
            sig = lax.fori_loop(0, 15, idx_body, jnp.zeros((SEL_ROWS, LANES), jnp.int32))
            drop_from = jnp.where(exc > 0.0, sig, jnp.int32(2 ** 30))

            def fix_body(s, carry):
                c0 = pl.multiple_of(s * LANES, LANES)
                kk = keys_ref[rows, pl.ds(c0, LANES)]
                col = lax.broadcasted_iota(jnp.int32, kk.shape, 1) + c0
                dropped = jnp.where(col >= drop_from, kk - 1, kk)
                keys_ref[rows, pl.ds(c0, LANES)] = jnp.where(kk == tau_b, dropped, kk)
                return carry
            lax.fori_loop(0, n_sel * (SEL_LANES // LANES), fix_body, 0)

    m_ref[...] = jnp.full(m_ref.shape, NEG_INIT, jnp.float32)
    acc_ref[...] = jnp.zeros(acc_ref.shape, jnp.float32)
    n_grp = ATT_TK // LANES

    def attn_body(j, carry):
        c0 = pl.multiple_of(j * ATT_TK, ATT_TK)
        tau_b = lo_ref[...]
        bias = jnp.concatenate(
            [jnp.where(keys_ref[:, pl.ds(c0 + u * LANES, LANES)] >= tau_b, 0.0, NEG_MASK)
             for u in range(n_grp)], axis=1)
        kt = k_ref[pl.ds(c0, ATT_TK), :]
        for n in range(DSA_KV_HEADS):
            qn = qp_ref[n * G:(n + 1) * G].reshape(G * QB, LANES)
            logits = _dot_nt(qn, kt)
            ps = []
            for gq in range(G):
                h = n * G + gq
                lm = logits[gq * QB:(gq + 1) * QB] + bias
                tmax = lm[:, 0:LANES]
                for u in range(1, n_grp):
                    tmax = jnp.maximum(tmax, lm[:, u * LANES:(u + 1) * LANES])
                m_old = m_ref[h]
                m_new = jnp.maximum(m_old, jnp.max(tmax, axis=-1, keepdims=True))
                ps.append(jnp.concatenate(
                    [_bf(jnp.exp2(lm[:, u * LANES:(u + 1) * LANES] - m_new)) for u in range(n_grp)], axis=1))
                m_ref[h] = m_new
                acc_ref[h] = jnp.exp2(m_old - m_new) * acc_ref[h]
            pv = _dot(jnp.concatenate(ps, axis=0), va_ref[n, pl.ds(c0, ATT_TK), :])
            for gq in range(G):
                h = n * G + gq
                acc_ref[h] = acc_ref[h] + pv[gq * QB:(gq + 1) * QB]
        return carry
    lax.fori_loop(0, n_att, attn_body, 0)

    lane = lax.broadcasted_iota(jnp.int32, (QB, LANES), 1)
    g = g_ref[...]
    for pair in range(DSA_HEADS // 2):
        halves = []
        for pos in range(2):
            a = acc_ref[2 * pair + pos]
            halves.append(a * (1.0 / pltpu.roll(a, DSA_DH, 1)))
        o = jnp.where(lane < DSA_DH, halves[0], pltpu.roll(halves[1], DSA_DH, 1))
        sl = slice(pair * LANES, (pair + 1) * LANES)
        o_ref[:, sl] = _bf(o * _silu(g[:, sl]))


def _dsa_call(p, qp, kk, va, iqp, ik, w4):
    s = p.shape[0]
    topk = min(TOPK_MAX, s // 4)
    QB = DSA_QB
    assert IDX_TK == QB and s % ATT_TK == 0 and ATT_TK % SEL_LANES == 0 and s // LANES <= 256
    once = pl.Buffered(1)
    return pl.pallas_call(
        functools.partial(_dsa_kernel, topk=topk),
        grid=(s // QB,),
        in_specs=[pl.BlockSpec((DSA_HEADS, QB, LANES), lambda i: (0, i, 0)),
                  pl.BlockSpec((IDX_HEADS, QB, LANES), lambda i: (0, i, 0)),
                  pl.BlockSpec((QB, LANES), lambda i: (i, 0)),
                  _pspec(QB, "dsa_g", 512),
                  pl.BlockSpec((s, LANES), lambda i: (0, 0), pipeline_mode=once),
                  pl.BlockSpec((DSA_KV_HEADS, s, LANES), lambda i: (0, 0, 0), pipeline_mode=once),
                  pl.BlockSpec((s, LANES), lambda i: (0, 0), pipeline_mode=once)],
        out_specs=pl.BlockSpec((QB, 512), lambda i: (i, 0)),
        out_shape=jax.ShapeDtypeStruct((s, 512), jnp.bfloat16),
        scratch_shapes=[pltpu.VMEM((QB, s), jnp.int32),
                        pltpu.VMEM((QB, LANES), jnp.int32),
                        pltpu.VMEM((QB, LANES), jnp.float32),
                        pltpu.VMEM((QB, LANES), jnp.float32),
                        pltpu.VMEM((DSA_HEADS, QB, LANES), jnp.float32),
                        pltpu.VMEM((DSA_HEADS, QB, LANES), jnp.float32)],
        compiler_params=_cparams(1, vmem_mb=56),
        name="dsa",
    )(qp, iqp, w4, p, kk, va, ik)


def _merge_kernel(x_ref, ret_ref, dsa_ref, gl_ref, m_ref, wr_ref, wd_ref, wg_ref, wo_ref,
                  post_ref, gate_ref, o_ref):
    d = D_MODEL
    y = (_sigmoid(m_ref[:, 0:d]) * _dot(ret_ref[...], wr_ref[...])
         + _sigmoid(m_ref[:, d:2 * d]) * _dot(dsa_ref[...], wd_ref[...])
         + _sigmoid(m_ref[:, 2 * d:3 * d]) * _dot(gl_ref[...], wg_ref[...]))
    y = _dot(_bf(y), wo_ref[...])
    yn = y * lax.rsqrt(jnp.mean(y * y, axis=-1, keepdims=True) + RMS_EPS) * post_ref[...]
    o_ref[...] = x_ref[...] + gate_ref[...] * yn


def _merge_call(x2, ret, dsa, gl, p, wr, wd, wg, wo, post, gate):
    s, d = x2.shape
    tm = min(512, s)
    rows = lambda w: pl.BlockSpec((tm, w), lambda i: (i, 0))
    whole = lambda a: pl.BlockSpec(a.shape, lambda i: (0, 0))
    return pl.pallas_call(
        _merge_kernel,
        grid=(s // tm,),
        in_specs=[rows(d), rows(512), rows(512), rows(512), _pspec(tm, "merge", 3072),
                  whole(wr), whole(wd), whole(wg), whole(wo), whole(post), whole(gate)],
        out_specs=rows(d),
        out_shape=jax.ShapeDtypeStruct((s, d), jnp.float32),
        compiler_params=_cparams(1),
        name="merge_out",
    )(x2, ret, dsa, gl, p, wr, wd, wg, wo, post, gate)


def _pack_w_in(w_in):
    depth, d, _ = w_in.shape
    zeros = lambda n: jnp.zeros((depth, d, n), w_in.dtype)
    src = lambda name: w_in[:, :, _SRC[name][0]:_SRC[name][0] + _SRC[name][1]]
    pieces, at = [], 0
    for name, width in _PACK:
        assert at == PCOL[name]
        if name == "idx_kw":
            cols = [src("idx_k"), src("idx_w"), zeros(width - IDX_DH - IDX_HEADS)]
        elif name == "gla_a":
            cols = [src("gla_a"), zeros(width - GLA_RANK)]
        else:
            cols = [src(name)]
        pieces += cols
        at += width
    pieces.append(zeros(P_WIDTH - at))
    return jnp.concatenate(pieces, axis=-1).astype(jnp.bfloat16)


def kernel(x, c, positions, ada_w, ada_b, pre_norm, post_norm, w_in, gla_w_lr, gla_b_lr,
           w_br_ret, w_br_dsa, w_br_gla, w_out):
    b, s, d = x.shape
    assert b == 1 and d == D_MODEL
    depth = ada_w.shape[0]
    x2 = x.reshape(s, d)
    mod = _mod_call(jnp.broadcast_to(c, (8, d)), ada_w, ada_b.reshape(depth, 1, 3 * d))[:, 0:1, :]
    rcos, rsin, dcos, dsin = _tab_call(positions.reshape(s, 1))
    w_pack = _pack_w_in(w_in)
    wlr_pad = jnp.pad(gla_w_lr, ((0, 0), (0, LANES - GLA_RANK), (0, 0)))
    for l in range(depth):
        shift, scale, gate = mod[l, :, 0:d], mod[l, :, d:2 * d], mod[l, :, 2 * d:3 * d]
        p = _proj_call(x2, pre_norm[l][None, :], scale, shift, w_pack[l])
        ret = _ret_call(p, rcos, rsin)
        gl = _gla_call(p, wlr_pad[l], gla_b_lr[l][None, :])
        qp, kk, va, iqp, ik, w4 = _dprep_call(p, dcos, dsin)
        dsa = _dsa_call(p, qp, kk, va, iqp, ik, w4)
        x2 = _merge_call(x2, ret, dsa, gl, p, _bf(w_br_ret[l]), _bf(w_br_dsa[l]), _bf(w_br_gla[l]),
                         _bf(w_out[l]), post_norm[l][None, :], gate)
    return x2.reshape(b, s, d)
```

```python
import functools
import math

import jax
import jax.numpy as jnp
from jax import lax
from jax.experimental import pallas as pl
from jax.experimental.pallas import tpu as pltpu

D_MODEL = 1024
DEPTH = 4
RET_HEADS, RET_DK, RET_DV, RET_CHUNK, RET_THETA = 4, 64, 128, 128, 10000.0
DSA_HEADS, DSA_KV_HEADS, DSA_DH = 8, 2, 64
DSA_ROT = DSA_DH // 4
ROPE_THETA = 500000.0
IDX_HEADS, IDX_DH = 4, 64
TOPK_MAX = 256
GLA_HEADS, GLA_DK, GLA_DV, GLA_RANK, GLA_TAU, GLA_CHUNK = 4, 64, 128, 16, 16.0, 64
RMS_EPS = 1e-6
LANES = 128

_SRC = {}
_off = 0
for _name, _w in (("ret_q", 256), ("ret_k", 256), ("ret_v", 512), ("ret_g", 512),
                  ("dsa_q", 512), ("dsa_k", 128), ("dsa_v", 128), ("dsa_g", 512),
                  ("idx_q", 256), ("idx_k", 64), ("idx_w", 4),
                  ("gla_q", 256), ("gla_k", 256), ("gla_v", 512), ("gla_g", 512), ("gla_a", 16),
                  ("merge", 3072)):
    _SRC[_name] = (_off, _w)
    _off += _w
IN_WIDTH = _off

_PACK = (("merge", 3072), ("ret_q", 256), ("ret_k", 256), ("ret_v", 512), ("ret_g", 512),
         ("dsa_q", 512), ("dsa_g", 512), ("gla_v", 512), ("gla_g", 512),
         ("gla_q", 256), ("gla_k", 256), ("idx_q", 256),
         ("dsa_k", 128), ("dsa_v", 128), ("idx_kw", 128), ("gla_a", 128))
PCOL = {}
_off = 0
for _name, _w in _PACK:
    assert _off % min(_w, 1024) == 0
    PCOL[_name] = _off
    _off += _w
P_WIDTH = 8192
assert _off <= P_WIDTH

LOG2E = math.log2(math.e)
INT_MIN = -(2 ** 31)
F32_LOWEST = -3.4028234663852886e38
NEG_INF_KEY = -0x7F800000
NEG_INIT = -1e30
NEG_MASK = -2e30


def _cparams(n_axes, vmem_mb=48):
    return pltpu.CompilerParams(dimension_semantics=("arbitrary",) * n_axes,
                                vmem_limit_bytes=vmem_mb * 1024 * 1024)


def _bf(x):
    return x.astype(jnp.bfloat16)


def _dot(a, b):
    return jnp.dot(a, b, preferred_element_type=jnp.float32)


def _dot_nt(a, b):
    return lax.dot_general(a, b, (((1,), (1,)), ((), ())), preferred_element_type=jnp.float32)


def _dot_tn(a, b):
    return lax.dot_general(a, b, (((0,), (0,)), ((), ())), preferred_element_type=jnp.float32)


def _split3(x):
    hi = _bf(x)
    r1 = x - hi.astype(jnp.float32)
    mid = _bf(r1)
    lo = _bf(r1 - mid.astype(jnp.float32))
    return hi, mid, lo


def _silu(x):
    return x * (1.0 / (1.0 + jnp.exp(-x)))


def _sigmoid(x):
    return 1.0 / (1.0 + jnp.exp(-x))


def _mod_kernel(c_ref, w_ref, b_ref, o_ref):
    c = c_ref[...]
    ca = _silu(c)
    acc = None
    for t in _split3(ca):
        for u in _split3(w_ref[0]):
            part = _dot(t, u)
            acc = part if acc is None else acc + part
    o_ref[0] = acc + b_ref[0]


def _mod_call(c8, ada_w, ada_b3):
    depth, d, n = ada_w.shape
    tn = 1024
    return pl.pallas_call(
        _mod_kernel,
        grid=(depth, n // tn),
        in_specs=[pl.BlockSpec((8, d), lambda l, j: (0, 0)),
                  pl.BlockSpec((1, d, tn), lambda l, j: (l, 0, j)),
                  pl.BlockSpec((1, 1, tn), lambda l, j: (l, 0, j))],
        out_specs=pl.BlockSpec((1, 8, tn), lambda l, j: (l, 0, j)),
        out_shape=jax.ShapeDtypeStruct((depth, 8, n), jnp.float32),
        compiler_params=_cparams(2),
        name="adaln_mod",
    )(c8, ada_w, ada_b3)


def _tab_kernel(pos_ref, rf_ref, rs_ref, df_ref, ds_ref, rc_o, rsn_o, dc_o, dsn_o):
    pos = pos_ref[...].astype(jnp.float32)
    ang = pos * rf_ref[...]
    rc_o[...] = jnp.cos(ang)
    rsn_o[...] = jnp.sin(ang) * rs_ref[...]
    ang = pos * df_ref[...]
    dc_o[...] = jnp.cos(ang)
    dsn_o[...] = jnp.sin(ang) * ds_ref[...]


def _rope_rows():
    half = RET_DK // 2
    f = RET_THETA ** (-jnp.arange(half, dtype=jnp.float32) * 2.0 / RET_DK)
    rf = jnp.tile(jnp.concatenate([f, f]), RET_HEADS)[None, :]
    rs = jnp.tile(jnp.concatenate([-jnp.ones(half), jnp.ones(half)]), RET_HEADS)[None, :].astype(jnp.float32)
    half = DSA_ROT // 2
    f = ROPE_THETA ** (-jnp.arange(half, dtype=jnp.float32) * 2.0 / DSA_ROT)
    z = jnp.zeros(DSA_DH - DSA_ROT, jnp.float32)
    df = jnp.tile(jnp.concatenate([f, f, z]), 2)[None, :]
    ds = jnp.tile(jnp.concatenate([-jnp.ones(half), jnp.ones(half), z]), 2)[None, :].astype(jnp.float32)
    return rf, rs, df, ds


def _tab_call(pos_col):
    s = pos_col.shape[0]
    tm = min(1024, s)
    rf, rs, df, ds = _rope_rows()
    row = lambda w: pl.BlockSpec((1, w), lambda i: (0, 0))
    out = lambda w: pl.BlockSpec((tm, w), lambda i: (i, 0))
    return pl.pallas_call(
        _tab_kernel,
        grid=(s // tm,),
        in_specs=[pl.BlockSpec((tm, 1), lambda i: (i, 0)), row(256), row(256), row(128), row(128)],
        out_specs=[out(256), out(256), out(128), out(128)],
        out_shape=[jax.ShapeDtypeStruct((s, 256), jnp.float32), jax.ShapeDtypeStruct((s, 256), jnp.float32),
                   jax.ShapeDtypeStruct((s, 128), jnp.float32), jax.ShapeDtypeStruct((s, 128), jnp.float32)],
        compiler_params=_cparams(1),
        name="rope_tables",
    )(pos_col, rf, rs, df, ds)


def _swap_halves(x, half, period):
    w = x.shape[-1]
    lane = lax.broadcasted_iota(jnp.int32, x.shape, x.ndim - 1) & (period - 1)
    up = pltpu.roll(x, w - half, x.ndim - 1)
    dn = pltpu.roll(x, half, x.ndim - 1)
    return jnp.where(lane < half, up, dn)


def _rope(x, cos, sin_signed, half, period):
    return x * cos + _swap_halves(x, half, period) * sin_signed


def _proj_kernel(x_ref, pre_ref, sc_ref, sh_ref, w_ref, o_ref):
    x = x_ref[...]
    xn = x * lax.rsqrt(jnp.mean(x * x, axis=-1, keepdims=True) + RMS_EPS)
    h = xn * pre_ref[...] * (1.0 + sc_ref[...]) + sh_ref[...]
    o_ref[...] = _dot(_bf(h), w_ref[...])


def _proj_call(x2, pre, scale, shift, w_pack):
    s, d = x2.shape
    tm, tn = min(512, s), 2048
    vec = pl.BlockSpec((1, d), lambda j, i: (0, 0))
    return pl.pallas_call(
        _proj_kernel,
        grid=(P_WIDTH // tn, s // tm),
        in_specs=[pl.BlockSpec((tm, d), lambda j, i: (i, 0)), vec, vec, vec,
                  pl.BlockSpec((d, tn), lambda j, i: (0, j))],
        out_specs=pl.BlockSpec((tm, tn), lambda j, i: (i, j)),
        out_shape=jax.ShapeDtypeStruct((s, P_WIDTH), jnp.float32),
        compiler_params=_cparams(2),
        name="in_proj",
    )(x2, pre, scale, shift, w_pack)


def _pspec(tm, name, width):
    blk = PCOL[name] // width
    assert PCOL[name] % width == 0
    return pl.BlockSpec((tm, width), lambda i: (i, blk))


def _ret_log_g(h):
    return math.log1p(-(2.0 ** (-5.0 - h)))


def _ret_kernel(q_ref, k_ref, v_ref, g_ref, cos_ref, sin_ref, o_ref,
                state_ref, decay_ref, qdec_ref, kend_ref, *, chunks):
    C = RET_CHUNK

    @pl.when(pl.program_id(0) == 0)
    def _():
        state_ref[...] = jnp.zeros_like(state_ref)
        ii = lax.broadcasted_iota(jnp.int32, (C, C), 0)
        jj = lax.broadcasted_iota(jnp.int32, (C, C), 1)
        rel = (ii - jj).astype(jnp.float32)
        row = lax.broadcasted_iota(jnp.int32, (C, RET_DK), 0).astype(jnp.float32)
        for h in range(RET_HEADS):
            lg = _ret_log_g(h)
            decay_ref[h] = jnp.where(rel >= 0, jnp.exp(lg * jnp.maximum(rel, 0.0)), 0.0)
            qdec_ref[:, h * RET_DK:(h + 1) * RET_DK] = jnp.exp((row + 1.0) * lg)
            kend_ref[:, h * RET_DK:(h + 1) * RET_DK] = jnp.exp((C - 1.0 - row) * lg)

    for c in range(chunks):
        rows = slice(c * C, (c + 1) * C)
        cos, sin = cos_ref[rows, :], sin_ref[rows, :]
        q = _rope(q_ref[rows, :], cos, sin, RET_DK // 2, RET_DK) * (RET_DK ** -0.5)
        k = _rope(k_ref[rows, :], cos, sin, RET_DK // 2, RET_DK)
        qd = _bf(q * qdec_ref[...])
        kd = _bf(k * kend_ref[...])
        qb, kb = _bf(q), _bf(k)
        v = v_ref[rows, :]
        g = g_ref[rows, :]
        for h in range(RET_HEADS):
            dk = slice(h * RET_DK, (h + 1) * RET_DK)
            dv = slice(h * RET_DV, (h + 1) * RET_DV)
            vh = _bf(v[:, dv])
            scores = _dot_nt(qb[:, dk], kb[:, dk]) * decay_ref[h]
            st = state_ref[h]
            o = _dot(_bf(scores), vh) + _dot(qd[:, dk], _bf(st))
            state_ref[h] = math.exp(C * _ret_log_g(h)) * st + _dot_tn(kd[:, dk], vh)
            o = o * lax.rsqrt(jnp.mean(o * o, axis=-1, keepdims=True) + RMS_EPS)
            o_ref[rows, dv] = _bf(o * _silu(g[:, dv]))


def _ret_call(p, rcos, rsin):
    s = p.shape[0]
    chunks = 2
    tm = RET_CHUNK * chunks
    tab = pl.BlockSpec((tm, 256), lambda i: (i, 0))
    return pl.pallas_call(
        functools.partial(_ret_kernel, chunks=chunks),
        grid=(s // tm,),
        in_specs=[_pspec(tm, "ret_q", 256), _pspec(tm, "ret_k", 256), _pspec(tm, "ret_v", 512),
                  _pspec(tm, "ret_g", 512), tab, tab],
        out_specs=pl.BlockSpec((tm, 512), lambda i: (i, 0)),
        out_shape=jax.ShapeDtypeStruct((s, 512), jnp.bfloat16),
        scratch_shapes=[pltpu.VMEM((RET_HEADS, RET_DK, RET_DV), jnp.float32),
                        pltpu.VMEM((RET_HEADS, RET_CHUNK, RET_CHUNK), jnp.float32),
                        pltpu.VMEM((RET_CHUNK, RET_HEADS * RET_DK), jnp.float32),
                        pltpu.VMEM((RET_CHUNK, RET_HEADS * RET_DK), jnp.float32)],
        compiler_params=_cparams(1),
        name="retention",
    )(p, p, p, p, rcos, rsin)


def _gla_kernel(q_ref, k_ref, v_ref, g_ref, a_ref, wlr_ref, blr_ref, o_ref, state_ref, *, chunks):
    C = GLA_CHUNK

    @pl.when(pl.program_id(0) == 0)
    def _():
        state_ref[...] = jnp.zeros_like(state_ref)

    ii = lax.broadcasted_iota(jnp.int32, (C, C), 0)
    jj = lax.broadcasted_iota(jnp.int32, (C, C), 1)
    causal = jj <= ii
    tril = _bf(jnp.where(causal, 1.0, 0.0))
    wlr = wlr_ref[...]
    w_hi, w_mid, w_lo = _split3(wlr)

    for c in range(chunks):
        rows = slice(c * C, (c + 1) * C)
        a_hi, a_mid, a_lo = _split3(a_ref[rows, :])
        z = (_dot(a_hi, w_hi) + (_dot(a_hi, w_mid) + _dot(a_mid, w_hi))
             + (_dot(a_hi, w_lo) + _dot(a_mid, w_mid) + _dot(a_lo, w_hi))) + blr_ref[...]
        log_a = (jnp.minimum(z, 0.0) - jnp.log1p(jnp.exp(-jnp.abs(z)))) * (1.0 / GLA_TAU)
        l_hi, l_mid, l_lo = _split3(log_a)
        bcum = _dot(tril, l_hi) + _dot(tril, l_mid) + _dot(tril, l_lo)
        b_mid = bcum[C // 2 - 1:C // 2, :]
        b_last = bcum[C - 1:C, :]
        q = q_ref[rows, :] * (GLA_DK ** -0.5)
        k = k_ref[rows, :]
        qt = _bf(q * jnp.exp(bcum - b_mid))
        kt = _bf(k * jnp.exp(b_mid - bcum))
        qg = _bf(q * jnp.exp(bcum))
        kd = _bf(k * jnp.exp(b_last - bcum))
        e_last = jnp.exp(b_last)
        v = v_ref[rows, :]
        g = g_ref[rows, :]
        for h in range(GLA_HEADS):
            dk = slice(h * GLA_DK, (h + 1) * GLA_DK)
            dv = slice(h * GLA_DV, (h + 1) * GLA_DV)
            vh = _bf(v[:, dv])
            attn = jnp.where(causal, _dot_nt(qt[:, dk], kt[:, dk]), 0.0)
            st = state_ref[h]
            o = _dot(_bf(attn), vh) + _dot_nt(qg[:, dk], _bf(st))
            state_ref[h] = e_last[:, dk] * st + _dot_tn(vh, kd[:, dk])
            o = o * lax.rsqrt(jnp.mean(o * o, axis=-1, keepdims=True) + RMS_EPS)
            o_ref[rows, dv] = _bf(o * _silu(g[:, dv]))


def _gla_call(p, wlr_pad, blr):
    s = p.shape[0]
    chunks = 4
    tm = GLA_CHUNK * chunks
    return pl.pallas_call(
        functools.partial(_gla_kernel, chunks=chunks),
        grid=(s // tm,),
        in_specs=[_pspec(tm, "gla_q", 256), _pspec(tm, "gla_k", 256), _pspec(tm, "gla_v", 512),
                  _pspec(tm, "gla_g", 512), _pspec(tm, "gla_a", 128),
                  pl.BlockSpec((128, 256), lambda i: (0, 0)), pl.BlockSpec((1, 256), lambda i: (0, 0))],
        out_specs=pl.BlockSpec((tm, 512), lambda i: (i, 0)),
        out_shape=jax.ShapeDtypeStruct((s, 512), jnp.bfloat16),
        scratch_shapes=[pltpu.VMEM((GLA_HEADS, GLA_DV, GLA_DK), jnp.float32)],
        compiler_params=_cparams(1),
        name="gla",
    )(p, p, p, p, p, wlr_pad, blr)


def _place_head(pair, src_pos, dst_pos):
    lane = lax.broadcasted_iota(jnp.int32, pair.shape, 1)
    if src_pos != dst_pos:
        pair = pltpu.roll(pair, 64, 1)
    keep = (lane < 64) if dst_pos == 0 else (lane >= 64)
    return jnp.where(keep, pair, 0.0)


def _dprep_kernel(q_ref, k_ref, v_ref, iq_ref, ikw_ref, cos_ref, sin_ref,
                  qp_o, k_o, v_o, iqp_o, ik_o, w_o):
    cos, sin = cos_ref[...], sin_ref[...]
    half = DSA_ROT // 2
    cos4 = jnp.concatenate([cos] * 4, axis=1)
    sin4 = jnp.concatenate([sin] * 4, axis=1)
    q = _rope(q_ref[...], cos4, sin4, half, DSA_DH) * (DSA_DH ** -0.5 * LOG2E)
    for h in range(DSA_HEADS):
        pair = q[:, (h // 2) * 128:(h // 2 + 1) * 128]
        qp_o[h] = _bf(_place_head(pair, h % 2, h // (DSA_HEADS // DSA_KV_HEADS)))
    k_o[...] = _bf(_rope(k_ref[...], cos, sin, half, DSA_DH))
    v = v_ref[...]
    lane = lax.broadcasted_iota(jnp.int32, v.shape, 1)
    v_o[0] = _bf(jnp.where(lane < DSA_DH, v, 1.0))
    v_o[1] = _bf(jnp.where(lane < DSA_DH, pltpu.roll(v, DSA_DH, 1), 1.0))
    iq = _rope(iq_ref[...], cos4[:, :256], sin4[:, :256], half, IDX_DH)
    for h in range(IDX_HEADS):
        pair = iq[:, (h // 2) * 128:(h // 2 + 1) * 128]
        iqp_o[h] = _bf(_place_head(pair, h % 2, 0))
    ikw = ikw_ref[...]
    lane = lax.broadcasted_iota(jnp.int32, ikw.shape, 1)
    ik_o[...] = _bf(jnp.where(lane < IDX_DH, _rope(ikw, cos, sin, half, DSA_DH), 0.0))
    wscale = (IDX_HEADS ** -0.5) * (IDX_DH ** -0.5)
    w_o[...] = pltpu.roll(ikw, 128 - IDX_DH, 1) * wscale


def _dprep_call(p, dcos, dsin):
    s = p.shape[0]
    tm = min(512, s)
    tab = pl.BlockSpec((tm, 128), lambda i: (i, 0))
    return pl.pallas_call(
        _dprep_kernel,
        grid=(s // tm,),
        in_specs=[_pspec(tm, "dsa_q", 512), _pspec(tm, "dsa_k", 128), _pspec(tm, "dsa_v", 128),
                  _pspec(tm, "idx_q", 256), _pspec(tm, "idx_kw", 128), tab, tab],
        out_specs=[pl.BlockSpec((DSA_HEADS, tm, 128), lambda i: (0, i, 0)), tab,
                   pl.BlockSpec((DSA_KV_HEADS, tm, 128), lambda i: (0, i, 0)),
                   pl.BlockSpec((IDX_HEADS, tm, 128), lambda i: (0, i, 0)), tab, tab],
        out_shape=[jax.ShapeDtypeStruct((DSA_HEADS, s, 128), jnp.bfloat16),
                   jax.ShapeDtypeStruct((s, 128), jnp.bfloat16),
                   jax.ShapeDtypeStruct((DSA_KV_HEADS, s, 128), jnp.bfloat16),
                   jax.ShapeDtypeStruct((IDX_HEADS, s, 128), jnp.bfloat16),
                   jax.ShapeDtypeStruct((s, 128), jnp.bfloat16),
                   jax.ShapeDtypeStruct((s, 128), jnp.float32)],
        compiler_params=_cparams(1),
        name="dsa_prep",
    )(p, p, p, p, p, dcos, dsin)


DSA_QB = 256
IDX_TK = 256
ATT_TK = 1024
SEL_ROWS = 128
SEL_LANES = 512
CAND_DEPTH = 10
CAND_SPLIT = 2
CAND_W = CAND_DEPTH * CAND_SPLIT * LANES
assert CAND_W % SEL_LANES == 0 and 8 % CAND_SPLIT == 0


def _key_to_f32(key):
    bits = jnp.where(key >= 0, key, (0 - key) | INT_MIN)
    return lax.bitcast_convert_type(bits, jnp.float32)


def _count_lanes(keys_ref, r0, n_steps, cand, idx_from=None):
    def body(s, acc):
        c0 = pl.multiple_of(s * SEL_LANES, SEL_LANES)
        for u in range(SEL_LANES // LANES):
            kk = keys_ref[pl.ds(r0, SEL_ROWS), pl.ds(c0 + u * LANES, LANES)]
            if idx_from is None:
                acc = acc + jnp.where(kk >= cand, 1.0, 0.0)
            else:
                col = lax.broadcasted_iota(jnp.int32, kk.shape, 1) + (c0 + u * LANES)
                acc = acc + jnp.where(kk == cand, jnp.where(col >= idx_from, 1.0, 0.0), 0.0)
        return acc
    return lax.fori_loop(0, n_steps, body, jnp.zeros((SEL_ROWS, LANES), jnp.float32))


def _row_total(lane_counts):
    ones = jnp.ones((LANES, LANES), jnp.bfloat16)
    return _dot(_bf(lane_counts), ones)


def _radix_search(src_ref, n_steps, lo_ref, cnt_ref, part_ref, topk):
    rows_total = lo_ref.shape[0]
    lo_ref[...] = jnp.full(lo_ref.shape, INT_MIN, jnp.int32)
    cnt_ref[...] = jnp.zeros(cnt_ref.shape, jnp.float32)

    def cond(c):
        b, pending = c
        return jnp.logical_and(b < 32, pending > 0)

    def body(c):
        b, _ = c
        bit = lax.shift_left(jnp.int32(1), 31 - b)
        for rb in range(rows_total // SEL_ROWS):
            rows = pl.ds(rb * SEL_ROWS, SEL_ROWS)
            part_ref[rows, :] = _count_lanes(src_ref, rb * SEL_ROWS, n_steps,
                                             _key_to_f32(lo_ref[rows, :] + bit))
        cnt = _row_total(part_ref[...])
        lo = lo_ref[...]
        ok = cnt >= float(topk)
        lo_ref[...] = jnp.where(ok, lo + bit, lo)
        cnt = jnp.where(ok, cnt, cnt_ref[...])
        cnt_ref[...] = cnt
        pending = (jnp.max(jnp.abs(cnt - float(topk))) > 0.0).astype(jnp.int32)
        return b + 1, pending
    lax.while_loop(cond, body, (jnp.int32(0), jnp.int32(1)))


def _lane_candidates(keys_ref, cand_ref, n_groups):
    n_stack = CAND_DEPTH * CAND_SPLIT

    def rg_body(rg, carry):
        r0 = pl.multiple_of(rg * 8, 8)

        def col_body(s, st):
            st = list(st)
            c0 = pl.multiple_of(s * 8 * LANES, 8 * LANES)
            for u in range(8):
                x = keys_ref[pl.ds(r0, 8), pl.ds(c0 + u * LANES, LANES)]
                base = (u % CAND_SPLIT) * CAND_DEPTH
                for d in range(CAND_DEPTH):
                    cur = st[base + d]
                    st[base + d] = jnp.maximum(cur, x)
                    x = jnp.minimum(cur, x)
            return tuple(st)
        init = tuple(jnp.full((8, LANES), -jnp.inf, jnp.float32) for _ in range(n_stack))
        st = lax.fori_loop(0, n_groups, col_body, init)
        for k in range(n_stack):
            cand_ref[pl.ds(r0, 8), k * LANES:(k + 1) * LANES] = st[k]
        return carry
    lax.fori_loop(0, cand_ref.shape[0] // 8, rg_body, 0)


def _dsa_kernel(qp_ref, iqp_ref, w_ref, g_ref, k_ref, va_ref, ik_ref, o_ref,
                keys_ref, cand_ref, lo_ref, thr_ref, cnt_ref, part_ref, m_ref, acc_ref, full_ref, *, topk):
    QB, TK = DSA_QB, IDX_TK
    G = DSA_HEADS // DSA_KV_HEADS
    i = pl.program_id(0)
    n_idx = i + 1
    n_att = (n_idx * IDX_TK + ATT_TK - 1) // ATT_TK
    n_sel = n_att * (ATT_TK // SEL_LANES)
    row_id = lax.broadcasted_iota(jnp.int32, (QB, TK), 0)
    col_id = lax.broadcasted_iota(jnp.int32, (QB, TK), 1)

    iq = iqp_ref[...].reshape(IDX_HEADS * QB, LANES)
    wb = [jnp.broadcast_to(w_ref[:, h:h + 1], (QB, TK)) for h in range(IDX_HEADS)]

    def score_tile(j, diag):
        c0 = pl.multiple_of(j * TK, TK)
        s_all = _dot_nt(iq, ik_ref[pl.ds(c0, TK), :])
        score = None
        for h in range(IDX_HEADS):
            term = wb[h] * jnp.maximum(s_all[h * QB:(h + 1) * QB], 0.0)
            score = term if score is None else score + term
        if diag:
            score = jnp.where(col_id <= row_id, score, -jnp.inf)
        keys_ref[:, pl.ds(c0, TK)] = score

    def score_body(j, carry):
        score_tile(j, False)
        return carry
    lax.fori_loop(0, i, score_body, 0)
    score_tile(i, True)

    def blank_body(j, carry):
        c0 = pl.multiple_of(j * TK, TK)
        keys_ref[:, pl.ds(c0, TK)] = jnp.full((QB, TK), -jnp.inf, jnp.float32)
        return carry
    lax.fori_loop(n_idx, n_att * (ATT_TK // TK), blank_body, 0)

    full_ref[0] = jnp.int32(1)

    @pl.when(n_att * ATT_TK > 2 * CAND_W)
    def _():
        _lane_candidates(keys_ref, cand_ref, n_att * (ATT_TK // (8 * LANES)))
        _radix_search(cand_ref, CAND_W // SEL_LANES, lo_ref, cnt_ref, part_ref, topk)
        lo = lo_ref[...]
        thr = _key_to_f32(jnp.maximum(lo, NEG_INF_KEY))
        lost = None
        for v in range(CAND_SPLIT):
            last = (v * CAND_DEPTH + CAND_DEPTH - 1) * LANES
            hit = jnp.where(cand_ref[:, last:last + LANES] >= thr, 1.0, 0.0)
            lost = hit if lost is None else jnp.maximum(lost, hit)
        full_ref[0] = (jnp.max(lost) > 0.0).astype(jnp.int32)

    @pl.when(full_ref[0] > 0)
    def _():
        _radix_search(keys_ref, n_sel, lo_ref, cnt_ref, part_ref, topk)

    tau = lo_ref[...]
    found = tau > NEG_INF_KEY
    excess = jnp.where(found, cnt_ref[...] - float(topk), 0.0)
    thr_ref[...] = jnp.where(found, _key_to_f32(jnp.maximum(tau, NEG_INF_KEY)), F32_LOWEST)

    @pl.when(jnp.max(excess) > 0.0)
    def _():
        cnt_ref[...] = excess
        for rb in range(QB // SEL_ROWS):
            r0 = rb * SEL_ROWS
            rows = pl.ds(r0, SEL_ROWS)
            tau_b = thr_ref[rows, :]
            exc = cnt_ref[rows, :]

            def idx_body(b, sig):
                cand = sig + lax.shift_left(jnp.int32(1), 14 - b)
                cnt = _row_total(_count_lanes(keys_ref, r0, n_sel, tau_b, idx_from=cand))
                return jnp.where(cnt >= exc, cand, sig)
            sig = lax.fori_loop(0, 15, idx_body, jnp.zeros((SEL_ROWS, LANES), jnp.int32))
            drop_from = jnp.where(exc > 0.0, sig, jnp.int32(2 ** 30))

            def fix_body(s, carry):
                c0 = pl.multiple_of(s * LANES, LANES)
                kk = keys_ref[rows, pl.ds(c0, LANES)]
                col = lax.broadcasted_iota(jnp.int32, kk.shape, 1) + c0
                dropped = jnp.where(col >= drop_from, -jnp.inf, kk)
                keys_ref[rows, pl.ds(c0, LANES)] = jnp.where(kk == tau_b, dropped, kk)
                return carry
            lax.fori_loop(0, n_sel * (SEL_LANES // LANES), fix_body, 0)

    m_ref[...] = jnp.full(m_ref.shape, NEG_INIT, jnp.float32)
    acc_ref[...] = jnp.zeros(acc_ref.shape, jnp.float32)
    n_grp = ATT_TK // LANES

    def attn_body(j, carry):
        c0 = pl.multiple_of(j * ATT_TK, ATT_TK)
        tau_b = thr_ref[...]
        bias = jnp.concatenate(
            [jnp.where(keys_ref[:, pl.ds(c0 + u * LANES, LANES)] >= tau_b, 0.0, NEG_MASK)
             for u in range(n_grp)], axis=1)
        kt = k_ref[pl.ds(c0, ATT_TK), :]
        for n in range(DSA_KV_HEADS):
            qn = qp_ref[n * G:(n + 1) * G].reshape(G * QB, LANES)
            logits = _dot_nt(qn, kt)
            ps = []
            for gq in range(G):
                h = n * G + gq
                lm = logits[gq * QB:(gq + 1) * QB] + bias
                tmax = lm[:, 0:LANES]
                for u in range(1, n_grp):
                    tmax = jnp.maximum(tmax, lm[:, u * LANES:(u + 1) * LANES])
                m_old = m_ref[h]
                m_new = jnp.maximum(m_old, jnp.max(tmax, axis=-1, keepdims=True))
                ps.append(jnp.concatenate(
                    [_bf(jnp.exp2(lm[:, u * LANES:(u + 1) * LANES] - m_new)) for u in range(n_grp)], axis=1))
                m_ref[h] = m_new
                acc_ref[h] = jnp.exp2(m_old - m_new) * acc_ref[h]
            pv = _dot(jnp.concatenate(ps, axis=0), va_ref[n, pl.ds(c0, ATT_TK), :])
            for gq in range(G):
                h = n * G + gq
                acc_ref[h] = acc_ref[h] + pv[gq * QB:(gq + 1) * QB]
        return carry
    lax.fori_loop(0, n_att, attn_body, 0)

    lane = lax.broadcasted_iota(jnp.int32, (QB, LANES), 1)
    g = g_ref[...]
    for pair in range(DSA_HEADS // 2):
        halves = []
        for pos in range(2):
            a = acc_ref[2 * pair + pos]
            halves.append(a * (1.0 / pltpu.roll(a, DSA_DH, 1)))
        o = jnp.where(lane < DSA_DH, halves[0], pltpu.roll(halves[1], DSA_DH, 1))
        sl = slice(pair * LANES, (pair + 1) * LANES)
        o_ref[:, sl] = _bf(o * _silu(g[:, sl]))


def _dsa_call(p, qp, kk, va, iqp, ik, w4):
    s = p.shape[0]
    topk = min(TOPK_MAX, s // 4)
    QB = DSA_QB
    assert IDX_TK == QB and s % ATT_TK == 0 and ATT_TK % SEL_LANES == 0 and s // LANES <= 256
    once = pl.Buffered(1)
    return pl.pallas_call(
        functools.partial(_dsa_kernel, topk=topk),
        grid=(s // QB,),
        in_specs=[pl.BlockSpec((DSA_HEADS, QB, LANES), lambda i: (0, i, 0)),
                  pl.BlockSpec((IDX_HEADS, QB, LANES), lambda i: (0, i, 0)),
                  pl.BlockSpec((QB, LANES), lambda i: (i, 0)),
                  _pspec(QB, "dsa_g", 512),
                  pl.BlockSpec((s, LANES), lambda i: (0, 0), pipeline_mode=once),
                  pl.BlockSpec((DSA_KV_HEADS, s, LANES), lambda i: (0, 0, 0), pipeline_mode=once),
                  pl.BlockSpec((s, LANES), lambda i: (0, 0), pipeline_mode=once)],
        out_specs=pl.BlockSpec((QB, 512), lambda i: (i, 0)),
        out_shape=jax.ShapeDtypeStruct((s, 512), jnp.bfloat16),
        scratch_shapes=[pltpu.VMEM((QB, s), jnp.float32),
                        pltpu.VMEM((QB, CAND_W), jnp.float32),
                        pltpu.VMEM((QB, LANES), jnp.int32),
                        pltpu.VMEM((QB, LANES), jnp.float32),
                        pltpu.VMEM((QB, LANES), jnp.float32),
                        pltpu.VMEM((QB, LANES), jnp.float32),
                        pltpu.VMEM((DSA_HEADS, QB, LANES), jnp.float32),
                        pltpu.VMEM((DSA_HEADS, QB, LANES), jnp.float32),
                        pltpu.SMEM((1,), jnp.int32)],
        compiler_params=_cparams(1, vmem_mb=56),
        name="dsa",
    )(qp, iqp, w4, p, kk, va, ik)


def _merge_kernel(x_ref, ret_ref, dsa_ref, gl_ref, m_ref, wr_ref, wd_ref, wg_ref, wo_ref,
                  post_ref, gate_ref, o_ref):
    d = D_MODEL
    y = (_sigmoid(m_ref[:, 0:d]) * _dot(ret_ref[...], wr_ref[...])
         + _sigmoid(m_ref[:, d:2 * d]) * _dot(dsa_ref[...], wd_ref[...])
         + _sigmoid(m_ref[:, 2 * d:3 * d]) * _dot(gl_ref[...], wg_ref[...]))
    y = _dot(_bf(y), wo_ref[...])
    yn = y * lax.rsqrt(jnp.mean(y * y, axis=-1, keepdims=True) + RMS_EPS) * post_ref[...]
    o_ref[...] = x_ref[...] + gate_ref[...] * yn


def _merge_call(x2, ret, dsa, gl, p, wr, wd, wg, wo, post, gate):
    s, d = x2.shape
    tm = min(512, s)
    rows = lambda w: pl.BlockSpec((tm, w), lambda i: (i, 0))
    whole = lambda a: pl.BlockSpec(a.shape, lambda i: (0, 0))
    return pl.pallas_call(
        _merge_kernel,
        grid=(s // tm,),
        in_specs=[rows(d), rows(512), rows(512), rows(512), _pspec(tm, "merge", 3072),
                  whole(wr), whole(wd), whole(wg), whole(wo), whole(post), whole(gate)],
        out_specs=rows(d),
        out_shape=jax.ShapeDtypeStruct((s, d), jnp.float32),
        compiler_params=_cparams(1),
        name="merge_out",
    )(x2, ret, dsa, gl, p, wr, wd, wg, wo, post, gate)


def _pack_w_in(w_in):
    depth, d, _ = w_in.shape
    zeros = lambda n: jnp.zeros((depth, d, n), w_in.dtype)
    src = lambda name: w_in[:, :, _SRC[name][0]:_SRC[name][0] + _SRC[name][1]]
    pieces, at = [], 0
    for name, width in _PACK:
        assert at == PCOL[name]
        if name == "idx_kw":
            cols = [src("idx_k"), src("idx_w"), zeros(width - IDX_DH - IDX_HEADS)]
        elif name == "gla_a":
            cols = [src("gla_a"), zeros(width - GLA_RANK)]
        else:
            cols = [src(name)]
        pieces += cols
        at += width
    pieces.append(zeros(P_WIDTH - at))
    return jnp.concatenate(pieces, axis=-1).astype(jnp.bfloat16)


def kernel(x, c, positions, ada_w, ada_b, pre_norm, post_norm, w_in, gla_w_lr, gla_b_lr,
           w_br_ret, w_br_dsa, w_br_gla, w_out):
    b, s, d = x.shape
    assert b == 1 and d == D_MODEL
    depth = ada_w.shape[0]
    x2 = x.reshape(s, d)
    mod = _mod_call(jnp.broadcast_to(c, (8, d)), ada_w, ada_b.reshape(depth, 1, 3 * d))[:, 0:1, :]
    rcos, rsin, dcos, dsin = _tab_call(positions.reshape(s, 1))
    w_pack = _pack_w_in(w_in)
    wlr_pad = jnp.pad(gla_w_lr, ((0, 0), (0, LANES - GLA_RANK), (0, 0)))
    for l in range(depth):
        shift, scale, gate = mod[l, :, 0:d], mod[l, :, d:2 * d], mod[l, :, 2 * d:3 * d]
        p = _proj_call(x2, pre_norm[l][None, :], scale, shift, w_pack[l])
        ret = _ret_call(p, rcos, rsin)
        gl = _gla_call(p, wlr_pad[l], gla_b_lr[l][None, :])
        qp, kk, va, iqp, ik, w4 = _dprep_call(p, dcos, dsin)
        dsa = _dsa_call(p, qp, kk, va, iqp, ik, w4)
        x2 = _merge_call(x2, ret, dsa, gl, p, _bf(w_br_ret[l]), _bf(w_br_dsa[l]), _bf(w_br_gla[l]),
                         _bf(w_out[l]), post_norm[l][None, :], gate)
    return x2.reshape(b, s, d)
```

```python
import functools
import math

import jax
import jax.numpy as jnp
from jax import lax
from jax.experimental import pallas as pl
from jax.experimental.pallas import tpu as pltpu

D_MODEL = 1024
DEPTH = 4
RET_HEADS, RET_DK, RET_DV, RET_CHUNK, RET_THETA = 4, 64, 128, 128, 10000.0
DSA_HEADS, DSA_KV_HEADS, DSA_DH = 8, 2, 64
DSA_ROT = DSA_DH // 4
ROPE_THETA = 500000.0
IDX_HEADS, IDX_DH = 4, 64
TOPK_MAX = 256
GLA_HEADS, GLA_DK, GLA_DV, GLA_RANK, GLA_TAU, GLA_CHUNK = 4, 64, 128, 16, 16.0, 64
RMS_EPS = 1e-6
LANES = 128

_SRC = {}
_off = 0
for _name, _w in (("ret_q", 256), ("ret_k", 256), ("ret_v", 512), ("ret_g", 512),
                  ("dsa_q", 512), ("dsa_k", 128), ("dsa_v", 128), ("dsa_g", 512),
                  ("idx_q", 256), ("idx_k", 64), ("idx_w", 4),
                  ("gla_q", 256), ("gla_k", 256), ("gla_v", 512), ("gla_g", 512), ("gla_a", 16),
                  ("merge", 3072)):
    _SRC[_name] = (_off, _w)
    _off += _w
IN_WIDTH = _off

_PACK = (("merge", 3072), ("ret_q", 256), ("ret_k", 256), ("ret_v", 512), ("ret_g", 512),
         ("dsa_q", 512), ("dsa_g", 512), ("gla_v", 512), ("gla_g", 512),
         ("gla_q", 256), ("gla_k", 256), ("idx_q", 256),
         ("dsa_k", 128), ("dsa_v", 128), ("idx_kw", 128), ("gla_a", 128))
PCOL = {}
_off = 0
for _name, _w in _PACK:
    assert _off % min(_w, 1024) == 0
    PCOL[_name] = _off
    _off += _w
P_WIDTH = 8192
assert _off <= P_WIDTH

LOG2E = math.log2(math.e)
INT_MIN = -(2 ** 31)
F32_LOWEST = -3.4028234663852886e38
NEG_INF_KEY = -0x7F800000
NEG_INIT = -1e30
NEG_MASK = -2e30


def _cparams(n_axes, vmem_mb=48):
    return pltpu.CompilerParams(dimension_semantics=("arbitrary",) * n_axes,
                                vmem_limit_bytes=vmem_mb * 1024 * 1024)


def _bf(x):
    return x.astype(jnp.bfloat16)


def _dot(a, b):
    return jnp.dot(a, b, preferred_element_type=jnp.float32)


def _dot_nt(a, b):
    return lax.dot_general(a, b, (((1,), (1,)), ((), ())), preferred_element_type=jnp.float32)


def _dot_tn(a, b):
    return lax.dot_general(a, b, (((0,), (0,)), ((), ())), preferred_element_type=jnp.float32)


def _split3(x):
    hi = _bf(x)
    r1 = x - hi.astype(jnp.float32)
    mid = _bf(r1)
    lo = _bf(r1 - mid.astype(jnp.float32))
    return hi, mid, lo


def _silu(x):
    return x * (1.0 / (1.0 + jnp.exp(-x)))


def _sigmoid(x):
    return 1.0 / (1.0 + jnp.exp(-x))


def _mod_kernel(c_ref, w_ref, b_ref, o_ref):
    c = c_ref[...]
    ca = _silu(c)
    acc = None
    for t in _split3(ca):
        for u in _split3(w_ref[0]):
            part = _dot(t, u)
            acc = part if acc is None else acc + part
    o_ref[0] = acc + b_ref[0]


def _mod_call(c8, ada_w, ada_b3):
    depth, d, n = ada_w.shape
    tn = 1024
    return pl.pallas_call(
        _mod_kernel,
        grid=(depth, n // tn),
        in_specs=[pl.BlockSpec((8, d), lambda l, j: (0, 0)),
                  pl.BlockSpec((1, d, tn), lambda l, j: (l, 0, j)),
                  pl.BlockSpec((1, 1, tn), lambda l, j: (l, 0, j))],
        out_specs=pl.BlockSpec((1, 8, tn), lambda l, j: (l, 0, j)),
        out_shape=jax.ShapeDtypeStruct((depth, 8, n), jnp.float32),
        compiler_params=_cparams(2),
        name="adaln_mod",
    )(c8, ada_w, ada_b3)


def _tab_kernel(pos_ref, rf_ref, rs_ref, df_ref, ds_ref, rc_o, rsn_o, dc_o, dsn_o):
    pos = pos_ref[...].astype(jnp.float32)
    ang = pos * rf_ref[...]
    rc_o[...] = jnp.cos(ang)
    rsn_o[...] = jnp.sin(ang) * rs_ref[...]
    ang = pos * df_ref[...]
    dc_o[...] = jnp.cos(ang)
    dsn_o[...] = jnp.sin(ang) * ds_ref[...]


def _rope_rows():
    half = RET_DK // 2
    f = RET_THETA ** (-jnp.arange(half, dtype=jnp.float32) * 2.0 / RET_DK)
    rf = jnp.tile(jnp.concatenate([f, f]), RET_HEADS)[None, :]
    rs = jnp.tile(jnp.concatenate([-jnp.ones(half), jnp.ones(half)]), RET_HEADS)[None, :].astype(jnp.float32)
    half = DSA_ROT // 2
    f = ROPE_THETA ** (-jnp.arange(half, dtype=jnp.float32) * 2.0 / DSA_ROT)
    z = jnp.zeros(DSA_DH - DSA_ROT, jnp.float32)
    df = jnp.tile(jnp.concatenate([f, f, z]), 2)[None, :]
    ds = jnp.tile(jnp.concatenate([-jnp.ones(half), jnp.ones(half), z]), 2)[None, :].astype(jnp.float32)
    return rf, rs, df, ds


def _tab_call(pos_col):
    s = pos_col.shape[0]
    tm = min(1024, s)
    rf, rs, df, ds = _rope_rows()
    row = lambda w: pl.BlockSpec((1, w), lambda i: (0, 0))
    out = lambda w: pl.BlockSpec((tm, w), lambda i: (i, 0))
    return pl.pallas_call(
        _tab_kernel,
        grid=(s // tm,),
        in_specs=[pl.BlockSpec((tm, 1), lambda i: (i, 0)), row(256), row(256), row(128), row(128)],
        out_specs=[out(256), out(256), out(128), out(128)],
        out_shape=[jax.ShapeDtypeStruct((s, 256), jnp.float32), jax.ShapeDtypeStruct((s, 256), jnp.float32),
                   jax.ShapeDtypeStruct((s, 128), jnp.float32), jax.ShapeDtypeStruct((s, 128), jnp.float32)],
        compiler_params=_cparams(1),
        name="rope_tables",
    )(pos_col, rf, rs, df, ds)


def _swap_halves(x, half, period):
    w = x.shape[-1]
    lane = lax.broadcasted_iota(jnp.int32, x.shape, x.ndim - 1) & (period - 1)
    up = pltpu.roll(x, w - half, x.ndim - 1)
    dn = pltpu.roll(x, half, x.ndim - 1)
    return jnp.where(lane < half, up, dn)


def _rope(x, cos, sin_signed, half, period):
    return x * cos + _swap_halves(x, half, period) * sin_signed


def _proj_kernel(x_ref, pre_ref, sc_ref, sh_ref, w_ref, o_ref):
    x = x_ref[...]
    xn = x * lax.rsqrt(jnp.mean(x * x, axis=-1, keepdims=True) + RMS_EPS)
    h = xn * pre_ref[...] * (1.0 + sc_ref[...]) + sh_ref[...]
    o_ref[...] = _dot(_bf(h), w_ref[...])


def _proj_call(x2, pre, scale, shift, w_pack):
    s, d = x2.shape
    tm, tn = min(512, s), 2048
    vec = pl.BlockSpec((1, d), lambda j, i: (0, 0))
    return pl.pallas_call(
        _proj_kernel,
        grid=(P_WIDTH // tn, s // tm),
        in_specs=[pl.BlockSpec((tm, d), lambda j, i: (i, 0)), vec, vec, vec,
                  pl.BlockSpec((d, tn), lambda j, i: (0, j))],
        out_specs=pl.BlockSpec((tm, tn), lambda j, i: (i, j)),
        out_shape=jax.ShapeDtypeStruct((s, P_WIDTH), jnp.float32),
        compiler_params=_cparams(2),
        name="in_proj",
    )(x2, pre, scale, shift, w_pack)


def _pspec(tm, name, width):
    blk = PCOL[name] // width
    assert PCOL[name] % width == 0
    return pl.BlockSpec((tm, width), lambda i: (i, blk))


def _ret_log_g(h):
    return math.log1p(-(2.0 ** (-5.0 - h)))


def _ret_kernel(q_ref, k_ref, v_ref, g_ref, cos_ref, sin_ref, o_ref,
                state_ref, decay_ref, qdec_ref, kend_ref, *, chunks):
    C = RET_CHUNK

    @pl.when(pl.program_id(0) == 0)
    def _():
        state_ref[...] = jnp.zeros_like(state_ref)
        ii = lax.broadcasted_iota(jnp.int32, (C, C), 0)
        jj = lax.broadcasted_iota(jnp.int32, (C, C), 1)
        rel = (ii - jj).astype(jnp.float32)
        row = lax.broadcasted_iota(jnp.int32, (C, RET_DK), 0).astype(jnp.float32)
        for h in range(RET_HEADS):
            lg = _ret_log_g(h)
            decay_ref[h] = jnp.where(rel >= 0, jnp.exp(lg * jnp.maximum(rel, 0.0)), 0.0)
            qdec_ref[:, h * RET_DK:(h + 1) * RET_DK] = jnp.exp((row + 1.0) * lg)
            kend_ref[:, h * RET_DK:(h + 1) * RET_DK] = jnp.exp((C - 1.0 - row) * lg)

    for c in range(chunks):
        rows = slice(c * C, (c + 1) * C)
        cos, sin = cos_ref[rows, :], sin_ref[rows, :]
        q = _rope(q_ref[rows, :], cos, sin, RET_DK // 2, RET_DK) * (RET_DK ** -0.5)
        k = _rope(k_ref[rows, :], cos, sin, RET_DK // 2, RET_DK)
        qd = _bf(q * qdec_ref[...])
        kd = _bf(k * kend_ref[...])
        qb, kb = _bf(q), _bf(k)
        v = v_ref[rows, :]
        g = g_ref[rows, :]
        for h in range(RET_HEADS):
            dk = slice(h * RET_DK, (h + 1) * RET_DK)
            dv = slice(h * RET_DV, (h + 1) * RET_DV)
            vh = _bf(v[:, dv])
            scores = _dot_nt(qb[:, dk], kb[:, dk]) * decay_ref[h]
            st = state_ref[h]
            o = _dot(_bf(scores), vh) + _dot(qd[:, dk], _bf(st))
            state_ref[h] = math.exp(C * _ret_log_g(h)) * st + _dot_tn(kd[:, dk], vh)
            o = o * lax.rsqrt(jnp.mean(o * o, axis=-1, keepdims=True) + RMS_EPS)
            o_ref[rows, dv] = _bf(o * _silu(g[:, dv]))


def _ret_call(p, rcos, rsin):
    s = p.shape[0]
    chunks = 2
    tm = RET_CHUNK * chunks
    tab = pl.BlockSpec((tm, 256), lambda i: (i, 0))
    return pl.pallas_call(
        functools.partial(_ret_kernel, chunks=chunks),
        grid=(s // tm,),
        in_specs=[_pspec(tm, "ret_q", 256), _pspec(tm, "ret_k", 256), _pspec(tm, "ret_v", 512),
                  _pspec(tm, "ret_g", 512), tab, tab],
        out_specs=pl.BlockSpec((tm, 512), lambda i: (i, 0)),
        out_shape=jax.ShapeDtypeStruct((s, 512), jnp.bfloat16),
        scratch_shapes=[pltpu.VMEM((RET_HEADS, RET_DK, RET_DV), jnp.float32),
                        pltpu.VMEM((RET_HEADS, RET_CHUNK, RET_CHUNK), jnp.float32),
                        pltpu.VMEM((RET_CHUNK, RET_HEADS * RET_DK), jnp.float32),
                        pltpu.VMEM((RET_CHUNK, RET_HEADS * RET_DK), jnp.float32)],
        compiler_params=_cparams(1),
        name="retention",
    )(p, p, p, p, rcos, rsin)


def _gla_kernel(q_ref, k_ref, v_ref, g_ref, a_ref, wlr_ref, blr_ref, o_ref, state_ref, *, chunks):
    C = GLA_CHUNK

    @pl.when(pl.program_id(0) == 0)
    def _():
        state_ref[...] = jnp.zeros_like(state_ref)

    ii = lax.broadcasted_iota(jnp.int32, (C, C), 0)
    jj = lax.broadcasted_iota(jnp.int32, (C, C), 1)
    causal = jj <= ii
    tril = _bf(jnp.where(causal, 1.0, 0.0))
    wlr = wlr_ref[...]
    w_hi, w_mid, w_lo = _split3(wlr)

    for c in range(chunks):
        rows = slice(c * C, (c + 1) * C)
        a_hi, a_mid, a_lo = _split3(a_ref[rows, :])
        z = (_dot(a_hi, w_hi) + (_dot(a_hi, w_mid) + _dot(a_mid, w_hi))
             + (_dot(a_hi, w_lo) + _dot(a_mid, w_mid) + _dot(a_lo, w_hi))) + blr_ref[...]
        log_a = (jnp.minimum(z, 0.0) - jnp.log1p(jnp.exp(-jnp.abs(z)))) * (1.0 / GLA_TAU)
        l_hi, l_mid, l_lo = _split3(log_a)
        bcum = _dot(tril, l_hi) + _dot(tril, l_mid) + _dot(tril, l_lo)
        b_mid = bcum[C // 2 - 1:C // 2, :]
        b_last = bcum[C - 1:C, :]
        q = q_ref[rows, :] * (GLA_DK ** -0.5)
        k = k_ref[rows, :]
        qt = _bf(q * jnp.exp(bcum - b_mid))
        kt = _bf(k * jnp.exp(b_mid - bcum))
        qg = _bf(q * jnp.exp(bcum))
        kd = _bf(k * jnp.exp(b_last - bcum))
        e_last = jnp.exp(b_last)
        v = v_ref[rows, :]
        g = g_ref[rows, :]
        for h in range(GLA_HEADS):
            dk = slice(h * GLA_DK, (h + 1) * GLA_DK)
            dv = slice(h * GLA_DV, (h + 1) * GLA_DV)
            vh = _bf(v[:, dv])
            attn = jnp.where(causal, _dot_nt(qt[:, dk], kt[:, dk]), 0.0)
            st = state_ref[h]
            o = _dot(_bf(attn), vh) + _dot_nt(qg[:, dk], _bf(st))
            state_ref[h] = e_last[:, dk] * st + _dot_tn(vh, kd[:, dk])
            o = o * lax.rsqrt(jnp.mean(o * o, axis=-1, keepdims=True) + RMS_EPS)
            o_ref[rows, dv] = _bf(o * _silu(g[:, dv]))


def _gla_call(p, wlr_pad, blr):
    s = p.shape[0]
    chunks = 4
    tm = GLA_CHUNK * chunks
    return pl.pallas_call(
        functools.partial(_gla_kernel, chunks=chunks),
        grid=(s // tm,),
        in_specs=[_pspec(tm, "gla_q", 256), _pspec(tm, "gla_k", 256), _pspec(tm, "gla_v", 512),
                  _pspec(tm, "gla_g", 512), _pspec(tm, "gla_a", 128),
                  pl.BlockSpec((128, 256), lambda i: (0, 0)), pl.BlockSpec((1, 256), lambda i: (0, 0))],
        out_specs=pl.BlockSpec((tm, 512), lambda i: (i, 0)),
        out_shape=jax.ShapeDtypeStruct((s, 512), jnp.bfloat16),
        scratch_shapes=[pltpu.VMEM((GLA_HEADS, GLA_DV, GLA_DK), jnp.float32)],
        compiler_params=_cparams(1),
        name="gla",
    )(p, p, p, p, p, wlr_pad, blr)


def _place_head(pair, src_pos, dst_pos):
    lane = lax.broadcasted_iota(jnp.int32, pair.shape, 1)
    if src_pos != dst_pos:
        pair = pltpu.roll(pair, 64, 1)
    keep = (lane < 64) if dst_pos == 0 else (lane >= 64)
    return jnp.where(keep, pair, 0.0)


def _dprep_kernel(q_ref, k_ref, v_ref, iq_ref, ikw_ref, cos_ref, sin_ref,
                  qp_o, k_o, v_o, iqp_o, ik_o, w_o):
    cos, sin = cos_ref[...], sin_ref[...]
    half = DSA_ROT // 2
    cos4 = jnp.concatenate([cos] * 4, axis=1)
    sin4 = jnp.concatenate([sin] * 4, axis=1)
    q = _rope(q_ref[...], cos4, sin4, half, DSA_DH) * (DSA_DH ** -0.5 * LOG2E)
    for h in range(DSA_HEADS):
        pair = q[:, (h // 2) * 128:(h // 2 + 1) * 128]
        qp_o[h] = _bf(_place_head(pair, h % 2, h // (DSA_HEADS // DSA_KV_HEADS)))
    k_o[...] = _bf(_rope(k_ref[...], cos, sin, half, DSA_DH))
    v = v_ref[...]
    lane = lax.broadcasted_iota(jnp.int32, v.shape, 1)
    v_o[0] = _bf(jnp.where(lane < DSA_DH, v, 1.0))
    v_o[1] = _bf(jnp.where(lane < DSA_DH, pltpu.roll(v, DSA_DH, 1), 1.0))
    iq = _rope(iq_ref[...], cos4[:, :256], sin4[:, :256], half, IDX_DH)
    for h in range(IDX_HEADS):
        pair = iq[:, (h // 2) * 128:(h // 2 + 1) * 128]
        iqp_o[h] = _bf(_place_head(pair, h % 2, 0))
    ikw = ikw_ref[...]
    lane = lax.broadcasted_iota(jnp.int32, ikw.shape, 1)
    ik_o[...] = _bf(jnp.where(lane < IDX_DH, _rope(ikw, cos, sin, half, DSA_DH), 0.0))
    wscale = (IDX_HEADS ** -0.5) * (IDX_DH ** -0.5)
    w_o[...] = pltpu.roll(ikw, 128 - IDX_DH, 1) * wscale


def _dprep_call(p, dcos, dsin):
    s = p.shape[0]
    tm = min(512, s)
    tab = pl.BlockSpec((tm, 128), lambda i: (i, 0))
    return pl.pallas_call(
        _dprep_kernel,
        grid=(s // tm,),
        in_specs=[_pspec(tm, "dsa_q", 512), _pspec(tm, "dsa_k", 128), _pspec(tm, "dsa_v", 128),
                  _pspec(tm, "idx_q", 256), _pspec(tm, "idx_kw", 128), tab, tab],
        out_specs=[pl.BlockSpec((DSA_HEADS, tm, 128), lambda i: (0, i, 0)), tab,
                   pl.BlockSpec((DSA_KV_HEADS, tm, 128), lambda i: (0, i, 0)),
                   pl.BlockSpec((IDX_HEADS, tm, 128), lambda i: (0, i, 0)), tab, tab],
        out_shape=[jax.ShapeDtypeStruct((DSA_HEADS, s, 128), jnp.bfloat16),
                   jax.ShapeDtypeStruct((s, 128), jnp.bfloat16),
                   jax.ShapeDtypeStruct((DSA_KV_HEADS, s, 128), jnp.bfloat16),
                   jax.ShapeDtypeStruct((IDX_HEADS, s, 128), jnp.bfloat16),
                   jax.ShapeDtypeStruct((s, 128), jnp.bfloat16),
                   jax.ShapeDtypeStruct((s, 128), jnp.float32)],
        compiler_params=_cparams(1),
        name="dsa_prep",
    )(p, p, p, p, p, dcos, dsin)


DSA_QB = 256
IDX_TK = 256
ATT_TK = 1024
SEL_ROWS = 128
SEL_LANES = 512
CAND_DEPTH = 10
CAND_SPLIT = 2
CAND_W = CAND_DEPTH * CAND_SPLIT * LANES
TIE_CHUNK = 4
NO_TIE_LIMIT = 1e9
assert CAND_W % SEL_LANES == 0 and 8 % CAND_SPLIT == 0


def _key_to_f32(key):
    bits = jnp.where(key >= 0, key, (0 - key) | INT_MIN)
    return lax.bitcast_convert_type(bits, jnp.float32)


def _count_lanes(keys_ref, r0, n_steps, cand, strict=False):
    def body(s, acc):
        c0 = pl.multiple_of(s * SEL_LANES, SEL_LANES)
        for u in range(SEL_LANES // LANES):
            kk = keys_ref[pl.ds(r0, SEL_ROWS), pl.ds(c0 + u * LANES, LANES)]
            acc = acc + jnp.where((kk > cand) if strict else (kk >= cand), 1.0, 0.0)
        return acc
    return lax.fori_loop(0, n_steps, body, jnp.zeros((SEL_ROWS, LANES), jnp.float32))


def _row_total(lane_counts):
    ones = jnp.ones((LANES, LANES), jnp.bfloat16)
    return _dot(_bf(lane_counts), ones)


def _radix_search(src_ref, n_steps, lo_ref, cnt_ref, part_ref, topk):
    rows_total = lo_ref.shape[0]
    lo_ref[...] = jnp.full(lo_ref.shape, INT_MIN, jnp.int32)
    cnt_ref[...] = jnp.zeros(cnt_ref.shape, jnp.float32)

    def cond(c):
        b, pending = c
        return jnp.logical_and(b < 32, pending > 0)

    def body(c):
        b, _ = c
        bit = lax.shift_left(jnp.int32(1), 31 - b)
        for rb in range(rows_total // SEL_ROWS):
            rows = pl.ds(rb * SEL_ROWS, SEL_ROWS)
            part_ref[rows, :] = _count_lanes(src_ref, rb * SEL_ROWS, n_steps,
                                             _key_to_f32(lo_ref[rows, :] + bit))
        cnt = _row_total(part_ref[...])
        lo = lo_ref[...]
        ok = cnt >= float(topk)
        lo_ref[...] = jnp.where(ok, lo + bit, lo)
        cnt = jnp.where(ok, cnt, cnt_ref[...])
        cnt_ref[...] = cnt
        pending = (jnp.max(jnp.abs(cnt - float(topk))) > 0.0).astype(jnp.int32)
        return b + 1, pending
    lax.while_loop(cond, body, (jnp.int32(0), jnp.int32(1)))


def _lane_candidates(keys_ref, cand_ref, n_groups):
    n_stack = CAND_DEPTH * CAND_SPLIT

    def rg_body(rg, carry):
        r0 = pl.multiple_of(rg * 8, 8)

        def col_body(s, st):
            st = list(st)
            c0 = pl.multiple_of(s * 8 * LANES, 8 * LANES)
            for u in range(8):
                x = keys_ref[pl.ds(r0, 8), pl.ds(c0 + u * LANES, LANES)]
                base = (u % CAND_SPLIT) * CAND_DEPTH
                for d in range(CAND_DEPTH):
                    cur = st[base + d]
                    st[base + d] = jnp.maximum(cur, x)
                    x = jnp.minimum(cur, x)
            return tuple(st)
        init = tuple(jnp.full((8, LANES), -jnp.inf, jnp.float32) for _ in range(n_stack))
        st = lax.fori_loop(0, n_groups, col_body, init)
        for k in range(n_stack):
            cand_ref[pl.ds(r0, 8), k * LANES:(k + 1) * LANES] = st[k]
        return carry
    lax.fori_loop(0, cand_ref.shape[0] // 8, rg_body, 0)


def _ties_to_keep(src_ref, n_steps, lo_ref, cnt_ref, part_ref, need_ref, topk):
    rows_total = lo_ref.shape[0]
    surplus = jnp.where(lo_ref[...] > NEG_INF_KEY, cnt_ref[...] - float(topk), 0.0)
    need_ref[...] = jnp.full(need_ref.shape, NO_TIE_LIMIT, jnp.float32)

    @pl.when(jnp.max(surplus) > 0.0)
    def _():
        for rb in range(rows_total // SEL_ROWS):
            rows = pl.ds(rb * SEL_ROWS, SEL_ROWS)
            thr = _key_to_f32(jnp.maximum(lo_ref[rows, :], NEG_INF_KEY))
            part_ref[rows, :] = _count_lanes(src_ref, rb * SEL_ROWS, n_steps, thr, strict=True)
        above = _row_total(part_ref[...])
        need_ref[...] = jnp.where(surplus > 0.0, float(topk) - above, NO_TIE_LIMIT)


def _tie_index_bound(keys_ref, thr_ref, need_ref, seen_ref, sig_ref, n_chunks):
    kk_i = lax.broadcasted_iota(jnp.int32, (LANES, LANES), 0)
    jj_i = lax.broadcasted_iota(jnp.int32, (LANES, LANES), 1)
    tri = _bf(jnp.where(kk_i <= jj_i, 1.0, 0.0))
    seen_ref[...] = jnp.zeros(seen_ref.shape, jnp.float32)

    def cond(c):
        j, pending = c
        return jnp.logical_and(j < n_chunks, pending > 0)

    def body(c):
        j, _ = c
        thr, need = thr_ref[...], need_ref[...]
        seen, sig = seen_ref[...], sig_ref[...]
        for u in range(TIE_CHUNK):
            c0 = pl.multiple_of((j * TIE_CHUNK + u) * LANES, LANES)
            tie = jnp.where(keys_ref[:, pl.ds(c0, LANES)] == thr, 1.0, 0.0)
            rank = seen + _dot(_bf(tie), tri)
            after = seen + jnp.sum(tie, axis=-1, keepdims=True)
            before = jnp.sum(jnp.where(rank < need, 1.0, 0.0), axis=-1, keepdims=True)
            here = jnp.where(seen < need, jnp.where(after >= need, 1.0, 0.0), 0.0)
            sig = jnp.where(here > 0.0, c0.astype(jnp.float32) + before, sig)
            seen = after
        seen_ref[...] = seen
        sig_ref[...] = sig
        waiting = jnp.where(need < NO_TIE_LIMIT, jnp.where(seen < need, 1.0, 0.0), 0.0)
        return j + 1, (jnp.max(waiting) > 0.0).astype(jnp.int32)
    lax.while_loop(cond, body, (jnp.int32(0), jnp.int32(1)))


def _dsa_kernel(qp_ref, iqp_ref, w_ref, g_ref, k_ref, va_ref, ik_ref, o_ref,
                keys_ref, cand_ref, lo_ref, thr_ref, cnt_ref, part_ref, need_ref, sig_ref,
                m_ref, acc_ref, full_ref, *, topk):
    QB, TK = DSA_QB, IDX_TK
    G = DSA_HEADS // DSA_KV_HEADS
    i = pl.program_id(0)
    n_idx = i + 1
    n_att = (n_idx * IDX_TK + ATT_TK - 1) // ATT_TK
    n_sel = n_att * (ATT_TK // SEL_LANES)
    row_id = lax.broadcasted_iota(jnp.int32, (QB, TK), 0)
    col_id = lax.broadcasted_iota(jnp.int32, (QB, TK), 1)

    iq = iqp_ref[...].reshape(IDX_HEADS * QB, LANES)
    wb = [jnp.broadcast_to(w_ref[:, h:h + 1], (QB, TK)) for h in range(IDX_HEADS)]

    def score_tile(j, diag):
        c0 = pl.multiple_of(j * TK, TK)
        s_all = _dot_nt(iq, ik_ref[pl.ds(c0, TK), :])
        score = None
        for h in range(IDX_HEADS):
            term = wb[h] * jnp.maximum(s_all[h * QB:(h + 1) * QB], 0.0)
            score = term if score is None else score + term
        if diag:
            score = jnp.where(col_id <= row_id, score, -jnp.inf)
        keys_ref[:, pl.ds(c0, TK)] = score

    def score_body(j, carry):
        score_tile(j, False)
        return carry
    lax.fori_loop(0, i, score_body, 0)
    score_tile(i, True)

    def blank_body(j, carry):
        c0 = pl.multiple_of(j * TK, TK)
        keys_ref[:, pl.ds(c0, TK)] = jnp.full((QB, TK), -jnp.inf, jnp.float32)
        return carry
    lax.fori_loop(n_idx, n_att * (ATT_TK // TK), blank_body, 0)

    full_ref[0] = jnp.int32(1)

    @pl.when(n_att * ATT_TK > 2 * CAND_W)
    def _():
        _lane_candidates(keys_ref, cand_ref, n_att * (ATT_TK // (8 * LANES)))
        _radix_search(cand_ref, CAND_W // SEL_LANES, lo_ref, cnt_ref, part_ref, topk)
        thr = _key_to_f32(jnp.maximum(lo_ref[...], NEG_INF_KEY))
        exact = cnt_ref[...] == float(topk)
        lost = None
        for v in range(CAND_SPLIT):
            last = cand_ref[:, (v * CAND_DEPTH + CAND_DEPTH - 1) * LANES:(v * CAND_DEPTH + CAND_DEPTH) * LANES]
            hit = jnp.where(last > thr, 1.0, jnp.where(last == thr, jnp.where(exact, 1.0, 0.0), 0.0))
            lost = hit if lost is None else jnp.maximum(lost, hit)
        full_ref[0] = (jnp.max(lost) > 0.0).astype(jnp.int32)
        _ties_to_keep(cand_ref, CAND_W // SEL_LANES, lo_ref, cnt_ref, part_ref, need_ref, topk)

    @pl.when(full_ref[0] > 0)
    def _():
        _radix_search(keys_ref, n_sel, lo_ref, cnt_ref, part_ref, topk)
        _ties_to_keep(keys_ref, n_sel, lo_ref, cnt_ref, part_ref, need_ref, topk)

    tau = lo_ref[...]
    found = tau > NEG_INF_KEY
    thr_ref[...] = jnp.where(found, _key_to_f32(jnp.maximum(tau, NEG_INF_KEY)), F32_LOWEST)
    sig_ref[...] = jnp.full(sig_ref.shape, NO_TIE_LIMIT, jnp.float32)

    @pl.when(jnp.min(need_ref[...]) < NO_TIE_LIMIT)
    def _():
        _tie_index_bound(keys_ref, thr_ref, need_ref, part_ref, sig_ref, n_att * (ATT_TK // (TIE_CHUNK * LANES)))

    m_ref[...] = jnp.full(m_ref.shape, NEG_INIT, jnp.float32)
    acc_ref[...] = jnp.zeros(acc_ref.shape, jnp.float32)
    n_grp = ATT_TK // LANES
    lane_f = lax.broadcasted_iota(jnp.int32, (QB, LANES), 1).astype(jnp.float32)

    def attn_body(j, carry):
        c0 = pl.multiple_of(j * ATT_TK, ATT_TK)
        tau_b = thr_ref[...]
        sig_rel = sig_ref[...] - c0.astype(jnp.float32)
        parts = []
        for u in range(n_grp):
            kk = keys_ref[:, pl.ds(c0 + u * LANES, LANES)]
            tie_bias = jnp.where(lane_f + float(u * LANES) <= sig_rel, 0.0, NEG_MASK)
            parts.append(jnp.where(kk > tau_b, 0.0, jnp.where(kk == tau_b, tie_bias, NEG_MASK)))
        bias = jnp.concatenate(parts, axis=1)
        kt = k_ref[pl.ds(c0, ATT_TK), :]
        for n in range(DSA_KV_HEADS):
            qn = qp_ref[n * G:(n + 1) * G].reshape(G * QB, LANES)
            logits = _dot_nt(qn, kt)
            ps = []
            for gq in range(G):
                h = n * G + gq
                lm = logits[gq * QB:(gq + 1) * QB] + bias
                tmax = lm[:, 0:LANES]
                for u in range(1, n_grp):
                    tmax = jnp.maximum(tmax, lm[:, u * LANES:(u + 1) * LANES])
                m_old = m_ref[h]
                m_new = jnp.maximum(m_old, jnp.max(tmax, axis=-1, keepdims=True))
                ps.append(jnp.concatenate(
                    [_bf(jnp.exp2(lm[:, u * LANES:(u + 1) * LANES] - m_new)) for u in range(n_grp)], axis=1))
                m_ref[h] = m_new
                acc_ref[h] = jnp.exp2(m_old - m_new) * acc_ref[h]
            pv = _dot(jnp.concatenate(ps, axis=0), va_ref[n, pl.ds(c0, ATT_TK), :])
            for gq in range(G):
                h = n * G + gq
                acc_ref[h] = acc_ref[h] + pv[gq * QB:(gq + 1) * QB]
        return carry
    lax.fori_loop(0, n_att, attn_body, 0)

    lane = lax.broadcasted_iota(jnp.int32, (QB, LANES), 1)
    g = g_ref[...]
    for pair in range(DSA_HEADS // 2):
        halves = []
        for pos in range(2):
            a = acc_ref[2 * pair + pos]
            halves.append(a * (1.0 / pltpu.roll(a, DSA_DH, 1)))
        o = jnp.where(lane < DSA_DH, halves[0], pltpu.roll(halves[1], DSA_DH, 1))
        sl = slice(pair * LANES, (pair + 1) * LANES)
        o_ref[:, sl] = _bf(o * _silu(g[:, sl]))


def _dsa_call(p, qp, kk, va, iqp, ik, w4):
    s = p.shape[0]
    topk = min(TOPK_MAX, s // 4)
    QB = DSA_QB
    assert IDX_TK == QB and s % ATT_TK == 0 and ATT_TK % SEL_LANES == 0 and s // LANES <= 256
    once = pl.Buffered(1)
    return pl.pallas_call(
        functools.partial(_dsa_kernel, topk=topk),
        grid=(s // QB,),
        in_specs=[pl.BlockSpec((DSA_HEADS, QB, LANES), lambda i: (0, i, 0)),
                  pl.BlockSpec((IDX_HEADS, QB, LANES), lambda i: (0, i, 0)),
                  pl.BlockSpec((QB, LANES), lambda i: (i, 0)),
                  _pspec(QB, "dsa_g", 512),
                  pl.BlockSpec((s, LANES), lambda i: (0, 0), pipeline_mode=once),
                  pl.BlockSpec((DSA_KV_HEADS, s, LANES), lambda i: (0, 0, 0), pipeline_mode=once),
                  pl.BlockSpec((s, LANES), lambda i: (0, 0), pipeline_mode=once)],
        out_specs=pl.BlockSpec((QB, 512), lambda i: (i, 0)),
        out_shape=jax.ShapeDtypeStruct((s, 512), jnp.bfloat16),
        scratch_shapes=[pltpu.VMEM((QB, s), jnp.float32),
                        pltpu.VMEM((QB, CAND_W), jnp.float32),
                        pltpu.VMEM((QB, LANES), jnp.int32),
                        pltpu.VMEM((QB, LANES), jnp.float32),
                        pltpu.VMEM((QB, LANES), jnp.float32),
                        pltpu.VMEM((QB, LANES), jnp.float32),
                        pltpu.VMEM((QB, LANES), jnp.float32),
                        pltpu.VMEM((QB, LANES), jnp.float32),
                        pltpu.VMEM((DSA_HEADS, QB, LANES), jnp.float32),
                        pltpu.VMEM((DSA_HEADS, QB, LANES), jnp.float32),
                        pltpu.SMEM((1,), jnp.int32)],
        compiler_params=_cparams(1, vmem_mb=56),
        name="dsa",
    )(qp, iqp, w4, p, kk, va, ik)


def _merge_kernel(x_ref, ret_ref, dsa_ref, gl_ref, m_ref, wr_ref, wd_ref, wg_ref, wo_ref,
                  post_ref, gate_ref, o_ref):
    d = D_MODEL
    y = (_sigmoid(m_ref[:, 0:d]) * _dot(ret_ref[...], wr_ref[...])
         + _sigmoid(m_ref[:, d:2 * d]) * _dot(dsa_ref[...], wd_ref[...])
         + _sigmoid(m_ref[:, 2 * d:3 * d]) * _dot(gl_ref[...], wg_ref[...]))
    y = _dot(_bf(y), wo_ref[...])
    yn = y * lax.rsqrt(jnp.mean(y * y, axis=-1, keepdims=True) + RMS_EPS) * post_ref[...]
    o_ref[...] = x_ref[...] + gate_ref[...] * yn


def _merge_call(x2, ret, dsa, gl, p, wr, wd, wg, wo, post, gate):
    s, d = x2.shape
    tm = min(512, s)
    rows = lambda w: pl.BlockSpec((tm, w), lambda i: (i, 0))
    whole = lambda a: pl.BlockSpec(a.shape, lambda i: (0, 0))
    return pl.pallas_call(
        _merge_kernel,
        grid=(s // tm,),
        in_specs=[rows(d), rows(512), rows(512), rows(512), _pspec(tm, "merge", 3072),
                  whole(wr), whole(wd), whole(wg), whole(wo), whole(post), whole(gate)],
        out_specs=rows(d),
        out_shape=jax.ShapeDtypeStruct((s, d), jnp.float32),
        compiler_params=_cparams(1),
        name="merge_out",
    )(x2, ret, dsa, gl, p, wr, wd, wg, wo, post, gate)


def _pack_w_in(w_in):
    depth, d, _ = w_in.shape
    zeros = lambda n: jnp.zeros((depth, d, n), w_in.dtype)
    src = lambda name: w_in[:, :, _SRC[name][0]:_SRC[name][0] + _SRC[name][1]]
    pieces, at = [], 0
    for name, width in _PACK:
        assert at == PCOL[name]
        if name == "idx_kw":
            cols = [src("idx_k"), src("idx_w"), zeros(width - IDX_DH - IDX_HEADS)]
        elif name == "gla_a":
            cols = [src("gla_a"), zeros(width - GLA_RANK)]
        else:
            cols = [src(name)]
        pieces += cols
        at += width
    pieces.append(zeros(P_WIDTH - at))
    return jnp.concatenate(pieces, axis=-1).astype(jnp.bfloat16)


def kernel(x, c, positions, ada_w, ada_b, pre_norm, post_norm, w_in, gla_w_lr, gla_b_lr,
           w_br_ret, w_br_dsa, w_br_gla, w_out):
    b, s, d = x.shape
    assert b == 1 and d == D_MODEL
    depth = ada_w.shape[0]
    x2 = x.reshape(s, d)
    mod = _mod_call(jnp.broadcast_to(c, (8, d)), ada_w, ada_b.reshape(depth, 1, 3 * d))[:, 0:1, :]
    rcos, rsin, dcos, dsin = _tab_call(positions.reshape(s, 1))
    w_pack = _pack_w_in(w_in)
    wlr_pad = jnp.pad(gla_w_lr, ((0, 0), (0, LANES - GLA_RANK), (0, 0)))
    for l in range(depth):
        shift, scale, gate = mod[l, :, 0:d], mod[l, :, d:2 * d], mod[l, :, 2 * d:3 * d]
        p = _proj_call(x2, pre_norm[l][None, :], scale, shift, w_pack[l])
        ret = _ret_call(p, rcos, rsin)
        gl = _gla_call(p, wlr_pad[l], gla_b_lr[l][None, :])
        qp, kk, va, iqp, ik, w4 = _dprep_call(p, dcos, dsin)
        dsa = _dsa_call(p, qp, kk, va, iqp, ik, w4)
        x2 = _merge_call(x2, ret, dsa, gl, p, _bf(w_br_ret[l]), _bf(w_br_dsa[l]), _bf(w_br_gla[l]),
                         _bf(w_out[l]), post_norm[l][None, :], gate)
    return x2.reshape(b, s, d)
```

```python
import functools
import math

import jax
import jax.numpy as jnp
from jax import lax
from jax.experimental import pallas as pl
from jax.experimental.pallas import tpu as pltpu

D_MODEL = 1024
DEPTH = 4
RET_HEADS, RET_DK, RET_DV, RET_CHUNK, RET_THETA = 4, 64, 128, 128, 10000.0
DSA_HEADS, DSA_KV_HEADS, DSA_DH = 8, 2, 64
DSA_ROT = DSA_DH // 4
ROPE_THETA = 500000.0
IDX_HEADS, IDX_DH = 4, 64
TOPK_MAX = 256
GLA_HEADS, GLA_DK, GLA_DV, GLA_RANK, GLA_TAU, GLA_CHUNK = 4, 64, 128, 16, 16.0, 64
RMS_EPS = 1e-6
LANES = 128

_SRC = {}
_off = 0
for _name, _w in (("ret_q", 256), ("ret_k", 256), ("ret_v", 512), ("ret_g", 512),
                  ("dsa_q", 512), ("dsa_k", 128), ("dsa_v", 128), ("dsa_g", 512),
                  ("idx_q", 256), ("idx_k", 64), ("idx_w", 4),
                  ("gla_q", 256), ("gla_k", 256), ("gla_v", 512), ("gla_g", 512), ("gla_a", 16),
                  ("merge", 3072)):
    _SRC[_name] = (_off, _w)
    _off += _w
IN_WIDTH = _off

_PACK = (("merge", 3072), ("ret_q", 256), ("ret_k", 256), ("ret_v", 512), ("ret_g", 512),
         ("dsa_q", 512), ("dsa_g", 512), ("gla_v", 512), ("gla_g", 512),
         ("gla_q", 256), ("gla_k", 256), ("idx_q", 256),
         ("dsa_k", 128), ("dsa_v", 128), ("idx_kw", 128), ("gla_a", 128))
PCOL = {}
_off = 0
for _name, _w in _PACK:
    assert _off % min(_w, 1024) == 0
    PCOL[_name] = _off
    _off += _w
P_WIDTH = 8192
assert _off <= P_WIDTH

LOG2E = math.log2(math.e)
INT_MIN = -(2 ** 31)
F32_LOWEST = -3.4028234663852886e38
NEG_INF_KEY = -0x7F800000
NEG_INIT = -1e30
NEG_MASK = -2e30


def _cparams(n_axes, vmem_mb=48):
    return pltpu.CompilerParams(dimension_semantics=("arbitrary",) * n_axes,
                                vmem_limit_bytes=vmem_mb * 1024 * 1024)


def _bf(x):
    return x.astype(jnp.bfloat16)


def _dot(a, b):
    return jnp.dot(a, b, preferred_element_type=jnp.float32)


def _dot_nt(a, b):
    return lax.dot_general(a, b, (((1,), (1,)), ((), ())), preferred_element_type=jnp.float32)


def _dot_tn(a, b):
    return lax.dot_general(a, b, (((0,), (0,)), ((), ())), preferred_element_type=jnp.float32)


def _split3(x):
    hi = _bf(x)
    r1 = x - hi.astype(jnp.float32)
    mid = _bf(r1)
    lo = _bf(r1 - mid.astype(jnp.float32))
    return hi, mid, lo


def _silu(x):
    return x * (1.0 / (1.0 + jnp.exp(-x)))


def _sigmoid(x):
    return 1.0 / (1.0 + jnp.exp(-x))


def _mod_kernel(c_ref, w_ref, b_ref, o_ref):
    c = c_ref[...]
    ca = _silu(c)
    acc = None
    for t in _split3(ca):
        for u in _split3(w_ref[0]):
            part = _dot(t, u)
            acc = part if acc is None else acc + part
    o_ref[0] = acc + b_ref[0]


def _mod_call(c8, ada_w, ada_b3):
    depth, d, n = ada_w.shape
    tn = 1024
    return pl.pallas_call(
        _mod_kernel,
        grid=(depth, n // tn),
        in_specs=[pl.BlockSpec((8, d), lambda l, j: (0, 0)),
                  pl.BlockSpec((1, d, tn), lambda l, j: (l, 0, j)),
                  pl.BlockSpec((1, 1, tn), lambda l, j: (l, 0, j))],
        out_specs=pl.BlockSpec((1, 8, tn), lambda l, j: (l, 0, j)),
        out_shape=jax.ShapeDtypeStruct((depth, 8, n), jnp.float32),
        compiler_params=_cparams(2),
        name="adaln_mod",
    )(c8, ada_w, ada_b3)


def _tab_kernel(pos_ref, rf_ref, rs_ref, df_ref, ds_ref, rc_o, rsn_o, dc_o, dsn_o):
    pos = pos_ref[...].astype(jnp.float32)
    ang = pos * rf_ref[...]
    rc_o[...] = jnp.cos(ang)
    rsn_o[...] = jnp.sin(ang) * rs_ref[...]
    ang = pos * df_ref[...]
    dc_o[...] = jnp.cos(ang)
    dsn_o[...] = jnp.sin(ang) * ds_ref[...]


def _rope_rows():
    half = RET_DK // 2
    f = RET_THETA ** (-jnp.arange(half, dtype=jnp.float32) * 2.0 / RET_DK)
    rf = jnp.tile(jnp.concatenate([f, f]), RET_HEADS)[None, :]
    rs = jnp.tile(jnp.concatenate([-jnp.ones(half), jnp.ones(half)]), RET_HEADS)[None, :].astype(jnp.float32)
    half = DSA_ROT // 2
    f = ROPE_THETA ** (-jnp.arange(half, dtype=jnp.float32) * 2.0 / DSA_ROT)
    z = jnp.zeros(DSA_DH - DSA_ROT, jnp.float32)
    df = jnp.tile(jnp.concatenate([f, f, z]), 2)[None, :]
    ds = jnp.tile(jnp.concatenate([-jnp.ones(half), jnp.ones(half), z]), 2)[None, :].astype(jnp.float32)
    return rf, rs, df, ds


def _tab_call(pos_col):
    s = pos_col.shape[0]
    tm = min(1024, s)
    rf, rs, df, ds = _rope_rows()
    row = lambda w: pl.BlockSpec((1, w), lambda i: (0, 0))
    out = lambda w: pl.BlockSpec((tm, w), lambda i: (i, 0))
    return pl.pallas_call(
        _tab_kernel,
        grid=(s // tm,),
        in_specs=[pl.BlockSpec((tm, 1), lambda i: (i, 0)), row(256), row(256), row(128), row(128)],
        out_specs=[out(256), out(256), out(128), out(128)],
        out_shape=[jax.ShapeDtypeStruct((s, 256), jnp.float32), jax.ShapeDtypeStruct((s, 256), jnp.float32),
                   jax.ShapeDtypeStruct((s, 128), jnp.float32), jax.ShapeDtypeStruct((s, 128), jnp.float32)],
        compiler_params=_cparams(1),
        name="rope_tables",
    )(pos_col, rf, rs, df, ds)


def _swap_halves(x, half, period):
    w = x.shape[-1]
    lane = lax.broadcasted_iota(jnp.int32, x.shape, x.ndim - 1) & (period - 1)
    up = pltpu.roll(x, w - half, x.ndim - 1)
    dn = pltpu.roll(x, half, x.ndim - 1)
    return jnp.where(lane < half, up, dn)


def _rope(x, cos, sin_signed, half, period):
    return x * cos + _swap_halves(x, half, period) * sin_signed


def _proj_kernel(x_ref, pre_ref, sc_ref, sh_ref, w_ref, o_ref):
    x = x_ref[...]
    xn = x * lax.rsqrt(jnp.mean(x * x, axis=-1, keepdims=True) + RMS_EPS)
    h = xn * pre_ref[...] * (1.0 + sc_ref[...]) + sh_ref[...]
    o_ref[...] = _dot(_bf(h), w_ref[...])


def _proj_call(x2, pre, scale, shift, w_pack):
    s, d = x2.shape
    tm, tn = min(512, s), 2048
    vec = pl.BlockSpec((1, d), lambda j, i: (0, 0))
    return pl.pallas_call(
        _proj_kernel,
        grid=(P_WIDTH // tn, s // tm),
        in_specs=[pl.BlockSpec((tm, d), lambda j, i: (i, 0)), vec, vec, vec,
                  pl.BlockSpec((d, tn), lambda j, i: (0, j))],
        out_specs=pl.BlockSpec((tm, tn), lambda j, i: (i, j)),
        out_shape=jax.ShapeDtypeStruct((s, P_WIDTH), jnp.float32),
        compiler_params=_cparams(2),
        name="in_proj",
    )(x2, pre, scale, shift, w_pack)


def _pspec(tm, name, width):
    blk = PCOL[name] // width
    assert PCOL[name] % width == 0
    return pl.BlockSpec((tm, width), lambda i: (i, blk))


def _ret_log_g(h):
    return math.log1p(-(2.0 ** (-5.0 - h)))


def _ret_kernel(q_ref, k_ref, v_ref, g_ref, cos_ref, sin_ref, o_ref,
                state_ref, decay_ref, qdec_ref, kend_ref, *, chunks):
    C = RET_CHUNK

    @pl.when(pl.program_id(0) == 0)
    def _():
        state_ref[...] = jnp.zeros_like(state_ref)
        ii = lax.broadcasted_iota(jnp.int32, (C, C), 0)
        jj = lax.broadcasted_iota(jnp.int32, (C, C), 1)
        rel = (ii - jj).astype(jnp.float32)
        row = lax.broadcasted_iota(jnp.int32, (C, RET_DK), 0).astype(jnp.float32)
        for h in range(RET_HEADS):
            lg = _ret_log_g(h)
            decay_ref[h] = jnp.where(rel >= 0, jnp.exp(lg * jnp.maximum(rel, 0.0)), 0.0)
            qdec_ref[:, h * RET_DK:(h + 1) * RET_DK] = jnp.exp((row + 1.0) * lg)
            kend_ref[:, h * RET_DK:(h + 1) * RET_DK] = jnp.exp((C - 1.0 - row) * lg)

    for c in range(chunks):
        rows = slice(c * C, (c + 1) * C)
        cos, sin = cos_ref[rows, :], sin_ref[rows, :]
        q = _rope(q_ref[rows, :], cos, sin, RET_DK // 2, RET_DK) * (RET_DK ** -0.5)
        k = _rope(k_ref[rows, :], cos, sin, RET_DK // 2, RET_DK)
        qd = _bf(q * qdec_ref[...])
        kd = _bf(k * kend_ref[...])
        qb, kb = _bf(q), _bf(k)
        v = v_ref[rows, :]
        g = g_ref[rows, :]
        for h in range(RET_HEADS):
            dk = slice(h * RET_DK, (h + 1) * RET_DK)
            dv = slice(h * RET_DV, (h + 1) * RET_DV)
            vh = _bf(v[:, dv])
            scores = _dot_nt(qb[:, dk], kb[:, dk]) * decay_ref[h]
            st = state_ref[h]
            o = _dot(_bf(scores), vh) + _dot(qd[:, dk], _bf(st))
            state_ref[h] = math.exp(C * _ret_log_g(h)) * st + _dot_tn(kd[:, dk], vh)
            o = o * lax.rsqrt(jnp.mean(o * o, axis=-1, keepdims=True) + RMS_EPS)
            o_ref[rows, dv] = _bf(o * _silu(g[:, dv]))


def _ret_call(p, rcos, rsin):
    s = p.shape[0]
    chunks = 2
    tm = RET_CHUNK * chunks
    tab = pl.BlockSpec((tm, 256), lambda i: (i, 0))
    return pl.pallas_call(
        functools.partial(_ret_kernel, chunks=chunks),
        grid=(s // tm,),
        in_specs=[_pspec(tm, "ret_q", 256), _pspec(tm, "ret_k", 256), _pspec(tm, "ret_v", 512),
                  _pspec(tm, "ret_g", 512), tab, tab],
        out_specs=pl.BlockSpec((tm, 512), lambda i: (i, 0)),
        out_shape=jax.ShapeDtypeStruct((s, 512), jnp.bfloat16),
        scratch_shapes=[pltpu.VMEM((RET_HEADS, RET_DK, RET_DV), jnp.float32),
                        pltpu.VMEM((RET_HEADS, RET_CHUNK, RET_CHUNK), jnp.float32),
                        pltpu.VMEM((RET_CHUNK, RET_HEADS * RET_DK), jnp.float32),
                        pltpu.VMEM((RET_CHUNK, RET_HEADS * RET_DK), jnp.float32)],
        compiler_params=_cparams(1),
        name="retention",
    )(p, p, p, p, rcos, rsin)


def _gla_kernel(q_ref, k_ref, v_ref, g_ref, a_ref, wlr_ref, blr_ref, o_ref, state_ref, *, chunks):
    C = GLA_CHUNK

    @pl.when(pl.program_id(0) == 0)
    def _():
        state_ref[...] = jnp.zeros_like(state_ref)

    ii = lax.broadcasted_iota(jnp.int32, (C, C), 0)
    jj = lax.broadcasted_iota(jnp.int32, (C, C), 1)
    causal = jj <= ii
    tril = _bf(jnp.where(causal, 1.0, 0.0))
    wlr = wlr_ref[...]
    w_hi, w_mid, w_lo = _split3(wlr)

    for c in range(chunks):
        rows = slice(c * C, (c + 1) * C)
        a_hi, a_mid, a_lo = _split3(a_ref[rows, :])
        z = (_dot(a_hi, w_hi) + (_dot(a_hi, w_mid) + _dot(a_mid, w_hi))
             + (_dot(a_hi, w_lo) + _dot(a_mid, w_mid) + _dot(a_lo, w_hi))) + blr_ref[...]
        log_a = (jnp.minimum(z, 0.0) - jnp.log1p(jnp.exp(-jnp.abs(z)))) * (1.0 / GLA_TAU)
        l_hi, l_mid, l_lo = _split3(log_a)
        bcum = _dot(tril, l_hi) + _dot(tril, l_mid) + _dot(tril, l_lo)
        b_mid = bcum[C // 2 - 1:C // 2, :]
        b_last = bcum[C - 1:C, :]
        q = q_ref[rows, :] * (GLA_DK ** -0.5)
        k = k_ref[rows, :]
        qt = _bf(q * jnp.exp(bcum - b_mid))
        kt = _bf(k * jnp.exp(b_mid - bcum))
        qg = _bf(q * jnp.exp(bcum))
        kd = _bf(k * jnp.exp(b_last - bcum))
        e_last = jnp.exp(b_last)
        v = v_ref[rows, :]
        g = g_ref[rows, :]
        for h in range(GLA_HEADS):
            dk = slice(h * GLA_DK, (h + 1) * GLA_DK)
            dv = slice(h * GLA_DV, (h + 1) * GLA_DV)
            vh = _bf(v[:, dv])
            attn = jnp.where(causal, _dot_nt(qt[:, dk], kt[:, dk]), 0.0)
            st = state_ref[h]
            o = _dot(_bf(attn), vh) + _dot_nt(qg[:, dk], _bf(st))
            state_ref[h] = e_last[:, dk] * st + _dot_tn(vh, kd[:, dk])
            o = o * lax.rsqrt(jnp.mean(o * o, axis=-1, keepdims=True) + RMS_EPS)
            o_ref[rows, dv] = _bf(o * _silu(g[:, dv]))


def _gla_call(p, wlr_pad, blr):
    s = p.shape[0]
    chunks = 4
    tm = GLA_CHUNK * chunks
    return pl.pallas_call(
        functools.partial(_gla_kernel, chunks=chunks),
        grid=(s // tm,),
        in_specs=[_pspec(tm, "gla_q", 256), _pspec(tm, "gla_k", 256), _pspec(tm, "gla_v", 512),
                  _pspec(tm, "gla_g", 512), _pspec(tm, "gla_a", 128),
                  pl.BlockSpec((128, 256), lambda i: (0, 0)), pl.BlockSpec((1, 256), lambda i: (0, 0))],
        out_specs=pl.BlockSpec((tm, 512), lambda i: (i, 0)),
        out_shape=jax.ShapeDtypeStruct((s, 512), jnp.bfloat16),
        scratch_shapes=[pltpu.VMEM((GLA_HEADS, GLA_DV, GLA_DK), jnp.float32)],
        compiler_params=_cparams(1),
        name="gla",
    )(p, p, p, p, p, wlr_pad, blr)


def _place_head(pair, src_pos, dst_pos):
    lane = lax.broadcasted_iota(jnp.int32, pair.shape, 1)
    if src_pos != dst_pos:
        pair = pltpu.roll(pair, 64, 1)
    keep = (lane < 64) if dst_pos == 0 else (lane >= 64)
    return jnp.where(keep, pair, 0.0)


def _dprep_kernel(q_ref, k_ref, v_ref, iq_ref, ikw_ref, cos_ref, sin_ref,
                  qp_o, k_o, v_o, iqp_o, ik_o, w_o):
    cos, sin = cos_ref[...], sin_ref[...]
    half = DSA_ROT // 2
    cos4 = jnp.concatenate([cos] * 4, axis=1)
    sin4 = jnp.concatenate([sin] * 4, axis=1)
    q = _rope(q_ref[...], cos4, sin4, half, DSA_DH) * (DSA_DH ** -0.5 * LOG2E)
    for h in range(DSA_HEADS):
        pair = q[:, (h // 2) * 128:(h // 2 + 1) * 128]
        qp_o[h] = _bf(_place_head(pair, h % 2, h // (DSA_HEADS // DSA_KV_HEADS)))
    k_o[...] = _bf(_rope(k_ref[...], cos, sin, half, DSA_DH))
    v = v_ref[...]
    lane = lax.broadcasted_iota(jnp.int32, v.shape, 1)
    v_o[0] = _bf(jnp.where(lane < DSA_DH, v, 1.0))
    v_o[1] = _bf(jnp.where(lane < DSA_DH, pltpu.roll(v, DSA_DH, 1), 1.0))
    iq = _rope(iq_ref[...], cos4[:, :256], sin4[:, :256], half, IDX_DH)
    for h in range(IDX_HEADS):
        pair = iq[:, (h // 2) * 128:(h // 2 + 1) * 128]
        iqp_o[h] = _bf(_place_head(pair, h % 2, 0))
    ikw = ikw_ref[...]
    lane = lax.broadcasted_iota(jnp.int32, ikw.shape, 1)
    ik_o[...] = _bf(jnp.where(lane < IDX_DH, _rope(ikw, cos, sin, half, DSA_DH), 0.0))
    wscale = (IDX_HEADS ** -0.5) * (IDX_DH ** -0.5)
    w_o[...] = pltpu.roll(ikw, 128 - IDX_DH, 1) * wscale


def _dprep_call(p, dcos, dsin):
    s = p.shape[0]
    tm = min(512, s)
    tab = pl.BlockSpec((tm, 128), lambda i: (i, 0))
    return pl.pallas_call(
        _dprep_kernel,
        grid=(s // tm,),
        in_specs=[_pspec(tm, "dsa_q", 512), _pspec(tm, "dsa_k", 128), _pspec(tm, "dsa_v", 128),
                  _pspec(tm, "idx_q", 256), _pspec(tm, "idx_kw", 128), tab, tab],
        out_specs=[pl.BlockSpec((DSA_HEADS, tm, 128), lambda i: (0, i, 0)), tab,
                   pl.BlockSpec((DSA_KV_HEADS, tm, 128), lambda i: (0, i, 0)),
                   pl.BlockSpec((IDX_HEADS, tm, 128), lambda i: (0, i, 0)), tab, tab],
        out_shape=[jax.ShapeDtypeStruct((DSA_HEADS, s, 128), jnp.bfloat16),
                   jax.ShapeDtypeStruct((s, 128), jnp.bfloat16),
                   jax.ShapeDtypeStruct((DSA_KV_HEADS, s, 128), jnp.bfloat16),
                   jax.ShapeDtypeStruct((IDX_HEADS, s, 128), jnp.bfloat16),
                   jax.ShapeDtypeStruct((s, 128), jnp.bfloat16),
                   jax.ShapeDtypeStruct((s, 128), jnp.float32)],
        compiler_params=_cparams(1),
        name="dsa_prep",
    )(p, p, p, p, p, dcos, dsin)


DSA_QB = 256
IDX_TK = 256
ATT_TK = 1024
SEL_ROWS = 128
SEL_LANES = 512
CAND_DEPTH = 10
CAND_SPLIT = 2
CAND_W = CAND_DEPTH * CAND_SPLIT * LANES
TIE_CHUNK = 4
NO_TIE_LIMIT = 1e9
assert CAND_W % SEL_LANES == 0 and 8 % CAND_SPLIT == 0


def _key_to_f32(key):
    bits = jnp.where(key >= 0, key, (0 - key) | INT_MIN)
    return lax.bitcast_convert_type(bits, jnp.float32)


def _count_lanes(keys_ref, r0, n_steps, cand, strict=False):
    def body(s, acc):
        c0 = s * SEL_LANES if isinstance(s, int) else pl.multiple_of(s * SEL_LANES, SEL_LANES)
        for u in range(SEL_LANES // LANES):
            kk = keys_ref[pl.ds(r0, SEL_ROWS), pl.ds(c0 + u * LANES, LANES)]
            acc = acc + jnp.where((kk > cand) if strict else (kk >= cand), 1.0, 0.0)
        return acc
    acc = jnp.zeros((SEL_ROWS, LANES), jnp.float32)
    if isinstance(n_steps, int):
        for s in range(n_steps):
            acc = body(s, acc)
        return acc
    return lax.fori_loop(0, n_steps, body, acc)


def _row_total(lane_counts):
    ones = jnp.ones((LANES, LANES), jnp.bfloat16)
    return _dot(_bf(lane_counts), ones)


def _f32_to_key(x):
    bits = lax.bitcast_convert_type(x, jnp.int32)
    return jnp.where(bits >= 0, bits, INT_MIN - bits)


def _search_window(cand_ref, lo_ref, final_ref):
    hi = lw = None
    for v in range(CAND_SPLIT):
        top = cand_ref[:, v * CAND_DEPTH * LANES:(v * CAND_DEPTH + 1) * LANES]
        hi = top if hi is None else jnp.maximum(hi, top)
        lw = top if lw is None else jnp.minimum(lw, top)
    k_hi = _f32_to_key(jnp.max(hi, axis=-1, keepdims=True))
    k_lw = _f32_to_key(jnp.min(lw, axis=-1, keepdims=True))
    nbits = 32 - lax.clz(k_hi ^ k_lw)
    nb = jnp.max(nbits.astype(jnp.float32)).astype(jnp.int32)
    low = lax.shift_left(jnp.int32(1), jnp.minimum(nb, 31)) - 1
    lo0 = jnp.where(nbits == 0, k_lw, jnp.where(nb >= 32, INT_MIN, k_lw & ~low))
    lo_ref[...] = jnp.broadcast_to(lo0, lo_ref.shape)
    final_ref[...] = jnp.broadcast_to(jnp.where(nbits == 0, 1.0, 0.0), final_ref.shape)
    return 32 - nb


def _radix_search(src_ref, n_steps, lo_ref, cnt_ref, topk, first_bit=None, final_ref=None):
    rows_total = lo_ref.shape[0]
    if first_bit is None:
        first_bit = jnp.int32(0)
        lo_ref[...] = jnp.full(lo_ref.shape, INT_MIN, jnp.int32)
        cnt_ref[...] = jnp.zeros(cnt_ref.shape, jnp.float32)
    else:
        cnt_ref[...] = jnp.full(cnt_ref.shape, NO_TIE_LIMIT, jnp.float32)

    def cond(c):
        b, pending = c
        return jnp.logical_and(b < 32, pending > 0)

    def body(c):
        b, _ = c
        bit = lax.shift_left(jnp.int32(1), 31 - b)
        off = None
        for rb in range(rows_total // SEL_ROWS):
            rows = pl.ds(rb * SEL_ROWS, SEL_ROWS)
            cand = lo_ref[rows, :] + bit
            cnt = _row_total(_count_lanes(src_ref, rb * SEL_ROWS, n_steps, _key_to_f32(cand)))
            ok = cnt >= float(topk)
            if final_ref is not None:
                ok = jnp.logical_and(ok, final_ref[rows, :] <= 0.0)
            lo_ref[rows, :] = jnp.where(ok, cand, lo_ref[rows, :])
            cnt = jnp.where(ok, cnt, cnt_ref[rows, :])
            cnt_ref[rows, :] = cnt
            miss = jnp.abs(cnt - float(topk))
            if final_ref is not None:
                miss = jnp.where(final_ref[rows, :] > 0.0, 0.0, miss)
            off = miss if off is None else jnp.maximum(off, miss)
        return b + 1, (jnp.max(off) > 0.0).astype(jnp.int32)
    lax.while_loop(cond, body, (first_bit, jnp.int32(1)))


def _lane_candidates(keys_ref, cand_ref, n_groups):
    n_stack = CAND_DEPTH * CAND_SPLIT

    def rg_body(rg, carry):
        r0 = pl.multiple_of(rg * 8, 8)

        def col_body(s, st):
            st = list(st)
            c0 = pl.multiple_of(s * 8 * LANES, 8 * LANES)
            for u in range(8):
                x = keys_ref[pl.ds(r0, 8), pl.ds(c0 + u * LANES, LANES)]
                base = (u % CAND_SPLIT) * CAND_DEPTH
                for d in range(CAND_DEPTH):
                    cur = st[base + d]
                    st[base + d] = jnp.maximum(cur, x)
                    x = jnp.minimum(cur, x)
            return tuple(st)
        init = tuple(jnp.full((8, LANES), -jnp.inf, jnp.float32) for _ in range(n_stack))
        st = lax.fori_loop(0, n_groups, col_body, init)
        for k in range(n_stack):
            cand_ref[pl.ds(r0, 8), k * LANES:(k + 1) * LANES] = st[k]
        return carry
    lax.fori_loop(0, cand_ref.shape[0] // 8, rg_body, 0)


def _ties_to_keep(src_ref, n_steps, lo_ref, cnt_ref, part_ref, need_ref, topk):
    rows_total = lo_ref.shape[0]
    surplus = jnp.where(lo_ref[...] > NEG_INF_KEY, cnt_ref[...] - float(topk), 0.0)
    need_ref[...] = jnp.full(need_ref.shape, NO_TIE_LIMIT, jnp.float32)

    @pl.when(jnp.max(surplus) > 0.0)
    def _():
        for rb in range(rows_total // SEL_ROWS):
            rows = pl.ds(rb * SEL_ROWS, SEL_ROWS)
            thr = _key_to_f32(jnp.maximum(lo_ref[rows, :], NEG_INF_KEY))
            part_ref[rows, :] = _count_lanes(src_ref, rb * SEL_ROWS, n_steps, thr, strict=True)
        above = _row_total(part_ref[...])
        need_ref[...] = jnp.where(surplus > 0.0, float(topk) - above, NO_TIE_LIMIT)


def _tie_index_bound(keys_ref, thr_ref, need_ref, seen_ref, sig_ref, n_chunks):
    kk_i = lax.broadcasted_iota(jnp.int32, (LANES, LANES), 0)
    jj_i = lax.broadcasted_iota(jnp.int32, (LANES, LANES), 1)
    tri = _bf(jnp.where(kk_i <= jj_i, 1.0, 0.0))
    seen_ref[...] = jnp.zeros(seen_ref.shape, jnp.float32)

    def cond(c):
        j, pending = c
        return jnp.logical_and(j < n_chunks, pending > 0)

    def body(c):
        j, _ = c
        thr, need = thr_ref[...], need_ref[...]
        seen, sig = seen_ref[...], sig_ref[...]
        for u in range(TIE_CHUNK):
            c0 = pl.multiple_of((j * TIE_CHUNK + u) * LANES, LANES)
            tie = jnp.where(keys_ref[:, pl.ds(c0, LANES)] == thr, 1.0, 0.0)
            rank = seen + _dot(_bf(tie), tri)
            after = seen + jnp.sum(tie, axis=-1, keepdims=True)
            before = jnp.sum(jnp.where(rank < need, 1.0, 0.0), axis=-1, keepdims=True)
            here = jnp.where(seen < need, jnp.where(after >= need, 1.0, 0.0), 0.0)
            sig = jnp.where(here > 0.0, c0.astype(jnp.float32) + before, sig)
            seen = after
        seen_ref[...] = seen
        sig_ref[...] = sig
        waiting = jnp.where(need < NO_TIE_LIMIT, jnp.where(seen < need, 1.0, 0.0), 0.0)
        return j + 1, (jnp.max(waiting) > 0.0).astype(jnp.int32)
    lax.while_loop(cond, body, (jnp.int32(0), jnp.int32(1)))


def _dsa_kernel(qp_ref, iqp_ref, w_ref, g_ref, k_ref, va_ref, ik_ref, o_ref,
                keys_ref, cand_ref, lo_ref, thr_ref, cnt_ref, part_ref, need_ref, sig_ref,
                m_ref, acc_ref, full_ref, *, topk):
    QB, TK = DSA_QB, IDX_TK
    G = DSA_HEADS // DSA_KV_HEADS
    i = pl.program_id(0)
    n_idx = i + 1
    n_att = (n_idx * IDX_TK + ATT_TK - 1) // ATT_TK
    n_sel = n_att * (ATT_TK // SEL_LANES)
    row_id = lax.broadcasted_iota(jnp.int32, (QB, TK), 0)
    col_id = lax.broadcasted_iota(jnp.int32, (QB, TK), 1)

    iq = iqp_ref[...].reshape(IDX_HEADS * QB, LANES)
    wb = [jnp.broadcast_to(w_ref[:, h:h + 1], (QB, TK)) for h in range(IDX_HEADS)]

    def score_tile(j, diag):
        c0 = pl.multiple_of(j * TK, TK)
        s_all = _dot_nt(iq, ik_ref[pl.ds(c0, TK), :])
        score = None
        for h in range(IDX_HEADS):
            term = wb[h] * jnp.maximum(s_all[h * QB:(h + 1) * QB], 0.0)
            score = term if score is None else score + term
        if diag:
            score = jnp.where(col_id <= row_id, score, -jnp.inf)
        keys_ref[:, pl.ds(c0, TK)] = score

    def score_body(j, carry):
        score_tile(j, False)
        return carry
    lax.fori_loop(0, i, score_body, 0)
    score_tile(i, True)

    def blank_body(j, carry):
        c0 = pl.multiple_of(j * TK, TK)
        keys_ref[:, pl.ds(c0, TK)] = jnp.full((QB, TK), -jnp.inf, jnp.float32)
        return carry
    lax.fori_loop(n_idx, n_att * (ATT_TK // TK), blank_body, 0)

    full_ref[0] = jnp.int32(1)

    @pl.when(n_att * ATT_TK > 2 * CAND_W)
    def _():
        _lane_candidates(keys_ref, cand_ref, n_att * (ATT_TK // (8 * LANES)))
        first_bit = _search_window(cand_ref, lo_ref, part_ref)
        _radix_search(cand_ref, CAND_W // SEL_LANES, lo_ref, cnt_ref, topk,
                      first_bit=first_bit, final_ref=part_ref)
        thr = _key_to_f32(jnp.maximum(lo_ref[...], NEG_INF_KEY))
        exact = cnt_ref[...] == float(topk)
        lost = None
        for v in range(CAND_SPLIT):
            last = cand_ref[:, (v * CAND_DEPTH + CAND_DEPTH - 1) * LANES:(v * CAND_DEPTH + CAND_DEPTH) * LANES]
            hit = jnp.where(last > thr, 1.0, jnp.where(last == thr, jnp.where(exact, 1.0, 0.0), 0.0))
            lost = hit if lost is None else jnp.maximum(lost, hit)
        full_ref[0] = (jnp.max(lost) > 0.0).astype(jnp.int32)
        _ties_to_keep(cand_ref, CAND_W // SEL_LANES, lo_ref, cnt_ref, part_ref, need_ref, topk)

    @pl.when(full_ref[0] > 0)
    def _():
        _radix_search(keys_ref, n_sel, lo_ref, cnt_ref, topk)
        _ties_to_keep(keys_ref, n_sel, lo_ref, cnt_ref, part_ref, need_ref, topk)

    tau = lo_ref[...]
    found = tau > NEG_INF_KEY
    thr_ref[...] = jnp.where(found, _key_to_f32(jnp.maximum(tau, NEG_INF_KEY)), F32_LOWEST)
    sig_ref[...] = jnp.full(sig_ref.shape, NO_TIE_LIMIT, jnp.float32)

    @pl.when(jnp.min(need_ref[...]) < NO_TIE_LIMIT)
    def _():
        _tie_index_bound(keys_ref, thr_ref, need_ref, part_ref, sig_ref, n_att * (ATT_TK // (TIE_CHUNK * LANES)))

    m_ref[...] = jnp.full(m_ref.shape, NEG_INIT, jnp.float32)
    acc_ref[...] = jnp.zeros(acc_ref.shape, jnp.float32)
    n_grp = ATT_TK // LANES
    lane_f = lax.broadcasted_iota(jnp.int32, (QB, LANES), 1).astype(jnp.float32)

    def attn_body(j, carry):
        c0 = pl.multiple_of(j * ATT_TK, ATT_TK)
        tau_b = thr_ref[...]
        sig_rel = sig_ref[...] - c0.astype(jnp.float32)
        parts = []
        for u in range(n_grp):
            kk = keys_ref[:, pl.ds(c0 + u * LANES, LANES)]
            tie_bias = jnp.where(lane_f + float(u * LANES) <= sig_rel, 0.0, NEG_MASK)
            parts.append(jnp.where(kk > tau_b, 0.0, jnp.where(kk == tau_b, tie_bias, NEG_MASK)))
        bias = jnp.concatenate(parts, axis=1)
        kt = k_ref[pl.ds(c0, ATT_TK), :]
        for n in range(DSA_KV_HEADS):
            qn = qp_ref[n * G:(n + 1) * G].reshape(G * QB, LANES)
            logits = _dot_nt(qn, kt)
            ps = []
            for gq in range(G):
                h = n * G + gq
                lm = logits[gq * QB:(gq + 1) * QB] + bias
                tmax = lm[:, 0:LANES]
                for u in range(1, n_grp):
                    tmax = jnp.maximum(tmax, lm[:, u * LANES:(u + 1) * LANES])
                m_old = m_ref[h]
                m_new = jnp.maximum(m_old, jnp.max(tmax, axis=-1, keepdims=True))
                ps.append(jnp.concatenate(
                    [_bf(jnp.exp2(lm[:, u * LANES:(u + 1) * LANES] - m_new)) for u in range(n_grp)], axis=1))
                m_ref[h] = m_new
                acc_ref[h] = jnp.exp2(m_old - m_new) * acc_ref[h]
            pv = _dot(jnp.concatenate(ps, axis=0), va_ref[n, pl.ds(c0, ATT_TK), :])
            for gq in range(G):
                h = n * G + gq
                acc_ref[h] = acc_ref[h] + pv[gq * QB:(gq + 1) * QB]
        return carry
    lax.fori_loop(0, n_att, attn_body, 0)

    lane = lax.broadcasted_iota(jnp.int32, (QB, LANES), 1)
    g = g_ref[...]
    for pair in range(DSA_HEADS // 2):
        halves = []
        for pos in range(2):
            a = acc_ref[2 * pair + pos]
            halves.append(a * (1.0 / pltpu.roll(a, DSA_DH, 1)))
        o = jnp.where(lane < DSA_DH, halves[0], pltpu.roll(halves[1], DSA_DH, 1))
        sl = slice(pair * LANES, (pair + 1) * LANES)
        o_ref[:, sl] = _bf(o * _silu(g[:, sl]))


def _dsa_call(p, qp, kk, va, iqp, ik, w4):
    s = p.shape[0]
    topk = min(TOPK_MAX, s // 4)
    QB = DSA_QB
    assert IDX_TK == QB and s % ATT_TK == 0 and ATT_TK % SEL_LANES == 0 and s // LANES <= 256
    once = pl.Buffered(1)
    return pl.pallas_call(
        functools.partial(_dsa_kernel, topk=topk),
        grid=(s // QB,),
        in_specs=[pl.BlockSpec((DSA_HEADS, QB, LANES), lambda i: (0, i, 0)),
                  pl.BlockSpec((IDX_HEADS, QB, LANES), lambda i: (0, i, 0)),
                  pl.BlockSpec((QB, LANES), lambda i: (i, 0)),
                  _pspec(QB, "dsa_g", 512),
                  pl.BlockSpec((s, LANES), lambda i: (0, 0), pipeline_mode=once),
                  pl.BlockSpec((DSA_KV_HEADS, s, LANES), lambda i: (0, 0, 0), pipeline_mode=once),
                  pl.BlockSpec((s, LANES), lambda i: (0, 0), pipeline_mode=once)],
        out_specs=pl.BlockSpec((QB, 512), lambda i: (i, 0)),
        out_shape=jax.ShapeDtypeStruct((s, 512), jnp.bfloat16),
        scratch_shapes=[pltpu.VMEM((QB, s), jnp.float32),
                        pltpu.VMEM((QB, CAND_W), jnp.float32),
                        pltpu.VMEM((QB, LANES), jnp.int32),
                        pltpu.VMEM((QB, LANES), jnp.float32),
                        pltpu.VMEM((QB, LANES), jnp.float32),
                        pltpu.VMEM((QB, LANES), jnp.float32),
                        pltpu.VMEM((QB, LANES), jnp.float32),
                        pltpu.VMEM((QB, LANES), jnp.float32),
                        pltpu.VMEM((DSA_HEADS, QB, LANES), jnp.float32),
                        pltpu.VMEM((DSA_HEADS, QB, LANES), jnp.float32),
                        pltpu.SMEM((1,), jnp.int32)],
        compiler_params=_cparams(1, vmem_mb=56),
        name="dsa",
    )(qp, iqp, w4, p, kk, va, ik)


def _merge_kernel(x_ref, ret_ref, dsa_ref, gl_ref, m_ref, wr_ref, wd_ref, wg_ref, wo_ref,
                  post_ref, gate_ref, o_ref):
    d = D_MODEL
    y = (_sigmoid(m_ref[:, 0:d]) * _dot(ret_ref[...], wr_ref[...])
         + _sigmoid(m_ref[:, d:2 * d]) * _dot(dsa_ref[...], wd_ref[...])
         + _sigmoid(m_ref[:, 2 * d:3 * d]) * _dot(gl_ref[...], wg_ref[...]))
    y = _dot(_bf(y), wo_ref[...])
    yn = y * lax.rsqrt(jnp.mean(y * y, axis=-1, keepdims=True) + RMS_EPS) * post_ref[...]
    o_ref[...] = x_ref[...] + gate_ref[...] * yn


def _merge_call(x2, ret, dsa, gl, p, wr, wd, wg, wo, post, gate):
    s, d = x2.shape
    tm = min(512, s)
    rows = lambda w: pl.BlockSpec((tm, w), lambda i: (i, 0))
    whole = lambda a: pl.BlockSpec(a.shape, lambda i: (0, 0))
    return pl.pallas_call(
        _merge_kernel,
        grid=(s // tm,),
        in_specs=[rows(d), rows(512), rows(512), rows(512), _pspec(tm, "merge", 3072),
                  whole(wr), whole(wd), whole(wg), whole(wo), whole(post), whole(gate)],
        out_specs=rows(d),
        out_shape=jax.ShapeDtypeStruct((s, d), jnp.float32),
        compiler_params=_cparams(1),
        name="merge_out",
    )(x2, ret, dsa, gl, p, wr, wd, wg, wo, post, gate)


def _pack_w_in(w_in):
    depth, d, _ = w_in.shape
    zeros = lambda n: jnp.zeros((depth, d, n), w_in.dtype)
    src = lambda name: w_in[:, :, _SRC[name][0]:_SRC[name][0] + _SRC[name][1]]
    pieces, at = [], 0
    for name, width in _PACK:
        assert at == PCOL[name]
        if name == "idx_kw":
            cols = [src("idx_k"), src("idx_w"), zeros(width - IDX_DH - IDX_HEADS)]
        elif name == "gla_a":
            cols = [src("gla_a"), zeros(width - GLA_RANK)]
        else:
            cols = [src(name)]
        pieces += cols
        at += width
    pieces.append(zeros(P_WIDTH - at))
    return jnp.concatenate(pieces, axis=-1).astype(jnp.bfloat16)


def kernel(x, c, positions, ada_w, ada_b, pre_norm, post_norm, w_in, gla_w_lr, gla_b_lr,
           w_br_ret, w_br_dsa, w_br_gla, w_out):
    b, s, d = x.shape
    assert b == 1 and d == D_MODEL
    depth = ada_w.shape[0]
    x2 = x.reshape(s, d)
    mod = _mod_call(jnp.broadcast_to(c, (8, d)), ada_w, ada_b.reshape(depth, 1, 3 * d))[:, 0:1, :]
    rcos, rsin, dcos, dsin = _tab_call(positions.reshape(s, 1))
    w_pack = _pack_w_in(w_in)
    wlr_pad = jnp.pad(gla_w_lr, ((0, 0), (0, LANES - GLA_RANK), (0, 0)))
    for l in range(depth):
        shift, scale, gate = mod[l, :, 0:d], mod[l, :, d:2 * d], mod[l, :, 2 * d:3 * d]
        p = _proj_call(x2, pre_norm[l][None, :], scale, shift, w_pack[l])
        ret = _ret_call(p, rcos, rsin)
        gl = _gla_call(p, wlr_pad[l], gla_b_lr[l][None, :])
        qp, kk, va, iqp, ik, w4 = _dprep_call(p, dcos, dsin)
        dsa = _dsa_call(p, qp, kk, va, iqp, ik, w4)
        x2 = _merge_call(x2, ret, dsa, gl, p, _bf(w_br_ret[l]), _bf(w_br_dsa[l]), _bf(w_br_gla[l]),
                         _bf(w_out[l]), post_norm[l][None, :], gate)
    return x2.reshape(b, s, d)
```

```python
import functools
import math

import jax
import jax.numpy as jnp
from jax import lax
from jax.experimental import pallas as pl
from jax.experimental.pallas import tpu as pltpu

D_MODEL = 1024
DEPTH = 4
RET_HEADS, RET_DK, RET_DV, RET_CHUNK, RET_THETA = 4, 64, 128, 128, 10000.0
DSA_HEADS, DSA_KV_HEADS, DSA_DH = 8, 2, 64
DSA_ROT = DSA_DH // 4
ROPE_THETA = 500000.0
IDX_HEADS, IDX_DH = 4, 64
TOPK_MAX = 256
GLA_HEADS, GLA_DK, GLA_DV, GLA_RANK, GLA_TAU, GLA_CHUNK = 4, 64, 128, 16, 16.0, 64
RMS_EPS = 1e-6
LANES = 128

_SRC = {}
_off = 0
for _name, _w in (("ret_q", 256), ("ret_k", 256), ("ret_v", 512), ("ret_g", 512),
                  ("dsa_q", 512), ("dsa_k", 128), ("dsa_v", 128), ("dsa_g", 512),
                  ("idx_q", 256), ("idx_k", 64), ("idx_w", 4),
                  ("gla_q", 256), ("gla_k", 256), ("gla_v", 512), ("gla_g", 512), ("gla_a", 16),
                  ("merge", 3072)):
    _SRC[_name] = (_off, _w)
    _off += _w
IN_WIDTH = _off

_PACK = (("merge", 3072), ("ret_q", 256), ("ret_k", 256), ("ret_v", 512), ("ret_g", 512),
         ("dsa_q", 512), ("dsa_g", 512), ("gla_v", 512), ("gla_g", 512),
         ("gla_q", 256), ("gla_k", 256), ("idx_q", 256),
         ("dsa_k", 128), ("dsa_v", 128), ("idx_kw", 128), ("gla_a", 128))
PCOL = {}
_off = 0
for _name, _w in _PACK:
    assert _off % min(_w, 1024) == 0
    PCOL[_name] = _off
    _off += _w
P_WIDTH = 8192
assert _off <= P_WIDTH

LOG2E = math.log2(math.e)
INT_MIN = -(2 ** 31)
F32_LOWEST = -3.4028234663852886e38
NEG_INF_KEY = -0x7F800000
NEG_INIT = -1e30
NEG_MASK = -2e30


def _cparams(n_axes, vmem_mb=48):
    return pltpu.CompilerParams(dimension_semantics=("arbitrary",) * n_axes,
                                vmem_limit_bytes=vmem_mb * 1024 * 1024)


def _bf(x):
    return x.astype(jnp.bfloat16)


def _dot(a, b):
    return jnp.dot(a, b, preferred_element_type=jnp.float32)


def _dot_nt(a, b):
    return lax.dot_general(a, b, (((1,), (1,)), ((), ())), preferred_element_type=jnp.float32)


def _dot_tn(a, b):
    return lax.dot_general(a, b, (((0,), (0,)), ((), ())), preferred_element_type=jnp.float32)


def _split3(x):
    hi = _bf(x)
    r1 = x - hi.astype(jnp.float32)
    mid = _bf(r1)
    lo = _bf(r1 - mid.astype(jnp.float32))
    return hi, mid, lo


def _silu(x):
    return x * (1.0 / (1.0 + jnp.exp(-x)))


def _sigmoid(x):
    return 1.0 / (1.0 + jnp.exp(-x))


def _mod_kernel(c_ref, w_ref, b_ref, o_ref):
    c = c_ref[...]
    ca = _silu(c)
    acc = None
    for t in _split3(ca):
        for u in _split3(w_ref[0]):
            part = _dot(t, u)
            acc = part if acc is None else acc + part
    o_ref[0] = acc + b_ref[0]


def _mod_call(c8, ada_w, ada_b3):
    depth, d, n = ada_w.shape
    tn = 1024
    return pl.pallas_call(
        _mod_kernel,
        grid=(depth, n // tn),
        in_specs=[pl.BlockSpec((8, d), lambda l, j: (0, 0)),
                  pl.BlockSpec((1, d, tn), lambda l, j: (l, 0, j)),
                  pl.BlockSpec((1, 1, tn), lambda l, j: (l, 0, j))],
        out_specs=pl.BlockSpec((1, 8, tn), lambda l, j: (l, 0, j)),
        out_shape=jax.ShapeDtypeStruct((depth, 8, n), jnp.float32),
        compiler_params=_cparams(2),
        name="adaln_mod",
    )(c8, ada_w, ada_b3)


def _tab_kernel(pos_ref, rf_ref, rs_ref, df_ref, ds_ref, rc_o, rsn_o, dc_o, dsn_o):
    pos = pos_ref[...].astype(jnp.float32)
    ang = pos * rf_ref[...]
    rc_o[...] = jnp.cos(ang)
    rsn_o[...] = jnp.sin(ang) * rs_ref[...]
    ang = pos * df_ref[...]
    dc_o[...] = jnp.cos(ang)
    dsn_o[...] = jnp.sin(ang) * ds_ref[...]


def _rope_rows():
    half = RET_DK // 2
    f = RET_THETA ** (-jnp.arange(half, dtype=jnp.float32) * 2.0 / RET_DK)
    rf = jnp.tile(jnp.concatenate([f, f]), RET_HEADS)[None, :]
    rs = jnp.tile(jnp.concatenate([-jnp.ones(half), jnp.ones(half)]), RET_HEADS)[None, :].astype(jnp.float32)
    half = DSA_ROT // 2
    f = ROPE_THETA ** (-jnp.arange(half, dtype=jnp.float32) * 2.0 / DSA_ROT)
    z = jnp.zeros(DSA_DH - DSA_ROT, jnp.float32)
    df = jnp.tile(jnp.concatenate([f, f, z]), 2)[None, :]
    ds = jnp.tile(jnp.concatenate([-jnp.ones(half), jnp.ones(half), z]), 2)[None, :].astype(jnp.float32)
    return rf, rs, df, ds


def _tab_call(pos_col):
    s = pos_col.shape[0]
    tm = min(1024, s)
    rf, rs, df, ds = _rope_rows()
    row = lambda w: pl.BlockSpec((1, w), lambda i: (0, 0))
    out = lambda w: pl.BlockSpec((tm, w), lambda i: (i, 0))
    return pl.pallas_call(
        _tab_kernel,
        grid=(s // tm,),
        in_specs=[pl.BlockSpec((tm, 1), lambda i: (i, 0)), row(256), row(256), row(128), row(128)],
        out_specs=[out(256), out(256), out(128), out(128)],
        out_shape=[jax.ShapeDtypeStruct((s, 256), jnp.float32), jax.ShapeDtypeStruct((s, 256), jnp.float32),
                   jax.ShapeDtypeStruct((s, 128), jnp.float32), jax.ShapeDtypeStruct((s, 128), jnp.float32)],
        compiler_params=_cparams(1),
        name="rope_tables",
    )(pos_col, rf, rs, df, ds)


def _swap_halves(x, half, period):
    w = x.shape[-1]
    lane = lax.broadcasted_iota(jnp.int32, x.shape, x.ndim - 1) & (period - 1)
    up = pltpu.roll(x, w - half, x.ndim - 1)
    dn = pltpu.roll(x, half, x.ndim - 1)
    return jnp.where(lane < half, up, dn)


def _rope(x, cos, sin_signed, half, period):
    return x * cos + _swap_halves(x, half, period) * sin_signed


def _proj_kernel(x_ref, pre_ref, sc_ref, sh_ref, w_ref, o_ref):
    x = x_ref[...]
    xn = x * lax.rsqrt(jnp.mean(x * x, axis=-1, keepdims=True) + RMS_EPS)
    h = xn * pre_ref[...] * (1.0 + sc_ref[...]) + sh_ref[...]
    o_ref[...] = _dot(_bf(h), w_ref[...])


def _proj_call(x2, pre, scale, shift, w_pack):
    s, d = x2.shape
    tm, tn = min(512, s), 2048
    vec = pl.BlockSpec((1, d), lambda j, i: (0, 0))
    return pl.pallas_call(
        _proj_kernel,
        grid=(P_WIDTH // tn, s // tm),
        in_specs=[pl.BlockSpec((tm, d), lambda j, i: (i, 0)), vec, vec, vec,
                  pl.BlockSpec((d, tn), lambda j, i: (0, j))],
        out_specs=pl.BlockSpec((tm, tn), lambda j, i: (i, j)),
        out_shape=jax.ShapeDtypeStruct((s, P_WIDTH), jnp.float32),
        compiler_params=_cparams(2),
        name="in_proj",
    )(x2, pre, scale, shift, w_pack)


def _pspec(tm, name, width):
    blk = PCOL[name] // width
    assert PCOL[name] % width == 0
    return pl.BlockSpec((tm, width), lambda i: (i, blk))


def _ret_log_g(h):
    return math.log1p(-(2.0 ** (-5.0 - h)))


def _ret_kernel(q_ref, k_ref, v_ref, g_ref, cos_ref, sin_ref, o_ref,
                state_ref, decay_ref, qdec_ref, kend_ref, *, chunks):
    C = RET_CHUNK

    @pl.when(pl.program_id(0) == 0)
    def _():
        state_ref[...] = jnp.zeros_like(state_ref)
        ii = lax.broadcasted_iota(jnp.int32, (C, C), 0)
        jj = lax.broadcasted_iota(jnp.int32, (C, C), 1)
        rel = (ii - jj).astype(jnp.float32)
        row = lax.broadcasted_iota(jnp.int32, (C, RET_DK), 0).astype(jnp.float32)
        for h in range(RET_HEADS):
            lg = _ret_log_g(h)
            decay_ref[h] = jnp.where(rel >= 0, jnp.exp(lg * jnp.maximum(rel, 0.0)), 0.0)
            qdec_ref[:, h * RET_DK:(h + 1) * RET_DK] = jnp.exp((row + 1.0) * lg)
            kend_ref[:, h * RET_DK:(h + 1) * RET_DK] = jnp.exp((C - 1.0 - row) * lg)

    for c in range(chunks):
        rows = slice(c * C, (c + 1) * C)
        cos, sin = cos_ref[rows, :], sin_ref[rows, :]
        q = _rope(q_ref[rows, :], cos, sin, RET_DK // 2, RET_DK) * (RET_DK ** -0.5)
        k = _rope(k_ref[rows, :], cos, sin, RET_DK // 2, RET_DK)
        qd = _bf(q * qdec_ref[...])
        kd = _bf(k * kend_ref[...])
        qb, kb = _bf(q), _bf(k)
        v = v_ref[rows, :]
        g = g_ref[rows, :]
        for h in range(RET_HEADS):
            dk = slice(h * RET_DK, (h + 1) * RET_DK)
            dv = slice(h * RET_DV, (h + 1) * RET_DV)
            vh = _bf(v[:, dv])
            scores = _dot_nt(qb[:, dk], kb[:, dk]) * decay_ref[h]
            st = state_ref[h]
            o = _dot(_bf(scores), vh) + _dot(qd[:, dk], _bf(st))
            state_ref[h] = math.exp(C * _ret_log_g(h)) * st + _dot_tn(kd[:, dk], vh)
            o = o * lax.rsqrt(jnp.mean(o * o, axis=-1, keepdims=True) + RMS_EPS)
            o_ref[rows, dv] = _bf(o * _silu(g[:, dv]))


def _ret_call(p, rcos, rsin):
    s = p.shape[0]
    chunks = 2
    tm = RET_CHUNK * chunks
    tab = pl.BlockSpec((tm, 256), lambda i: (i, 0))
    return pl.pallas_call(
        functools.partial(_ret_kernel, chunks=chunks),
        grid=(s // tm,),
        in_specs=[_pspec(tm, "ret_q", 256), _pspec(tm, "ret_k", 256), _pspec(tm, "ret_v", 512),
                  _pspec(tm, "ret_g", 512), tab, tab],
        out_specs=pl.BlockSpec((tm, 512), lambda i: (i, 0)),
        out_shape=jax.ShapeDtypeStruct((s, 512), jnp.bfloat16),
        scratch_shapes=[pltpu.VMEM((RET_HEADS, RET_DK, RET_DV), jnp.float32),
                        pltpu.VMEM((RET_HEADS, RET_CHUNK, RET_CHUNK), jnp.float32),
                        pltpu.VMEM((RET_CHUNK, RET_HEADS * RET_DK), jnp.float32),
                        pltpu.VMEM((RET_CHUNK, RET_HEADS * RET_DK), jnp.float32)],
        compiler_params=_cparams(1),
        name="retention",
    )(p, p, p, p, rcos, rsin)


def _gla_kernel(q_ref, k_ref, v_ref, g_ref, a_ref, wlr_ref, blr_ref, o_ref, state_ref, *, chunks):
    C = GLA_CHUNK

    @pl.when(pl.program_id(0) == 0)
    def _():
        state_ref[...] = jnp.zeros_like(state_ref)

    ii = lax.broadcasted_iota(jnp.int32, (C, C), 0)
    jj = lax.broadcasted_iota(jnp.int32, (C, C), 1)
    causal = jj <= ii
    tril = _bf(jnp.where(causal, 1.0, 0.0))
    wlr = wlr_ref[...]
    w_hi, w_mid, w_lo = _split3(wlr)

    for c in range(chunks):
        rows = slice(c * C, (c + 1) * C)
        a_hi, a_mid, a_lo = _split3(a_ref[rows, :])
        z = (_dot(a_hi, w_hi) + (_dot(a_hi, w_mid) + _dot(a_mid, w_hi))
             + (_dot(a_hi, w_lo) + _dot(a_mid, w_mid) + _dot(a_lo, w_hi))) + blr_ref[...]
        log_a = (jnp.minimum(z, 0.0) - jnp.log1p(jnp.exp(-jnp.abs(z)))) * (1.0 / GLA_TAU)
        l_hi, l_mid, l_lo = _split3(log_a)
        bcum = _dot(tril, l_hi) + _dot(tril, l_mid) + _dot(tril, l_lo)
        b_mid = bcum[C // 2 - 1:C // 2, :]
        b_last = bcum[C - 1:C, :]
        q = q_ref[rows, :] * (GLA_DK ** -0.5)
        k = k_ref[rows, :]
        qt = _bf(q * jnp.exp(bcum - b_mid))
        kt = _bf(k * jnp.exp(b_mid - bcum))
        qg = _bf(q * jnp.exp(bcum))
        kd = _bf(k * jnp.exp(b_last - bcum))
        e_last = jnp.exp(b_last)
        v = v_ref[rows, :]
        g = g_ref[rows, :]
        for h in range(GLA_HEADS):
            dk = slice(h * GLA_DK, (h + 1) * GLA_DK)
            dv = slice(h * GLA_DV, (h + 1) * GLA_DV)
            vh = _bf(v[:, dv])
            attn = jnp.where(causal, _dot_nt(qt[:, dk], kt[:, dk]), 0.0)
            st = state_ref[h]
            o = _dot(_bf(attn), vh) + _dot_nt(qg[:, dk], _bf(st))
            state_ref[h] = e_last[:, dk] * st + _dot_tn(vh, kd[:, dk])
            o = o * lax.rsqrt(jnp.mean(o * o, axis=-1, keepdims=True) + RMS_EPS)
            o_ref[rows, dv] = _bf(o * _silu(g[:, dv]))


def _gla_call(p, wlr_pad, blr):
    s = p.shape[0]
    chunks = 4
    tm = GLA_CHUNK * chunks
    return pl.pallas_call(
        functools.partial(_gla_kernel, chunks=chunks),
        grid=(s // tm,),
        in_specs=[_pspec(tm, "gla_q", 256), _pspec(tm, "gla_k", 256), _pspec(tm, "gla_v", 512),
                  _pspec(tm, "gla_g", 512), _pspec(tm, "gla_a", 128),
                  pl.BlockSpec((128, 256), lambda i: (0, 0)), pl.BlockSpec((1, 256), lambda i: (0, 0))],
        out_specs=pl.BlockSpec((tm, 512), lambda i: (i, 0)),
        out_shape=jax.ShapeDtypeStruct((s, 512), jnp.bfloat16),
        scratch_shapes=[pltpu.VMEM((GLA_HEADS, GLA_DV, GLA_DK), jnp.float32)],
        compiler_params=_cparams(1),
        name="gla",
    )(p, p, p, p, p, wlr_pad, blr)


def _place_head(pair, src_pos, dst_pos):
    lane = lax.broadcasted_iota(jnp.int32, pair.shape, 1)
    if src_pos != dst_pos:
        pair = pltpu.roll(pair, 64, 1)
    keep = (lane < 64) if dst_pos == 0 else (lane >= 64)
    return jnp.where(keep, pair, 0.0)


def _dprep_kernel(q_ref, k_ref, v_ref, iq_ref, ikw_ref, cos_ref, sin_ref,
                  qp_o, k_o, v_o, iqp_o, ik_o, w_o):
    cos, sin = cos_ref[...], sin_ref[...]
    half = DSA_ROT // 2
    cos4 = jnp.concatenate([cos] * 4, axis=1)
    sin4 = jnp.concatenate([sin] * 4, axis=1)
    q = _rope(q_ref[...], cos4, sin4, half, DSA_DH) * (DSA_DH ** -0.5 * LOG2E)
    for h in range(DSA_HEADS):
        pair = q[:, (h // 2) * 128:(h // 2 + 1) * 128]
        qp_o[h] = _bf(_place_head(pair, h % 2, h // (DSA_HEADS // DSA_KV_HEADS)))
    k_o[...] = _bf(_rope(k_ref[...], cos, sin, half, DSA_DH))
    v = v_ref[...]
    lane = lax.broadcasted_iota(jnp.int32, v.shape, 1)
    v_o[0] = _bf(jnp.where(lane < DSA_DH, v, 1.0))
    v_o[1] = _bf(jnp.where(lane < DSA_DH, pltpu.roll(v, DSA_DH, 1), 1.0))
    iq = _rope(iq_ref[...], cos4[:, :256], sin4[:, :256], half, IDX_DH)
    for h in range(IDX_HEADS):
        pair = iq[:, (h // 2) * 128:(h // 2 + 1) * 128]
        iqp_o[h] = _bf(_place_head(pair, h % 2, 0))
    ikw = ikw_ref[...]
    lane = lax.broadcasted_iota(jnp.int32, ikw.shape, 1)
    ik_o[...] = _bf(jnp.where(lane < IDX_DH, _rope(ikw, cos, sin, half, DSA_DH), 0.0))
    wscale = (IDX_HEADS ** -0.5) * (IDX_DH ** -0.5)
    w_o[...] = pltpu.roll(ikw, 128 - IDX_DH, 1) * wscale


def _dprep_call(p, dcos, dsin):
    s = p.shape[0]
    tm = min(512, s)
    tab = pl.BlockSpec((tm, 128), lambda i: (i, 0))
    return pl.pallas_call(
        _dprep_kernel,
        grid=(s // tm,),
        in_specs=[_pspec(tm, "dsa_q", 512), _pspec(tm, "dsa_k", 128), _pspec(tm, "dsa_v", 128),
                  _pspec(tm, "idx_q", 256), _pspec(tm, "idx_kw", 128), tab, tab],
        out_specs=[pl.BlockSpec((DSA_HEADS, tm, 128), lambda i: (0, i, 0)), tab,
                   pl.BlockSpec((DSA_KV_HEADS, tm, 128), lambda i: (0, i, 0)),
                   pl.BlockSpec((IDX_HEADS, tm, 128), lambda i: (0, i, 0)), tab, tab],
        out_shape=[jax.ShapeDtypeStruct((DSA_HEADS, s, 128), jnp.bfloat16),
                   jax.ShapeDtypeStruct((s, 128), jnp.bfloat16),
                   jax.ShapeDtypeStruct((DSA_KV_HEADS, s, 128), jnp.bfloat16),
                   jax.ShapeDtypeStruct((IDX_HEADS, s, 128), jnp.bfloat16),
                   jax.ShapeDtypeStruct((s, 128), jnp.bfloat16),
                   jax.ShapeDtypeStruct((s, 128), jnp.float32)],
        compiler_params=_cparams(1),
        name="dsa_prep",
    )(p, p, p, p, p, dcos, dsin)


DSA_QB = 256
IDX_TK = 256
ATT_TK = 1024
SEL_ROWS = 128
SEL_LANES = 512
CAND_DEPTH = 10
CAND_SPLIT = 2
CAND_W = CAND_DEPTH * CAND_SPLIT * LANES
TIE_CHUNK = 8
NO_TIE_LIMIT = 1e9
assert CAND_W % SEL_LANES == 0 and 8 % CAND_SPLIT == 0


def _key_to_f32(key):
    bits = jnp.where(key >= 0, key, (0 - key) | INT_MIN)
    return lax.bitcast_convert_type(bits, jnp.float32)


def _count_lanes(keys_ref, r0, n_steps, cand, strict=False):
    def body(s, acc):
        c0 = s * SEL_LANES if isinstance(s, int) else pl.multiple_of(s * SEL_LANES, SEL_LANES)
        for u in range(SEL_LANES // LANES):
            kk = keys_ref[pl.ds(r0, SEL_ROWS), pl.ds(c0 + u * LANES, LANES)]
            acc = acc + jnp.where((kk > cand) if strict else (kk >= cand), 1.0, 0.0)
        return acc
    acc = jnp.zeros((SEL_ROWS, LANES), jnp.float32)
    if isinstance(n_steps, int):
        for s in range(n_steps):
            acc = body(s, acc)
        return acc
    return lax.fori_loop(0, n_steps, body, acc)


def _row_total(lane_counts):
    ones = jnp.ones((LANES, LANES), jnp.bfloat16)
    return _dot(_bf(lane_counts), ones)


def _f32_to_key(x):
    bits = lax.bitcast_convert_type(x, jnp.int32)
    return jnp.where(bits >= 0, bits, INT_MIN - bits)


def _search_window(cand_ref, lo_ref, final_ref):
    hi = lw = None
    for v in range(CAND_SPLIT):
        top = cand_ref[:, v * CAND_DEPTH * LANES:(v * CAND_DEPTH + 1) * LANES]
        hi = top if hi is None else jnp.maximum(hi, top)
        lw = top if lw is None else jnp.minimum(lw, top)
    k_hi = _f32_to_key(jnp.max(hi, axis=-1, keepdims=True))
    k_lw = _f32_to_key(jnp.min(lw, axis=-1, keepdims=True))
    nbits = 32 - lax.clz(k_hi ^ k_lw)
    nb = jnp.max(nbits.astype(jnp.float32)).astype(jnp.int32)
    low = lax.shift_left(jnp.int32(1), jnp.minimum(nb, 31)) - 1
    lo0 = jnp.where(nbits == 0, k_lw, jnp.where(nb >= 32, INT_MIN, k_lw & ~low))
    lo_ref[...] = jnp.broadcast_to(lo0, lo_ref.shape)
    final_ref[...] = jnp.broadcast_to(jnp.where(nbits == 0, 1.0, 0.0), final_ref.shape)
    return 32 - nb


def _radix_search(src_ref, n_steps, lo_ref, cnt_ref, topk, first_bit=None, final_ref=None):
    rows_total = lo_ref.shape[0]
    if first_bit is None:
        first_bit = jnp.int32(0)
        lo_ref[...] = jnp.full(lo_ref.shape, INT_MIN, jnp.int32)
        cnt_ref[...] = jnp.zeros(cnt_ref.shape, jnp.float32)
    else:
        cnt_ref[...] = jnp.full(cnt_ref.shape, NO_TIE_LIMIT, jnp.float32)

    def cond(c):
        b, pending = c
        return jnp.logical_and(b < 32, pending > 0)

    def body(c):
        b, _ = c
        bit = lax.shift_left(jnp.int32(1), 31 - b)
        off = None
        for rb in range(rows_total // SEL_ROWS):
            rows = pl.ds(rb * SEL_ROWS, SEL_ROWS)
            cand = lo_ref[rows, :] + bit
            cnt = _row_total(_count_lanes(src_ref, rb * SEL_ROWS, n_steps, _key_to_f32(cand)))
            ok = cnt >= float(topk)
            if final_ref is not None:
                ok = jnp.logical_and(ok, final_ref[rows, :] <= 0.0)
            lo_ref[rows, :] = jnp.where(ok, cand, lo_ref[rows, :])
            cnt = jnp.where(ok, cnt, cnt_ref[rows, :])
            cnt_ref[rows, :] = cnt
            miss = jnp.abs(cnt - float(topk))
            if final_ref is not None:
                miss = jnp.where(final_ref[rows, :] > 0.0, 0.0, miss)
            off = miss if off is None else jnp.maximum(off, miss)
        return b + 1, (jnp.max(off) > 0.0).astype(jnp.int32)
    lax.while_loop(cond, body, (first_bit, jnp.int32(1)))


def _lane_candidates(keys_ref, cand_ref, n_groups):
    n_stack = CAND_DEPTH * CAND_SPLIT

    def rg_body(rg, carry):
        r0 = pl.multiple_of(rg * 8, 8)

        def col_body(s, st):
            st = list(st)
            c0 = pl.multiple_of(s * 8 * LANES, 8 * LANES)
            for u in range(8):
                x = keys_ref[pl.ds(r0, 8), pl.ds(c0 + u * LANES, LANES)]
                base = (u % CAND_SPLIT) * CAND_DEPTH
                for d in range(CAND_DEPTH):
                    cur = st[base + d]
                    st[base + d] = jnp.maximum(cur, x)
                    x = jnp.minimum(cur, x)
            return tuple(st)
        init = tuple(jnp.full((8, LANES), -jnp.inf, jnp.float32) for _ in range(n_stack))
        st = lax.fori_loop(0, n_groups, col_body, init)
        for k in range(n_stack):
            cand_ref[pl.ds(r0, 8), k * LANES:(k + 1) * LANES] = st[k]
        return carry
    lax.fori_loop(0, cand_ref.shape[0] // 8, rg_body, 0)


def _ties_to_keep(src_ref, n_steps, lo_ref, cnt_ref, part_ref, need_ref, topk):
    rows_total = lo_ref.shape[0]
    surplus = jnp.where(lo_ref[...] > NEG_INF_KEY, cnt_ref[...] - float(topk), 0.0)
    need_ref[...] = jnp.full(need_ref.shape, NO_TIE_LIMIT, jnp.float32)

    @pl.when(jnp.max(surplus) > 0.0)
    def _():
        for rb in range(rows_total // SEL_ROWS):
            rows = pl.ds(rb * SEL_ROWS, SEL_ROWS)
            thr = _key_to_f32(jnp.maximum(lo_ref[rows, :], NEG_INF_KEY))
            part_ref[rows, :] = _count_lanes(src_ref, rb * SEL_ROWS, n_steps, thr, strict=True)
        above = _row_total(part_ref[...])
        need_ref[...] = jnp.where(surplus > 0.0, float(topk) - above, NO_TIE_LIMIT)


def _tie_index_bound(keys_ref, thr_ref, need_ref, seen_ref, sig_ref, n_chunks):
    kk_i = lax.broadcasted_iota(jnp.int32, (LANES, LANES), 0)
    jj_i = lax.broadcasted_iota(jnp.int32, (LANES, LANES), 1)
    tri = _bf(jnp.where(kk_i <= jj_i, 1.0, 0.0))
    seen_ref[...] = jnp.zeros(seen_ref.shape, jnp.float32)

    def cond(c):
        j, pending = c
        return jnp.logical_and(j < n_chunks, pending > 0)

    def body(c):
        j, _ = c
        thr, need = thr_ref[...], need_ref[...]
        seen, sig = seen_ref[...], sig_ref[...]
        for u in range(TIE_CHUNK):
            c0 = pl.multiple_of((j * TIE_CHUNK + u) * LANES, LANES)
            tie = jnp.where(keys_ref[:, pl.ds(c0, LANES)] == thr, 1.0, 0.0)
            rank = seen + _dot(_bf(tie), tri)
            after = seen + _row_total(tie)
            before = _row_total(jnp.where(rank < need, 1.0, 0.0))
            here = jnp.where(seen < need, jnp.where(after >= need, 1.0, 0.0), 0.0)
            sig = jnp.where(here > 0.0, c0.astype(jnp.float32) + before, sig)
            seen = after
        seen_ref[...] = seen
        sig_ref[...] = sig
        waiting = jnp.where(need < NO_TIE_LIMIT, jnp.where(seen < need, 1.0, 0.0), 0.0)
        return j + 1, (jnp.max(waiting) > 0.0).astype(jnp.int32)
    lax.while_loop(cond, body, (jnp.int32(0), jnp.int32(1)))


def _dsa_kernel(qp_ref, iqp_ref, w_ref, g_ref, k_ref, va_ref, ik_ref, o_ref,
                keys_ref, cand_ref, lo_ref, thr_ref, cnt_ref, part_ref, need_ref, sig_ref,
                m_ref, acc_ref, full_ref, *, topk):
    QB, TK = DSA_QB, IDX_TK
    G = DSA_HEADS // DSA_KV_HEADS
    i = pl.program_id(0)
    n_idx = i + 1
    n_att = (n_idx * IDX_TK + ATT_TK - 1) // ATT_TK
    n_sel = n_att * (ATT_TK // SEL_LANES)
    row_id = lax.broadcasted_iota(jnp.int32, (QB, TK), 0)
    col_id = lax.broadcasted_iota(jnp.int32, (QB, TK), 1)

    iq = iqp_ref[...].reshape(IDX_HEADS * QB, LANES)
    wb = [jnp.broadcast_to(w_ref[:, h:h + 1], (QB, TK)) for h in range(IDX_HEADS)]

    def score_tile(j, diag):
        c0 = pl.multiple_of(j * TK, TK)
        s_all = _dot_nt(iq, ik_ref[pl.ds(c0, TK), :])
        score = None
        for h in range(IDX_HEADS):
            term = wb[h] * jnp.maximum(s_all[h * QB:(h + 1) * QB], 0.0)
            score = term if score is None else score + term
        if diag:
            score = jnp.where(col_id <= row_id, score, -jnp.inf)
        keys_ref[:, pl.ds(c0, TK)] = score

    def score_body(j, carry):
        score_tile(j, False)
        return carry
    lax.fori_loop(0, i, score_body, 0)
    score_tile(i, True)

    def blank_body(j, carry):
        c0 = pl.multiple_of(j * TK, TK)
        keys_ref[:, pl.ds(c0, TK)] = jnp.full((QB, TK), -jnp.inf, jnp.float32)
        return carry
    n_vis = n_idx * TK
    windows = [w for w in (CAND_W, 2 * CAND_W) if w <= keys_ref.shape[1]]
    att_end = n_att * (ATT_TK // TK)
    blank_end = att_end
    for w in reversed(windows):
        blank_end = jnp.where(n_vis <= w, jnp.maximum(att_end, w // TK), blank_end)
    lax.fori_loop(n_idx, blank_end, blank_body, 0)

    full_ref[0] = jnp.int32(1)

    @pl.when(n_att * ATT_TK > 2 * CAND_W)
    def _():
        _lane_candidates(keys_ref, cand_ref, n_att * (ATT_TK // (8 * LANES)))
        first_bit = _search_window(cand_ref, lo_ref, part_ref)
        _radix_search(cand_ref, CAND_W // SEL_LANES, lo_ref, cnt_ref, topk,
                      first_bit=first_bit, final_ref=part_ref)
        thr = _key_to_f32(jnp.maximum(lo_ref[...], NEG_INF_KEY))
        exact = cnt_ref[...] == float(topk)
        lost = None
        for v in range(CAND_SPLIT):
            last = cand_ref[:, (v * CAND_DEPTH + CAND_DEPTH - 1) * LANES:(v * CAND_DEPTH + CAND_DEPTH) * LANES]
            hit = jnp.where(last > thr, 1.0, jnp.where(last == thr, jnp.where(exact, 1.0, 0.0), 0.0))
            lost = hit if lost is None else jnp.maximum(lost, hit)
        full_ref[0] = (jnp.max(lost) > 0.0).astype(jnp.int32)
        _ties_to_keep(cand_ref, CAND_W // SEL_LANES, lo_ref, cnt_ref, part_ref, need_ref, topk)

    @pl.when(full_ref[0] > 0)
    def _():
        def search_rows(n_steps):
            _radix_search(keys_ref, n_steps, lo_ref, cnt_ref, topk)
            _ties_to_keep(keys_ref, n_steps, lo_ref, cnt_ref, part_ref, need_ref, topk)
        below = 0
        for w in windows:
            pl.when(jnp.logical_and(n_vis > below, n_vis <= w))(
                functools.partial(search_rows, w // SEL_LANES))
            below = w
        pl.when(n_vis > below)(functools.partial(search_rows, n_sel))

    tau = lo_ref[...]
    found = tau > NEG_INF_KEY
    thr_ref[...] = jnp.where(found, _key_to_f32(jnp.maximum(tau, NEG_INF_KEY)), F32_LOWEST)
    sig_ref[...] = jnp.full(sig_ref.shape, NO_TIE_LIMIT, jnp.float32)

    @pl.when(jnp.min(need_ref[...]) < NO_TIE_LIMIT)
    def _():
        _tie_index_bound(keys_ref, thr_ref, need_ref, part_ref, sig_ref, n_att * (ATT_TK // (TIE_CHUNK * LANES)))

    m_ref[...] = jnp.full(m_ref.shape, NEG_INIT, jnp.float32)
    acc_ref[...] = jnp.zeros(acc_ref.shape, jnp.float32)
    n_grp = ATT_TK // LANES
    lane_f = lax.broadcasted_iota(jnp.int32, (QB, LANES), 1).astype(jnp.float32)

    def attn_body(j, carry):
        c0 = pl.multiple_of(j * ATT_TK, ATT_TK)
        tau_b = thr_ref[...]
        sig_rel = sig_ref[...] - c0.astype(jnp.float32)
        parts = []
        for u in range(n_grp):
            kk = keys_ref[:, pl.ds(c0 + u * LANES, LANES)]
            tie_bias = jnp.where(lane_f + float(u * LANES) <= sig_rel, 0.0, NEG_MASK)
            parts.append(jnp.where(kk > tau_b, 0.0, jnp.where(kk == tau_b, tie_bias, NEG_MASK)))
        bias = jnp.concatenate(parts, axis=1)
        kt = k_ref[pl.ds(c0, ATT_TK), :]
        for n in range(DSA_KV_HEADS):
            qn = qp_ref[n * G:(n + 1) * G].reshape(G * QB, LANES)
            logits = _dot_nt(qn, kt)
            ps = []
            for gq in range(G):
                h = n * G + gq
                lm = logits[gq * QB:(gq + 1) * QB] + bias
                tmax = lm[:, 0:LANES]
                for u in range(1, n_grp):
                    tmax = jnp.maximum(tmax, lm[:, u * LANES:(u + 1) * LANES])
                m_old = m_ref[h]
                m_new = jnp.maximum(m_old, jnp.max(tmax, axis=-1, keepdims=True))
                ps.append(jnp.concatenate(
                    [_bf(jnp.exp2(lm[:, u * LANES:(u + 1) * LANES] - m_new)) for u in range(n_grp)], axis=1))
                m_ref[h] = m_new
                acc_ref[h] = jnp.exp2(m_old - m_new) * acc_ref[h]
            pv = _dot(jnp.concatenate(ps, axis=0), va_ref[n, pl.ds(c0, ATT_TK), :])
            for gq in range(G):
                h = n * G + gq
                acc_ref[h] = acc_ref[h] + pv[gq * QB:(gq + 1) * QB]
        return carry
    lax.fori_loop(0, n_att, attn_body, 0)

    lane = lax.broadcasted_iota(jnp.int32, (QB, LANES), 1)
    g = g_ref[...]
    for pair in range(DSA_HEADS // 2):
        halves = []
        for pos in range(2):
            a = acc_ref[2 * pair + pos]
            halves.append(a * (1.0 / pltpu.roll(a, DSA_DH, 1)))
        o = jnp.where(lane < DSA_DH, halves[0], pltpu.roll(halves[1], DSA_DH, 1))
        sl = slice(pair * LANES, (pair + 1) * LANES)
        o_ref[:, sl] = _bf(o * _silu(g[:, sl]))


def _dsa_call(p, qp, kk, va, iqp, ik, w4):
    s = p.shape[0]
    topk = min(TOPK_MAX, s // 4)
    QB = DSA_QB
    assert IDX_TK == QB and s % ATT_TK == 0 and ATT_TK % SEL_LANES == 0 and s // LANES <= 256
    once = pl.Buffered(1)
    return pl.pallas_call(
        functools.partial(_dsa_kernel, topk=topk),
        grid=(s // QB,),
        in_specs=[pl.BlockSpec((DSA_HEADS, QB, LANES), lambda i: (0, i, 0)),
                  pl.BlockSpec((IDX_HEADS, QB, LANES), lambda i: (0, i, 0)),
                  pl.BlockSpec((QB, LANES), lambda i: (i, 0)),
                  _pspec(QB, "dsa_g", 512),
                  pl.BlockSpec((s, LANES), lambda i: (0, 0), pipeline_mode=once),
                  pl.BlockSpec((DSA_KV_HEADS, s, LANES), lambda i: (0, 0, 0), pipeline_mode=once),
                  pl.BlockSpec((s, LANES), lambda i: (0, 0), pipeline_mode=once)],
        out_specs=pl.BlockSpec((QB, 512), lambda i: (i, 0)),
        out_shape=jax.ShapeDtypeStruct((s, 512), jnp.bfloat16),
        scratch_shapes=[pltpu.VMEM((QB, s), jnp.float32),
                        pltpu.VMEM((QB, CAND_W), jnp.float32),
                        pltpu.VMEM((QB, LANES), jnp.int32),
                        pltpu.VMEM((QB, LANES), jnp.float32),
                        pltpu.VMEM((QB, LANES), jnp.float32),
                        pltpu.VMEM((QB, LANES), jnp.float32),
                        pltpu.VMEM((QB, LANES), jnp.float32),
                        pltpu.VMEM((QB, LANES), jnp.float32),
                        pltpu.VMEM((DSA_HEADS, QB, LANES), jnp.float32),
                        pltpu.VMEM((DSA_HEADS, QB, LANES), jnp.float32),
                        pltpu.SMEM((1,), jnp.int32)],
        compiler_params=_cparams(1, vmem_mb=56),
        name="dsa",
    )(qp, iqp, w4, p, kk, va, ik)


def _merge_kernel(x_ref, ret_ref, dsa_ref, gl_ref, m_ref, wr_ref, wd_ref, wg_ref, wo_ref,
                  post_ref, gate_ref, o_ref):
    d = D_MODEL
    y = (_sigmoid(m_ref[:, 0:d]) * _dot(ret_ref[...], wr_ref[...])
         + _sigmoid(m_ref[:, d:2 * d]) * _dot(dsa_ref[...], wd_ref[...])
         + _sigmoid(m_ref[:, 2 * d:3 * d]) * _dot(gl_ref[...], wg_ref[...]))
    y = _dot(_bf(y), wo_ref[...])
    yn = y * lax.rsqrt(jnp.mean(y * y, axis=-1, keepdims=True) + RMS_EPS) * post_ref[...]
    o_ref[...] = x_ref[...] + gate_ref[...] * yn


def _merge_call(x2, ret, dsa, gl, p, wr, wd, wg, wo, post, gate):
    s, d = x2.shape
    tm = min(512, s)
    rows = lambda w: pl.BlockSpec((tm, w), lambda i: (i, 0))
    whole = lambda a: pl.BlockSpec(a.shape, lambda i: (0, 0))
    return pl.pallas_call(
        _merge_kernel,
        grid=(s // tm,),
        in_specs=[rows(d), rows(512), rows(512), rows(512), _pspec(tm, "merge", 3072),
                  whole(wr), whole(wd), whole(wg), whole(wo), whole(post), whole(gate)],
        out_specs=rows(d),
        out_shape=jax.ShapeDtypeStruct((s, d), jnp.float32),
        compiler_params=_cparams(1),
        name="merge_out",
    )(x2, ret, dsa, gl, p, wr, wd, wg, wo, post, gate)


def _pack_w_in(w_in):
    depth, d, _ = w_in.shape
    w_in = w_in.astype(jnp.bfloat16)
    zeros = lambda n: jnp.zeros((depth, d, n), w_in.dtype)
    src = lambda name: w_in[:, :, _SRC[name][0]:_SRC[name][0] + _SRC[name][1]]
    pieces, at = [], 0
    for name, width in _PACK:
        assert at == PCOL[name]
        if name == "idx_kw":
            cols = [src("idx_k"), src("idx_w"), zeros(width - IDX_DH - IDX_HEADS)]
        elif name == "gla_a":
            cols = [src("gla_a"), zeros(width - GLA_RANK)]
        else:
            cols = [src(name)]
        pieces += cols
        at += width
    pieces.append(zeros(P_WIDTH - at))
    return jnp.concatenate(pieces, axis=-1).astype(jnp.bfloat16)


def kernel(x, c, positions, ada_w, ada_b, pre_norm, post_norm, w_in, gla_w_lr, gla_b_lr,
           w_br_ret, w_br_dsa, w_br_gla, w_out):
    b, s, d = x.shape
    assert b == 1 and d == D_MODEL
    depth = ada_w.shape[0]
    x2 = x.reshape(s, d)
    mod = _mod_call(jnp.broadcast_to(c, (8, d)), ada_w, ada_b.reshape(depth, 1, 3 * d))[:, 0:1, :]
    rcos, rsin, dcos, dsin = _tab_call(positions.reshape(s, 1))
    w_pack = _pack_w_in(w_in)
    wlr_pad = jnp.pad(gla_w_lr, ((0, 0), (0, LANES - GLA_RANK), (0, 0)))
    for l in range(depth):
        shift, scale, gate = mod[l, :, 0:d], mod[l, :, d:2 * d], mod[l, :, 2 * d:3 * d]
        p = _proj_call(x2, pre_norm[l][None, :], scale, shift, w_pack[l])
        ret = _ret_call(p, rcos, rsin)
        gl = _gla_call(p, wlr_pad[l], gla_b_lr[l][None, :])
        qp, kk, va, iqp, ik, w4 = _dprep_call(p, dcos, dsin)
        dsa = _dsa_call(p, qp, kk, va, iqp, ik, w4)
        x2 = _merge_call(x2, ret, dsa, gl, p, _bf(w_br_ret[l]), _bf(w_br_dsa[l]), _bf(w_br_gla[l]),
                         _bf(w_out[l]), post_norm[l][None, :], gate)
    return x2.reshape(b, s, d)
```

```python
import functools
import math

import jax
import jax.numpy as jnp
from jax import lax
from jax.experimental import pallas as pl
from jax.experimental.pallas import tpu as pltpu

D_MODEL = 1024
DEPTH = 4
RET_HEADS, RET_DK, RET_DV, RET_CHUNK, RET_THETA = 4, 64, 128, 128, 10000.0
DSA_HEADS, DSA_KV_HEADS, DSA_DH = 8, 2, 64
DSA_ROT = DSA_DH // 4
ROPE_THETA = 500000.0
IDX_HEADS, IDX_DH = 4, 64
TOPK_MAX = 256
GLA_HEADS, GLA_DK, GLA_DV, GLA_RANK, GLA_TAU, GLA_CHUNK = 4, 64, 128, 16, 16.0, 64
RMS_EPS = 1e-6
LANES = 128

_SRC = {}
_off = 0
for _name, _w in (("ret_q", 256), ("ret_k", 256), ("ret_v", 512), ("ret_g", 512),
                  ("dsa_q", 512), ("dsa_k", 128), ("dsa_v", 128), ("dsa_g", 512),
                  ("idx_q", 256), ("idx_k", 64), ("idx_w", 4),
                  ("gla_q", 256), ("gla_k", 256), ("gla_v", 512), ("gla_g", 512), ("gla_a", 16),
                  ("merge", 3072)):
    _SRC[_name] = (_off, _w)
    _off += _w
IN_WIDTH = _off

_PACK = (("merge", 3072), ("ret_q", 256), ("ret_k", 256), ("ret_v", 512), ("ret_g", 512),
         ("dsa_q", 512), ("dsa_g", 512), ("gla_v", 512), ("gla_g", 512),
         ("gla_q", 256), ("gla_k", 256), ("idx_q", 256),
         ("dsa_k", 128), ("dsa_v", 128), ("idx_kw", 128), ("gla_a", 128))
PCOL = {}
_off = 0
for _name, _w in _PACK:
    assert _off % min(_w, 1024) == 0
    PCOL[_name] = _off
    _off += _w
P_WIDTH = 8192
assert _off <= P_WIDTH

LOG2E = math.log2(math.e)
INT_MIN = -(2 ** 31)
F32_LOWEST = -3.4028234663852886e38
NEG_INF_KEY = -0x7F800000
NEG_INIT = -1e30
NEG_MASK = -2e30


def _cparams(n_axes, vmem_mb=48):
    return pltpu.CompilerParams(dimension_semantics=("arbitrary",) * n_axes,
                                vmem_limit_bytes=vmem_mb * 1024 * 1024)


def _bf(x):
    return x.astype(jnp.bfloat16)


def _dot(a, b):
    return jnp.dot(a, b, preferred_element_type=jnp.float32)


def _dot_nt(a, b):
    return lax.dot_general(a, b, (((1,), (1,)), ((), ())), preferred_element_type=jnp.float32)


def _dot_tn(a, b):
    return lax.dot_general(a, b, (((0,), (0,)), ((), ())), preferred_element_type=jnp.float32)


def _split3(x):
    hi = _bf(x)
    r1 = x - hi.astype(jnp.float32)
    mid = _bf(r1)
    lo = _bf(r1 - mid.astype(jnp.float32))
    return hi, mid, lo


def _silu(x):
    return x * (1.0 / (1.0 + jnp.exp(-x)))


def _sigmoid(x):
    return 1.0 / (1.0 + jnp.exp(-x))


def _mod_kernel(c_ref, w_ref, b_ref, o_ref):
    c = c_ref[...]
    ca = _silu(c)
    acc = None
    for t in _split3(ca):
        for u in _split3(w_ref[0]):
            part = _dot(t, u)
            acc = part if acc is None else acc + part
    o_ref[0] = acc + b_ref[0]


def _mod_call(c8, ada_w, ada_b3):
    depth, d, n = ada_w.shape
    tn = 1024
    return pl.pallas_call(
        _mod_kernel,
        grid=(depth, n // tn),
        in_specs=[pl.BlockSpec((8, d), lambda l, j: (0, 0)),
                  pl.BlockSpec((1, d, tn), lambda l, j: (l, 0, j)),
                  pl.BlockSpec((1, 1, tn), lambda l, j: (l, 0, j))],
        out_specs=pl.BlockSpec((1, 8, tn), lambda l, j: (l, 0, j)),
        out_shape=jax.ShapeDtypeStruct((depth, 8, n), jnp.float32),
        compiler_params=_cparams(2),
        name="adaln_mod",
    )(c8, ada_w, ada_b3)


def _tab_kernel(pos_ref, rf_ref, rs_ref, df_ref, ds_ref, rc_o, rsn_o, dc_o, dsn_o):
    pos = pos_ref[...].astype(jnp.float32)
    ang = pos * rf_ref[...]
    rc_o[...] = jnp.cos(ang)
    rsn_o[...] = jnp.sin(ang) * rs_ref[...]
    ang = pos * df_ref[...]
    dc_o[...] = jnp.cos(ang)
    dsn_o[...] = jnp.sin(ang) * ds_ref[...]


def _rope_rows():
    half = RET_DK // 2
    f = RET_THETA ** (-jnp.arange(half, dtype=jnp.float32) * 2.0 / RET_DK)
    rf = jnp.tile(jnp.concatenate([f, f]), RET_HEADS)[None, :]
    rs = jnp.tile(jnp.concatenate([-jnp.ones(half), jnp.ones(half)]), RET_HEADS)[None, :].astype(jnp.float32)
    half = DSA_ROT // 2
    f = ROPE_THETA ** (-jnp.arange(half, dtype=jnp.float32) * 2.0 / DSA_ROT)
    z = jnp.zeros(DSA_DH - DSA_ROT, jnp.float32)
    df = jnp.tile(jnp.concatenate([f, f, z]), 2)[None, :]
    ds = jnp.tile(jnp.concatenate([-jnp.ones(half), jnp.ones(half), z]), 2)[None, :].astype(jnp.float32)
    return rf, rs, df, ds


def _tab_call(pos_col):
    s = pos_col.shape[0]
    tm = min(1024, s)
    rf, rs, df, ds = _rope_rows()
    row = lambda w: pl.BlockSpec((1, w), lambda i: (0, 0))
    out = lambda w: pl.BlockSpec((tm, w), lambda i: (i, 0))
    return pl.pallas_call(
        _tab_kernel,
        grid=(s // tm,),
        in_specs=[pl.BlockSpec((tm, 1), lambda i: (i, 0)), row(256), row(256), row(128), row(128)],
        out_specs=[out(256), out(256), out(128), out(128)],
        out_shape=[jax.ShapeDtypeStruct((s, 256), jnp.float32), jax.ShapeDtypeStruct((s, 256), jnp.float32),
                   jax.ShapeDtypeStruct((s, 128), jnp.float32), jax.ShapeDtypeStruct((s, 128), jnp.float32)],
        compiler_params=_cparams(1),
        name="rope_tables",
    )(pos_col, rf, rs, df, ds)


def _swap_halves(x, half, period):
    w = x.shape[-1]
    lane = lax.broadcasted_iota(jnp.int32, x.shape, x.ndim - 1) & (period - 1)
    up = pltpu.roll(x, w - half, x.ndim - 1)
    dn = pltpu.roll(x, half, x.ndim - 1)
    return jnp.where(lane < half, up, dn)


def _rope(x, cos, sin_signed, half, period):
    return x * cos + _swap_halves(x, half, period) * sin_signed


def _proj_kernel(x_ref, pre_ref, sc_ref, sh_ref, w_ref, o_ref):
    x = x_ref[...]
    xn = x * lax.rsqrt(jnp.mean(x * x, axis=-1, keepdims=True) + RMS_EPS)
    h = xn * pre_ref[...] * (1.0 + sc_ref[...]) + sh_ref[...]
    o_ref[...] = _dot(_bf(h), w_ref[...])


def _proj_call(x2, pre, scale, shift, w_pack):
    s, d = x2.shape
    tm, tn = min(512, s), 2048
    vec = pl.BlockSpec((1, d), lambda j, i: (0, 0))
    return pl.pallas_call(
        _proj_kernel,
        grid=(P_WIDTH // tn, s // tm),
        in_specs=[pl.BlockSpec((tm, d), lambda j, i: (i, 0)), vec, vec, vec,
                  pl.BlockSpec((d, tn), lambda j, i: (0, j))],
        out_specs=pl.BlockSpec((tm, tn), lambda j, i: (i, j)),
        out_shape=jax.ShapeDtypeStruct((s, P_WIDTH), jnp.float32),
        compiler_params=_cparams(2),
        name="in_proj",
    )(x2, pre, scale, shift, w_pack)


def _pspec(tm, name, width):
    blk = PCOL[name] // width
    assert PCOL[name] % width == 0
    return pl.BlockSpec((tm, width), lambda i: (i, blk))


def _ret_log_g(h):
    return math.log1p(-(2.0 ** (-5.0 - h)))


def _ret_kernel(q_ref, k_ref, v_ref, g_ref, cos_ref, sin_ref, o_ref,
                state_ref, decay_ref, qdec_ref, kend_ref, *, chunks):
    C = RET_CHUNK

    @pl.when(pl.program_id(0) == 0)
    def _():
        state_ref[...] = jnp.zeros_like(state_ref)
        ii = lax.broadcasted_iota(jnp.int32, (C, C), 0)
        jj = lax.broadcasted_iota(jnp.int32, (C, C), 1)
        rel = (ii - jj).astype(jnp.float32)
        row = lax.broadcasted_iota(jnp.int32, (C, RET_DK), 0).astype(jnp.float32)
        for h in range(RET_HEADS):
            lg = _ret_log_g(h)
            decay_ref[h] = jnp.where(rel >= 0, jnp.exp(lg * jnp.maximum(rel, 0.0)), 0.0)
            qdec_ref[:, h * RET_DK:(h + 1) * RET_DK] = jnp.exp((row + 1.0) * lg)
            kend_ref[:, h * RET_DK:(h + 1) * RET_DK] = jnp.exp((C - 1.0 - row) * lg)

    for c in range(chunks):
        rows = slice(c * C, (c + 1) * C)
        cos, sin = cos_ref[rows, :], sin_ref[rows, :]
        q = _rope(q_ref[rows, :], cos, sin, RET_DK // 2, RET_DK) * (RET_DK ** -0.5)
        k = _rope(k_ref[rows, :], cos, sin, RET_DK // 2, RET_DK)
        qd = _bf(q * qdec_ref[...])
        kd = _bf(k * kend_ref[...])
        qb, kb = _bf(q), _bf(k)
        v = v_ref[rows, :]
        g = g_ref[rows, :]
        for h in range(RET_HEADS):
            dk = slice(h * RET_DK, (h + 1) * RET_DK)
            dv = slice(h * RET_DV, (h + 1) * RET_DV)
            vh = _bf(v[:, dv])
            scores = _dot_nt(qb[:, dk], kb[:, dk]) * decay_ref[h]
            st = state_ref[h]
            o = _dot(_bf(scores), vh) + _dot(qd[:, dk], _bf(st))
            state_ref[h] = math.exp(C * _ret_log_g(h)) * st + _dot_tn(kd[:, dk], vh)
            o = o * lax.rsqrt(jnp.mean(o * o, axis=-1, keepdims=True) + RMS_EPS)
            o_ref[rows, dv] = _bf(o * _silu(g[:, dv]))


def _ret_call(p, rcos, rsin):
    s = p.shape[0]
    chunks = 2
    tm = RET_CHUNK * chunks
    tab = pl.BlockSpec((tm, 256), lambda i: (i, 0))
    return pl.pallas_call(
        functools.partial(_ret_kernel, chunks=chunks),
        grid=(s // tm,),
        in_specs=[_pspec(tm, "ret_q", 256), _pspec(tm, "ret_k", 256), _pspec(tm, "ret_v", 512),
                  _pspec(tm, "ret_g", 512), tab, tab],
        out_specs=pl.BlockSpec((tm, 512), lambda i: (i, 0)),
        out_shape=jax.ShapeDtypeStruct((s, 512), jnp.bfloat16),
        scratch_shapes=[pltpu.VMEM((RET_HEADS, RET_DK, RET_DV), jnp.float32),
                        pltpu.VMEM((RET_HEADS, RET_CHUNK, RET_CHUNK), jnp.float32),
                        pltpu.VMEM((RET_CHUNK, RET_HEADS * RET_DK), jnp.float32),
                        pltpu.VMEM((RET_CHUNK, RET_HEADS * RET_DK), jnp.float32)],
        compiler_params=_cparams(1),
        name="retention",
    )(p, p, p, p, rcos, rsin)


def _gla_kernel(q_ref, k_ref, v_ref, g_ref, a_ref, wlr_ref, blr_ref, o_ref, state_ref, *, chunks):
    C = GLA_CHUNK

    @pl.when(pl.program_id(0) == 0)
    def _():
        state_ref[...] = jnp.zeros_like(state_ref)

    ii = lax.broadcasted_iota(jnp.int32, (C, C), 0)
    jj = lax.broadcasted_iota(jnp.int32, (C, C), 1)
    causal = jj <= ii
    tril = _bf(jnp.where(causal, 1.0, 0.0))
    wlr = wlr_ref[...]
    w_hi, w_mid, w_lo = _split3(wlr)

    for c in range(chunks):
        rows = slice(c * C, (c + 1) * C)
        a_hi, a_mid, a_lo = _split3(a_ref[rows, :])
        z = (_dot(a_hi, w_hi) + (_dot(a_hi, w_mid) + _dot(a_mid, w_hi))
             + (_dot(a_hi, w_lo) + _dot(a_mid, w_mid) + _dot(a_lo, w_hi))) + blr_ref[...]
        log_a = (jnp.minimum(z, 0.0) - jnp.log1p(jnp.exp(-jnp.abs(z)))) * (1.0 / GLA_TAU)
        l_hi, l_mid, l_lo = _split3(log_a)
        bcum = _dot(tril, l_hi) + _dot(tril, l_mid) + _dot(tril, l_lo)
        b_mid = bcum[C // 2 - 1:C // 2, :]
        b_last = bcum[C - 1:C, :]
        q = q_ref[rows, :] * (GLA_DK ** -0.5)
        k = k_ref[rows, :]
        qt = _bf(q * jnp.exp(bcum - b_mid))
        kt = _bf(k * jnp.exp(b_mid - bcum))
        qg = _bf(q * jnp.exp(bcum))
        kd = _bf(k * jnp.exp(b_last - bcum))
        e_last = jnp.exp(b_last)
        v = v_ref[rows, :]
        g = g_ref[rows, :]
        for h in range(GLA_HEADS):
            dk = slice(h * GLA_DK, (h + 1) * GLA_DK)
            dv = slice(h * GLA_DV, (h + 1) * GLA_DV)
            vh = _bf(v[:, dv])
            attn = jnp.where(causal, _dot_nt(qt[:, dk], kt[:, dk]), 0.0)
            st = state_ref[h]
            o = _dot(_bf(attn), vh) + _dot_nt(qg[:, dk], _bf(st))
            state_ref[h] = e_last[:, dk] * st + _dot_tn(vh, kd[:, dk])
            o = o * lax.rsqrt(jnp.mean(o * o, axis=-1, keepdims=True) + RMS_EPS)
            o_ref[rows, dv] = _bf(o * _silu(g[:, dv]))


def _gla_call(p, wlr_pad, blr):
    s = p.shape[0]
    chunks = 4
    tm = GLA_CHUNK * chunks
    return pl.pallas_call(
        functools.partial(_gla_kernel, chunks=chunks),
        grid=(s // tm,),
        in_specs=[_pspec(tm, "gla_q", 256), _pspec(tm, "gla_k", 256), _pspec(tm, "gla_v", 512),
                  _pspec(tm, "gla_g", 512), _pspec(tm, "gla_a", 128),
                  pl.BlockSpec((128, 256), lambda i: (0, 0)), pl.BlockSpec((1, 256), lambda i: (0, 0))],
        out_specs=pl.BlockSpec((tm, 512), lambda i: (i, 0)),
        out_shape=jax.ShapeDtypeStruct((s, 512), jnp.bfloat16),
        scratch_shapes=[pltpu.VMEM((GLA_HEADS, GLA_DV, GLA_DK), jnp.float32)],
        compiler_params=_cparams(1),
        name="gla",
    )(p, p, p, p, p, wlr_pad, blr)


def _place_head(pair, src_pos, dst_pos):
    lane = lax.broadcasted_iota(jnp.int32, pair.shape, 1)
    if src_pos != dst_pos:
        pair = pltpu.roll(pair, 64, 1)
    keep = (lane < 64) if dst_pos == 0 else (lane >= 64)
    return jnp.where(keep, pair, 0.0)


def _dprep_kernel(q_ref, k_ref, v_ref, iq_ref, ikw_ref, cos_ref, sin_ref,
                  qp_o, k_o, v_o, iqp_o, ik_o, w_o):
    cos, sin = cos_ref[...], sin_ref[...]
    half = DSA_ROT // 2
    cos4 = jnp.concatenate([cos] * 4, axis=1)
    sin4 = jnp.concatenate([sin] * 4, axis=1)
    q = _rope(q_ref[...], cos4, sin4, half, DSA_DH) * (DSA_DH ** -0.5 * LOG2E)
    for h in range(DSA_HEADS):
        pair = q[:, (h // 2) * 128:(h // 2 + 1) * 128]
        qp_o[h] = _bf(_place_head(pair, h % 2, h // (DSA_HEADS // DSA_KV_HEADS)))
    k_o[...] = _bf(_rope(k_ref[...], cos, sin, half, DSA_DH))
    v = v_ref[...]
    lane = lax.broadcasted_iota(jnp.int32, v.shape, 1)
    v_o[0] = _bf(jnp.where(lane < DSA_DH, v, 1.0))
    v_o[1] = _bf(jnp.where(lane < DSA_DH, pltpu.roll(v, DSA_DH, 1), 1.0))
    iq = _rope(iq_ref[...], cos4[:, :256], sin4[:, :256], half, IDX_DH)
    for h in range(IDX_HEADS):
        pair = iq[:, (h // 2) * 128:(h // 2 + 1) * 128]
        iqp_o[h] = _bf(_place_head(pair, h % 2, 0))
    ikw = ikw_ref[...]
    lane = lax.broadcasted_iota(jnp.int32, ikw.shape, 1)
    ik_o[...] = _bf(jnp.where(lane < IDX_DH, _rope(ikw, cos, sin, half, DSA_DH), 0.0))
    wscale = (IDX_HEADS ** -0.5) * (IDX_DH ** -0.5)
    w_o[...] = pltpu.roll(ikw, 128 - IDX_DH, 1) * wscale


def _dprep_call(p, dcos, dsin):
    s = p.shape[0]
    tm = min(512, s)
    tab = pl.BlockSpec((tm, 128), lambda i: (i, 0))
    return pl.pallas_call(
        _dprep_kernel,
        grid=(s // tm,),
        in_specs=[_pspec(tm, "dsa_q", 512), _pspec(tm, "dsa_k", 128), _pspec(tm, "dsa_v", 128),
                  _pspec(tm, "idx_q", 256), _pspec(tm, "idx_kw", 128), tab, tab],
        out_specs=[pl.BlockSpec((DSA_HEADS, tm, 128), lambda i: (0, i, 0)), tab,
                   pl.BlockSpec((DSA_KV_HEADS, tm, 128), lambda i: (0, i, 0)),
                   pl.BlockSpec((IDX_HEADS, tm, 128), lambda i: (0, i, 0)), tab, tab],
        out_shape=[jax.ShapeDtypeStruct((DSA_HEADS, s, 128), jnp.bfloat16),
                   jax.ShapeDtypeStruct((s, 128), jnp.bfloat16),
                   jax.ShapeDtypeStruct((DSA_KV_HEADS, s, 128), jnp.bfloat16),
                   jax.ShapeDtypeStruct((IDX_HEADS, s, 128), jnp.bfloat16),
                   jax.ShapeDtypeStruct((s, 128), jnp.bfloat16),
                   jax.ShapeDtypeStruct((s, 128), jnp.float32)],
        compiler_params=_cparams(1),
        name="dsa_prep",
    )(p, p, p, p, p, dcos, dsin)


DSA_QB = 256
IDX_TK = 256
ATT_TK = 1024
SEL_ROWS = 128
SEL_LANES = 512
CAND_DEPTH = 10
CAND_SPLIT = 2
CAND_W = CAND_DEPTH * CAND_SPLIT * LANES
TIE_CHUNK = 8
NO_TIE_LIMIT = 1e9
assert CAND_W % SEL_LANES == 0 and 8 % CAND_SPLIT == 0


def _key_to_f32(key):
    bits = jnp.where(key >= 0, key, (0 - key) | INT_MIN)
    return lax.bitcast_convert_type(bits, jnp.float32)


def _count_lanes(keys_ref, r0, n_steps, cand, strict=False):
    def body(s, acc):
        c0 = s * SEL_LANES if isinstance(s, int) else pl.multiple_of(s * SEL_LANES, SEL_LANES)
        for u in range(SEL_LANES // LANES):
            kk = keys_ref[pl.ds(r0, SEL_ROWS), pl.ds(c0 + u * LANES, LANES)]
            acc = acc + jnp.where((kk > cand) if strict else (kk >= cand), 1.0, 0.0)
        return acc
    acc = jnp.zeros((SEL_ROWS, LANES), jnp.float32)
    if isinstance(n_steps, int):
        for s in range(n_steps):
            acc = body(s, acc)
        return acc
    return lax.fori_loop(0, n_steps, body, acc)


def _row_total(lane_counts):
    ones = jnp.ones((LANES, LANES), jnp.bfloat16)
    return _dot(_bf(lane_counts), ones)


def _f32_to_key(x):
    bits = lax.bitcast_convert_type(x, jnp.int32)
    return jnp.where(bits >= 0, bits, INT_MIN - bits)


def _search_window(cand_ref, lo_ref, final_ref):
    hi = lw = None
    for v in range(CAND_SPLIT):
        top = cand_ref[:, v * CAND_DEPTH * LANES:(v * CAND_DEPTH + 1) * LANES]
        hi = top if hi is None else jnp.maximum(hi, top)
        lw = top if lw is None else jnp.minimum(lw, top)
    k_hi = _f32_to_key(jnp.max(hi, axis=-1, keepdims=True))
    k_lw = _f32_to_key(jnp.min(lw, axis=-1, keepdims=True))
    nbits = 32 - lax.clz(k_hi ^ k_lw)
    nb = jnp.max(nbits.astype(jnp.float32)).astype(jnp.int32)
    low = lax.shift_left(jnp.int32(1), jnp.minimum(nb, 31)) - 1
    lo0 = jnp.where(nbits == 0, k_lw, jnp.where(nb >= 32, INT_MIN, k_lw & ~low))
    lo_ref[...] = jnp.broadcast_to(lo0, lo_ref.shape)
    final_ref[...] = jnp.broadcast_to(jnp.where(nbits == 0, 1.0, 0.0), final_ref.shape)
    return 32 - nb


def _radix_search(src_ref, n_steps, lo_ref, cnt_ref, topk, first_bit=None, final_ref=None):
    rows_total = lo_ref.shape[0]
    if first_bit is None:
        first_bit = jnp.int32(0)
        lo_ref[...] = jnp.full(lo_ref.shape, INT_MIN, jnp.int32)
        cnt_ref[...] = jnp.zeros(cnt_ref.shape, jnp.float32)
    else:
        cnt_ref[...] = jnp.full(cnt_ref.shape, NO_TIE_LIMIT, jnp.float32)

    def cond(c):
        b, pending = c
        return jnp.logical_and(b < 32, pending > 0)

    def body(c):
        b, _ = c
        bit = lax.shift_left(jnp.int32(1), 31 - b)
        off = None
        for rb in range(rows_total // SEL_ROWS):
            rows = pl.ds(rb * SEL_ROWS, SEL_ROWS)
            cand = lo_ref[rows, :] + bit
            cnt = _row_total(_count_lanes(src_ref, rb * SEL_ROWS, n_steps, _key_to_f32(cand)))
            ok = cnt >= float(topk)
            if final_ref is not None:
                ok = jnp.logical_and(ok, final_ref[rows, :] <= 0.0)
            lo_ref[rows, :] = jnp.where(ok, cand, lo_ref[rows, :])
            cnt = jnp.where(ok, cnt, cnt_ref[rows, :])
            cnt_ref[rows, :] = cnt
            miss = jnp.abs(cnt - float(topk))
            if final_ref is not None:
                miss = jnp.where(final_ref[rows, :] > 0.0, 0.0, miss)
            off = miss if off is None else jnp.maximum(off, miss)
        return b + 1, (jnp.max(off) > 0.0).astype(jnp.int32)
    lax.while_loop(cond, body, (first_bit, jnp.int32(1)))


def _lane_candidates(keys_ref, cand_ref, n_groups):
    n_stack = CAND_DEPTH * CAND_SPLIT

    def rg_body(rg, carry):
        r0 = pl.multiple_of(rg * 8, 8)

        def col_body(s, st):
            st = list(st)
            c0 = pl.multiple_of(s * 8 * LANES, 8 * LANES)
            for u in range(8):
                x = keys_ref[pl.ds(r0, 8), pl.ds(c0 + u * LANES, LANES)]
                base = (u % CAND_SPLIT) * CAND_DEPTH
                for d in range(CAND_DEPTH):
                    cur = st[base + d]
                    st[base + d] = jnp.maximum(cur, x)
                    x = jnp.minimum(cur, x)
            return tuple(st)
        init = tuple(jnp.full((8, LANES), -jnp.inf, jnp.float32) for _ in range(n_stack))
        st = lax.fori_loop(0, n_groups, col_body, init)
        for k in range(n_stack):
            cand_ref[pl.ds(r0, 8), k * LANES:(k + 1) * LANES] = st[k]
        return carry
    lax.fori_loop(0, cand_ref.shape[0] // 8, rg_body, 0)


def _ties_to_keep(src_ref, n_steps, lo_ref, cnt_ref, part_ref, need_ref, topk):
    rows_total = lo_ref.shape[0]
    surplus = jnp.where(lo_ref[...] > NEG_INF_KEY, cnt_ref[...] - float(topk), 0.0)
    need_ref[...] = jnp.full(need_ref.shape, NO_TIE_LIMIT, jnp.float32)

    @pl.when(jnp.max(surplus) > 0.0)
    def _():
        for rb in range(rows_total // SEL_ROWS):
            rows = pl.ds(rb * SEL_ROWS, SEL_ROWS)
            thr = _key_to_f32(jnp.maximum(lo_ref[rows, :], NEG_INF_KEY))
            part_ref[rows, :] = _count_lanes(src_ref, rb * SEL_ROWS, n_steps, thr, strict=True)
        above = _row_total(part_ref[...])
        need_ref[...] = jnp.where(surplus > 0.0, float(topk) - above, NO_TIE_LIMIT)


def _tie_index_bound(keys_ref, thr_ref, need_ref, seen_ref, sig_ref, n_chunks):
    kk_i = lax.broadcasted_iota(jnp.int32, (LANES, LANES), 0)
    jj_i = lax.broadcasted_iota(jnp.int32, (LANES, LANES), 1)
    tri = _bf(jnp.where(kk_i <= jj_i, 1.0, 0.0))
    seen_ref[...] = jnp.zeros(seen_ref.shape, jnp.float32)

    def cond(c):
        j, pending = c
        return jnp.logical_and(j < n_chunks, pending > 0)

    def body(c):
        j, _ = c
        thr, need = thr_ref[...], need_ref[...]
        seen, sig = seen_ref[...], sig_ref[...]
        for u in range(TIE_CHUNK):
            c0 = pl.multiple_of((j * TIE_CHUNK + u) * LANES, LANES)
            tie = jnp.where(keys_ref[:, pl.ds(c0, LANES)] == thr, 1.0, 0.0)
            rank = seen + _dot(_bf(tie), tri)
            after = seen + _row_total(tie)
            before = _row_total(jnp.where(rank < need, 1.0, 0.0))
            here = jnp.where(seen < need, jnp.where(after >= need, 1.0, 0.0), 0.0)
            sig = jnp.where(here > 0.0, c0.astype(jnp.float32) + before, sig)
            seen = after
        seen_ref[...] = seen
        sig_ref[...] = sig
        waiting = jnp.where(need < NO_TIE_LIMIT, jnp.where(seen < need, 1.0, 0.0), 0.0)
        return j + 1, (jnp.max(waiting) > 0.0).astype(jnp.int32)
    lax.while_loop(cond, body, (jnp.int32(0), jnp.int32(1)))


def _dsa_kernel(qp_ref, iqp_ref, w_ref, g_ref, k_ref, va_ref, ik_ref, o_ref,
                keys_ref, cand_ref, lo_ref, thr_ref, cnt_ref, part_ref, need_ref, sig_ref,
                m_ref, acc_ref, full_ref, *, topk):
    QB, TK = DSA_QB, IDX_TK
    G = DSA_HEADS // DSA_KV_HEADS
    i = pl.program_id(0)
    n_idx = i + 1
    n_att = (n_idx * IDX_TK + ATT_TK - 1) // ATT_TK
    n_sel = n_att * (ATT_TK // SEL_LANES)
    row_id = lax.broadcasted_iota(jnp.int32, (QB, TK), 0)
    col_id = lax.broadcasted_iota(jnp.int32, (QB, TK), 1)

    iq = iqp_ref[...].reshape(IDX_HEADS * QB, LANES)
    wb = [jnp.broadcast_to(w_ref[:, h:h + 1], (QB, TK)) for h in range(IDX_HEADS)]

    def score_tile(j, diag):
        c0 = pl.multiple_of(j * TK, TK)
        s_all = _dot_nt(iq, ik_ref[pl.ds(c0, TK), :])
        score = None
        for h in range(IDX_HEADS):
            term = wb[h] * jnp.maximum(s_all[h * QB:(h + 1) * QB], 0.0)
            score = term if score is None else score + term
        if diag:
            score = jnp.where(col_id <= row_id, score, -jnp.inf)
        keys_ref[:, pl.ds(c0, TK)] = score

    def score_body(j, carry):
        score_tile(j, False)
        return carry

    def score_quad(q, carry):
        for u in range(4):
            score_tile(q * 4 + u, False)
        return carry
    n_quads = lax.shift_right_logical(i, 2)
    lax.fori_loop(0, n_quads, score_quad, 0)
    lax.fori_loop(n_quads * 4, i, score_body, 0)
    score_tile(i, True)

    def blank_body(j, carry):
        c0 = pl.multiple_of(j * TK, TK)
        keys_ref[:, pl.ds(c0, TK)] = jnp.full((QB, TK), -jnp.inf, jnp.float32)
        return carry
    n_vis = n_idx * TK
    windows = [w for w in (CAND_W, 2 * CAND_W) if w <= keys_ref.shape[1]]
    att_end = n_att * (ATT_TK // TK)
    blank_end = att_end
    for w in reversed(windows):
        blank_end = jnp.where(n_vis <= w, jnp.maximum(att_end, w // TK), blank_end)
    lax.fori_loop(n_idx, blank_end, blank_body, 0)

    full_ref[0] = jnp.int32(1)

    @pl.when(n_att * ATT_TK > 2 * CAND_W)
    def _():
        _lane_candidates(keys_ref, cand_ref, n_att * (ATT_TK // (8 * LANES)))
        first_bit = _search_window(cand_ref, lo_ref, part_ref)
        _radix_search(cand_ref, CAND_W // SEL_LANES, lo_ref, cnt_ref, topk,
                      first_bit=first_bit, final_ref=part_ref)
        thr = _key_to_f32(jnp.maximum(lo_ref[...], NEG_INF_KEY))
        exact = cnt_ref[...] == float(topk)
        lost = None
        for v in range(CAND_SPLIT):
            last = cand_ref[:, (v * CAND_DEPTH + CAND_DEPTH - 1) * LANES:(v * CAND_DEPTH + CAND_DEPTH) * LANES]
            hit = jnp.where(last > thr, 1.0, jnp.where(last == thr, jnp.where(exact, 1.0, 0.0), 0.0))
            lost = hit if lost is None else jnp.maximum(lost, hit)
        full_ref[0] = (jnp.max(lost) > 0.0).astype(jnp.int32)
        _ties_to_keep(cand_ref, CAND_W // SEL_LANES, lo_ref, cnt_ref, part_ref, need_ref, topk)

    @pl.when(full_ref[0] > 0)
    def _():
        def search_rows(n_steps):
            _radix_search(keys_ref, n_steps, lo_ref, cnt_ref, topk)
            _ties_to_keep(keys_ref, n_steps, lo_ref, cnt_ref, part_ref, need_ref, topk)
        below = 0
        for w in windows:
            pl.when(jnp.logical_and(n_vis > below, n_vis <= w))(
                functools.partial(search_rows, w // SEL_LANES))
            below = w
        pl.when(n_vis > below)(functools.partial(search_rows, n_sel))

    tau = lo_ref[...]
    found = tau > NEG_INF_KEY
    thr_ref[...] = jnp.where(found, _key_to_f32(jnp.maximum(tau, NEG_INF_KEY)), F32_LOWEST)
    sig_ref[...] = jnp.full(sig_ref.shape, NO_TIE_LIMIT, jnp.float32)

    @pl.when(jnp.min(need_ref[...]) < NO_TIE_LIMIT)
    def _():
        _tie_index_bound(keys_ref, thr_ref, need_ref, part_ref, sig_ref, n_att * (ATT_TK // (TIE_CHUNK * LANES)))

    m_ref[...] = jnp.full(m_ref.shape, NEG_INIT, jnp.float32)
    acc_ref[...] = jnp.zeros(acc_ref.shape, jnp.float32)
    n_grp = ATT_TK // LANES
    lane_f = lax.broadcasted_iota(jnp.int32, (QB, LANES), 1).astype(jnp.float32)

    def attn_body(j, carry):
        c0 = pl.multiple_of(j * ATT_TK, ATT_TK)
        tau_b = thr_ref[...]
        sig_rel = sig_ref[...] - c0.astype(jnp.float32)
        parts = []
        for u in range(n_grp):
            kk = keys_ref[:, pl.ds(c0 + u * LANES, LANES)]
            tie_bias = jnp.where(lane_f + float(u * LANES) <= sig_rel, 0.0, NEG_MASK)
            parts.append(jnp.where(kk > tau_b, 0.0, jnp.where(kk == tau_b, tie_bias, NEG_MASK)))
        bias = jnp.concatenate(parts, axis=1)
        kt = k_ref[pl.ds(c0, ATT_TK), :]
        for n in range(DSA_KV_HEADS):
            qn = qp_ref[n * G:(n + 1) * G].reshape(G * QB, LANES)
            logits = _dot_nt(qn, kt)
            ps = []
            for gq in range(G):
                h = n * G + gq
                lm = logits[gq * QB:(gq + 1) * QB] + bias
                tmax = lm[:, 0:LANES]
                for u in range(1, n_grp):
                    tmax = jnp.maximum(tmax, lm[:, u * LANES:(u + 1) * LANES])
                m_old = m_ref[h]
                m_new = jnp.maximum(m_old, jnp.max(tmax, axis=-1, keepdims=True))
                ps.append(jnp.concatenate(
                    [_bf(jnp.exp2(lm[:, u * LANES:(u + 1) * LANES] - m_new)) for u in range(n_grp)], axis=1))
                m_ref[h] = m_new
                acc_ref[h] = jnp.exp2(m_old - m_new) * acc_ref[h]
            pv = _dot(jnp.concatenate(ps, axis=0), va_ref[n, pl.ds(c0, ATT_TK), :])
            for gq in range(G):
                h = n * G + gq
                acc_ref[h] = acc_ref[h] + pv[gq * QB:(gq + 1) * QB]
        return carry
    lax.fori_loop(0, n_att, attn_body, 0)

    lane = lax.broadcasted_iota(jnp.int32, (QB, LANES), 1)
    g = g_ref[...]
    for pair in range(DSA_HEADS // 2):
        halves = []
        for pos in range(2):
            a = acc_ref[2 * pair + pos]
            halves.append(a * (1.0 / pltpu.roll(a, DSA_DH, 1)))
        o = jnp.where(lane < DSA_DH, halves[0], pltpu.roll(halves[1], DSA_DH, 1))
        sl = slice(pair * LANES, (pair + 1) * LANES)
        o_ref[:, sl] = _bf(o * _silu(g[:, sl]))


def _dsa_call(p, qp, kk, va, iqp, ik, w4):
    s = p.shape[0]
    topk = min(TOPK_MAX, s // 4)
    QB = DSA_QB
    assert IDX_TK == QB and s % ATT_TK == 0 and ATT_TK % SEL_LANES == 0 and s // LANES <= 256
    once = pl.Buffered(1)
    return pl.pallas_call(
        functools.partial(_dsa_kernel, topk=topk),
        grid=(s // QB,),
        in_specs=[pl.BlockSpec((DSA_HEADS, QB, LANES), lambda i: (0, i, 0)),
                  pl.BlockSpec((IDX_HEADS, QB, LANES), lambda i: (0, i, 0)),
                  pl.BlockSpec((QB, LANES), lambda i: (i, 0)),
                  _pspec(QB, "dsa_g", 512),
                  pl.BlockSpec((s, LANES), lambda i: (0, 0), pipeline_mode=once),
                  pl.BlockSpec((DSA_KV_HEADS, s, LANES), lambda i: (0, 0, 0), pipeline_mode=once),
                  pl.BlockSpec((s, LANES), lambda i: (0, 0), pipeline_mode=once)],
        out_specs=pl.BlockSpec((QB, 512), lambda i: (i, 0)),
        out_shape=jax.ShapeDtypeStruct((s, 512), jnp.bfloat16),
        scratch_shapes=[pltpu.VMEM((QB, s), jnp.float32),
                        pltpu.VMEM((QB, CAND_W), jnp.float32),
                        pltpu.VMEM((QB, LANES), jnp.int32),
                        pltpu.VMEM((QB, LANES), jnp.float32),
                        pltpu.VMEM((QB, LANES), jnp.float32),
                        pltpu.VMEM((QB, LANES), jnp.float32),
                        pltpu.VMEM((QB, LANES), jnp.float32),
                        pltpu.VMEM((QB, LANES), jnp.float32),
                        pltpu.VMEM((DSA_HEADS, QB, LANES), jnp.float32),
                        pltpu.VMEM((DSA_HEADS, QB, LANES), jnp.float32),
                        pltpu.SMEM((1,), jnp.int32)],
        compiler_params=_cparams(1, vmem_mb=56),
        name="dsa",
    )(qp, iqp, w4, p, kk, va, ik)


def _merge_kernel(x_ref, ret_ref, dsa_ref, gl_ref, m_ref, wr_ref, wd_ref, wg_ref, wo_ref,
                  post_ref, gate_ref, o_ref):
    d = D_MODEL
    y = (_sigmoid(m_ref[:, 0:d]) * _dot(ret_ref[...], wr_ref[...])
         + _sigmoid(m_ref[:, d:2 * d]) * _dot(dsa_ref[...], wd_ref[...])
         + _sigmoid(m_ref[:, 2 * d:3 * d]) * _dot(gl_ref[...], wg_ref[...]))
    y = _dot(_bf(y), wo_ref[...])
    yn = y * lax.rsqrt(jnp.mean(y * y, axis=-1, keepdims=True) + RMS_EPS) * post_ref[...]
    o_ref[...] = x_ref[...] + gate_ref[...] * yn


def _merge_call(x2, ret, dsa, gl, p, wr, wd, wg, wo, post, gate):
    s, d = x2.shape
    tm = min(512, s)
    rows = lambda w: pl.BlockSpec((tm, w), lambda i: (i, 0))
    whole = lambda a: pl.BlockSpec(a.shape, lambda i: (0, 0))
    return pl.pallas_call(
        _merge_kernel,
        grid=(s // tm,),
        in_specs=[rows(d), rows(512), rows(512), rows(512), _pspec(tm, "merge", 3072),
                  whole(wr), whole(wd), whole(wg), whole(wo), whole(post), whole(gate)],
        out_specs=rows(d),
        out_shape=jax.ShapeDtypeStruct((s, d), jnp.float32),
        compiler_params=_cparams(1),
        name="merge_out",
    )(x2, ret, dsa, gl, p, wr, wd, wg, wo, post, gate)


def _pack_w_in(w_in):
    depth, d, _ = w_in.shape
    w_in = w_in.astype(jnp.bfloat16)
    zeros = lambda n: jnp.zeros((depth, d, n), w_in.dtype)
    src = lambda name: w_in[:, :, _SRC[name][0]:_SRC[name][0] + _SRC[name][1]]
    pieces, at = [], 0
    for name, width in _PACK:
        assert at == PCOL[name]
        if name == "idx_kw":
            cols = [src("idx_k"), src("idx_w"), zeros(width - IDX_DH - IDX_HEADS)]
        elif name == "gla_a":
            cols = [src("gla_a"), zeros(width - GLA_RANK)]
        else:
            cols = [src(name)]
        pieces += cols
        at += width
    pieces.append(zeros(P_WIDTH - at))
    return jnp.concatenate(pieces, axis=-1).astype(jnp.bfloat16)


def kernel(x, c, positions, ada_w, ada_b, pre_norm, post_norm, w_in, gla_w_lr, gla_b_lr,
           w_br_ret, w_br_dsa, w_br_gla, w_out):
    b, s, d = x.shape
    assert b == 1 and d == D_MODEL
    depth = ada_w.shape[0]
    x2 = x.reshape(s, d)
    mod = _mod_call(jnp.broadcast_to(c, (8, d)), ada_w, ada_b.reshape(depth, 1, 3 * d))[:, 0:1, :]
    rcos, rsin, dcos, dsin = _tab_call(positions.reshape(s, 1))
    w_pack = _pack_w_in(w_in)
    wlr_pad = jnp.pad(gla_w_lr, ((0, 0), (0, LANES - GLA_RANK), (0, 0)))
    for l in range(depth):
        shift, scale, gate = mod[l, :, 0:d], mod[l, :, d:2 * d], mod[l, :, 2 * d:3 * d]
        p = _proj_call(x2, pre_norm[l][None, :], scale, shift, w_pack[l])
        ret = _ret_call(p, rcos, rsin)
        gl = _gla_call(p, wlr_pad[l], gla_b_lr[l][None, :])
        qp, kk, va, iqp, ik, w4 = _dprep_call(p, dcos, dsin)
        dsa = _dsa_call(p, qp, kk, va, iqp, ik, w4)
        x2 = _merge_call(x2, ret, dsa, gl, p, _bf(w_br_ret[l]), _bf(w_br_dsa[l]), _bf(w_br_gla[l]),
                         _bf(w_out[l]), post_norm[l][None, :], gate)
    return x2.reshape(b, s, d)
```

```python
import functools
import math

import jax
import jax.numpy as jnp
from jax import lax
from jax.experimental import pallas as pl
from jax.experimental.pallas import tpu as pltpu

D_MODEL = 1024
DEPTH = 4
RET_HEADS, RET_DK, RET_DV, RET_CHUNK, RET_THETA = 4, 64, 128, 128, 10000.0
DSA_HEADS, DSA_KV_HEADS, DSA_DH = 8, 2, 64
DSA_ROT = DSA_DH // 4
ROPE_THETA = 500000.0
IDX_HEADS, IDX_DH = 4, 64
TOPK_MAX = 256
GLA_HEADS, GLA_DK, GLA_DV, GLA_RANK, GLA_TAU, GLA_CHUNK = 4, 64, 128, 16, 16.0, 64
RMS_EPS = 1e-6
LANES = 128

_SRC = {}
_off = 0
for _name, _w in (("ret_q", 256), ("ret_k", 256), ("ret_v", 512), ("ret_g", 512),
                  ("dsa_q", 512), ("dsa_k", 128), ("dsa_v", 128), ("dsa_g", 512),
                  ("idx_q", 256), ("idx_k", 64), ("idx_w", 4),
                  ("gla_q", 256), ("gla_k", 256), ("gla_v", 512), ("gla_g", 512), ("gla_a", 16),
                  ("merge", 3072)):
    _SRC[_name] = (_off, _w)
    _off += _w
IN_WIDTH = _off

_PACK = (("merge", 3072), ("ret_q", 256), ("ret_k", 256), ("ret_v", 512), ("ret_g", 512),
         ("dsa_q", 512), ("dsa_g", 512), ("gla_v", 512), ("gla_g", 512),
         ("gla_q", 256), ("gla_k", 256), ("idx_q", 256),
         ("dsa_k", 128), ("dsa_v", 128), ("idx_kw", 128), ("gla_a", 128))
PCOL = {}
_off = 0
for _name, _w in _PACK:
    assert _off % min(_w, 1024) == 0
    PCOL[_name] = _off
    _off += _w
P_WIDTH = 8192
assert _off <= P_WIDTH

LOG2E = math.log2(math.e)
INT_MIN = -(2 ** 31)
F32_LOWEST = -3.4028234663852886e38
NEG_INF_KEY = -0x7F800000
NEG_INIT = -1e30
NEG_MASK = -2e30


def _cparams(n_axes, vmem_mb=48):
    return pltpu.CompilerParams(dimension_semantics=("arbitrary",) * n_axes,
                                vmem_limit_bytes=vmem_mb * 1024 * 1024)


def _bf(x):
    return x.astype(jnp.bfloat16)


def _dot(a, b):
    return jnp.dot(a, b, preferred_element_type=jnp.float32)


def _dot_nt(a, b):
    return lax.dot_general(a, b, (((1,), (1,)), ((), ())), preferred_element_type=jnp.float32)


def _dot_tn(a, b):
    return lax.dot_general(a, b, (((0,), (0,)), ((), ())), preferred_element_type=jnp.float32)


def _split3(x):
    hi = _bf(x)
    r1 = x - hi.astype(jnp.float32)
    mid = _bf(r1)
    lo = _bf(r1 - mid.astype(jnp.float32))
    return hi, mid, lo


def _silu(x):
    return x * (1.0 / (1.0 + jnp.exp(-x)))


def _sigmoid(x):
    return 1.0 / (1.0 + jnp.exp(-x))


def _mod_kernel(c_ref, w_ref, b_ref, o_ref):
    c = c_ref[...]
    ca = _silu(c)
    acc = None
    for t in _split3(ca):
        for u in _split3(w_ref[0]):
            part = _dot(t, u)
            acc = part if acc is None else acc + part
    o_ref[0] = acc + b_ref[0]


def _mod_call(c8, ada_w, ada_b3):
    depth, d, n = ada_w.shape
    tn = 1024
    return pl.pallas_call(
        _mod_kernel,
        grid=(depth, n // tn),
        in_specs=[pl.BlockSpec((8, d), lambda l, j: (0, 0)),
                  pl.BlockSpec((1, d, tn), lambda l, j: (l, 0, j)),
                  pl.BlockSpec((1, 1, tn), lambda l, j: (l, 0, j))],
        out_specs=pl.BlockSpec((1, 8, tn), lambda l, j: (l, 0, j)),
        out_shape=jax.ShapeDtypeStruct((depth, 8, n), jnp.float32),
        compiler_params=_cparams(2),
        name="adaln_mod",
    )(c8, ada_w, ada_b3)


def _tab_kernel(pos_ref, rf_ref, rs_ref, df_ref, ds_ref, rc_o, rsn_o, dc_o, dsn_o):
    pos = pos_ref[...].astype(jnp.float32)
    ang = pos * rf_ref[...]
    rc_o[...] = jnp.cos(ang)
    rsn_o[...] = jnp.sin(ang) * rs_ref[...]
    ang = pos * df_ref[...]
    dc_o[...] = jnp.cos(ang)
    dsn_o[...] = jnp.sin(ang) * ds_ref[...]


def _rope_rows():
    half = RET_DK // 2
    f = RET_THETA ** (-jnp.arange(half, dtype=jnp.float32) * 2.0 / RET_DK)
    rf = jnp.tile(jnp.concatenate([f, f]), RET_HEADS)[None, :]
    rs = jnp.tile(jnp.concatenate([-jnp.ones(half), jnp.ones(half)]), RET_HEADS)[None, :].astype(jnp.float32)
    half = DSA_ROT // 2
    f = ROPE_THETA ** (-jnp.arange(half, dtype=jnp.float32) * 2.0 / DSA_ROT)
    z = jnp.zeros(DSA_DH - DSA_ROT, jnp.float32)
    df = jnp.tile(jnp.concatenate([f, f, z]), 2)[None, :]
    ds = jnp.tile(jnp.concatenate([-jnp.ones(half), jnp.ones(half), z]), 2)[None, :].astype(jnp.float32)
    return rf, rs, df, ds


def _tab_call(pos_col):
    s = pos_col.shape[0]
    tm = min(1024, s)
    rf, rs, df, ds = _rope_rows()
    row = lambda w: pl.BlockSpec((1, w), lambda i: (0, 0))
    out = lambda w: pl.BlockSpec((tm, w), lambda i: (i, 0))
    return pl.pallas_call(
        _tab_kernel,
        grid=(s // tm,),
        in_specs=[pl.BlockSpec((tm, 1), lambda i: (i, 0)), row(256), row(256), row(128), row(128)],
        out_specs=[out(256), out(256), out(128), out(128)],
        out_shape=[jax.ShapeDtypeStruct((s, 256), jnp.float32), jax.ShapeDtypeStruct((s, 256), jnp.float32),
                   jax.ShapeDtypeStruct((s, 128), jnp.float32), jax.ShapeDtypeStruct((s, 128), jnp.float32)],
        compiler_params=_cparams(1),
        name="rope_tables",
    )(pos_col, rf, rs, df, ds)


def _swap_halves(x, half, period):
    w = x.shape[-1]
    lane = lax.broadcasted_iota(jnp.int32, x.shape, x.ndim - 1) & (period - 1)
    up = pltpu.roll(x, w - half, x.ndim - 1)
    dn = pltpu.roll(x, half, x.ndim - 1)
    return jnp.where(lane < half, up, dn)


def _rope(x, cos, sin_signed, half, period):
    return x * cos + _swap_halves(x, half, period) * sin_signed


def _proj_kernel(x_ref, pre_ref, sc_ref, sh_ref, w_ref, o_ref):
    x = x_ref[...]
    xn = x * lax.rsqrt(jnp.mean(x * x, axis=-1, keepdims=True) + RMS_EPS)
    h = xn * pre_ref[...] * (1.0 + sc_ref[...]) + sh_ref[...]
    o_ref[...] = _dot(_bf(h), w_ref[...])


def _proj_call(x2, pre, scale, shift, w_pack):
    s, d = x2.shape
    tm, tn = min(512, s), 2048
    vec = pl.BlockSpec((1, d), lambda j, i: (0, 0))
    return pl.pallas_call(
        _proj_kernel,
        grid=(P_WIDTH // tn, s // tm),
        in_specs=[pl.BlockSpec((tm, d), lambda j, i: (i, 0)), vec, vec, vec,
                  pl.BlockSpec((d, tn), lambda j, i: (0, j))],
        out_specs=pl.BlockSpec((tm, tn), lambda j, i: (i, j)),
        out_shape=jax.ShapeDtypeStruct((s, P_WIDTH), jnp.float32),
        compiler_params=_cparams(2),
        name="in_proj",
    )(x2, pre, scale, shift, w_pack)


def _pspec(tm, name, width):
    blk = PCOL[name] // width
    assert PCOL[name] % width == 0
    return pl.BlockSpec((tm, width), lambda i: (i, blk))


def _ret_log_g(h):
    return math.log1p(-(2.0 ** (-5.0 - h)))


def _ret_kernel(q_ref, k_ref, v_ref, g_ref, cos_ref, sin_ref, o_ref,
                state_ref, decay_ref, qdec_ref, kend_ref, *, chunks):
    C = RET_CHUNK

    @pl.when(pl.program_id(0) == 0)
    def _():
        state_ref[...] = jnp.zeros_like(state_ref)
        ii = lax.broadcasted_iota(jnp.int32, (C, C), 0)
        jj = lax.broadcasted_iota(jnp.int32, (C, C), 1)
        rel = (ii - jj).astype(jnp.float32)
        row = lax.broadcasted_iota(jnp.int32, (C, RET_DK), 0).astype(jnp.float32)
        for h in range(RET_HEADS):
            lg = _ret_log_g(h)
            decay_ref[h] = jnp.where(rel >= 0, jnp.exp(lg * jnp.maximum(rel, 0.0)), 0.0)
            qdec_ref[:, h * RET_DK:(h + 1) * RET_DK] = jnp.exp((row + 1.0) * lg)
            kend_ref[:, h * RET_DK:(h + 1) * RET_DK] = jnp.exp((C - 1.0 - row) * lg)

    for c in range(chunks):
        rows = slice(c * C, (c + 1) * C)
        cos, sin = cos_ref[rows, :], sin_ref[rows, :]
        q = _rope(q_ref[rows, :], cos, sin, RET_DK // 2, RET_DK) * (RET_DK ** -0.5)
        k = _rope(k_ref[rows, :], cos, sin, RET_DK // 2, RET_DK)
        qd = _bf(q * qdec_ref[...])
        kd = _bf(k * kend_ref[...])
        qb, kb = _bf(q), _bf(k)
        v = v_ref[rows, :]
        g = g_ref[rows, :]
        for h in range(RET_HEADS):
            dk = slice(h * RET_DK, (h + 1) * RET_DK)
            dv = slice(h * RET_DV, (h + 1) * RET_DV)
            vh = _bf(v[:, dv])
            scores = _dot_nt(qb[:, dk], kb[:, dk]) * decay_ref[h]
            st = state_ref[h]
            o = _dot(_bf(scores), vh) + _dot(qd[:, dk], _bf(st))
            state_ref[h] = math.exp(C * _ret_log_g(h)) * st + _dot_tn(kd[:, dk], vh)
            o = o * lax.rsqrt(jnp.mean(o * o, axis=-1, keepdims=True) + RMS_EPS)
            o_ref[rows, dv] = _bf(o * _silu(g[:, dv]))


def _ret_call(p, rcos, rsin):
    s = p.shape[0]
    chunks = 2
    tm = RET_CHUNK * chunks
    tab = pl.BlockSpec((tm, 256), lambda i: (i, 0))
    return pl.pallas_call(
        functools.partial(_ret_kernel, chunks=chunks),
        grid=(s // tm,),
        in_specs=[_pspec(tm, "ret_q", 256), _pspec(tm, "ret_k", 256), _pspec(tm, "ret_v", 512),
                  _pspec(tm, "ret_g", 512), tab, tab],
        out_specs=pl.BlockSpec((tm, 512), lambda i: (i, 0)),
        out_shape=jax.ShapeDtypeStruct((s, 512), jnp.bfloat16),
        scratch_shapes=[pltpu.VMEM((RET_HEADS, RET_DK, RET_DV), jnp.float32),
                        pltpu.VMEM((RET_HEADS, RET_CHUNK, RET_CHUNK), jnp.float32),
                        pltpu.VMEM((RET_CHUNK, RET_HEADS * RET_DK), jnp.float32),
                        pltpu.VMEM((RET_CHUNK, RET_HEADS * RET_DK), jnp.float32)],
        compiler_params=_cparams(1),
        name="retention",
    )(p, p, p, p, rcos, rsin)


def _gla_kernel(q_ref, k_ref, v_ref, g_ref, a_ref, wlr_ref, blr_ref, o_ref, state_ref, *, chunks):
    C = GLA_CHUNK

    @pl.when(pl.program_id(0) == 0)
    def _():
        state_ref[...] = jnp.zeros_like(state_ref)

    ii = lax.broadcasted_iota(jnp.int32, (C, C), 0)
    jj = lax.broadcasted_iota(jnp.int32, (C, C), 1)
    causal = jj <= ii
    tril = _bf(jnp.where(causal, 1.0, 0.0))
    wlr = wlr_ref[...]
    w_hi, w_mid, w_lo = _split3(wlr)

    for c in range(chunks):
        rows = slice(c * C, (c + 1) * C)
        a_hi, a_mid, a_lo = _split3(a_ref[rows, :])
        z = (_dot(a_hi, w_hi) + (_dot(a_hi, w_mid) + _dot(a_mid, w_hi))
             + (_dot(a_hi, w_lo) + _dot(a_mid, w_mid) + _dot(a_lo, w_hi))) + blr_ref[...]
        log_a = (jnp.minimum(z, 0.0) - jnp.log1p(jnp.exp(-jnp.abs(z)))) * (1.0 / GLA_TAU)
        l_hi, l_mid, l_lo = _split3(log_a)
        bcum = _dot(tril, l_hi) + _dot(tril, l_mid) + _dot(tril, l_lo)
        b_mid = bcum[C // 2 - 1:C // 2, :]
        b_last = bcum[C - 1:C, :]
        q = q_ref[rows, :] * (GLA_DK ** -0.5)
        k = k_ref[rows, :]
        qt = _bf(q * jnp.exp(bcum - b_mid))
        kt = _bf(k * jnp.exp(b_mid - bcum))
        qg = _bf(q * jnp.exp(bcum))
        kd = _bf(k * jnp.exp(b_last - bcum))
        e_last = jnp.exp(b_last)
        v = v_ref[rows, :]
        g = g_ref[rows, :]
        for h in range(GLA_HEADS):
            dk = slice(h * GLA_DK, (h + 1) * GLA_DK)
            dv = slice(h * GLA_DV, (h + 1) * GLA_DV)
            vh = _bf(v[:, dv])
            attn = jnp.where(causal, _dot_nt(qt[:, dk], kt[:, dk]), 0.0)
            st = state_ref[h]
            o = _dot(_bf(attn), vh) + _dot_nt(qg[:, dk], _bf(st))
            state_ref[h] = e_last[:, dk] * st + _dot_tn(vh, kd[:, dk])
            o = o * lax.rsqrt(jnp.mean(o * o, axis=-1, keepdims=True) + RMS_EPS)
            o_ref[rows, dv] = _bf(o * _silu(g[:, dv]))


def _gla_call(p, wlr_pad, blr):
    s = p.shape[0]
    chunks = 4
    tm = GLA_CHUNK * chunks
    return pl.pallas_call(
        functools.partial(_gla_kernel, chunks=chunks),
        grid=(s // tm,),
        in_specs=[_pspec(tm, "gla_q", 256), _pspec(tm, "gla_k", 256), _pspec(tm, "gla_v", 512),
                  _pspec(tm, "gla_g", 512), _pspec(tm, "gla_a", 128),
                  pl.BlockSpec((128, 256), lambda i: (0, 0)), pl.BlockSpec((1, 256), lambda i: (0, 0))],
        out_specs=pl.BlockSpec((tm, 512), lambda i: (i, 0)),
        out_shape=jax.ShapeDtypeStruct((s, 512), jnp.bfloat16),
        scratch_shapes=[pltpu.VMEM((GLA_HEADS, GLA_DV, GLA_DK), jnp.float32)],
        compiler_params=_cparams(1),
        name="gla",
    )(p, p, p, p, p, wlr_pad, blr)


def _place_head(pair, src_pos, dst_pos):
    lane = lax.broadcasted_iota(jnp.int32, pair.shape, 1)
    if src_pos != dst_pos:
        pair = pltpu.roll(pair, 64, 1)
    keep = (lane < 64) if dst_pos == 0 else (lane >= 64)
    return jnp.where(keep, pair, 0.0)


def _dprep_kernel(q_ref, k_ref, v_ref, iq_ref, ikw_ref, cos_ref, sin_ref,
                  qp_o, k_o, v_o, iqp_o, ik_o, w_o):
    cos, sin = cos_ref[...], sin_ref[...]
    half = DSA_ROT // 2
    cos4 = jnp.concatenate([cos] * 4, axis=1)
    sin4 = jnp.concatenate([sin] * 4, axis=1)
    q = _rope(q_ref[...], cos4, sin4, half, DSA_DH) * (DSA_DH ** -0.5 * LOG2E)
    for h in range(DSA_HEADS):
        pair = q[:, (h // 2) * 128:(h // 2 + 1) * 128]
        qp_o[h] = _bf(_place_head(pair, h % 2, h // (DSA_HEADS // DSA_KV_HEADS)))
    k_o[...] = _bf(_rope(k_ref[...], cos, sin, half, DSA_DH))
    v = v_ref[...]
    lane = lax.broadcasted_iota(jnp.int32, v.shape, 1)
    v_o[0] = _bf(jnp.where(lane < DSA_DH, v, 1.0))
    v_o[1] = _bf(jnp.where(lane < DSA_DH, pltpu.roll(v, DSA_DH, 1), 1.0))
    iq = _rope(iq_ref[...], cos4[:, :256], sin4[:, :256], half, IDX_DH)
    for h in range(IDX_HEADS):
        pair = iq[:, (h // 2) * 128:(h // 2 + 1) * 128]
        iqp_o[h] = _bf(_place_head(pair, h % 2, 0))
    ikw = ikw_ref[...]
    lane = lax.broadcasted_iota(jnp.int32, ikw.shape, 1)
    ik_o[...] = _bf(jnp.where(lane < IDX_DH, _rope(ikw, cos, sin, half, DSA_DH), 0.0))
    wscale = (IDX_HEADS ** -0.5) * (IDX_DH ** -0.5)
    w_o[...] = pltpu.roll(ikw, 128 - IDX_DH, 1) * wscale


def _dprep_call(p, dcos, dsin):
    s = p.shape[0]
    tm = min(512, s)
    tab = pl.BlockSpec((tm, 128), lambda i: (i, 0))
    return pl.pallas_call(
        _dprep_kernel,
        grid=(s // tm,),
        in_specs=[_pspec(tm, "dsa_q", 512), _pspec(tm, "dsa_k", 128), _pspec(tm, "dsa_v", 128),
                  _pspec(tm, "idx_q", 256), _pspec(tm, "idx_kw", 128), tab, tab],
        out_specs=[pl.BlockSpec((DSA_HEADS, tm, 128), lambda i: (0, i, 0)), tab,
                   pl.BlockSpec((DSA_KV_HEADS, tm, 128), lambda i: (0, i, 0)),
                   pl.BlockSpec((IDX_HEADS, tm, 128), lambda i: (0, i, 0)), tab, tab],
        out_shape=[jax.ShapeDtypeStruct((DSA_HEADS, s, 128), jnp.bfloat16),
                   jax.ShapeDtypeStruct((s, 128), jnp.bfloat16),
                   jax.ShapeDtypeStruct((DSA_KV_HEADS, s, 128), jnp.bfloat16),
                   jax.ShapeDtypeStruct((IDX_HEADS, s, 128), jnp.bfloat16),
                   jax.ShapeDtypeStruct((s, 128), jnp.bfloat16),
                   jax.ShapeDtypeStruct((s, 128), jnp.float32)],
        compiler_params=_cparams(1),
        name="dsa_prep",
    )(p, p, p, p, p, dcos, dsin)


DSA_QB = 256
IDX_TK = 256
ATT_TK = 1024
SEL_ROWS = 128
SEL_LANES = 512
CAND_DEPTH = 10
CAND_SPLIT = 2
CAND_W = CAND_DEPTH * CAND_SPLIT * LANES
TIE_CHUNK = 8
NO_TIE_LIMIT = 1e9
assert CAND_W % SEL_LANES == 0 and 8 % CAND_SPLIT == 0


def _key_to_f32(key):
    bits = jnp.where(key >= 0, key, (0 - key) | INT_MIN)
    return lax.bitcast_convert_type(bits, jnp.float32)


def _count_lanes(keys_ref, r0, n_steps, cand, strict=False):
    def body(s, acc):
        c0 = s * SEL_LANES if isinstance(s, int) else pl.multiple_of(s * SEL_LANES, SEL_LANES)
        for u in range(SEL_LANES // LANES):
            kk = keys_ref[pl.ds(r0, SEL_ROWS), pl.ds(c0 + u * LANES, LANES)]
            acc = acc + jnp.where((kk > cand) if strict else (kk >= cand), 1.0, 0.0)
        return acc
    acc = jnp.zeros((SEL_ROWS, LANES), jnp.float32)
    if isinstance(n_steps, int):
        for s in range(n_steps):
            acc = body(s, acc)
        return acc
    return lax.fori_loop(0, n_steps, body, acc)


def _row_total(lane_counts):
    ones = jnp.ones((LANES, LANES), jnp.bfloat16)
    return _dot(_bf(lane_counts), ones)


def _f32_to_key(x):
    bits = lax.bitcast_convert_type(x, jnp.int32)
    return jnp.where(bits >= 0, bits, INT_MIN - bits)


def _search_window(cand_ref, lo_ref, final_ref):
    hi = lw = None
    for v in range(CAND_SPLIT):
        top = cand_ref[:, v * CAND_DEPTH * LANES:(v * CAND_DEPTH + 1) * LANES]
        hi = top if hi is None else jnp.maximum(hi, top)
        lw = top if lw is None else jnp.minimum(lw, top)
    k_hi = _f32_to_key(jnp.max(hi, axis=-1, keepdims=True))
    k_lw = _f32_to_key(jnp.min(lw, axis=-1, keepdims=True))
    nbits = 32 - lax.clz(k_hi ^ k_lw)
    nb = jnp.max(nbits.astype(jnp.float32)).astype(jnp.int32)
    low = lax.shift_left(jnp.int32(1), jnp.minimum(nb, 31)) - 1
    lo0 = jnp.where(nbits == 0, k_lw, jnp.where(nb >= 32, INT_MIN, k_lw & ~low))
    lo_ref[...] = jnp.broadcast_to(lo0, lo_ref.shape)
    final_ref[...] = jnp.broadcast_to(jnp.where(nbits == 0, 1.0, 0.0), final_ref.shape)
    return 32 - nb


def _radix_search(src_ref, n_steps, lo_ref, cnt_ref, topk, first_bit=None, final_ref=None, stage_ref=None):
    rows_total = lo_ref.shape[0]
    if first_bit is None:
        first_bit = jnp.int32(0)
        lo_ref[...] = jnp.full(lo_ref.shape, INT_MIN, jnp.int32)
        cnt_ref[...] = jnp.zeros(cnt_ref.shape, jnp.float32)
    else:
        cnt_ref[...] = jnp.full(cnt_ref.shape, NO_TIE_LIMIT, jnp.float32)

    if isinstance(n_steps, int):
        assert stage_ref is not None and rows_total == 2 * SEL_ROWS
        half_a, half_b = pl.ds(0, SEL_ROWS), pl.ds(SEL_ROWS, SEL_ROWS)

        def settle(rows, lane_counts, bit):
            cnt = _row_total(lane_counts)
            lo = lo_ref[rows, :]
            ok = cnt >= float(topk)
            if final_ref is not None:
                ok = jnp.logical_and(ok, final_ref[rows, :] <= 0.0)
            lo_ref[rows, :] = jnp.where(ok, lo + bit, lo)
            cnt_ref[rows, :] = jnp.where(ok, cnt, cnt_ref[rows, :])

        stage_ref[half_b, :] = jnp.zeros((SEL_ROWS, LANES), jnp.float32)

        def pass_body(b, carry):
            bit = lax.shift_left(jnp.int32(1), 31 - b)
            prev_bit = lax.shift_left(jnp.int32(1), jnp.minimum(32 - b, 31))
            staged = stage_ref[half_b, :]
            acc_a = _count_lanes(src_ref, 0, n_steps, _key_to_f32(lo_ref[half_a, :] + bit))
            settle(half_b, staged, prev_bit)
            acc_b = _count_lanes(src_ref, SEL_ROWS, n_steps, _key_to_f32(lo_ref[half_b, :] + bit))
            settle(half_a, acc_a, bit)
            stage_ref[half_b, :] = acc_b
            return carry
        lax.fori_loop(first_bit, 32, pass_body, 0)
        settle(half_b, stage_ref[half_b, :], jnp.int32(1))
        return

    def cond(c):
        b, pending = c
        return jnp.logical_and(b < 32, pending > 0)

    def body(c):
        b, _ = c
        bit = lax.shift_left(jnp.int32(1), 31 - b)
        off = None
        for rb in range(rows_total // SEL_ROWS):
            rows = pl.ds(rb * SEL_ROWS, SEL_ROWS)
            cand = lo_ref[rows, :] + bit
            cnt = _row_total(_count_lanes(src_ref, rb * SEL_ROWS, n_steps, _key_to_f32(cand)))
            ok = cnt >= float(topk)
            if final_ref is not None:
                ok = jnp.logical_and(ok, final_ref[rows, :] <= 0.0)
            lo_ref[rows, :] = jnp.where(ok, cand, lo_ref[rows, :])
            cnt = jnp.where(ok, cnt, cnt_ref[rows, :])
            cnt_ref[rows, :] = cnt
            miss = jnp.abs(cnt - float(topk))
            if final_ref is not None:
                miss = jnp.where(final_ref[rows, :] > 0.0, 0.0, miss)
            off = miss if off is None else jnp.maximum(off, miss)
        return b + 1, (jnp.max(off) > 0.0).astype(jnp.int32)
    lax.while_loop(cond, body, (first_bit, jnp.int32(1)))


def _lane_candidates(keys_ref, cand_ref, n_groups):
    n_stack = CAND_DEPTH * CAND_SPLIT

    def rg_body(rg, carry):
        r0 = pl.multiple_of(rg * 8, 8)

        def col_body(s, st):
            st = list(st)
            c0 = pl.multiple_of(s * 8 * LANES, 8 * LANES)
            for u in range(8):
                x = keys_ref[pl.ds(r0, 8), pl.ds(c0 + u * LANES, LANES)]
                base = (u % CAND_SPLIT) * CAND_DEPTH
                for d in range(CAND_DEPTH):
                    cur = st[base + d]
                    st[base + d] = jnp.maximum(cur, x)
                    x = jnp.minimum(cur, x)
            return tuple(st)
        init = tuple(jnp.full((8, LANES), -jnp.inf, jnp.float32) for _ in range(n_stack))
        st = lax.fori_loop(0, n_groups, col_body, init)
        for k in range(n_stack):
            cand_ref[pl.ds(r0, 8), k * LANES:(k + 1) * LANES] = st[k]
        return carry
    lax.fori_loop(0, cand_ref.shape[0] // 8, rg_body, 0)


def _ties_to_keep(src_ref, n_steps, lo_ref, cnt_ref, part_ref, need_ref, topk):
    rows_total = lo_ref.shape[0]
    surplus = jnp.where(lo_ref[...] > NEG_INF_KEY, cnt_ref[...] - float(topk), 0.0)
    need_ref[...] = jnp.full(need_ref.shape, NO_TIE_LIMIT, jnp.float32)

    @pl.when(jnp.max(surplus) > 0.0)
    def _():
        for rb in range(rows_total // SEL_ROWS):
            rows = pl.ds(rb * SEL_ROWS, SEL_ROWS)
            thr = _key_to_f32(jnp.maximum(lo_ref[rows, :], NEG_INF_KEY))
            part_ref[rows, :] = _count_lanes(src_ref, rb * SEL_ROWS, n_steps, thr, strict=True)
        above = _row_total(part_ref[...])
        need_ref[...] = jnp.where(surplus > 0.0, float(topk) - above, NO_TIE_LIMIT)


def _tie_index_bound(keys_ref, thr_ref, need_ref, seen_ref, sig_ref, n_chunks):
    kk_i = lax.broadcasted_iota(jnp.int32, (LANES, LANES), 0)
    jj_i = lax.broadcasted_iota(jnp.int32, (LANES, LANES), 1)
    tri = _bf(jnp.where(kk_i <= jj_i, 1.0, 0.0))
    seen_ref[...] = jnp.zeros(seen_ref.shape, jnp.float32)

    def cond(c):
        j, pending = c
        return jnp.logical_and(j < n_chunks, pending > 0)

    def body(c):
        j, _ = c
        thr, need = thr_ref[...], need_ref[...]
        seen, sig = seen_ref[...], sig_ref[...]
        for u in range(TIE_CHUNK):
            c0 = pl.multiple_of((j * TIE_CHUNK + u) * LANES, LANES)
            tie = jnp.where(keys_ref[:, pl.ds(c0, LANES)] == thr, 1.0, 0.0)
            rank = seen + _dot(_bf(tie), tri)
            after = seen + _row_total(tie)
            before = _row_total(jnp.where(rank < need, 1.0, 0.0))
            here = jnp.where(seen < need, jnp.where(after >= need, 1.0, 0.0), 0.0)
            sig = jnp.where(here > 0.0, c0.astype(jnp.float32) + before, sig)
            seen = after
        seen_ref[...] = seen
        sig_ref[...] = sig
        waiting = jnp.where(need < NO_TIE_LIMIT, jnp.where(seen < need, 1.0, 0.0), 0.0)
        return j + 1, (jnp.max(waiting) > 0.0).astype(jnp.int32)
    lax.while_loop(cond, body, (jnp.int32(0), jnp.int32(1)))


def _dsa_kernel(qp_ref, iqp_ref, w_ref, g_ref, k_ref, va_ref, ik_ref, o_ref,
                keys_ref, cand_ref, lo_ref, thr_ref, cnt_ref, part_ref, need_ref, sig_ref,
                m_ref, acc_ref, full_ref, *, topk):
    QB, TK = DSA_QB, IDX_TK
    G = DSA_HEADS // DSA_KV_HEADS
    i = pl.program_id(0)
    n_idx = i + 1
    n_att = (n_idx * IDX_TK + ATT_TK - 1) // ATT_TK
    n_sel = n_att * (ATT_TK // SEL_LANES)
    row_id = lax.broadcasted_iota(jnp.int32, (QB, TK), 0)
    col_id = lax.broadcasted_iota(jnp.int32, (QB, TK), 1)

    iq = iqp_ref[...].reshape(IDX_HEADS * QB, LANES)
    wb = [jnp.broadcast_to(w_ref[:, h:h + 1], (QB, TK)) for h in range(IDX_HEADS)]

    def score_tile(j, diag):
        c0 = pl.multiple_of(j * TK, TK)
        s_all = _dot_nt(iq, ik_ref[pl.ds(c0, TK), :])
        score = None
        for h in range(IDX_HEADS):
            term = wb[h] * jnp.maximum(s_all[h * QB:(h + 1) * QB], 0.0)
            score = term if score is None else score + term
        if diag:
            score = jnp.where(col_id <= row_id, score, -jnp.inf)
        keys_ref[:, pl.ds(c0, TK)] = score

    def score_body(j, carry):
        score_tile(j, False)
        return carry

    def score_quad(q, carry):
        for u in range(4):
            score_tile(q * 4 + u, False)
        return carry
    n_quads = lax.shift_right_logical(i, 2)
    lax.fori_loop(0, n_quads, score_quad, 0)
    lax.fori_loop(n_quads * 4, i, score_body, 0)
    score_tile(i, True)

    def blank_body(j, carry):
        c0 = pl.multiple_of(j * TK, TK)
        keys_ref[:, pl.ds(c0, TK)] = jnp.full((QB, TK), -jnp.inf, jnp.float32)
        return carry
    n_vis = n_idx * TK
    windows = [w for w in (CAND_W, 2 * CAND_W) if w <= keys_ref.shape[1]]
    att_end = n_att * (ATT_TK // TK)
    blank_end = att_end
    for w in reversed(windows):
        blank_end = jnp.where(n_vis <= w, jnp.maximum(att_end, w // TK), blank_end)
    lax.fori_loop(n_idx, blank_end, blank_body, 0)

    full_ref[0] = jnp.int32(1)

    @pl.when(n_att * ATT_TK > 2 * CAND_W)
    def _():
        _lane_candidates(keys_ref, cand_ref, n_att * (ATT_TK // (8 * LANES)))
        first_bit = _search_window(cand_ref, lo_ref, part_ref)
        _radix_search(cand_ref, CAND_W // SEL_LANES, lo_ref, cnt_ref, topk,
                      first_bit=first_bit, final_ref=part_ref, stage_ref=need_ref)
        thr = _key_to_f32(jnp.maximum(lo_ref[...], NEG_INF_KEY))
        exact = cnt_ref[...] == float(topk)
        lost = None
        for v in range(CAND_SPLIT):
            last = cand_ref[:, (v * CAND_DEPTH + CAND_DEPTH - 1) * LANES:(v * CAND_DEPTH + CAND_DEPTH) * LANES]
            hit = jnp.where(last > thr, 1.0, jnp.where(last == thr, jnp.where(exact, 1.0, 0.0), 0.0))
            lost = hit if lost is None else jnp.maximum(lost, hit)
        full_ref[0] = (jnp.max(lost) > 0.0).astype(jnp.int32)
        _ties_to_keep(cand_ref, CAND_W // SEL_LANES, lo_ref, cnt_ref, part_ref, need_ref, topk)

    @pl.when(full_ref[0] > 0)
    def _():
        def search_rows(n_steps):
            _radix_search(keys_ref, n_steps, lo_ref, cnt_ref, topk, stage_ref=need_ref)
            _ties_to_keep(keys_ref, n_steps, lo_ref, cnt_ref, part_ref, need_ref, topk)
        below = 0
        for w in windows:
            pl.when(jnp.logical_and(n_vis > below, n_vis <= w))(
                functools.partial(search_rows, w // SEL_LANES))
            below = w
        pl.when(n_vis > below)(functools.partial(search_rows, n_sel))

    tau = lo_ref[...]
    found = tau > NEG_INF_KEY
    thr_ref[...] = jnp.where(found, _key_to_f32(jnp.maximum(tau, NEG_INF_KEY)), F32_LOWEST)
    sig_ref[...] = jnp.full(sig_ref.shape, NO_TIE_LIMIT, jnp.float32)

    @pl.when(jnp.min(need_ref[...]) < NO_TIE_LIMIT)
    def _():
        _tie_index_bound(keys_ref, thr_ref, need_ref, part_ref, sig_ref, n_att * (ATT_TK // (TIE_CHUNK * LANES)))

    m_ref[...] = jnp.full(m_ref.shape, NEG_INIT, jnp.float32)
    acc_ref[...] = jnp.zeros(acc_ref.shape, jnp.float32)
    n_grp = ATT_TK // LANES
    lane_f = lax.broadcasted_iota(jnp.int32, (QB, LANES), 1).astype(jnp.float32)

    def attn_body(j, carry):
        c0 = pl.multiple_of(j * ATT_TK, ATT_TK)
        tau_b = thr_ref[...]
        sig_rel = sig_ref[...] - c0.astype(jnp.float32)
        parts = []
        for u in range(n_grp):
            kk = keys_ref[:, pl.ds(c0 + u * LANES, LANES)]
            tie_bias = jnp.where(lane_f + float(u * LANES) <= sig_rel, 0.0, NEG_MASK)
            parts.append(jnp.where(kk > tau_b, 0.0, jnp.where(kk == tau_b, tie_bias, NEG_MASK)))
        bias = jnp.concatenate(parts, axis=1)
        kt = k_ref[pl.ds(c0, ATT_TK), :]
        for n in range(DSA_KV_HEADS):
            qn = qp_ref[n * G:(n + 1) * G].reshape(G * QB, LANES)
            logits = _dot_nt(qn, kt)
            ps = []
            for gq in range(G):
                h = n * G + gq
                lm = logits[gq * QB:(gq + 1) * QB] + bias
                tmax = lm[:, 0:LANES]
                for u in range(1, n_grp):
                    tmax = jnp.maximum(tmax, lm[:, u * LANES:(u + 1) * LANES])
                m_old = m_ref[h]
                m_new = jnp.maximum(m_old, jnp.max(tmax, axis=-1, keepdims=True))
                ps.append(jnp.concatenate(
                    [_bf(jnp.exp2(lm[:, u * LANES:(u + 1) * LANES] - m_new)) for u in range(n_grp)], axis=1))
                m_ref[h] = m_new
                acc_ref[h] = jnp.exp2(m_old - m_new) * acc_ref[h]
            pv = _dot(jnp.concatenate(ps, axis=0), va_ref[n, pl.ds(c0, ATT_TK), :])
            for gq in range(G):
                h = n * G + gq
                acc_ref[h] = acc_ref[h] + pv[gq * QB:(gq + 1) * QB]
        return carry
    lax.fori_loop(0, n_att, attn_body, 0)

    lane = lax.broadcasted_iota(jnp.int32, (QB, LANES), 1)
    g = g_ref[...]
    for pair in range(DSA_HEADS // 2):
        halves = []
        for pos in range(2):
            a = acc_ref[2 * pair + pos]
            halves.append(a * (1.0 / pltpu.roll(a, DSA_DH, 1)))
        o = jnp.where(lane < DSA_DH, halves[0], pltpu.roll(halves[1], DSA_DH, 1))
        sl = slice(pair * LANES, (pair + 1) * LANES)
        o_ref[:, sl] = _bf(o * _silu(g[:, sl]))


def _dsa_call(p, qp, kk, va, iqp, ik, w4):
    s = p.shape[0]
    topk = min(TOPK_MAX, s // 4)
    QB = DSA_QB
    assert IDX_TK == QB and s % ATT_TK == 0 and ATT_TK % SEL_LANES == 0 and s // LANES <= 256
    once = pl.Buffered(1)
    return pl.pallas_call(
        functools.partial(_dsa_kernel, topk=topk),
        grid=(s // QB,),
        in_specs=[pl.BlockSpec((DSA_HEADS, QB, LANES), lambda i: (0, i, 0)),
                  pl.BlockSpec((IDX_HEADS, QB, LANES), lambda i: (0, i, 0)),
                  pl.BlockSpec((QB, LANES), lambda i: (i, 0)),
                  _pspec(QB, "dsa_g", 512),
                  pl.BlockSpec((s, LANES), lambda i: (0, 0), pipeline_mode=once),
                  pl.BlockSpec((DSA_KV_HEADS, s, LANES), lambda i: (0, 0, 0), pipeline_mode=once),
                  pl.BlockSpec((s, LANES), lambda i: (0, 0), pipeline_mode=once)],
        out_specs=pl.BlockSpec((QB, 512), lambda i: (i, 0)),
        out_shape=jax.ShapeDtypeStruct((s, 512), jnp.bfloat16),
        scratch_shapes=[pltpu.VMEM((QB, s), jnp.float32),
                        pltpu.VMEM((QB, CAND_W), jnp.float32),
                        pltpu.VMEM((QB, LANES), jnp.int32),
                        pltpu.VMEM((QB, LANES), jnp.float32),
                        pltpu.VMEM((QB, LANES), jnp.float32),
                        pltpu.VMEM((QB, LANES), jnp.float32),
                        pltpu.VMEM((QB, LANES), jnp.float32),
                        pltpu.VMEM((QB, LANES), jnp.float32),
                        pltpu.VMEM((DSA_HEADS, QB, LANES), jnp.float32),
                        pltpu.VMEM((DSA_HEADS, QB, LANES), jnp.float32),
                        pltpu.SMEM((1,), jnp.int32)],
        compiler_params=_cparams(1, vmem_mb=56),
        name="dsa",
    )(qp, iqp, w4, p, kk, va, ik)


def _merge_kernel(x_ref, ret_ref, dsa_ref, gl_ref, m_ref, wr_ref, wd_ref, wg_ref, wo_ref,
                  post_ref, gate_ref, o_ref):
    d = D_MODEL
    y = (_sigmoid(m_ref[:, 0:d]) * _dot(ret_ref[...], wr_ref[...])
         + _sigmoid(m_ref[:, d:2 * d]) * _dot(dsa_ref[...], wd_ref[...])
         + _sigmoid(m_ref[:, 2 * d:3 * d]) * _dot(gl_ref[...], wg_ref[...]))
    y = _dot(_bf(y), wo_ref[...])
    yn = y * lax.rsqrt(jnp.mean(y * y, axis=-1, keepdims=True) + RMS_EPS) * post_ref[...]
    o_ref[...] = x_ref[...] + gate_ref[...] * yn


def _merge_call(x2, ret, dsa, gl, p, wr, wd, wg, wo, post, gate):
    s, d = x2.shape
    tm = min(512, s)
    rows = lambda w: pl.BlockSpec((tm, w), lambda i: (i, 0))
    whole = lambda a: pl.BlockSpec(a.shape, lambda i: (0, 0))
    return pl.pallas_call(
        _merge_kernel,
        grid=(s // tm,),
        in_specs=[rows(d), rows(512), rows(512), rows(512), _pspec(tm, "merge", 3072),
                  whole(wr), whole(wd), whole(wg), whole(wo), whole(post), whole(gate)],
        out_specs=rows(d),
        out_shape=jax.ShapeDtypeStruct((s, d), jnp.float32),
        compiler_params=_cparams(1),
        name="merge_out",
    )(x2, ret, dsa, gl, p, wr, wd, wg, wo, post, gate)


def _pack_w_in(w_in):
    depth, d, _ = w_in.shape
    w_in = w_in.astype(jnp.bfloat16)
    zeros = lambda n: jnp.zeros((depth, d, n), w_in.dtype)
    src = lambda name: w_in[:, :, _SRC[name][0]:_SRC[name][0] + _SRC[name][1]]
    pieces, at = [], 0
    for name, width in _PACK:
        assert at == PCOL[name]
        if name == "idx_kw":
            cols = [src("idx_k"), src("idx_w"), zeros(width - IDX_DH - IDX_HEADS)]
        elif name == "gla_a":
            cols = [src("gla_a"), zeros(width - GLA_RANK)]
        else:
            cols = [src(name)]
        pieces += cols
        at += width
    pieces.append(zeros(P_WIDTH - at))
    return jnp.concatenate(pieces, axis=-1).astype(jnp.bfloat16)


def kernel(x, c, positions, ada_w, ada_b, pre_norm, post_norm, w_in, gla_w_lr, gla_b_lr,
           w_br_ret, w_br_dsa, w_br_gla, w_out):
    b, s, d = x.shape
    assert b == 1 and d == D_MODEL
    depth = ada_w.shape[0]
    x2 = x.reshape(s, d)
    mod = _mod_call(jnp.broadcast_to(c, (8, d)), ada_w, ada_b.reshape(depth, 1, 3 * d))[:, 0:1, :]
    rcos, rsin, dcos, dsin = _tab_call(positions.reshape(s, 1))
    w_pack = _pack_w_in(w_in)
    wlr_pad = jnp.pad(gla_w_lr, ((0, 0), (0, LANES - GLA_RANK), (0, 0)))
    for l in range(depth):
        shift, scale, gate = mod[l, :, 0:d], mod[l, :, d:2 * d], mod[l, :, 2 * d:3 * d]
        p = _proj_call(x2, pre_norm[l][None, :], scale, shift, w_pack[l])
        ret = _ret_call(p, rcos, rsin)
        gl = _gla_call(p, wlr_pad[l], gla_b_lr[l][None, :])
        qp, kk, va, iqp, ik, w4 = _dprep_call(p, dcos, dsin)
        dsa = _dsa_call(p, qp, kk, va, iqp, ik, w4)
        x2 = _merge_call(x2, ret, dsa, gl, p, _bf(w_br_ret[l]), _bf(w_br_dsa[l]), _bf(w_br_gla[l]),
                         _bf(w_out[l]), post_norm[l][None, :], gate)
    return x2.reshape(b, s, d)
```

```python
import functools
import math

import jax
import jax.numpy as jnp
from jax import lax
from jax.experimental import pallas as pl
from jax.experimental.pallas import tpu as pltpu

D_MODEL = 1024
DEPTH = 4
RET_HEADS, RET_DK, RET_DV, RET_CHUNK, RET_THETA = 4, 64, 128, 128, 10000.0
DSA_HEADS, DSA_KV_HEADS, DSA_DH = 8, 2, 64
DSA_ROT = DSA_DH // 4
ROPE_THETA = 500000.0
IDX_HEADS, IDX_DH = 4, 64
TOPK_MAX = 256
GLA_HEADS, GLA_DK, GLA_DV, GLA_RANK, GLA_TAU, GLA_CHUNK = 4, 64, 128, 16, 16.0, 64
RMS_EPS = 1e-6
LANES = 128

_SRC = {}
_off = 0
for _name, _w in (("ret_q", 256), ("ret_k", 256), ("ret_v", 512), ("ret_g", 512),
                  ("dsa_q", 512), ("dsa_k", 128), ("dsa_v", 128), ("dsa_g", 512),
                  ("idx_q", 256), ("idx_k", 64), ("idx_w", 4),
                  ("gla_q", 256), ("gla_k", 256), ("gla_v", 512), ("gla_g", 512), ("gla_a", 16),
                  ("merge", 3072)):
    _SRC[_name] = (_off, _w)
    _off += _w
IN_WIDTH = _off

_PACK = (("merge", 3072), ("ret_q", 256), ("ret_k", 256), ("ret_v", 512), ("ret_g", 512),
         ("dsa_q", 512), ("dsa_g", 512), ("gla_v", 512), ("gla_g", 512),
         ("gla_q", 256), ("gla_k", 256), ("idx_q", 256),
         ("dsa_k", 128), ("dsa_v", 128), ("idx_kw", 128), ("gla_a", 128))
PCOL = {}
_off = 0
for _name, _w in _PACK:
    assert _off % min(_w, 1024) == 0
    PCOL[_name] = _off
    _off += _w
P_WIDTH = 8192
assert _off <= P_WIDTH

LOG2E = math.log2(math.e)
INT_MIN = -(2 ** 31)
F32_LOWEST = -3.4028234663852886e38
NEG_INF_KEY = -0x7F800000
NEG_INIT = -1e30
NEG_MASK = -2e30


def _cparams(n_axes, vmem_mb=48):
    return pltpu.CompilerParams(dimension_semantics=("arbitrary",) * n_axes,
                                vmem_limit_bytes=vmem_mb * 1024 * 1024)


def _bf(x):
    return x.astype(jnp.bfloat16)


def _f32(x):
    return x.astype(jnp.float32)


def _dot(a, b):
    return jnp.dot(a, b, preferred_element_type=jnp.float32)


def _dot_nt(a, b):
    return lax.dot_general(a, b, (((1,), (1,)), ((), ())), preferred_element_type=jnp.float32)


def _dot_tn(a, b):
    return lax.dot_general(a, b, (((0,), (0,)), ((), ())), preferred_element_type=jnp.float32)


def _split3(x):
    hi = _bf(x)
    r1 = x - hi.astype(jnp.float32)
    mid = _bf(r1)
    lo = _bf(r1 - mid.astype(jnp.float32))
    return hi, mid, lo


def _silu(x):
    return x * (1.0 / (1.0 + jnp.exp(-x)))


def _sigmoid(x):
    return 1.0 / (1.0 + jnp.exp(-x))


def _mod_kernel(c_ref, w_ref, b_ref, o_ref):
    c = c_ref[...]
    ca = _silu(c)
    acc = None
    for t in _split3(ca):
        for u in _split3(w_ref[0]):
            part = _dot(t, u)
            acc = part if acc is None else acc + part
    o_ref[0] = acc + b_ref[0]


def _mod_call(c8, ada_w, ada_b3):
    depth, d, n = ada_w.shape
    tn = 1024
    return pl.pallas_call(
        _mod_kernel,
        grid=(depth, n // tn),
        in_specs=[pl.BlockSpec((8, d), lambda l, j: (0, 0)),
                  pl.BlockSpec((1, d, tn), lambda l, j: (l, 0, j)),
                  pl.BlockSpec((1, 1, tn), lambda l, j: (l, 0, j))],
        out_specs=pl.BlockSpec((1, 8, tn), lambda l, j: (l, 0, j)),
        out_shape=jax.ShapeDtypeStruct((depth, 8, n), jnp.float32),
        compiler_params=_cparams(2),
        name="adaln_mod",
    )(c8, ada_w, ada_b3)


def _tab_kernel(pos_ref, rf_ref, rs_ref, df_ref, ds_ref, rc_o, rsn_o, dc_o, dsn_o):
    pos = pos_ref[...].astype(jnp.float32)
    ang = pos * rf_ref[...]
    rc_o[...] = jnp.cos(ang)
    rsn_o[...] = jnp.sin(ang) * rs_ref[...]
    ang = pos * df_ref[...]
    dc_o[...] = jnp.cos(ang)
    dsn_o[...] = jnp.sin(ang) * ds_ref[...]


def _rope_rows():
    half = RET_DK // 2
    f = RET_THETA ** (-jnp.arange(half, dtype=jnp.float32) * 2.0 / RET_DK)
    rf = jnp.tile(jnp.concatenate([f, f]), RET_HEADS)[None, :]
    rs = jnp.tile(jnp.concatenate([-jnp.ones(half), jnp.ones(half)]), RET_HEADS)[None, :].astype(jnp.float32)
    half = DSA_ROT // 2
    f = ROPE_THETA ** (-jnp.arange(half, dtype=jnp.float32) * 2.0 / DSA_ROT)
    z = jnp.zeros(DSA_DH - DSA_ROT, jnp.float32)
    df = jnp.tile(jnp.concatenate([f, f, z]), 2)[None, :]
    ds = jnp.tile(jnp.concatenate([-jnp.ones(half), jnp.ones(half), z]), 2)[None, :].astype(jnp.float32)
    return rf, rs, df, ds


def _tab_call(pos_col):
    s = pos_col.shape[0]
    tm = min(1024, s)
    rf, rs, df, ds = _rope_rows()
    row = lambda w: pl.BlockSpec((1, w), lambda i: (0, 0))
    out = lambda w: pl.BlockSpec((tm, w), lambda i: (i, 0))
    return pl.pallas_call(
        _tab_kernel,
        grid=(s // tm,),
        in_specs=[pl.BlockSpec((tm, 1), lambda i: (i, 0)), row(256), row(256), row(128), row(128)],
        out_specs=[out(256), out(256), out(128), out(128)],
        out_shape=[jax.ShapeDtypeStruct((s, 256), jnp.float32), jax.ShapeDtypeStruct((s, 256), jnp.float32),
                   jax.ShapeDtypeStruct((s, 128), jnp.float32), jax.ShapeDtypeStruct((s, 128), jnp.float32)],
        compiler_params=_cparams(1),
        name="rope_tables",
    )(pos_col, rf, rs, df, ds)


def _swap_halves(x, half, period):
    w = x.shape[-1]
    lane = lax.broadcasted_iota(jnp.int32, x.shape, x.ndim - 1) & (period - 1)
    up = pltpu.roll(x, w - half, x.ndim - 1)
    dn = pltpu.roll(x, half, x.ndim - 1)
    return jnp.where(lane < half, up, dn)


def _rope(x, cos, sin_signed, half, period):
    return x * cos + _swap_halves(x, half, period) * sin_signed


def _proj_kernel(x_ref, pre_ref, sc_ref, sh_ref, w_ref, o_ref):
    x = x_ref[...]
    xn = x * lax.rsqrt(jnp.mean(x * x, axis=-1, keepdims=True) + RMS_EPS)
    h = xn * pre_ref[...] * (1.0 + sc_ref[...]) + sh_ref[...]
    o_ref[...] = _bf(_dot(_bf(h), w_ref[...]))


def _proj_call(x2, pre, scale, shift, w_pack):
    s, d = x2.shape
    tm, tn = min(512, s), 2048
    vec = pl.BlockSpec((1, d), lambda j, i: (0, 0))
    return pl.pallas_call(
        _proj_kernel,
        grid=(P_WIDTH // tn, s // tm),
        in_specs=[pl.BlockSpec((tm, d), lambda j, i: (i, 0)), vec, vec, vec,
                  pl.BlockSpec((d, tn), lambda j, i: (0, j))],
        out_specs=pl.BlockSpec((tm, tn), lambda j, i: (i, j)),
        out_shape=jax.ShapeDtypeStruct((s, P_WIDTH), jnp.bfloat16),
        compiler_params=_cparams(2),
        name="in_proj",
    )(x2, pre, scale, shift, w_pack)


def _pspec(tm, name, width):
    blk = PCOL[name] // width
    assert PCOL[name] % width == 0
    return pl.BlockSpec((tm, width), lambda i: (i, blk))


def _ret_log_g(h):
    return math.log1p(-(2.0 ** (-5.0 - h)))


def _ret_kernel(q_ref, k_ref, v_ref, g_ref, cos_ref, sin_ref, o_ref,
                state_ref, decay_ref, qdec_ref, kend_ref, *, chunks):
    C = RET_CHUNK

    @pl.when(pl.program_id(0) == 0)
    def _():
        state_ref[...] = jnp.zeros_like(state_ref)
        ii = lax.broadcasted_iota(jnp.int32, (C, C), 0)
        jj = lax.broadcasted_iota(jnp.int32, (C, C), 1)
        rel = (ii - jj).astype(jnp.float32)
        row = lax.broadcasted_iota(jnp.int32, (C, RET_DK), 0).astype(jnp.float32)
        for h in range(RET_HEADS):
            lg = _ret_log_g(h)
            decay_ref[h] = jnp.where(rel >= 0, jnp.exp(lg * jnp.maximum(rel, 0.0)), 0.0)
            qdec_ref[:, h * RET_DK:(h + 1) * RET_DK] = jnp.exp((row + 1.0) * lg)
            kend_ref[:, h * RET_DK:(h + 1) * RET_DK] = jnp.exp((C - 1.0 - row) * lg)

    for c in range(chunks):
        rows = slice(c * C, (c + 1) * C)
        cos, sin = cos_ref[rows, :], sin_ref[rows, :]
        q = _rope(_f32(q_ref[rows, :]), cos, sin, RET_DK // 2, RET_DK) * (RET_DK ** -0.5)
        k = _rope(_f32(k_ref[rows, :]), cos, sin, RET_DK // 2, RET_DK)
        qd = _bf(q * qdec_ref[...])
        kd = _bf(k * kend_ref[...])
        qb, kb = _bf(q), _bf(k)
        v = v_ref[rows, :]
        g = _f32(g_ref[rows, :])
        for h in range(RET_HEADS):
            dk = slice(h * RET_DK, (h + 1) * RET_DK)
            dv = slice(h * RET_DV, (h + 1) * RET_DV)
            vh = _bf(v[:, dv])
            scores = _dot_nt(qb[:, dk], kb[:, dk]) * decay_ref[h]
            st = state_ref[h]
            o = _dot(_bf(scores), vh) + _dot(qd[:, dk], _bf(st))
            state_ref[h] = math.exp(C * _ret_log_g(h)) * st + _dot_tn(kd[:, dk], vh)
            o = o * lax.rsqrt(jnp.mean(o * o, axis=-1, keepdims=True) + RMS_EPS)
            o_ref[rows, dv] = _bf(o * _silu(g[:, dv]))


def _ret_call(p, rcos, rsin):
    s = p.shape[0]
    chunks = 2
    tm = RET_CHUNK * chunks
    tab = pl.BlockSpec((tm, 256), lambda i: (i, 0))
    return pl.pallas_call(
        functools.partial(_ret_kernel, chunks=chunks),
        grid=(s // tm,),
        in_specs=[_pspec(tm, "ret_q", 256), _pspec(tm, "ret_k", 256), _pspec(tm, "ret_v", 512),
                  _pspec(tm, "ret_g", 512), tab, tab],
        out_specs=pl.BlockSpec((tm, 512), lambda i: (i, 0)),
        out_shape=jax.ShapeDtypeStruct((s, 512), jnp.bfloat16),
        scratch_shapes=[pltpu.VMEM((RET_HEADS, RET_DK, RET_DV), jnp.float32),
                        pltpu.VMEM((RET_HEADS, RET_CHUNK, RET_CHUNK), jnp.float32),
                        pltpu.VMEM((RET_CHUNK, RET_HEADS * RET_DK), jnp.float32),
                        pltpu.VMEM((RET_CHUNK, RET_HEADS * RET_DK), jnp.float32)],
        compiler_params=_cparams(1),
        name="retention",
    )(p, p, p, p, rcos, rsin)


def _gla_kernel(q_ref, k_ref, v_ref, g_ref, a_ref, wlr_ref, blr_ref, o_ref, state_ref, *, chunks):
    C = GLA_CHUNK

    @pl.when(pl.program_id(0) == 0)
    def _():
        state_ref[...] = jnp.zeros_like(state_ref)

    ii = lax.broadcasted_iota(jnp.int32, (C, C), 0)
    jj = lax.broadcasted_iota(jnp.int32, (C, C), 1)
    causal = jj <= ii
    tril = _bf(jnp.where(causal, 1.0, 0.0))
    wlr = wlr_ref[...]
    w_hi, w_mid, w_lo = _split3(wlr)

    for c in range(chunks):
        rows = slice(c * C, (c + 1) * C)
        a_hi, a_mid, a_lo = _split3(_f32(a_ref[rows, :]))
        z = (_dot(a_hi, w_hi) + (_dot(a_hi, w_mid) + _dot(a_mid, w_hi))
             + (_dot(a_hi, w_lo) + _dot(a_mid, w_mid) + _dot(a_lo, w_hi))) + blr_ref[...]
        log_a = (jnp.minimum(z, 0.0) - jnp.log1p(jnp.exp(-jnp.abs(z)))) * (1.0 / GLA_TAU)
        l_hi, l_mid, l_lo = _split3(log_a)
        bcum = _dot(tril, l_hi) + _dot(tril, l_mid) + _dot(tril, l_lo)
        b_mid = bcum[C // 2 - 1:C // 2, :]
        b_last = bcum[C - 1:C, :]
        q = _f32(q_ref[rows, :]) * (GLA_DK ** -0.5)
        k = _f32(k_ref[rows, :])
        qt = _bf(q * jnp.exp(bcum - b_mid))
        kt = _bf(k * jnp.exp(b_mid - bcum))
        qg = _bf(q * jnp.exp(bcum))
        kd = _bf(k * jnp.exp(b_last - bcum))
        e_last = jnp.exp(b_last)
        v = v_ref[rows, :]
        g = _f32(g_ref[rows, :])
        for h in range(GLA_HEADS):
            dk = slice(h * GLA_DK, (h + 1) * GLA_DK)
            dv = slice(h * GLA_DV, (h + 1) * GLA_DV)
            vh = _bf(v[:, dv])
            attn = jnp.where(causal, _dot_nt(qt[:, dk], kt[:, dk]), 0.0)
            st = state_ref[h]
            o = _dot(_bf(attn), vh) + _dot_nt(qg[:, dk], _bf(st))
            state_ref[h] = e_last[:, dk] * st + _dot_tn(vh, kd[:, dk])
            o = o * lax.rsqrt(jnp.mean(o * o, axis=-1, keepdims=True) + RMS_EPS)
            o_ref[rows, dv] = _bf(o * _silu(g[:, dv]))


def _gla_call(p, wlr_pad, blr):
    s = p.shape[0]
    chunks = 4
    tm = GLA_CHUNK * chunks
    return pl.pallas_call(
        functools.partial(_gla_kernel, chunks=chunks),
        grid=(s // tm,),
        in_specs=[_pspec(tm, "gla_q", 256), _pspec(tm, "gla_k", 256), _pspec(tm, "gla_v", 512),
                  _pspec(tm, "gla_g", 512), _pspec(tm, "gla_a", 128),
                  pl.BlockSpec((128, 256), lambda i: (0, 0)), pl.BlockSpec((1, 256), lambda i: (0, 0))],
        out_specs=pl.BlockSpec((tm, 512), lambda i: (i, 0)),
        out_shape=jax.ShapeDtypeStruct((s, 512), jnp.bfloat16),
        scratch_shapes=[pltpu.VMEM((GLA_HEADS, GLA_DV, GLA_DK), jnp.float32)],
        compiler_params=_cparams(1),
        name="gla",
    )(p, p, p, p, p, wlr_pad, blr)


def _place_head(pair, src_pos, dst_pos):
    lane = lax.broadcasted_iota(jnp.int32, pair.shape, 1)
    if src_pos != dst_pos:
        pair = pltpu.roll(pair, 64, 1)
    keep = (lane < 64) if dst_pos == 0 else (lane >= 64)
    return jnp.where(keep, pair, 0.0)


def _dprep_kernel(q_ref, k_ref, v_ref, iq_ref, ikw_ref, cos_ref, sin_ref,
                  qp_o, k_o, v_o, iqp_o, ik_o, w_o):
    cos, sin = cos_ref[...], sin_ref[...]
    half = DSA_ROT // 2
    cos4 = jnp.concatenate([cos] * 4, axis=1)
    sin4 = jnp.concatenate([sin] * 4, axis=1)
    q = _rope(_f32(q_ref[...]), cos4, sin4, half, DSA_DH) * (DSA_DH ** -0.5 * LOG2E)
    for h in range(DSA_HEADS):
        pair = q[:, (h // 2) * 128:(h // 2 + 1) * 128]
        qp_o[h] = _bf(_place_head(pair, h % 2, h // (DSA_HEADS // DSA_KV_HEADS)))
    k_o[...] = _bf(_rope(_f32(k_ref[...]), cos, sin, half, DSA_DH))
    v = _f32(v_ref[...])
    lane = lax.broadcasted_iota(jnp.int32, v.shape, 1)
    v_o[0] = _bf(jnp.where(lane < DSA_DH, v, 1.0))
    v_o[1] = _bf(jnp.where(lane < DSA_DH, pltpu.roll(v, DSA_DH, 1), 1.0))
    iq = _rope(_f32(iq_ref[...]), cos4[:, :256], sin4[:, :256], half, IDX_DH)
    for h in range(IDX_HEADS):
        pair = iq[:, (h // 2) * 128:(h // 2 + 1) * 128]
        iqp_o[h] = _bf(_place_head(pair, h % 2, 0))
    ikw = _f32(ikw_ref[...])
    lane = lax.broadcasted_iota(jnp.int32, ikw.shape, 1)
    ik_o[...] = _bf(jnp.where(lane < IDX_DH, _rope(ikw, cos, sin, half, DSA_DH), 0.0))
    wscale = (IDX_HEADS ** -0.5) * (IDX_DH ** -0.5)
    w_o[...] = pltpu.roll(ikw, 128 - IDX_DH, 1) * wscale


def _dprep_call(p, dcos, dsin):
    s = p.shape[0]
    tm = min(512, s)
    tab = pl.BlockSpec((tm, 128), lambda i: (i, 0))
    return pl.pallas_call(
        _dprep_kernel,
        grid=(s // tm,),
        in_specs=[_pspec(tm, "dsa_q", 512), _pspec(tm, "dsa_k", 128), _pspec(tm, "dsa_v", 128),
                  _pspec(tm, "idx_q", 256), _pspec(tm, "idx_kw", 128), tab, tab],
        out_specs=[pl.BlockSpec((DSA_HEADS, tm, 128), lambda i: (0, i, 0)), tab,
                   pl.BlockSpec((DSA_KV_HEADS, tm, 128), lambda i: (0, i, 0)),
                   pl.BlockSpec((IDX_HEADS, tm, 128), lambda i: (0, i, 0)), tab, tab],
        out_shape=[jax.ShapeDtypeStruct((DSA_HEADS, s, 128), jnp.bfloat16),
                   jax.ShapeDtypeStruct((s, 128), jnp.bfloat16),
                   jax.ShapeDtypeStruct((DSA_KV_HEADS, s, 128), jnp.bfloat16),
                   jax.ShapeDtypeStruct((IDX_HEADS, s, 128), jnp.bfloat16),
                   jax.ShapeDtypeStruct((s, 128), jnp.bfloat16),
                   jax.ShapeDtypeStruct((s, 128), jnp.float32)],
        compiler_params=_cparams(1),
        name="dsa_prep",
    )(p, p, p, p, p, dcos, dsin)


DSA_QB = 256
IDX_TK = 256
ATT_TK = 1024
SEL_ROWS = 128
SEL_LANES = 512
CAND_DEPTH = 10
CAND_SPLIT = 2
CAND_W = CAND_DEPTH * CAND_SPLIT * LANES
TIE_CHUNK = 8
NO_TIE_LIMIT = 1e9
assert CAND_W % SEL_LANES == 0 and 8 % CAND_SPLIT == 0


def _key_to_f32(key):
    bits = jnp.where(key >= 0, key, (0 - key) | INT_MIN)
    return lax.bitcast_convert_type(bits, jnp.float32)


def _count_lanes(keys_ref, r0, n_steps, cand, strict=False):
    def body(s, acc):
        c0 = s * SEL_LANES if isinstance(s, int) else pl.multiple_of(s * SEL_LANES, SEL_LANES)
        for u in range(SEL_LANES // LANES):
            kk = keys_ref[pl.ds(r0, SEL_ROWS), pl.ds(c0 + u * LANES, LANES)]
            acc = acc + jnp.where((kk > cand) if strict else (kk >= cand), 1.0, 0.0)
        return acc
    acc = jnp.zeros((SEL_ROWS, LANES), jnp.float32)
    if isinstance(n_steps, int):
        for s in range(n_steps):
            acc = body(s, acc)
        return acc
    return lax.fori_loop(0, n_steps, body, acc)


def _row_total(lane_counts):
    ones = jnp.ones((LANES, LANES), jnp.bfloat16)
    return _dot(_bf(lane_counts), ones)


def _f32_to_key(x):
    bits = lax.bitcast_convert_type(x, jnp.int32)
    return jnp.where(bits >= 0, bits, INT_MIN - bits)


def _search_window(cand_ref, lo_ref, final_ref):
    hi = lw = None
    for v in range(CAND_SPLIT):
        top = cand_ref[:, v * CAND_DEPTH * LANES:(v * CAND_DEPTH + 1) * LANES]
        hi = top if hi is None else jnp.maximum(hi, top)
        lw = top if lw is None else jnp.minimum(lw, top)
    k_hi = _f32_to_key(jnp.max(hi, axis=-1, keepdims=True))
    k_lw = _f32_to_key(jnp.min(lw, axis=-1, keepdims=True))
    nbits = 32 - lax.clz(k_hi ^ k_lw)
    nb = jnp.max(nbits.astype(jnp.float32)).astype(jnp.int32)
    low = lax.shift_left(jnp.int32(1), jnp.minimum(nb, 31)) - 1
    lo0 = jnp.where(nbits == 0, k_lw, jnp.where(nb >= 32, INT_MIN, k_lw & ~low))
    lo_ref[...] = jnp.broadcast_to(lo0, lo_ref.shape)
    final_ref[...] = jnp.broadcast_to(jnp.where(nbits == 0, 1.0, 0.0), final_ref.shape)
    return 32 - nb


def _radix_search(src_ref, n_steps, lo_ref, cnt_ref, topk, first_bit=None, final_ref=None, stage_ref=None):
    rows_total = lo_ref.shape[0]
    if first_bit is None:
        first_bit = jnp.int32(0)
        lo_ref[...] = jnp.full(lo_ref.shape, INT_MIN, jnp.int32)
        cnt_ref[...] = jnp.zeros(cnt_ref.shape, jnp.float32)
    else:
        cnt_ref[...] = jnp.full(cnt_ref.shape, NO_TIE_LIMIT, jnp.float32)

    if isinstance(n_steps, int):
        assert stage_ref is not None and rows_total == 2 * SEL_ROWS
        half_a, half_b = pl.ds(0, SEL_ROWS), pl.ds(SEL_ROWS, SEL_ROWS)

        def settle(rows, lane_counts, bit):
            cnt = _row_total(lane_counts)
            lo = lo_ref[rows, :]
            ok = cnt >= float(topk)
            if final_ref is not None:
                ok = jnp.logical_and(ok, final_ref[rows, :] <= 0.0)
            lo_ref[rows, :] = jnp.where(ok, lo + bit, lo)
            cnt_ref[rows, :] = jnp.where(ok, cnt, cnt_ref[rows, :])

        stage_ref[half_b, :] = jnp.zeros((SEL_ROWS, LANES), jnp.float32)

        def pass_body(b, carry):
            bit = lax.shift_left(jnp.int32(1), 31 - b)
            prev_bit = lax.shift_left(jnp.int32(1), jnp.minimum(32 - b, 31))
            staged = stage_ref[half_b, :]
            acc_a = _count_lanes(src_ref, 0, n_steps, _key_to_f32(lo_ref[half_a, :] + bit))
            settle(half_b, staged, prev_bit)
            acc_b = _count_lanes(src_ref, SEL_ROWS, n_steps, _key_to_f32(lo_ref[half_b, :] + bit))
            settle(half_a, acc_a, bit)
            stage_ref[half_b, :] = acc_b
            return carry
        lax.fori_loop(first_bit, 32, pass_body, 0)
        settle(half_b, stage_ref[half_b, :], jnp.int32(1))
        return

    def cond(c):
        b, pending = c
        return jnp.logical_and(b < 32, pending > 0)

    def body(c):
        b, _ = c
        bit = lax.shift_left(jnp.int32(1), 31 - b)
        off = None
        for rb in range(rows_total // SEL_ROWS):
            rows = pl.ds(rb * SEL_ROWS, SEL_ROWS)
            cand = lo_ref[rows, :] + bit
            cnt = _row_total(_count_lanes(src_ref, rb * SEL_ROWS, n_steps, _key_to_f32(cand)))
            ok = cnt >= float(topk)
            if final_ref is not None:
                ok = jnp.logical_and(ok, final_ref[rows, :] <= 0.0)
            lo_ref[rows, :] = jnp.where(ok, cand, lo_ref[rows, :])
            cnt = jnp.where(ok, cnt, cnt_ref[rows, :])
            cnt_ref[rows, :] = cnt
            miss = jnp.abs(cnt - float(topk))
            if final_ref is not None:
                miss = jnp.where(final_ref[rows, :] > 0.0, 0.0, miss)
            off = miss if off is None else jnp.maximum(off, miss)
        return b + 1, (jnp.max(off) > 0.0).astype(jnp.int32)
    lax.while_loop(cond, body, (first_bit, jnp.int32(1)))


def _lane_candidates(keys_ref, cand_ref, n_groups):
    n_stack = CAND_DEPTH * CAND_SPLIT

    def rg_body(rg, carry):
        r0 = pl.multiple_of(rg * 8, 8)

        def col_body(s, st):
            st = list(st)
            c0 = pl.multiple_of(s * 8 * LANES, 8 * LANES)
            for u in range(8):
                x = keys_ref[pl.ds(r0, 8), pl.ds(c0 + u * LANES, LANES)]
                base = (u % CAND_SPLIT) * CAND_DEPTH
                for d in range(CAND_DEPTH):
                    cur = st[base + d]
                    st[base + d] = jnp.maximum(cur, x)
                    x = jnp.minimum(cur, x)
            return tuple(st)
        init = tuple(jnp.full((8, LANES), -jnp.inf, jnp.float32) for _ in range(n_stack))
        st = lax.fori_loop(0, n_groups, col_body, init)
        for k in range(n_stack):
            cand_ref[pl.ds(r0, 8), k * LANES:(k + 1) * LANES] = st[k]
        return carry
    lax.fori_loop(0, cand_ref.shape[0] // 8, rg_body, 0)


def _ties_to_keep(src_ref, n_steps, lo_ref, cnt_ref, part_ref, need_ref, topk):
    rows_total = lo_ref.shape[0]
    surplus = jnp.where(lo_ref[...] > NEG_INF_KEY, cnt_ref[...] - float(topk), 0.0)
    need_ref[...] = jnp.full(need_ref.shape, NO_TIE_LIMIT, jnp.float32)

    @pl.when(jnp.max(surplus) > 0.0)
    def _():
        for rb in range(rows_total // SEL_ROWS):
            rows = pl.ds(rb * SEL_ROWS, SEL_ROWS)
            thr = _key_to_f32(jnp.maximum(lo_ref[rows, :], NEG_INF_KEY))
            part_ref[rows, :] = _count_lanes(src_ref, rb * SEL_ROWS, n_steps, thr, strict=True)
        above = _row_total(part_ref[...])
        need_ref[...] = jnp.where(surplus > 0.0, float(topk) - above, NO_TIE_LIMIT)


def _tie_index_bound(keys_ref, thr_ref, need_ref, seen_ref, sig_ref, n_chunks):
    kk_i = lax.broadcasted_iota(jnp.int32, (LANES, LANES), 0)
    jj_i = lax.broadcasted_iota(jnp.int32, (LANES, LANES), 1)
    tri = _bf(jnp.where(kk_i <= jj_i, 1.0, 0.0))
    seen_ref[...] = jnp.zeros(seen_ref.shape, jnp.float32)

    def cond(c):
        j, pending = c
        return jnp.logical_and(j < n_chunks, pending > 0)

    def body(c):
        j, _ = c
        thr, need = thr_ref[...], need_ref[...]
        seen, sig = seen_ref[...], sig_ref[...]
        for u in range(TIE_CHUNK):
            c0 = pl.multiple_of((j * TIE_CHUNK + u) * LANES, LANES)
            tie = jnp.where(keys_ref[:, pl.ds(c0, LANES)] == thr, 1.0, 0.0)
            rank = seen + _dot(_bf(tie), tri)
            after = seen + _row_total(tie)
            before = _row_total(jnp.where(rank < need, 1.0, 0.0))
            here = jnp.where(seen < need, jnp.where(after >= need, 1.0, 0.0), 0.0)
            sig = jnp.where(here > 0.0, c0.astype(jnp.float32) + before, sig)
            seen = after
        seen_ref[...] = seen
        sig_ref[...] = sig
        waiting = jnp.where(need < NO_TIE_LIMIT, jnp.where(seen < need, 1.0, 0.0), 0.0)
        return j + 1, (jnp.max(waiting) > 0.0).astype(jnp.int32)
    lax.while_loop(cond, body, (jnp.int32(0), jnp.int32(1)))


def _dsa_kernel(qp_ref, iqp_ref, w_ref, g_ref, k_ref, va_ref, ik_ref, o_ref,
                keys_ref, cand_ref, lo_ref, thr_ref, cnt_ref, part_ref, need_ref, sig_ref,
                m_ref, acc_ref, full_ref, *, topk):
    QB, TK = DSA_QB, IDX_TK
    G = DSA_HEADS // DSA_KV_HEADS
    i = pl.program_id(0)
    n_idx = i + 1
    n_att = (n_idx * IDX_TK + ATT_TK - 1) // ATT_TK
    n_sel = n_att * (ATT_TK // SEL_LANES)
    row_id = lax.broadcasted_iota(jnp.int32, (QB, TK), 0)
    col_id = lax.broadcasted_iota(jnp.int32, (QB, TK), 1)

    iq = iqp_ref[...].reshape(IDX_HEADS * QB, LANES)
    wb = [jnp.broadcast_to(w_ref[:, h:h + 1], (QB, TK)) for h in range(IDX_HEADS)]

    def score_tile(j, diag):
        c0 = pl.multiple_of(j * TK, TK)
        s_all = _dot_nt(iq, ik_ref[pl.ds(c0, TK), :])
        score = None
        for h in range(IDX_HEADS):
            term = wb[h] * jnp.maximum(s_all[h * QB:(h + 1) * QB], 0.0)
            score = term if score is None else score + term
        if diag:
            score = jnp.where(col_id <= row_id, score, -jnp.inf)
        keys_ref[:, pl.ds(c0, TK)] = score

    def score_body(j, carry):
        score_tile(j, False)
        return carry

    def score_quad(q, carry):
        for u in range(4):
            score_tile(q * 4 + u, False)
        return carry
    n_quads = lax.shift_right_logical(i, 2)
    lax.fori_loop(0, n_quads, score_quad, 0)
    lax.fori_loop(n_quads * 4, i, score_body, 0)
    score_tile(i, True)

    def blank_body(j, carry):
        c0 = pl.multiple_of(j * TK, TK)
        keys_ref[:, pl.ds(c0, TK)] = jnp.full((QB, TK), -jnp.inf, jnp.float32)
        return carry
    n_vis = n_idx * TK
    windows = [w for w in (CAND_W, 2 * CAND_W) if w <= keys_ref.shape[1]]
    att_end = n_att * (ATT_TK // TK)
    blank_end = att_end
    for w in reversed(windows):
        blank_end = jnp.where(n_vis <= w, jnp.maximum(att_end, w // TK), blank_end)
    lax.fori_loop(n_idx, blank_end, blank_body, 0)

    full_ref[0] = jnp.int32(1)

    @pl.when(n_vis > CAND_W)
    def _():
        _lane_candidates(keys_ref, cand_ref, n_att * (ATT_TK // (8 * LANES)))
        first_bit = _search_window(cand_ref, lo_ref, part_ref)
        _radix_search(cand_ref, CAND_W // SEL_LANES, lo_ref, cnt_ref, topk,
                      first_bit=first_bit, final_ref=part_ref, stage_ref=need_ref)
        thr = _key_to_f32(jnp.maximum(lo_ref[...], NEG_INF_KEY))
        exact = cnt_ref[...] == float(topk)
        lost = None
        for v in range(CAND_SPLIT):
            last = cand_ref[:, (v * CAND_DEPTH + CAND_DEPTH - 1) * LANES:(v * CAND_DEPTH + CAND_DEPTH) * LANES]
            hit = jnp.where(last > thr, 1.0, jnp.where(last == thr, jnp.where(exact, 1.0, 0.0), 0.0))
            lost = hit if lost is None else jnp.maximum(lost, hit)
        full_ref[0] = (jnp.max(lost) > 0.0).astype(jnp.int32)
        _ties_to_keep(cand_ref, CAND_W // SEL_LANES, lo_ref, cnt_ref, part_ref, need_ref, topk)

    @pl.when(full_ref[0] > 0)
    def _():
        def search_rows(n_steps):
            _radix_search(keys_ref, n_steps, lo_ref, cnt_ref, topk, stage_ref=need_ref)
            _ties_to_keep(keys_ref, n_steps, lo_ref, cnt_ref, part_ref, need_ref, topk)
        below = 0
        for w in windows:
            pl.when(jnp.logical_and(n_vis > below, n_vis <= w))(
                functools.partial(search_rows, w // SEL_LANES))
            below = w
        pl.when(n_vis > below)(functools.partial(search_rows, n_sel))

    tau = lo_ref[...]
    found = tau > NEG_INF_KEY
    thr_ref[...] = jnp.where(found, _key_to_f32(jnp.maximum(tau, NEG_INF_KEY)), F32_LOWEST)
    sig_ref[...] = jnp.full(sig_ref.shape, NO_TIE_LIMIT, jnp.float32)

    @pl.when(jnp.min(need_ref[...]) < NO_TIE_LIMIT)
    def _():
        _tie_index_bound(keys_ref, thr_ref, need_ref, part_ref, sig_ref, n_att * (ATT_TK // (TIE_CHUNK * LANES)))

    m_ref[...] = jnp.full(m_ref.shape, NEG_INIT, jnp.float32)
    acc_ref[...] = jnp.zeros(acc_ref.shape, jnp.float32)
    n_grp = ATT_TK // LANES
    lane_f = lax.broadcasted_iota(jnp.int32, (QB, LANES), 1).astype(jnp.float32)

    def attn_body(j, carry):
        c0 = pl.multiple_of(j * ATT_TK, ATT_TK)
        tau_b = thr_ref[...]
        sig_rel = sig_ref[...] - c0.astype(jnp.float32)
        parts = []
        for u in range(n_grp):
            kk = keys_ref[:, pl.ds(c0 + u * LANES, LANES)]
            tie_bias = jnp.where(lane_f + float(u * LANES) <= sig_rel, 0.0, NEG_MASK)
            parts.append(jnp.where(kk > tau_b, 0.0, jnp.where(kk == tau_b, tie_bias, NEG_MASK)))
        bias = jnp.concatenate(parts, axis=1)
        kt = k_ref[pl.ds(c0, ATT_TK), :]
        for n in range(DSA_KV_HEADS):
            qn = qp_ref[n * G:(n + 1) * G].reshape(G * QB, LANES)
            logits = _dot_nt(qn, kt)
            ps = []
            for gq in range(G):
                h = n * G + gq
                lm = logits[gq * QB:(gq + 1) * QB] + bias
                tmax = lm[:, 0:LANES]
                for u in range(1, n_grp):
                    tmax = jnp.maximum(tmax, lm[:, u * LANES:(u + 1) * LANES])
                m_old = m_ref[h]
                m_new = jnp.maximum(m_old, jnp.max(tmax, axis=-1, keepdims=True))
                ps.append(jnp.concatenate(
                    [_bf(jnp.exp2(lm[:, u * LANES:(u + 1) * LANES] - m_new)) for u in range(n_grp)], axis=1))
                m_ref[h] = m_new
                acc_ref[h] = jnp.exp2(m_old - m_new) * acc_ref[h]
            pv = _dot(jnp.concatenate(ps, axis=0), va_ref[n, pl.ds(c0, ATT_TK), :])
            for gq in range(G):
                h = n * G + gq
                acc_ref[h] = acc_ref[h] + pv[gq * QB:(gq + 1) * QB]
        return carry
    lax.fori_loop(0, n_att, attn_body, 0)

    lane = lax.broadcasted_iota(jnp.int32, (QB, LANES), 1)
    g = _f32(g_ref[...])
    for pair in range(DSA_HEADS // 2):
        halves = []
        for pos in range(2):
            a = acc_ref[2 * pair + pos]
            halves.append(a * (1.0 / pltpu.roll(a, DSA_DH, 1)))
        o = jnp.where(lane < DSA_DH, halves[0], pltpu.roll(halves[1], DSA_DH, 1))
        sl = slice(pair * LANES, (pair + 1) * LANES)
        o_ref[:, sl] = _bf(o * _silu(g[:, sl]))


def _dsa_call(p, qp, kk, va, iqp, ik, w4):
    s = p.shape[0]
    topk = min(TOPK_MAX, s // 4)
    QB = DSA_QB
    assert IDX_TK == QB and s % ATT_TK == 0 and ATT_TK % SEL_LANES == 0 and s // LANES <= 256
    once = pl.Buffered(1)
    return pl.pallas_call(
        functools.partial(_dsa_kernel, topk=topk),
        grid=(s // QB,),
        in_specs=[pl.BlockSpec((DSA_HEADS, QB, LANES), lambda i: (0, i, 0)),
                  pl.BlockSpec((IDX_HEADS, QB, LANES), lambda i: (0, i, 0)),
                  pl.BlockSpec((QB, LANES), lambda i: (i, 0)),
                  _pspec(QB, "dsa_g", 512),
                  pl.BlockSpec((s, LANES), lambda i: (0, 0), pipeline_mode=once),
                  pl.BlockSpec((DSA_KV_HEADS, s, LANES), lambda i: (0, 0, 0), pipeline_mode=once),
                  pl.BlockSpec((s, LANES), lambda i: (0, 0), pipeline_mode=once)],
        out_specs=pl.BlockSpec((QB, 512), lambda i: (i, 0)),
        out_shape=jax.ShapeDtypeStruct((s, 512), jnp.bfloat16),
        scratch_shapes=[pltpu.VMEM((QB, s), jnp.float32),
                        pltpu.VMEM((QB, CAND_W), jnp.float32),
                        pltpu.VMEM((QB, LANES), jnp.int32),
                        pltpu.VMEM((QB, LANES), jnp.float32),
                        pltpu.VMEM((QB, LANES), jnp.float32),
                        pltpu.VMEM((QB, LANES), jnp.float32),
                        pltpu.VMEM((QB, LANES), jnp.float32),
                        pltpu.VMEM((QB, LANES), jnp.float32),
                        pltpu.VMEM((DSA_HEADS, QB, LANES), jnp.float32),
                        pltpu.VMEM((DSA_HEADS, QB, LANES), jnp.float32),
                        pltpu.SMEM((1,), jnp.int32)],
        compiler_params=_cparams(1, vmem_mb=56),
        name="dsa",
    )(qp, iqp, w4, p, kk, va, ik)


def _merge_kernel(x_ref, ret_ref, dsa_ref, gl_ref, m_ref, wr_ref, wd_ref, wg_ref, wo_ref,
                  post_ref, gate_ref, o_ref):
    d = D_MODEL
    y = (_sigmoid(_f32(m_ref[:, 0:d])) * _dot(ret_ref[...], wr_ref[...])
         + _sigmoid(_f32(m_ref[:, d:2 * d])) * _dot(dsa_ref[...], wd_ref[...])
         + _sigmoid(_f32(m_ref[:, 2 * d:3 * d])) * _dot(gl_ref[...], wg_ref[...]))
    y = _dot(_bf(y), wo_ref[...])
    yn = y * lax.rsqrt(jnp.mean(y * y, axis=-1, keepdims=True) + RMS_EPS) * post_ref[...]
    o_ref[...] = x_ref[...] + gate_ref[...] * yn


def _merge_call(x2, ret, dsa, gl, p, wr, wd, wg, wo, post, gate):
    s, d = x2.shape
    tm = min(512, s)
    rows = lambda w: pl.BlockSpec((tm, w), lambda i: (i, 0))
    whole = lambda a: pl.BlockSpec(a.shape, lambda i: (0, 0))
    return pl.pallas_call(
        _merge_kernel,
        grid=(s // tm,),
        in_specs=[rows(d), rows(512), rows(512), rows(512), _pspec(tm, "merge", 3072),
                  whole(wr), whole(wd), whole(wg), whole(wo), whole(post), whole(gate)],
        out_specs=rows(d),
        out_shape=jax.ShapeDtypeStruct((s, d), jnp.float32),
        compiler_params=_cparams(1),
        name="merge_out",
    )(x2, ret, dsa, gl, p, wr, wd, wg, wo, post, gate)


def _pack_w_in(w_in):
    depth, d, _ = w_in.shape
    w_in = w_in.astype(jnp.bfloat16)
    zeros = lambda n: jnp.zeros((depth, d, n), w_in.dtype)
    src = lambda name: w_in[:, :, _SRC[name][0]:_SRC[name][0] + _SRC[name][1]]
    pieces, at = [], 0
    for name, width in _PACK:
        assert at == PCOL[name]
        if name == "idx_kw":
            cols = [src("idx_k"), src("idx_w"), zeros(width - IDX_DH - IDX_HEADS)]
        elif name == "gla_a":
            cols = [src("gla_a"), zeros(width - GLA_RANK)]
        else:
            cols = [src(name)]
        pieces += cols
        at += width
    pieces.append(zeros(P_WIDTH - at))
    return jnp.concatenate(pieces, axis=-1).astype(jnp.bfloat16)


def kernel(x, c, positions, ada_w, ada_b, pre_norm, post_norm, w_in, gla_w_lr, gla_b_lr,
           w_br_ret, w_br_dsa, w_br_gla, w_out):
    b, s, d = x.shape
    assert b == 1 and d == D_MODEL
    depth = ada_w.shape[0]
    x2 = x.reshape(s, d)
    mod = _mod_call(jnp.broadcast_to(c, (8, d)), ada_w, ada_b.reshape(depth, 1, 3 * d))[:, 0:1, :]
    rcos, rsin, dcos, dsin = _tab_call(positions.reshape(s, 1))
    w_pack = _pack_w_in(w_in)
    wlr_pad = jnp.pad(gla_w_lr, ((0, 0), (0, LANES - GLA_RANK), (0, 0)))
    for l in range(depth):
        shift, scale, gate = mod[l, :, 0:d], mod[l, :, d:2 * d], mod[l, :, 2 * d:3 * d]
        p = _proj_call(x2, pre_norm[l][None, :], scale, shift, w_pack[l])
        ret = _ret_call(p, rcos, rsin)
        gl = _gla_call(p, wlr_pad[l], gla_b_lr[l][None, :])
        qp, kk, va, iqp, ik, w4 = _dprep_call(p, dcos, dsin)
        dsa = _dsa_call(p, qp, kk, va, iqp, ik, w4)
        x2 = _merge_call(x2, ret, dsa, gl, p, _bf(w_br_ret[l]), _bf(w_br_dsa[l]), _bf(w_br_gla[l]),
                         _bf(w_out[l]), post_norm[l][None, :], gate)
    return x2.reshape(b, s, d)
```

```python
import functools
import math

import jax
import jax.numpy as jnp
from jax import lax
from jax.experimental import pallas as pl
from jax.experimental.pallas import tpu as pltpu

D_MODEL = 1024
DEPTH = 4
RET_HEADS, RET_DK, RET_DV, RET_CHUNK, RET_THETA = 4, 64, 128, 128, 10000.0
DSA_HEADS, DSA_KV_HEADS, DSA_DH = 8, 2, 64
DSA_ROT = DSA_DH // 4
ROPE_THETA = 500000.0
IDX_HEADS, IDX_DH = 4, 64
TOPK_MAX = 256
GLA_HEADS, GLA_DK, GLA_DV, GLA_RANK, GLA_TAU, GLA_CHUNK = 4, 64, 128, 16, 16.0, 64
RMS_EPS = 1e-6
LANES = 128

_SRC = {}
_off = 0
for _name, _w in (("ret_q", 256), ("ret_k", 256), ("ret_v", 512), ("ret_g", 512),
                  ("dsa_q", 512), ("dsa_k", 128), ("dsa_v", 128), ("dsa_g", 512),
                  ("idx_q", 256), ("idx_k", 64), ("idx_w", 4),
                  ("gla_q", 256), ("gla_k", 256), ("gla_v", 512), ("gla_g", 512), ("gla_a", 16),
                  ("merge", 3072)):
    _SRC[_name] = (_off, _w)
    _off += _w
IN_WIDTH = _off

_PACK = (("merge", 3072), ("ret_q", 256), ("ret_k", 256), ("ret_v", 512), ("ret_g", 512),
         ("dsa_q", 512), ("dsa_g", 512), ("gla_v", 512), ("gla_g", 512),
         ("gla_q", 256), ("gla_k", 256), ("idx_q", 256),
         ("dsa_k", 128), ("dsa_v", 128), ("idx_kw", 128), ("gla_a", 128))
PCOL = {}
_off = 0
for _name, _w in _PACK:
    assert _off % min(_w, 1024) == 0
    PCOL[_name] = _off
    _off += _w
P_WIDTH = 8192
assert _off <= P_WIDTH

LOG2E = math.log2(math.e)
INT_MIN = -(2 ** 31)
F32_LOWEST = -3.4028234663852886e38
NEG_INF_KEY = -0x7F800000
NEG_INIT = -1e30
NEG_MASK = -2e30


def _cparams(n_axes, vmem_mb=48):
    return pltpu.CompilerParams(dimension_semantics=("arbitrary",) * n_axes,
                                vmem_limit_bytes=vmem_mb * 1024 * 1024)


def _bf(x):
    return x.astype(jnp.bfloat16)


def _f32(x):
    return x.astype(jnp.float32)


def _dot(a, b):
    return jnp.dot(a, b, preferred_element_type=jnp.float32)


def _dot_nt(a, b):
    return lax.dot_general(a, b, (((1,), (1,)), ((), ())), preferred_element_type=jnp.float32)


def _dot_tn(a, b):
    return lax.dot_general(a, b, (((0,), (0,)), ((), ())), preferred_element_type=jnp.float32)


def _split3(x):
    hi = _bf(x)
    r1 = x - hi.astype(jnp.float32)
    mid = _bf(r1)
    lo = _bf(r1 - mid.astype(jnp.float32))
    return hi, mid, lo


def _silu(x):
    return x * (1.0 / (1.0 + jnp.exp(-x)))


def _sigmoid(x):
    return 1.0 / (1.0 + jnp.exp(-x))


def _mod_kernel(c_ref, w_ref, b_ref, o_ref):
    c = c_ref[...]
    ca = _silu(c)
    acc = None
    for t in _split3(ca):
        for u in _split3(w_ref[0]):
            part = _dot(t, u)
            acc = part if acc is None else acc + part
    o_ref[0] = acc + b_ref[0]


def _mod_call(c8, ada_w, ada_b3):
    depth, d, n = ada_w.shape
    tn = 1024
    return pl.pallas_call(
        _mod_kernel,
        grid=(depth, n // tn),
        in_specs=[pl.BlockSpec((8, d), lambda l, j: (0, 0)),
                  pl.BlockSpec((1, d, tn), lambda l, j: (l, 0, j)),
                  pl.BlockSpec((1, 1, tn), lambda l, j: (l, 0, j))],
        out_specs=pl.BlockSpec((1, 8, tn), lambda l, j: (l, 0, j)),
        out_shape=jax.ShapeDtypeStruct((depth, 8, n), jnp.float32),
        compiler_params=_cparams(2),
        name="adaln_mod",
    )(c8, ada_w, ada_b3)


def _tab_kernel(pos_ref, rf_ref, rs_ref, df_ref, ds_ref, rc_o, rsn_o, dc_o, dsn_o):
    pos = pos_ref[...].astype(jnp.float32)
    ang = pos * rf_ref[...]
    rc_o[...] = jnp.cos(ang)
    rsn_o[...] = jnp.sin(ang) * rs_ref[...]
    ang = pos * df_ref[...]
    dc_o[...] = jnp.cos(ang)
    dsn_o[...] = jnp.sin(ang) * ds_ref[...]


def _rope_rows():
    half = RET_DK // 2
    f = RET_THETA ** (-jnp.arange(half, dtype=jnp.float32) * 2.0 / RET_DK)
    rf = jnp.tile(jnp.concatenate([f, f]), RET_HEADS)[None, :]
    rs = jnp.tile(jnp.concatenate([-jnp.ones(half), jnp.ones(half)]), RET_HEADS)[None, :].astype(jnp.float32)
    half = DSA_ROT // 2
    f = ROPE_THETA ** (-jnp.arange(half, dtype=jnp.float32) * 2.0 / DSA_ROT)
    z = jnp.zeros(DSA_DH - DSA_ROT, jnp.float32)
    df = jnp.tile(jnp.concatenate([f, f, z]), 2)[None, :]
    ds = jnp.tile(jnp.concatenate([-jnp.ones(half), jnp.ones(half), z]), 2)[None, :].astype(jnp.float32)
    return rf, rs, df, ds


def _tab_call(pos_col):
    s = pos_col.shape[0]
    tm = min(1024, s)
    rf, rs, df, ds = _rope_rows()
    row = lambda w: pl.BlockSpec((1, w), lambda i: (0, 0))
    out = lambda w: pl.BlockSpec((tm, w), lambda i: (i, 0))
    return pl.pallas_call(
        _tab_kernel,
        grid=(s // tm,),
        in_specs=[pl.BlockSpec((tm, 1), lambda i: (i, 0)), row(256), row(256), row(128), row(128)],
        out_specs=[out(256), out(256), out(128), out(128)],
        out_shape=[jax.ShapeDtypeStruct((s, 256), jnp.float32), jax.ShapeDtypeStruct((s, 256), jnp.float32),
                   jax.ShapeDtypeStruct((s, 128), jnp.float32), jax.ShapeDtypeStruct((s, 128), jnp.float32)],
        compiler_params=_cparams(1),
        name="rope_tables",
    )(pos_col, rf, rs, df, ds)


def _swap_halves(x, half, period):
    w = x.shape[-1]
    lane = lax.broadcasted_iota(jnp.int32, x.shape, x.ndim - 1) & (period - 1)
    up = pltpu.roll(x, w - half, x.ndim - 1)
    dn = pltpu.roll(x, half, x.ndim - 1)
    return jnp.where(lane < half, up, dn)


def _rope(x, cos, sin_signed, half, period):
    return x * cos + _swap_halves(x, half, period) * sin_signed


def _proj_kernel(x_ref, pre_ref, sc_ref, sh_ref, w_ref, o_ref):
    x = x_ref[...]
    xn = x * lax.rsqrt(jnp.mean(x * x, axis=-1, keepdims=True) + RMS_EPS)
    h = xn * pre_ref[...] * (1.0 + sc_ref[...]) + sh_ref[...]
    o_ref[...] = _bf(_dot(_bf(h), w_ref[...]))


def _proj_call(x2, pre, scale, shift, w_pack):
    s, d = x2.shape
    tm, tn = min(512, s), 2048
    vec = pl.BlockSpec((1, d), lambda j, i: (0, 0))
    return pl.pallas_call(
        _proj_kernel,
        grid=(P_WIDTH // tn, s // tm),
        in_specs=[pl.BlockSpec((tm, d), lambda j, i: (i, 0)), vec, vec, vec,
                  pl.BlockSpec((d, tn), lambda j, i: (0, j))],
        out_specs=pl.BlockSpec((tm, tn), lambda j, i: (i, j)),
        out_shape=jax.ShapeDtypeStruct((s, P_WIDTH), jnp.bfloat16),
        compiler_params=_cparams(2),
        name="in_proj",
    )(x2, pre, scale, shift, w_pack)


def _pspec(tm, name, width):
    blk = PCOL[name] // width
    assert PCOL[name] % width == 0
    return pl.BlockSpec((tm, width), lambda i: (i, blk))


def _ret_log_g(h):
    return math.log1p(-(2.0 ** (-5.0 - h)))


def _ret_kernel(q_ref, k_ref, v_ref, g_ref, cos_ref, sin_ref, o_ref,
                state_ref, decay_ref, qdec_ref, kend_ref, *, chunks):
    C = RET_CHUNK

    @pl.when(pl.program_id(0) == 0)
    def _():
        state_ref[...] = jnp.zeros_like(state_ref)
        ii = lax.broadcasted_iota(jnp.int32, (C, C), 0)
        jj = lax.broadcasted_iota(jnp.int32, (C, C), 1)
        rel = (ii - jj).astype(jnp.float32)
        row = lax.broadcasted_iota(jnp.int32, (C, RET_DK), 0).astype(jnp.float32)
        for h in range(RET_HEADS):
            lg = _ret_log_g(h)
            decay_ref[h] = jnp.where(rel >= 0, jnp.exp(lg * jnp.maximum(rel, 0.0)), 0.0)
            qdec_ref[:, h * RET_DK:(h + 1) * RET_DK] = jnp.exp((row + 1.0) * lg)
            kend_ref[:, h * RET_DK:(h + 1) * RET_DK] = jnp.exp((C - 1.0 - row) * lg)

    heads = [(slice(h * RET_DK, (h + 1) * RET_DK), slice(h * RET_DV, (h + 1) * RET_DV))
             for h in range(RET_HEADS)]
    chunk_rows = [slice(c * C, (c + 1) * C) for c in range(chunks)]
    qbs, kbs, qds, kds = [], [], [], []
    for rows in chunk_rows:
        cos, sin = cos_ref[rows, :], sin_ref[rows, :]
        q = _rope(_f32(q_ref[rows, :]), cos, sin, RET_DK // 2, RET_DK) * (RET_DK ** -0.5)
        k = _rope(_f32(k_ref[rows, :]), cos, sin, RET_DK // 2, RET_DK)
        qbs.append(_bf(q))
        kbs.append(_bf(k))
        qds.append(_bf(q * qdec_ref[...]))
        kds.append(_bf(k * kend_ref[...]))
    scores = [[_bf(_dot_nt(qbs[c][:, dk], kbs[c][:, dk]) * decay_ref[h]) for h, (dk, _) in enumerate(heads)]
              for c in range(chunks)]
    intra = [[_dot(scores[c][h], v_ref[chunk_rows[c], heads[h][1]]) for h in range(RET_HEADS)]
             for c in range(chunks)]
    kvs = [[_dot_tn(kds[c][:, heads[h][0]], v_ref[chunk_rows[c], heads[h][1]]) for h in range(RET_HEADS)]
           for c in range(chunks)]
    states = [state_ref[h] for h in range(RET_HEADS)]
    for c, rows in enumerate(chunk_rows):
        g = _f32(g_ref[rows, :])
        for h, (dk, dv) in enumerate(heads):
            o = intra[c][h] + _dot(qds[c][:, dk], _bf(states[h]))
            states[h] = math.exp(C * _ret_log_g(h)) * states[h] + kvs[c][h]
            o = o * lax.rsqrt(jnp.mean(o * o, axis=-1, keepdims=True) + RMS_EPS)
            o_ref[rows, dv] = _bf(o * _silu(g[:, dv]))
    for h in range(RET_HEADS):
        state_ref[h] = states[h]


def _ret_call(p, rcos, rsin):
    s = p.shape[0]
    chunks = 4
    tm = RET_CHUNK * chunks
    tab = pl.BlockSpec((tm, 256), lambda i: (i, 0))
    return pl.pallas_call(
        functools.partial(_ret_kernel, chunks=chunks),
        grid=(s // tm,),
        in_specs=[_pspec(tm, "ret_q", 256), _pspec(tm, "ret_k", 256), _pspec(tm, "ret_v", 512),
                  _pspec(tm, "ret_g", 512), tab, tab],
        out_specs=pl.BlockSpec((tm, 512), lambda i: (i, 0)),
        out_shape=jax.ShapeDtypeStruct((s, 512), jnp.bfloat16),
        scratch_shapes=[pltpu.VMEM((RET_HEADS, RET_DK, RET_DV), jnp.float32),
                        pltpu.VMEM((RET_HEADS, RET_CHUNK, RET_CHUNK), jnp.float32),
                        pltpu.VMEM((RET_CHUNK, RET_HEADS * RET_DK), jnp.float32),
                        pltpu.VMEM((RET_CHUNK, RET_HEADS * RET_DK), jnp.float32)],
        compiler_params=_cparams(1),
        name="retention",
    )(p, p, p, p, rcos, rsin)


def _gla_kernel(q_ref, k_ref, v_ref, g_ref, a_ref, wlr_ref, blr_ref, o_ref, state_ref, *, chunks):
    C = GLA_CHUNK

    @pl.when(pl.program_id(0) == 0)
    def _():
        state_ref[...] = jnp.zeros_like(state_ref)

    ii = lax.broadcasted_iota(jnp.int32, (C, C), 0)
    jj = lax.broadcasted_iota(jnp.int32, (C, C), 1)
    causal = jj <= ii
    tril = _bf(jnp.where(causal, 1.0, 0.0))
    wlr = wlr_ref[...]
    w_hi, w_mid, w_lo = _split3(wlr)

    a_hi, a_mid, a_lo = _split3(_f32(a_ref[...]))
    z = (_dot(a_hi, w_hi) + (_dot(a_hi, w_mid) + _dot(a_mid, w_hi))
         + (_dot(a_hi, w_lo) + _dot(a_mid, w_mid) + _dot(a_lo, w_hi))) + blr_ref[...]
    log_a_all = (jnp.minimum(z, 0.0) - jnp.log1p(jnp.exp(-jnp.abs(z)))) * (1.0 / GLA_TAU)

    heads = [(slice(h * GLA_DK, (h + 1) * GLA_DK), slice(h * GLA_DV, (h + 1) * GLA_DV))
             for h in range(GLA_HEADS)]
    chunk_rows = [slice(c * C, (c + 1) * C) for c in range(chunks)]
    bcums = []
    for rows in chunk_rows:
        l_hi, l_mid, l_lo = _split3(log_a_all[rows, :])
        bcums.append(_dot(tril, l_hi) + _dot(tril, l_mid) + _dot(tril, l_lo))
    qts, kts, qgs, kds, e_lasts = [], [], [], [], []
    for rows, bcum in zip(chunk_rows, bcums):
        b_mid = bcum[C // 2 - 1:C // 2, :]
        b_last = bcum[C - 1:C, :]
        q = _f32(q_ref[rows, :]) * (GLA_DK ** -0.5)
        k = _f32(k_ref[rows, :])
        qts.append(_bf(q * jnp.exp(bcum - b_mid)))
        kts.append(_bf(k * jnp.exp(b_mid - bcum)))
        qgs.append(_bf(q * jnp.exp(bcum)))
        kds.append(_bf(k * jnp.exp(b_last - bcum)))
        e_lasts.append(jnp.exp(b_last))
    attns = [[_bf(jnp.where(causal, _dot_nt(qts[c][:, dk], kts[c][:, dk]), 0.0)) for dk, _ in heads]
             for c in range(chunks)]
    intra = [[_dot(attns[c][h], v_ref[chunk_rows[c], heads[h][1]]) for h in range(GLA_HEADS)]
             for c in range(chunks)]
    kvs = [[_dot_tn(v_ref[chunk_rows[c], heads[h][1]], kds[c][:, heads[h][0]]) for h in range(GLA_HEADS)]
           for c in range(chunks)]
    states = [state_ref[h] for h in range(GLA_HEADS)]
    for c, rows in enumerate(chunk_rows):
        g = _f32(g_ref[rows, :])
        for h, (dk, dv) in enumerate(heads):
            o = intra[c][h] + _dot_nt(qgs[c][:, dk], _bf(states[h]))
            states[h] = e_lasts[c][:, dk] * states[h] + kvs[c][h]
            o = o * lax.rsqrt(jnp.mean(o * o, axis=-1, keepdims=True) + RMS_EPS)
            o_ref[rows, dv] = _bf(o * _silu(g[:, dv]))
    for h in range(GLA_HEADS):
        state_ref[h] = states[h]


def _gla_call(p, wlr_pad, blr):
    s = p.shape[0]
    chunks = 8
    tm = GLA_CHUNK * chunks
    return pl.pallas_call(
        functools.partial(_gla_kernel, chunks=chunks),
        grid=(s // tm,),
        in_specs=[_pspec(tm, "gla_q", 256), _pspec(tm, "gla_k", 256), _pspec(tm, "gla_v", 512),
                  _pspec(tm, "gla_g", 512), _pspec(tm, "gla_a", 128),
                  pl.BlockSpec((128, 256), lambda i: (0, 0)), pl.BlockSpec((1, 256), lambda i: (0, 0))],
        out_specs=pl.BlockSpec((tm, 512), lambda i: (i, 0)),
        out_shape=jax.ShapeDtypeStruct((s, 512), jnp.bfloat16),
        scratch_shapes=[pltpu.VMEM((GLA_HEADS, GLA_DV, GLA_DK), jnp.float32)],
        compiler_params=_cparams(1),
        name="gla",
    )(p, p, p, p, p, wlr_pad, blr)


def _place_head(pair, src_pos, dst_pos):
    lane = lax.broadcasted_iota(jnp.int32, pair.shape, 1)
    if src_pos != dst_pos:
        pair = pltpu.roll(pair, 64, 1)
    keep = (lane < 64) if dst_pos == 0 else (lane >= 64)
    return jnp.where(keep, pair, 0.0)


def _dprep_kernel(q_ref, k_ref, v_ref, iq_ref, ikw_ref, cos_ref, sin_ref,
                  qp_o, k_o, v_o, iqp_o, ik_o, w_o):
    cos, sin = cos_ref[...], sin_ref[...]
    half = DSA_ROT // 2
    cos4 = jnp.concatenate([cos] * 4, axis=1)
    sin4 = jnp.concatenate([sin] * 4, axis=1)
    q = _rope(_f32(q_ref[...]), cos4, sin4, half, DSA_DH) * (DSA_DH ** -0.5 * LOG2E)
    for h in range(DSA_HEADS):
        pair = q[:, (h // 2) * 128:(h // 2 + 1) * 128]
        qp_o[h] = _bf(_place_head(pair, h % 2, h // (DSA_HEADS // DSA_KV_HEADS)))
    k_o[...] = _bf(_rope(_f32(k_ref[...]), cos, sin, half, DSA_DH))
    v = _f32(v_ref[...])
    lane = lax.broadcasted_iota(jnp.int32, v.shape, 1)
    v_o[0] = _bf(jnp.where(lane < DSA_DH, v, 1.0))
    v_o[1] = _bf(jnp.where(lane < DSA_DH, pltpu.roll(v, DSA_DH, 1), 1.0))
    iq = _rope(_f32(iq_ref[...]), cos4[:, :256], sin4[:, :256], half, IDX_DH)
    for h in range(IDX_HEADS):
        pair = iq[:, (h // 2) * 128:(h // 2 + 1) * 128]
        iqp_o[h] = _bf(_place_head(pair, h % 2, 0))
    ikw = _f32(ikw_ref[...])
    lane = lax.broadcasted_iota(jnp.int32, ikw.shape, 1)
    ik_o[...] = _bf(jnp.where(lane < IDX_DH, _rope(ikw, cos, sin, half, DSA_DH), 0.0))
    wscale = (IDX_HEADS ** -0.5) * (IDX_DH ** -0.5)
    w_o[...] = pltpu.roll(ikw, 128 - IDX_DH, 1) * wscale


def _dprep_call(p, dcos, dsin):
    s = p.shape[0]
    tm = min(512, s)
    tab = pl.BlockSpec((tm, 128), lambda i: (i, 0))
    return pl.pallas_call(
        _dprep_kernel,
        grid=(s // tm,),
        in_specs=[_pspec(tm, "dsa_q", 512), _pspec(tm, "dsa_k", 128), _pspec(tm, "dsa_v", 128),
                  _pspec(tm, "idx_q", 256), _pspec(tm, "idx_kw", 128), tab, tab],
        out_specs=[pl.BlockSpec((DSA_HEADS, tm, 128), lambda i: (0, i, 0)), tab,
                   pl.BlockSpec((DSA_KV_HEADS, tm, 128), lambda i: (0, i, 0)),
                   pl.BlockSpec((IDX_HEADS, tm, 128), lambda i: (0, i, 0)), tab, tab],
        out_shape=[jax.ShapeDtypeStruct((DSA_HEADS, s, 128), jnp.bfloat16),
                   jax.ShapeDtypeStruct((s, 128), jnp.bfloat16),
                   jax.ShapeDtypeStruct((DSA_KV_HEADS, s, 128), jnp.bfloat16),
                   jax.ShapeDtypeStruct((IDX_HEADS, s, 128), jnp.bfloat16),
                   jax.ShapeDtypeStruct((s, 128), jnp.bfloat16),
                   jax.ShapeDtypeStruct((s, 128), jnp.float32)],
        compiler_params=_cparams(1),
        name="dsa_prep",
    )(p, p, p, p, p, dcos, dsin)


DSA_QB = 256
IDX_TK = 256
ATT_TK = 1024
SEL_ROWS = 128
SEL_LANES = 512
CAND_DEPTH = 10
CAND_SPLIT = 2
CAND_W = CAND_DEPTH * CAND_SPLIT * LANES
TIE_CHUNK = 8
NO_TIE_LIMIT = 1e9
assert CAND_W % SEL_LANES == 0 and 8 % CAND_SPLIT == 0


def _key_to_f32(key):
    bits = jnp.where(key >= 0, key, (0 - key) | INT_MIN)
    return lax.bitcast_convert_type(bits, jnp.float32)


def _count_lanes(keys_ref, r0, n_steps, cand, strict=False):
    def body(s, acc):
        c0 = s * SEL_LANES if isinstance(s, int) else pl.multiple_of(s * SEL_LANES, SEL_LANES)
        for u in range(SEL_LANES // LANES):
            kk = keys_ref[pl.ds(r0, SEL_ROWS), pl.ds(c0 + u * LANES, LANES)]
            acc = acc + jnp.where((kk > cand) if strict else (kk >= cand), 1.0, 0.0)
        return acc
    acc = jnp.zeros((SEL_ROWS, LANES), jnp.float32)
    if isinstance(n_steps, int):
        for s in range(n_steps):
            acc = body(s, acc)
        return acc
    return lax.fori_loop(0, n_steps, body, acc)


def _row_total(lane_counts):
    ones = jnp.ones((LANES, LANES), jnp.bfloat16)
    return _dot(_bf(lane_counts), ones)


def _f32_to_key(x):
    bits = lax.bitcast_convert_type(x, jnp.int32)
    return jnp.where(bits >= 0, bits, INT_MIN - bits)


def _search_window(cand_ref, lo_ref, final_ref):
    hi = lw = None
    for v in range(CAND_SPLIT):
        top = cand_ref[:, v * CAND_DEPTH * LANES:(v * CAND_DEPTH + 1) * LANES]
        hi = top if hi is None else jnp.maximum(hi, top)
        lw = top if lw is None else jnp.minimum(lw, top)
    k_hi = _f32_to_key(jnp.max(hi, axis=-1, keepdims=True))
    k_lw = _f32_to_key(jnp.min(lw, axis=-1, keepdims=True))
    nbits = 32 - lax.clz(k_hi ^ k_lw)
    nb = jnp.max(nbits.astype(jnp.float32)).astype(jnp.int32)
    low = lax.shift_left(jnp.int32(1), jnp.minimum(nb, 31)) - 1
    lo0 = jnp.where(nbits == 0, k_lw, jnp.where(nb >= 32, INT_MIN, k_lw & ~low))
    lo_ref[...] = jnp.broadcast_to(lo0, lo_ref.shape)
    final_ref[...] = jnp.broadcast_to(jnp.where(nbits == 0, 1.0, 0.0), final_ref.shape)
    return 32 - nb


def _radix_search(src_ref, n_steps, lo_ref, cnt_ref, topk, first_bit=None, final_ref=None, stage_ref=None):
    rows_total = lo_ref.shape[0]
    if first_bit is None:
        first_bit = jnp.int32(0)
        lo_ref[...] = jnp.full(lo_ref.shape, INT_MIN, jnp.int32)
        cnt_ref[...] = jnp.zeros(cnt_ref.shape, jnp.float32)
    else:
        cnt_ref[...] = jnp.full(cnt_ref.shape, NO_TIE_LIMIT, jnp.float32)

    if isinstance(n_steps, int):
        assert stage_ref is not None and rows_total == 2 * SEL_ROWS
        half_a, half_b = pl.ds(0, SEL_ROWS), pl.ds(SEL_ROWS, SEL_ROWS)

        def settle(rows, lane_counts, bit):
            cnt = _row_total(lane_counts)
            lo = lo_ref[rows, :]
            ok = cnt >= float(topk)
            if final_ref is not None:
                ok = jnp.logical_and(ok, final_ref[rows, :] <= 0.0)
            lo_ref[rows, :] = jnp.where(ok, lo + bit, lo)
            cnt_ref[rows, :] = jnp.where(ok, cnt, cnt_ref[rows, :])

        stage_ref[half_b, :] = jnp.zeros((SEL_ROWS, LANES), jnp.float32)

        def pass_body(b, carry):
            bit = lax.shift_left(jnp.int32(1), 31 - b)
            prev_bit = lax.shift_left(jnp.int32(1), jnp.minimum(32 - b, 31))
            staged = stage_ref[half_b, :]
            acc_a = _count_lanes(src_ref, 0, n_steps, _key_to_f32(lo_ref[half_a, :] + bit))
            settle(half_b, staged, prev_bit)
            acc_b = _count_lanes(src_ref, SEL_ROWS, n_steps, _key_to_f32(lo_ref[half_b, :] + bit))
            settle(half_a, acc_a, bit)
            stage_ref[half_b, :] = acc_b
            return carry
        lax.fori_loop(first_bit, 32, pass_body, 0)
        settle(half_b, stage_ref[half_b, :], jnp.int32(1))
        return

    def cond(c):
        b, pending = c
        return jnp.logical_and(b < 32, pending > 0)

    def body(c):
        b, _ = c
        bit = lax.shift_left(jnp.int32(1), 31 - b)
        off = None
        for rb in range(rows_total // SEL_ROWS):
            rows = pl.ds(rb * SEL_ROWS, SEL_ROWS)
            cand = lo_ref[rows, :] + bit
            cnt = _row_total(_count_lanes(src_ref, rb * SEL_ROWS, n_steps, _key_to_f32(cand)))
            ok = cnt >= float(topk)
            if final_ref is not None:
                ok = jnp.logical_and(ok, final_ref[rows, :] <= 0.0)
            lo_ref[rows, :] = jnp.where(ok, cand, lo_ref[rows, :])
            cnt = jnp.where(ok, cnt, cnt_ref[rows, :])
            cnt_ref[rows, :] = cnt
            miss = jnp.abs(cnt - float(topk))
            if final_ref is not None:
                miss = jnp.where(final_ref[rows, :] > 0.0, 0.0, miss)
            off = miss if off is None else jnp.maximum(off, miss)
        return b + 1, (jnp.max(off) > 0.0).astype(jnp.int32)
    lax.while_loop(cond, body, (first_bit, jnp.int32(1)))


def _lane_candidates(keys_ref, cand_ref, n_groups):
    n_stack = CAND_DEPTH * CAND_SPLIT

    def rg_body(rg, carry):
        r0 = pl.multiple_of(rg * 8, 8)

        def col_body(s, st):
            st = list(st)
            c0 = pl.multiple_of(s * 8 * LANES, 8 * LANES)
            for u in range(8):
                x = keys_ref[pl.ds(r0, 8), pl.ds(c0 + u * LANES, LANES)]
                base = (u % CAND_SPLIT) * CAND_DEPTH
                for d in range(CAND_DEPTH):
                    cur = st[base + d]
                    st[base + d] = jnp.maximum(cur, x)
                    x = jnp.minimum(cur, x)
            return tuple(st)
        init = tuple(jnp.full((8, LANES), -jnp.inf, jnp.float32) for _ in range(n_stack))
        st = lax.fori_loop(0, n_groups, col_body, init)
        for k in range(n_stack):
            cand_ref[pl.ds(r0, 8), k * LANES:(k + 1) * LANES] = st[k]
        return carry
    lax.fori_loop(0, cand_ref.shape[0] // 8, rg_body, 0)


def _ties_to_keep(src_ref, n_steps, lo_ref, cnt_ref, part_ref, need_ref, topk):
    rows_total = lo_ref.shape[0]
    surplus = jnp.where(lo_ref[...] > NEG_INF_KEY, cnt_ref[...] - float(topk), 0.0)
    need_ref[...] = jnp.full(need_ref.shape, NO_TIE_LIMIT, jnp.float32)

    @pl.when(jnp.max(surplus) > 0.0)
    def _():
        for rb in range(rows_total // SEL_ROWS):
            rows = pl.ds(rb * SEL_ROWS, SEL_ROWS)
            thr = _key_to_f32(jnp.maximum(lo_ref[rows, :], NEG_INF_KEY))
            part_ref[rows, :] = _count_lanes(src_ref, rb * SEL_ROWS, n_steps, thr, strict=True)
        above = _row_total(part_ref[...])
        need_ref[...] = jnp.where(surplus > 0.0, float(topk) - above, NO_TIE_LIMIT)


def _tie_index_bound(keys_ref, thr_ref, need_ref, seen_ref, sig_ref, n_chunks):
    kk_i = lax.broadcasted_iota(jnp.int32, (LANES, LANES), 0)
    jj_i = lax.broadcasted_iota(jnp.int32, (LANES, LANES), 1)
    tri = _bf(jnp.where(kk_i <= jj_i, 1.0, 0.0))
    seen_ref[...] = jnp.zeros(seen_ref.shape, jnp.float32)

    def cond(c):
        j, pending = c
        return jnp.logical_and(j < n_chunks, pending > 0)

    def body(c):
        j, _ = c
        thr, need = thr_ref[...], need_ref[...]
        seen, sig = seen_ref[...], sig_ref[...]
        starts = [pl.multiple_of((j * TIE_CHUNK + u) * LANES, LANES) for u in range(TIE_CHUNK)]
        ties = [_bf(jnp.where(keys_ref[:, pl.ds(c0, LANES)] == thr, 1.0, 0.0)) for c0 in starts]
        prefixes = [_dot(t, tri) for t in ties]
        totals = [_row_total(t) for t in ties]
        seens = [seen]
        for u in range(TIE_CHUNK):
            seens.append(seens[u] + totals[u])
        befores = [_row_total(jnp.where(seens[u] + prefixes[u] < need, 1.0, 0.0))
                   for u in range(TIE_CHUNK)]
        for u, c0 in enumerate(starts):
            here = jnp.where(seens[u] < need, jnp.where(seens[u + 1] >= need, 1.0, 0.0), 0.0)
            sig = jnp.where(here > 0.0, c0.astype(jnp.float32) + befores[u], sig)
        seen = seens[TIE_CHUNK]
        seen_ref[...] = seen
        sig_ref[...] = sig
        waiting = jnp.where(need < NO_TIE_LIMIT, jnp.where(seen < need, 1.0, 0.0), 0.0)
        return j + 1, (jnp.max(waiting) > 0.0).astype(jnp.int32)
    lax.while_loop(cond, body, (jnp.int32(0), jnp.int32(1)))


def _dsa_kernel(qp_ref, iqp_ref, w_ref, g_ref, k_ref, va_ref, ik_ref, o_ref,
                keys_ref, cand_ref, lo_ref, thr_ref, cnt_ref, part_ref, need_ref, sig_ref,
                m_ref, acc_ref, full_ref, *, topk):
    QB, TK = DSA_QB, IDX_TK
    G = DSA_HEADS // DSA_KV_HEADS
    i = pl.program_id(0)
    n_idx = i + 1
    n_att = (n_idx * IDX_TK + ATT_TK - 1) // ATT_TK
    n_sel = n_att * (ATT_TK // SEL_LANES)
    row_id = lax.broadcasted_iota(jnp.int32, (QB, TK), 0)
    col_id = lax.broadcasted_iota(jnp.int32, (QB, TK), 1)

    iq = iqp_ref[...].reshape(IDX_HEADS * QB, LANES)
    wb = [jnp.broadcast_to(w_ref[:, h:h + 1], (QB, TK)) for h in range(IDX_HEADS)]

    def score_tile(j, diag):
        c0 = pl.multiple_of(j * TK, TK)
        s_all = _dot_nt(iq, ik_ref[pl.ds(c0, TK), :])
        score = None
        for h in range(IDX_HEADS):
            term = wb[h] * jnp.maximum(s_all[h * QB:(h + 1) * QB], 0.0)
            score = term if score is None else score + term
        if diag:
            score = jnp.where(col_id <= row_id, score, -jnp.inf)
        keys_ref[:, pl.ds(c0, TK)] = score

    def score_body(j, carry):
        score_tile(j, False)
        return carry

    def score_quad(q, carry):
        for u in range(4):
            score_tile(q * 4 + u, False)
        return carry
    n_quads = lax.shift_right_logical(i, 2)
    lax.fori_loop(0, n_quads, score_quad, 0)
    lax.fori_loop(n_quads * 4, i, score_body, 0)
    score_tile(i, True)

    def blank_body(j, carry):
        c0 = pl.multiple_of(j * TK, TK)
        keys_ref[:, pl.ds(c0, TK)] = jnp.full((QB, TK), -jnp.inf, jnp.float32)
        return carry
    n_vis = n_idx * TK
    windows = [w for w in (CAND_W, 2 * CAND_W) if w <= keys_ref.shape[1]]
    att_end = n_att * (ATT_TK // TK)
    blank_end = att_end
    for w in reversed(windows):
        blank_end = jnp.where(n_vis <= w, jnp.maximum(att_end, w // TK), blank_end)
    lax.fori_loop(n_idx, blank_end, blank_body, 0)

    full_ref[0] = jnp.int32(1)

    @pl.when(n_vis > CAND_W)
    def _():
        _lane_candidates(keys_ref, cand_ref, n_att * (ATT_TK // (8 * LANES)))
        first_bit = _search_window(cand_ref, lo_ref, part_ref)
        _radix_search(cand_ref, CAND_W // SEL_LANES, lo_ref, cnt_ref, topk,
                      first_bit=first_bit, final_ref=part_ref, stage_ref=need_ref)
        thr = _key_to_f32(jnp.maximum(lo_ref[...], NEG_INF_KEY))
        exact = cnt_ref[...] == float(topk)
        lost = None
        for v in range(CAND_SPLIT):
            last = cand_ref[:, (v * CAND_DEPTH + CAND_DEPTH - 1) * LANES:(v * CAND_DEPTH + CAND_DEPTH) * LANES]
            hit = jnp.where(last > thr, 1.0, jnp.where(last == thr, jnp.where(exact, 1.0, 0.0), 0.0))
            lost = hit if lost is None else jnp.maximum(lost, hit)
        full_ref[0] = (jnp.max(lost) > 0.0).astype(jnp.int32)
        _ties_to_keep(cand_ref, CAND_W // SEL_LANES, lo_ref, cnt_ref, part_ref, need_ref, topk)

    @pl.when(full_ref[0] > 0)
    def _():
        def search_rows(n_steps):
            _radix_search(keys_ref, n_steps, lo_ref, cnt_ref, topk, stage_ref=need_ref)
            _ties_to_keep(keys_ref, n_steps, lo_ref, cnt_ref, part_ref, need_ref, topk)
        below = 0
        for w in windows:
            pl.when(jnp.logical_and(n_vis > below, n_vis <= w))(
                functools.partial(search_rows, w // SEL_LANES))
            below = w
        pl.when(n_vis > below)(functools.partial(search_rows, n_sel))

    tau = lo_ref[...]
    found = tau > NEG_INF_KEY
    thr_ref[...] = jnp.where(found, _key_to_f32(jnp.maximum(tau, NEG_INF_KEY)), F32_LOWEST)
    sig_ref[...] = jnp.full(sig_ref.shape, NO_TIE_LIMIT, jnp.float32)

    @pl.when(jnp.min(need_ref[...]) < NO_TIE_LIMIT)
    def _():
        _tie_index_bound(keys_ref, thr_ref, need_ref, part_ref, sig_ref, n_att * (ATT_TK // (TIE_CHUNK * LANES)))

    m_ref[...] = jnp.full(m_ref.shape, NEG_INIT, jnp.float32)
    acc_ref[...] = jnp.zeros(acc_ref.shape, jnp.float32)
    n_grp = ATT_TK // LANES
    lane_f = lax.broadcasted_iota(jnp.int32, (QB, LANES), 1).astype(jnp.float32)

    def attn_tile(j):
        c0 = pl.multiple_of(j * ATT_TK, ATT_TK)
        tau_b = thr_ref[...]
        sig_rel = sig_ref[...] - c0.astype(jnp.float32)
        parts = []
        for u in range(n_grp):
            kk = keys_ref[:, pl.ds(c0 + u * LANES, LANES)]
            tie_bias = jnp.where(lane_f + float(u * LANES) <= sig_rel, 0.0, NEG_MASK)
            parts.append(jnp.where(kk > tau_b, 0.0, jnp.where(kk == tau_b, tie_bias, NEG_MASK)))
        bias = jnp.concatenate(parts, axis=1)
        kt = k_ref[pl.ds(c0, ATT_TK), :]
        all_logits = [_dot_nt(qp_ref[n * G:(n + 1) * G].reshape(G * QB, LANES), kt)
                      for n in range(DSA_KV_HEADS)]
        for n in range(DSA_KV_HEADS):
            logits = all_logits[n]
            va = va_ref[n, pl.ds(c0, ATT_TK), :]
            for gq in range(G):
                h = n * G + gq
                lm = logits[gq * QB:(gq + 1) * QB] + bias
                tmax = lm[:, 0:LANES]
                for u in range(1, n_grp):
                    tmax = jnp.maximum(tmax, lm[:, u * LANES:(u + 1) * LANES])
                m_old = m_ref[h]
                m_new = jnp.maximum(m_old, jnp.max(tmax, axis=-1, keepdims=True))
                p = jnp.concatenate(
                    [_bf(jnp.exp2(lm[:, u * LANES:(u + 1) * LANES] - m_new)) for u in range(n_grp)], axis=1)
                m_ref[h] = m_new
                acc_ref[h] = jnp.exp2(m_old - m_new) * acc_ref[h] + _dot(p, va)

    def attn_body(j, carry):
        attn_tile(j)
        return carry
    lax.fori_loop(0, n_att, attn_body, 0)

    lane = lax.broadcasted_iota(jnp.int32, (QB, LANES), 1)
    g = _f32(g_ref[...])
    for pair in range(DSA_HEADS // 2):
        halves = []
        for pos in range(2):
            a = acc_ref[2 * pair + pos]
            halves.append(a * (1.0 / pltpu.roll(a, DSA_DH, 1)))
        o = jnp.where(lane < DSA_DH, halves[0], pltpu.roll(halves[1], DSA_DH, 1))
        sl = slice(pair * LANES, (pair + 1) * LANES)
        o_ref[:, sl] = _bf(o * _silu(g[:, sl]))


def _dsa_call(p, qp, kk, va, iqp, ik, w4):
    s = p.shape[0]
    topk = min(TOPK_MAX, s // 4)
    QB = DSA_QB
    assert IDX_TK == QB and s % ATT_TK == 0 and ATT_TK % SEL_LANES == 0 and s // LANES <= 256
    once = pl.Buffered(1)
    return pl.pallas_call(
        functools.partial(_dsa_kernel, topk=topk),
        grid=(s // QB,),
        in_specs=[pl.BlockSpec((DSA_HEADS, QB, LANES), lambda i: (0, i, 0)),
                  pl.BlockSpec((IDX_HEADS, QB, LANES), lambda i: (0, i, 0)),
                  pl.BlockSpec((QB, LANES), lambda i: (i, 0)),
                  _pspec(QB, "dsa_g", 512),
                  pl.BlockSpec((s, LANES), lambda i: (0, 0), pipeline_mode=once),
                  pl.BlockSpec((DSA_KV_HEADS, s, LANES), lambda i: (0, 0, 0), pipeline_mode=once),
                  pl.BlockSpec((s, LANES), lambda i: (0, 0), pipeline_mode=once)],
        out_specs=pl.BlockSpec((QB, 512), lambda i: (i, 0)),
        out_shape=jax.ShapeDtypeStruct((s, 512), jnp.bfloat16),
        scratch_shapes=[pltpu.VMEM((QB, s), jnp.float32),
                        pltpu.VMEM((QB, CAND_W), jnp.float32),
                        pltpu.VMEM((QB, LANES), jnp.int32),
                        pltpu.VMEM((QB, LANES), jnp.float32),
                        pltpu.VMEM((QB, LANES), jnp.float32),
                        pltpu.VMEM((QB, LANES), jnp.float32),
                        pltpu.VMEM((QB, LANES), jnp.float32),
                        pltpu.VMEM((QB, LANES), jnp.float32),
                        pltpu.VMEM((DSA_HEADS, QB, LANES), jnp.float32),
                        pltpu.VMEM((DSA_HEADS, QB, LANES), jnp.float32),
                        pltpu.SMEM((1,), jnp.int32)],
        compiler_params=_cparams(1, vmem_mb=56),
        name="dsa",
    )(qp, iqp, w4, p, kk, va, ik)


def _merge_kernel(x_ref, ret_ref, dsa_ref, gl_ref, m_ref, wr_ref, wd_ref, wg_ref, wo_ref,
                  post_ref, gate_ref, o_ref):
    d = D_MODEL
    y = (_sigmoid(_f32(m_ref[:, 0:d])) * _dot(ret_ref[...], wr_ref[...])
         + _sigmoid(_f32(m_ref[:, d:2 * d])) * _dot(dsa_ref[...], wd_ref[...])
         + _sigmoid(_f32(m_ref[:, 2 * d:3 * d])) * _dot(gl_ref[...], wg_ref[...]))
    y = _dot(_bf(y), wo_ref[...])
    yn = y * lax.rsqrt(jnp.mean(y * y, axis=-1, keepdims=True) + RMS_EPS) * post_ref[...]
    o_ref[...] = x_ref[...] + gate_ref[...] * yn


def _merge_call(x2, ret, dsa, gl, p, wr, wd, wg, wo, post, gate):
    s, d = x2.shape
    tm = min(512, s)
    rows = lambda w: pl.BlockSpec((tm, w), lambda i: (i, 0))
    whole = lambda a: pl.BlockSpec(a.shape, lambda i: (0, 0))
    return pl.pallas_call(
        _merge_kernel,
        grid=(s // tm,),
        in_specs=[rows(d), rows(512), rows(512), rows(512), _pspec(tm, "merge", 3072),
                  whole(wr), whole(wd), whole(wg), whole(wo), whole(post), whole(gate)],
        out_specs=rows(d),
        out_shape=jax.ShapeDtypeStruct((s, d), jnp.float32),
        compiler_params=_cparams(1),
        name="merge_out",
    )(x2, ret, dsa, gl, p, wr, wd, wg, wo, post, gate)


def _pack_w_in(w_in):
    depth, d, _ = w_in.shape
    w_in = w_in.astype(jnp.bfloat16)
    zeros = lambda n: jnp.zeros((depth, d, n), w_in.dtype)
    src = lambda name: w_in[:, :, _SRC[name][0]:_SRC[name][0] + _SRC[name][1]]
    pieces, at = [], 0
    for name, width in _PACK:
        assert at == PCOL[name]
        if name == "idx_kw":
            cols = [src("idx_k"), src("idx_w"), zeros(width - IDX_DH - IDX_HEADS)]
        elif name == "gla_a":
            cols = [src("gla_a"), zeros(width - GLA_RANK)]
        else:
            cols = [src(name)]
        pieces += cols
        at += width
    pieces.append(zeros(P_WIDTH - at))
    return jnp.concatenate(pieces, axis=-1).astype(jnp.bfloat16)


def kernel(x, c, positions, ada_w, ada_b, pre_norm, post_norm, w_in, gla_w_lr, gla_b_lr,
           w_br_ret, w_br_dsa, w_br_gla, w_out):
    b, s, d = x.shape
    assert b == 1 and d == D_MODEL
    depth = ada_w.shape[0]
    x2 = x.reshape(s, d)
    mod = _mod_call(jnp.broadcast_to(c, (8, d)), ada_w, ada_b.reshape(depth, 1, 3 * d))[:, 0:1, :]
    rcos, rsin, dcos, dsin = _tab_call(positions.reshape(s, 1))
    w_pack = _pack_w_in(w_in)
    wlr_pad = jnp.pad(gla_w_lr, ((0, 0), (0, LANES - GLA_RANK), (0, 0)))
    for l in range(depth):
        shift, scale, gate = mod[l, :, 0:d], mod[l, :, d:2 * d], mod[l, :, 2 * d:3 * d]
        p = _proj_call(x2, pre_norm[l][None, :], scale, shift, w_pack[l])
        ret = _ret_call(p, rcos, rsin)
        gl = _gla_call(p, wlr_pad[l], gla_b_lr[l][None, :])
        qp, kk, va, iqp, ik, w4 = _dprep_call(p, dcos, dsin)
        dsa = _dsa_call(p, qp, kk, va, iqp, ik, w4)
        x2 = _merge_call(x2, ret, dsa, gl, p, _bf(w_br_ret[l]), _bf(w_br_dsa[l]), _bf(w_br_gla[l]),
                         _bf(w_out[l]), post_norm[l][None, :], gate)
    return x2.reshape(b, s, d)
```

```python
import functools
import math

import jax
import jax.numpy as jnp
from jax import lax
from jax.experimental import pallas as pl
from jax.experimental.pallas import tpu as pltpu

D_MODEL = 1024
DEPTH = 4
RET_HEADS, RET_DK, RET_DV, RET_CHUNK, RET_THETA = 4, 64, 128, 128, 10000.0
DSA_HEADS, DSA_KV_HEADS, DSA_DH = 8, 2, 64
DSA_ROT = DSA_DH // 4
ROPE_THETA = 500000.0
IDX_HEADS, IDX_DH = 4, 64
TOPK_MAX = 256
GLA_HEADS, GLA_DK, GLA_DV, GLA_RANK, GLA_TAU, GLA_CHUNK = 4, 64, 128, 16, 16.0, 64
RMS_EPS = 1e-6
LANES = 128

_SRC = {}
_off = 0
for _name, _w in (("ret_q", 256), ("ret_k", 256), ("ret_v", 512), ("ret_g", 512),
                  ("dsa_q", 512), ("dsa_k", 128), ("dsa_v", 128), ("dsa_g", 512),
                  ("idx_q", 256), ("idx_k", 64), ("idx_w", 4),
                  ("gla_q", 256), ("gla_k", 256), ("gla_v", 512), ("gla_g", 512), ("gla_a", 16),
                  ("merge", 3072)):
    _SRC[_name] = (_off, _w)
    _off += _w
IN_WIDTH = _off

_PACK = (("merge", 3072), ("ret_q", 256), ("ret_k", 256), ("ret_v", 512), ("ret_g", 512),
         ("dsa_q", 512), ("dsa_g", 512), ("gla_v", 512), ("gla_g", 512),
         ("gla_q", 256), ("gla_k", 256), ("idx_q", 256),
         ("dsa_k", 128), ("dsa_v", 128), ("idx_kw", 128), ("gla_a", 128))
PCOL = {}
_off = 0
for _name, _w in _PACK:
    assert _off % min(_w, 1024) == 0
    PCOL[_name] = _off
    _off += _w
P_WIDTH = 8192
assert _off <= P_WIDTH

LOG2E = math.log2(math.e)
INT_MIN = -(2 ** 31)
F32_LOWEST = -3.4028234663852886e38
NEG_INF_KEY = -0x7F800000
NEG_INIT = -1e30
NEG_MASK = -2e30


def _cparams(n_axes, vmem_mb=48):
    return pltpu.CompilerParams(dimension_semantics=("arbitrary",) * n_axes,
                                vmem_limit_bytes=vmem_mb * 1024 * 1024)


def _bf(x):
    return x.astype(jnp.bfloat16)


def _f32(x):
    return x.astype(jnp.float32)


def _dot(a, b):
    return jnp.dot(a, b, preferred_element_type=jnp.float32)


def _dot_nt(a, b):
    return lax.dot_general(a, b, (((1,), (1,)), ((), ())), preferred_element_type=jnp.float32)


def _dot_tn(a, b):
    return lax.dot_general(a, b, (((0,), (0,)), ((), ())), preferred_element_type=jnp.float32)


def _split3(x):
    hi = _bf(x)
    r1 = x - hi.astype(jnp.float32)
    mid = _bf(r1)
    lo = _bf(r1 - mid.astype(jnp.float32))
    return hi, mid, lo


def _silu(x):
    return x * (1.0 / (1.0 + jnp.exp(-x)))


def _sigmoid(x):
    return 1.0 / (1.0 + jnp.exp(-x))


def _mod_kernel(c_ref, w_ref, b_ref, o_ref):
    c = c_ref[...]
    ca = _silu(c)
    acc = None
    for t in _split3(ca):
        for u in _split3(w_ref[0]):
            part = _dot(t, u)
            acc = part if acc is None else acc + part
    o_ref[0] = acc + b_ref[0]


def _mod_call(c8, ada_w, ada_b3):
    depth, d, n = ada_w.shape
    tn = 1024
    return pl.pallas_call(
        _mod_kernel,
        grid=(depth, n // tn),
        in_specs=[pl.BlockSpec((8, d), lambda l, j: (0, 0)),
                  pl.BlockSpec((1, d, tn), lambda l, j: (l, 0, j)),
                  pl.BlockSpec((1, 1, tn), lambda l, j: (l, 0, j))],
        out_specs=pl.BlockSpec((1, 8, tn), lambda l, j: (l, 0, j)),
        out_shape=jax.ShapeDtypeStruct((depth, 8, n), jnp.float32),
        compiler_params=_cparams(2),
        name="adaln_mod",
    )(c8, ada_w, ada_b3)


def _tab_kernel(pos_ref, rf_ref, rs_ref, df_ref, ds_ref, rc_o, rsn_o, dc_o, dsn_o):
    pos = pos_ref[...].astype(jnp.float32)
    ang = pos * rf_ref[...]
    rc_o[...] = jnp.cos(ang)
    rsn_o[...] = jnp.sin(ang) * rs_ref[...]
    ang = pos * df_ref[...]
    dc_o[...] = jnp.cos(ang)
    dsn_o[...] = jnp.sin(ang) * ds_ref[...]


def _rope_rows():
    half = RET_DK // 2
    f = RET_THETA ** (-jnp.arange(half, dtype=jnp.float32) * 2.0 / RET_DK)
    rf = jnp.tile(jnp.concatenate([f, f]), RET_HEADS)[None, :]
    rs = jnp.tile(jnp.concatenate([-jnp.ones(half), jnp.ones(half)]), RET_HEADS)[None, :].astype(jnp.float32)
    half = DSA_ROT // 2
    f = ROPE_THETA ** (-jnp.arange(half, dtype=jnp.float32) * 2.0 / DSA_ROT)
    z = jnp.zeros(DSA_DH - DSA_ROT, jnp.float32)
    df = jnp.tile(jnp.concatenate([f, f, z]), 2)[None, :]
    ds = jnp.tile(jnp.concatenate([-jnp.ones(half), jnp.ones(half), z]), 2)[None, :].astype(jnp.float32)
    return rf, rs, df, ds


def _tab_call(pos_col):
    s = pos_col.shape[0]
    tm = min(1024, s)
    rf, rs, df, ds = _rope_rows()
    row = lambda w: pl.BlockSpec((1, w), lambda i: (0, 0))
    out = lambda w: pl.BlockSpec((tm, w), lambda i: (i, 0))
    return pl.pallas_call(
        _tab_kernel,
        grid=(s // tm,),
        in_specs=[pl.BlockSpec((tm, 1), lambda i: (i, 0)), row(256), row(256), row(128), row(128)],
        out_specs=[out(256), out(256), out(128), out(128)],
        out_shape=[jax.ShapeDtypeStruct((s, 256), jnp.float32), jax.ShapeDtypeStruct((s, 256), jnp.float32),
                   jax.ShapeDtypeStruct((s, 128), jnp.float32), jax.ShapeDtypeStruct((s, 128), jnp.float32)],
        compiler_params=_cparams(1),
        name="rope_tables",
    )(pos_col, rf, rs, df, ds)


def _swap_halves(x, half, period):
    w = x.shape[-1]
    lane = lax.broadcasted_iota(jnp.int32, x.shape, x.ndim - 1) & (period - 1)
    up = pltpu.roll(x, w - half, x.ndim - 1)
    dn = pltpu.roll(x, half, x.ndim - 1)
    return jnp.where(lane < half, up, dn)


def _rope(x, cos, sin_signed, half, period):
    return x * cos + _swap_halves(x, half, period) * sin_signed


def _proj_kernel(x_ref, pre_ref, sc_ref, sh_ref, w_ref, o_ref):
    x = x_ref[...]
    xn = x * lax.rsqrt(jnp.mean(x * x, axis=-1, keepdims=True) + RMS_EPS)
    h = xn * pre_ref[...] * (1.0 + sc_ref[...]) + sh_ref[...]
    o_ref[...] = _bf(_dot(_bf(h), w_ref[...]))


def _proj_call(x2, pre, scale, shift, w_pack):
    s, d = x2.shape
    tm, tn = min(512, s), 2048
    vec = pl.BlockSpec((1, d), lambda j, i: (0, 0))
    return pl.pallas_call(
        _proj_kernel,
        grid=(P_WIDTH // tn, s // tm),
        in_specs=[pl.BlockSpec((tm, d), lambda j, i: (i, 0)), vec, vec, vec,
                  pl.BlockSpec((d, tn), lambda j, i: (0, j))],
        out_specs=pl.BlockSpec((tm, tn), lambda j, i: (i, j)),
        out_shape=jax.ShapeDtypeStruct((s, P_WIDTH), jnp.bfloat16),
        compiler_params=_cparams(2),
        name="in_proj",
    )(x2, pre, scale, shift, w_pack)


def _pspec(tm, name, width):
    blk = PCOL[name] // width
    assert PCOL[name] % width == 0
    return pl.BlockSpec((tm, width), lambda i: (i, blk))


def _ret_log_g(h):
    return math.log1p(-(2.0 ** (-5.0 - h)))


def _ret_kernel(q_ref, k_ref, v_ref, g_ref, cos_ref, sin_ref, o_ref,
                state_ref, decay_ref, qdec_ref, kend_ref, *, chunks):
    C = RET_CHUNK

    @pl.when(pl.program_id(0) == 0)
    def _():
        state_ref[...] = jnp.zeros_like(state_ref)
        ii = lax.broadcasted_iota(jnp.int32, (C, C), 0)
        jj = lax.broadcasted_iota(jnp.int32, (C, C), 1)
        rel = (ii - jj).astype(jnp.float32)
        row = lax.broadcasted_iota(jnp.int32, (C, RET_DK), 0).astype(jnp.float32)
        for h in range(RET_HEADS):
            lg = _ret_log_g(h)
            decay_ref[h] = jnp.where(rel >= 0, jnp.exp(lg * jnp.maximum(rel, 0.0)), 0.0)
            qdec_ref[:, h * RET_DK:(h + 1) * RET_DK] = jnp.exp((row + 1.0) * lg)
            kend_ref[:, h * RET_DK:(h + 1) * RET_DK] = jnp.exp((C - 1.0 - row) * lg)

    heads = [(slice(h * RET_DK, (h + 1) * RET_DK), slice(h * RET_DV, (h + 1) * RET_DV))
             for h in range(RET_HEADS)]
    chunk_rows = [slice(c * C, (c + 1) * C) for c in range(chunks)]
    qbs, kbs, qds, kds = [], [], [], []
    for rows in chunk_rows:
        cos, sin = cos_ref[rows, :], sin_ref[rows, :]
        q = _rope(_f32(q_ref[rows, :]), cos, sin, RET_DK // 2, RET_DK) * (RET_DK ** -0.5)
        k = _rope(_f32(k_ref[rows, :]), cos, sin, RET_DK // 2, RET_DK)
        qbs.append(_bf(q))
        kbs.append(_bf(k))
        qds.append(_bf(q * qdec_ref[...]))
        kds.append(_bf(k * kend_ref[...]))
    scores = [[_bf(_dot_nt(qbs[c][:, dk], kbs[c][:, dk]) * decay_ref[h]) for h, (dk, _) in enumerate(heads)]
              for c in range(chunks)]
    intra = [[_dot(scores[c][h], v_ref[chunk_rows[c], heads[h][1]]) for h in range(RET_HEADS)]
             for c in range(chunks)]
    kvs = [[_dot_tn(kds[c][:, heads[h][0]], v_ref[chunk_rows[c], heads[h][1]]) for h in range(RET_HEADS)]
           for c in range(chunks)]
    states = [state_ref[h] for h in range(RET_HEADS)]
    for c, rows in enumerate(chunk_rows):
        g = _f32(g_ref[rows, :])
        for h, (dk, dv) in enumerate(heads):
            o = intra[c][h] + _dot(qds[c][:, dk], _bf(states[h]))
            states[h] = math.exp(C * _ret_log_g(h)) * states[h] + kvs[c][h]
            o = o * lax.rsqrt(jnp.mean(o * o, axis=-1, keepdims=True) + RMS_EPS)
            o_ref[rows, dv] = _bf(o * _silu(g[:, dv]))
    for h in range(RET_HEADS):
        state_ref[h] = states[h]


def _ret_call(p, rcos, rsin):
    s = p.shape[0]
    chunks = 4
    tm = RET_CHUNK * chunks
    tab = pl.BlockSpec((tm, 256), lambda i: (i, 0))
    return pl.pallas_call(
        functools.partial(_ret_kernel, chunks=chunks),
        grid=(s // tm,),
        in_specs=[_pspec(tm, "ret_q", 256), _pspec(tm, "ret_k", 256), _pspec(tm, "ret_v", 512),
                  _pspec(tm, "ret_g", 512), tab, tab],
        out_specs=pl.BlockSpec((tm, 512), lambda i: (i, 0)),
        out_shape=jax.ShapeDtypeStruct((s, 512), jnp.bfloat16),
        scratch_shapes=[pltpu.VMEM((RET_HEADS, RET_DK, RET_DV), jnp.float32),
                        pltpu.VMEM((RET_HEADS, RET_CHUNK, RET_CHUNK), jnp.float32),
                        pltpu.VMEM((RET_CHUNK, RET_HEADS * RET_DK), jnp.float32),
                        pltpu.VMEM((RET_CHUNK, RET_HEADS * RET_DK), jnp.float32)],
        compiler_params=_cparams(1),
        name="retention",
    )(p, p, p, p, rcos, rsin)


def _gla_kernel(q_ref, k_ref, v_ref, g_ref, a_ref, wlr_ref, blr_ref, o_ref, state_ref, *, chunks):
    C = GLA_CHUNK

    @pl.when(pl.program_id(0) == 0)
    def _():
        state_ref[...] = jnp.zeros_like(state_ref)

    ii = lax.broadcasted_iota(jnp.int32, (C, C), 0)
    jj = lax.broadcasted_iota(jnp.int32, (C, C), 1)
    levels = [1 << b for b in range(C.bit_length() - 2, -1, -1)]
    parent = lambda x, s: lax.shift_right_logical(x, s.bit_length())
    level_masks = [jnp.logical_and(parent(ii, s) == parent(jj, s), (ii & s) > (jj & s)) for s in levels]
    diagonal = ii == jj
    sums = [jnp.where(jj <= ii, 1.0, 0.0)]
    sums += [jnp.where(jj <= lax.shift_left(parent(ii, s), s.bit_length()) + (s - 1), 1.0, 0.0) for s in levels]
    cum_rows = _bf(jnp.concatenate(sums, axis=0))
    wlr = wlr_ref[...]
    w_hi, w_mid, w_lo = _split3(wlr)

    a_hi, a_mid, a_lo = _split3(_f32(a_ref[...]))
    z = (_dot(a_hi, w_hi) + (_dot(a_hi, w_mid) + _dot(a_mid, w_hi))
         + (_dot(a_hi, w_lo) + _dot(a_mid, w_mid) + _dot(a_lo, w_hi))) + blr_ref[...]
    log_a_all = (jnp.minimum(z, 0.0) - jnp.log1p(jnp.exp(-jnp.abs(z)))) * (1.0 / GLA_TAU)

    heads = [(slice(h * GLA_DK, (h + 1) * GLA_DK), slice(h * GLA_DV, (h + 1) * GLA_DV))
             for h in range(GLA_HEADS)]
    chunk_rows = [slice(c * C, (c + 1) * C) for c in range(chunks)]
    sums_all = []
    for rows in chunk_rows:
        l_hi, l_mid, l_lo = _split3(log_a_all[rows, :])
        sums_all.append(_dot(cum_rows, l_hi) + _dot(cum_rows, l_mid) + _dot(cum_rows, l_lo))
    qls, kls, qgs, kds, e_lasts = [], [], [], [], []
    for rows, stacked in zip(chunk_rows, sums_all):
        bcum = stacked[0:C, :]
        b_last = bcum[C - 1:C, :]
        q = _f32(q_ref[rows, :]) * (GLA_DK ** -0.5)
        k = _f32(k_ref[rows, :])
        q_lv, k_lv = [_bf(q)], [_bf(k)]
        for n in range(len(levels)):
            base = stacked[(n + 1) * C:(n + 2) * C, :]
            q_lv.append(_bf(q * jnp.exp(jnp.minimum(bcum - base, 0.0))))
            k_lv.append(_bf(k * jnp.exp(jnp.minimum(base - bcum, 0.0))))
        qls.append(q_lv)
        kls.append(k_lv)
        qgs.append(_bf(q * jnp.exp(bcum)))
        kds.append(_bf(k * jnp.exp(b_last - bcum)))
        e_lasts.append(jnp.exp(b_last))
    pair_masks = [diagonal] + level_masks

    def chunk_attn(c, dk):
        total = None
        for mask, q_l, k_l in zip(pair_masks, qls[c], kls[c]):
            part = jnp.where(mask, _dot_nt(q_l[:, dk], k_l[:, dk]), 0.0)
            total = part if total is None else total + part
        return _bf(total)
    attns = [[chunk_attn(c, dk) for dk, _ in heads] for c in range(chunks)]
    intra = [[_dot(attns[c][h], v_ref[chunk_rows[c], heads[h][1]]) for h in range(GLA_HEADS)]
             for c in range(chunks)]
    kvs = [[_dot_tn(v_ref[chunk_rows[c], heads[h][1]], kds[c][:, heads[h][0]]) for h in range(GLA_HEADS)]
           for c in range(chunks)]
    states = [state_ref[h] for h in range(GLA_HEADS)]
    for c, rows in enumerate(chunk_rows):
        g = _f32(g_ref[rows, :])
        for h, (dk, dv) in enumerate(heads):
            o = intra[c][h] + _dot_nt(qgs[c][:, dk], _bf(states[h]))
            states[h] = e_lasts[c][:, dk] * states[h] + kvs[c][h]
            o = o * lax.rsqrt(jnp.mean(o * o, axis=-1, keepdims=True) + RMS_EPS)
            o_ref[rows, dv] = _bf(o * _silu(g[:, dv]))
    for h in range(GLA_HEADS):
        state_ref[h] = states[h]


def _gla_call(p, wlr_pad, blr):
    s = p.shape[0]
    chunks = 8
    tm = GLA_CHUNK * chunks
    return pl.pallas_call(
        functools.partial(_gla_kernel, chunks=chunks),
        grid=(s // tm,),
        in_specs=[_pspec(tm, "gla_q", 256), _pspec(tm, "gla_k", 256), _pspec(tm, "gla_v", 512),
                  _pspec(tm, "gla_g", 512), _pspec(tm, "gla_a", 128),
                  pl.BlockSpec((128, 256), lambda i: (0, 0)), pl.BlockSpec((1, 256), lambda i: (0, 0))],
        out_specs=pl.BlockSpec((tm, 512), lambda i: (i, 0)),
        out_shape=jax.ShapeDtypeStruct((s, 512), jnp.bfloat16),
        scratch_shapes=[pltpu.VMEM((GLA_HEADS, GLA_DV, GLA_DK), jnp.float32)],
        compiler_params=_cparams(1),
        name="gla",
    )(p, p, p, p, p, wlr_pad, blr)


def _place_head(pair, src_pos, dst_pos):
    lane = lax.broadcasted_iota(jnp.int32, pair.shape, 1)
    if src_pos != dst_pos:
        pair = pltpu.roll(pair, 64, 1)
    keep = (lane < 64) if dst_pos == 0 else (lane >= 64)
    return jnp.where(keep, pair, 0.0)


def _dprep_kernel(q_ref, k_ref, v_ref, iq_ref, ikw_ref, cos_ref, sin_ref,
                  qp_o, k_o, v_o, iqp_o, ik_o, w_o):
    cos, sin = cos_ref[...], sin_ref[...]
    half = DSA_ROT // 2
    cos4 = jnp.concatenate([cos] * 4, axis=1)
    sin4 = jnp.concatenate([sin] * 4, axis=1)
    q = _rope(_f32(q_ref[...]), cos4, sin4, half, DSA_DH) * (DSA_DH ** -0.5 * LOG2E)
    for h in range(DSA_HEADS):
        pair = q[:, (h // 2) * 128:(h // 2 + 1) * 128]
        qp_o[h] = _bf(_place_head(pair, h % 2, h // (DSA_HEADS // DSA_KV_HEADS)))
    k_o[...] = _bf(_rope(_f32(k_ref[...]), cos, sin, half, DSA_DH))
    v = _f32(v_ref[...])
    lane = lax.broadcasted_iota(jnp.int32, v.shape, 1)
    v_o[0] = _bf(jnp.where(lane < DSA_DH, v, 1.0))
    v_o[1] = _bf(jnp.where(lane < DSA_DH, pltpu.roll(v, DSA_DH, 1), 1.0))
    iq = _rope(_f32(iq_ref[...]), cos4[:, :256], sin4[:, :256], half, IDX_DH)
    for h in range(IDX_HEADS):
        pair = iq[:, (h // 2) * 128:(h // 2 + 1) * 128]
        iqp_o[h] = _bf(_place_head(pair, h % 2, 0))
    ikw = _f32(ikw_ref[...])
    lane = lax.broadcasted_iota(jnp.int32, ikw.shape, 1)
    ik_o[...] = _bf(jnp.where(lane < IDX_DH, _rope(ikw, cos, sin, half, DSA_DH), 0.0))
    wscale = (IDX_HEADS ** -0.5) * (IDX_DH ** -0.5)
    w_o[...] = pltpu.roll(ikw, 128 - IDX_DH, 1) * wscale


def _dprep_call(p, dcos, dsin):
    s = p.shape[0]
    tm = min(512, s)
    tab = pl.BlockSpec((tm, 128), lambda i: (i, 0))
    return pl.pallas_call(
        _dprep_kernel,
        grid=(s // tm,),
        in_specs=[_pspec(tm, "dsa_q", 512), _pspec(tm, "dsa_k", 128), _pspec(tm, "dsa_v", 128),
                  _pspec(tm, "idx_q", 256), _pspec(tm, "idx_kw", 128), tab, tab],
        out_specs=[pl.BlockSpec((DSA_HEADS, tm, 128), lambda i: (0, i, 0)), tab,
                   pl.BlockSpec((DSA_KV_HEADS, tm, 128), lambda i: (0, i, 0)),
                   pl.BlockSpec((IDX_HEADS, tm, 128), lambda i: (0, i, 0)), tab, tab],
        out_shape=[jax.ShapeDtypeStruct((DSA_HEADS, s, 128), jnp.bfloat16),
                   jax.ShapeDtypeStruct((s, 128), jnp.bfloat16),
                   jax.ShapeDtypeStruct((DSA_KV_HEADS, s, 128), jnp.bfloat16),
                   jax.ShapeDtypeStruct((IDX_HEADS, s, 128), jnp.bfloat16),
                   jax.ShapeDtypeStruct((s, 128), jnp.bfloat16),
                   jax.ShapeDtypeStruct((s, 128), jnp.float32)],
        compiler_params=_cparams(1),
        name="dsa_prep",
    )(p, p, p, p, p, dcos, dsin)


DSA_QB = 256
IDX_TK = 256
ATT_TK = 1024
SEL_ROWS = 128
SEL_LANES = 512
CAND_DEPTH = 10
CAND_SPLIT = 2
CAND_W = CAND_DEPTH * CAND_SPLIT * LANES
TIE_CHUNK = 8
NO_TIE_LIMIT = 1e9
assert CAND_W % SEL_LANES == 0 and 8 % CAND_SPLIT == 0


def _key_to_f32(key):
    bits = jnp.where(key >= 0, key, (0 - key) | INT_MIN)
    return lax.bitcast_convert_type(bits, jnp.float32)


def _count_lanes(keys_ref, r0, n_steps, cand, strict=False):
    def body(s, acc):
        c0 = s * SEL_LANES if isinstance(s, int) else pl.multiple_of(s * SEL_LANES, SEL_LANES)
        for u in range(SEL_LANES // LANES):
            kk = keys_ref[pl.ds(r0, SEL_ROWS), pl.ds(c0 + u * LANES, LANES)]
            acc = acc + jnp.where((kk > cand) if strict else (kk >= cand), 1.0, 0.0)
        return acc
    acc = jnp.zeros((SEL_ROWS, LANES), jnp.float32)
    if isinstance(n_steps, int):
        for s in range(n_steps):
            acc = body(s, acc)
        return acc
    return lax.fori_loop(0, n_steps, body, acc)


def _row_total(lane_counts):
    ones = jnp.ones((LANES, LANES), jnp.bfloat16)
    return _dot(_bf(lane_counts), ones)


def _f32_to_key(x):
    bits = lax.bitcast_convert_type(x, jnp.int32)
    return jnp.where(bits >= 0, bits, INT_MIN - bits)


def _search_window(cand_ref, lo_ref, final_ref):
    hi = lw = None
    for v in range(CAND_SPLIT):
        top = cand_ref[:, v * CAND_DEPTH * LANES:(v * CAND_DEPTH + 1) * LANES]
        hi = top if hi is None else jnp.maximum(hi, top)
        lw = top if lw is None else jnp.minimum(lw, top)
    k_hi = _f32_to_key(jnp.max(hi, axis=-1, keepdims=True))
    k_lw = _f32_to_key(jnp.min(lw, axis=-1, keepdims=True))
    nbits = 32 - lax.clz(k_hi ^ k_lw)
    nb = jnp.max(nbits.astype(jnp.float32)).astype(jnp.int32)
    low = lax.shift_left(jnp.int32(1), jnp.minimum(nb, 31)) - 1
    lo0 = jnp.where(nbits == 0, k_lw, jnp.where(nb >= 32, INT_MIN, k_lw & ~low))
    lo_ref[...] = jnp.broadcast_to(lo0, lo_ref.shape)
    final_ref[...] = jnp.broadcast_to(jnp.where(nbits == 0, 1.0, 0.0), final_ref.shape)
    return 32 - nb


def _radix_search(src_ref, n_steps, lo_ref, cnt_ref, topk, first_bit=None, final_ref=None, stage_ref=None):
    rows_total = lo_ref.shape[0]
    if first_bit is None:
        first_bit = jnp.int32(0)
        lo_ref[...] = jnp.full(lo_ref.shape, INT_MIN, jnp.int32)
        cnt_ref[...] = jnp.zeros(cnt_ref.shape, jnp.float32)
    else:
        cnt_ref[...] = jnp.full(cnt_ref.shape, NO_TIE_LIMIT, jnp.float32)

    if isinstance(n_steps, int):
        assert stage_ref is not None and rows_total == 2 * SEL_ROWS
        half_a, half_b = pl.ds(0, SEL_ROWS), pl.ds(SEL_ROWS, SEL_ROWS)

        def settle(rows, lane_counts, bit):
            cnt = _row_total(lane_counts)
            lo = lo_ref[rows, :]
            ok = cnt >= float(topk)
            if final_ref is not None:
                ok = jnp.logical_and(ok, final_ref[rows, :] <= 0.0)
            lo_ref[rows, :] = jnp.where(ok, lo + bit, lo)
            cnt_ref[rows, :] = jnp.where(ok, cnt, cnt_ref[rows, :])

        stage_ref[half_b, :] = jnp.zeros((SEL_ROWS, LANES), jnp.float32)

        def pass_body(b, carry):
            bit = lax.shift_left(jnp.int32(1), 31 - b)
            prev_bit = lax.shift_left(jnp.int32(1), jnp.minimum(32 - b, 31))
            staged = stage_ref[half_b, :]
            acc_a = _count_lanes(src_ref, 0, n_steps, _key_to_f32(lo_ref[half_a, :] + bit))
            settle(half_b, staged, prev_bit)
            acc_b = _count_lanes(src_ref, SEL_ROWS, n_steps, _key_to_f32(lo_ref[half_b, :] + bit))
            settle(half_a, acc_a, bit)
            stage_ref[half_b, :] = acc_b
            return carry
        lax.fori_loop(first_bit, 32, pass_body, 0)
        settle(half_b, stage_ref[half_b, :], jnp.int32(1))
        return

    def cond(c):
        b, pending = c
        return jnp.logical_and(b < 32, pending > 0)

    def body(c):
        b, _ = c
        bit = lax.shift_left(jnp.int32(1), 31 - b)
        off = None
        for rb in range(rows_total // SEL_ROWS):
            rows = pl.ds(rb * SEL_ROWS, SEL_ROWS)
            cand = lo_ref[rows, :] + bit
            cnt = _row_total(_count_lanes(src_ref, rb * SEL_ROWS, n_steps, _key_to_f32(cand)))
            ok = cnt >= float(topk)
            if final_ref is not None:
                ok = jnp.logical_and(ok, final_ref[rows, :] <= 0.0)
            lo_ref[rows, :] = jnp.where(ok, cand, lo_ref[rows, :])
            cnt = jnp.where(ok, cnt, cnt_ref[rows, :])
            cnt_ref[rows, :] = cnt
            miss = jnp.abs(cnt - float(topk))
            if final_ref is not None:
                miss = jnp.where(final_ref[rows, :] > 0.0, 0.0, miss)
            off = miss if off is None else jnp.maximum(off, miss)
        return b + 1, (jnp.max(off) > 0.0).astype(jnp.int32)
    lax.while_loop(cond, body, (first_bit, jnp.int32(1)))


def _lane_candidates(keys_ref, cand_ref, n_groups):
    n_stack = CAND_DEPTH * CAND_SPLIT

    def rg_body(rg, carry):
        r0 = pl.multiple_of(rg * 8, 8)

        def col_body(s, st):
            st = list(st)
            c0 = pl.multiple_of(s * 8 * LANES, 8 * LANES)
            for u in range(8):
                x = keys_ref[pl.ds(r0, 8), pl.ds(c0 + u * LANES, LANES)]
                base = (u % CAND_SPLIT) * CAND_DEPTH
                for d in range(CAND_DEPTH):
                    cur = st[base + d]
                    st[base + d] = jnp.maximum(cur, x)
                    x = jnp.minimum(cur, x)
            return tuple(st)
        init = tuple(jnp.full((8, LANES), -jnp.inf, jnp.float32) for _ in range(n_stack))
        st = lax.fori_loop(0, n_groups, col_body, init)
        for k in range(n_stack):
            cand_ref[pl.ds(r0, 8), k * LANES:(k + 1) * LANES] = st[k]
        return carry
    lax.fori_loop(0, cand_ref.shape[0] // 8, rg_body, 0)


def _ties_to_keep(src_ref, n_steps, lo_ref, cnt_ref, part_ref, need_ref, topk):
    rows_total = lo_ref.shape[0]
    surplus = jnp.where(lo_ref[...] > NEG_INF_KEY, cnt_ref[...] - float(topk), 0.0)
    need_ref[...] = jnp.full(need_ref.shape, NO_TIE_LIMIT, jnp.float32)

    @pl.when(jnp.max(surplus) > 0.0)
    def _():
        for rb in range(rows_total // SEL_ROWS):
            rows = pl.ds(rb * SEL_ROWS, SEL_ROWS)
            thr = _key_to_f32(jnp.maximum(lo_ref[rows, :], NEG_INF_KEY))
            part_ref[rows, :] = _count_lanes(src_ref, rb * SEL_ROWS, n_steps, thr, strict=True)
        above = _row_total(part_ref[...])
        need_ref[...] = jnp.where(surplus > 0.0, float(topk) - above, NO_TIE_LIMIT)


def _tie_index_bound(keys_ref, thr_ref, need_ref, seen_ref, sig_ref, n_chunks):
    kk_i = lax.broadcasted_iota(jnp.int32, (LANES, LANES), 0)
    jj_i = lax.broadcasted_iota(jnp.int32, (LANES, LANES), 1)
    tri = _bf(jnp.where(kk_i <= jj_i, 1.0, 0.0))
    seen_ref[...] = jnp.zeros(seen_ref.shape, jnp.float32)

    def cond(c):
        j, pending = c
        return jnp.logical_and(j < n_chunks, pending > 0)

    def body(c):
        j, _ = c
        thr, need = thr_ref[...], need_ref[...]
        seen, sig = seen_ref[...], sig_ref[...]
        starts = [pl.multiple_of((j * TIE_CHUNK + u) * LANES, LANES) for u in range(TIE_CHUNK)]
        ties = [_bf(jnp.where(keys_ref[:, pl.ds(c0, LANES)] == thr, 1.0, 0.0)) for c0 in starts]
        prefixes = [_dot(t, tri) for t in ties]
        totals = [_row_total(t) for t in ties]
        seens = [seen]
        for u in range(TIE_CHUNK):
            seens.append(seens[u] + totals[u])
        befores = [_row_total(jnp.where(seens[u] + prefixes[u] < need, 1.0, 0.0))
                   for u in range(TIE_CHUNK)]
        for u, c0 in enumerate(starts):
            here = jnp.where(seens[u] < need, jnp.where(seens[u + 1] >= need, 1.0, 0.0), 0.0)
            sig = jnp.where(here > 0.0, c0.astype(jnp.float32) + befores[u], sig)
        seen = seens[TIE_CHUNK]
        seen_ref[...] = seen
        sig_ref[...] = sig
        waiting = jnp.where(need < NO_TIE_LIMIT, jnp.where(seen < need, 1.0, 0.0), 0.0)
        return j + 1, (jnp.max(waiting) > 0.0).astype(jnp.int32)
    lax.while_loop(cond, body, (jnp.int32(0), jnp.int32(1)))


def _dsa_kernel(qp_ref, iqp_ref, w_ref, g_ref, k_ref, va_ref, ik_ref, o_ref,
                keys_ref, cand_ref, lo_ref, thr_ref, cnt_ref, part_ref, need_ref, sig_ref,
                m_ref, acc_ref, full_ref, *, topk):
    QB, TK = DSA_QB, IDX_TK
    G = DSA_HEADS // DSA_KV_HEADS
    i = pl.program_id(0)
    n_idx = i + 1
    n_att = (n_idx * IDX_TK + ATT_TK - 1) // ATT_TK
    n_sel = n_att * (ATT_TK // SEL_LANES)
    row_id = lax.broadcasted_iota(jnp.int32, (QB, TK), 0)
    col_id = lax.broadcasted_iota(jnp.int32, (QB, TK), 1)

    iq = iqp_ref[...].reshape(IDX_HEADS * QB, LANES)
    wb = [jnp.broadcast_to(w_ref[:, h:h + 1], (QB, TK)) for h in range(IDX_HEADS)]

    def score_tile(j, diag):
        c0 = pl.multiple_of(j * TK, TK)
        s_all = _dot_nt(iq, ik_ref[pl.ds(c0, TK), :])
        score = None
        for h in range(IDX_HEADS):
            term = wb[h] * jnp.maximum(s_all[h * QB:(h + 1) * QB], 0.0)
            score = term if score is None else score + term
        if diag:
            score = jnp.where(col_id <= row_id, score, -jnp.inf)
        keys_ref[:, pl.ds(c0, TK)] = score

    def score_body(j, carry):
        score_tile(j, False)
        return carry

    def score_quad(q, carry):
        for u in range(4):
            score_tile(q * 4 + u, False)
        return carry
    n_quads = lax.shift_right_logical(i, 2)
    lax.fori_loop(0, n_quads, score_quad, 0)
    lax.fori_loop(n_quads * 4, i, score_body, 0)
    score_tile(i, True)

    def blank_body(j, carry):
        c0 = pl.multiple_of(j * TK, TK)
        keys_ref[:, pl.ds(c0, TK)] = jnp.full((QB, TK), -jnp.inf, jnp.float32)
        return carry
    n_vis = n_idx * TK
    windows = [w for w in (CAND_W, 2 * CAND_W) if w <= keys_ref.shape[1]]
    att_end = n_att * (ATT_TK // TK)
    blank_end = att_end
    for w in reversed(windows):
        blank_end = jnp.where(n_vis <= w, jnp.maximum(att_end, w // TK), blank_end)
    lax.fori_loop(n_idx, blank_end, blank_body, 0)

    full_ref[0] = jnp.int32(1)

    @pl.when(n_vis > CAND_W)
    def _():
        _lane_candidates(keys_ref, cand_ref, n_att * (ATT_TK // (8 * LANES)))
        first_bit = _search_window(cand_ref, lo_ref, part_ref)
        _radix_search(cand_ref, CAND_W // SEL_LANES, lo_ref, cnt_ref, topk,
                      first_bit=first_bit, final_ref=part_ref, stage_ref=need_ref)
        thr = _key_to_f32(jnp.maximum(lo_ref[...], NEG_INF_KEY))
        exact = cnt_ref[...] == float(topk)
        lost = None
        for v in range(CAND_SPLIT):
            last = cand_ref[:, (v * CAND_DEPTH + CAND_DEPTH - 1) * LANES:(v * CAND_DEPTH + CAND_DEPTH) * LANES]
            hit = jnp.where(last > thr, 1.0, jnp.where(last == thr, jnp.where(exact, 1.0, 0.0), 0.0))
            lost = hit if lost is None else jnp.maximum(lost, hit)
        full_ref[0] = (jnp.max(lost) > 0.0).astype(jnp.int32)
        _ties_to_keep(cand_ref, CAND_W // SEL_LANES, lo_ref, cnt_ref, part_ref, need_ref, topk)

    @pl.when(full_ref[0] > 0)
    def _():
        def search_rows(n_steps):
            _radix_search(keys_ref, n_steps, lo_ref, cnt_ref, topk, stage_ref=need_ref)
            _ties_to_keep(keys_ref, n_steps, lo_ref, cnt_ref, part_ref, need_ref, topk)
        below = 0
        for w in windows:
            pl.when(jnp.logical_and(n_vis > below, n_vis <= w))(
                functools.partial(search_rows, w // SEL_LANES))
            below = w
        pl.when(n_vis > below)(functools.partial(search_rows, n_sel))

    tau = lo_ref[...]
    found = tau > NEG_INF_KEY
    thr_ref[...] = jnp.where(found, _key_to_f32(jnp.maximum(tau, NEG_INF_KEY)), F32_LOWEST)
    sig_ref[...] = jnp.full(sig_ref.shape, NO_TIE_LIMIT, jnp.float32)

    @pl.when(jnp.min(need_ref[...]) < NO_TIE_LIMIT)
    def _():
        _tie_index_bound(keys_ref, thr_ref, need_ref, part_ref, sig_ref, n_att * (ATT_TK // (TIE_CHUNK * LANES)))

    m_ref[...] = jnp.full(m_ref.shape, NEG_INIT, jnp.float32)
    acc_ref[...] = jnp.zeros(acc_ref.shape, jnp.float32)
    n_grp = ATT_TK // LANES
    lane_f = lax.broadcasted_iota(jnp.int32, (QB, LANES), 1).astype(jnp.float32)

    def attn_tile(j):
        c0 = pl.multiple_of(j * ATT_TK, ATT_TK)
        tau_b = thr_ref[...]
        sig_rel = sig_ref[...] - c0.astype(jnp.float32)
        parts = []
        for u in range(n_grp):
            kk = keys_ref[:, pl.ds(c0 + u * LANES, LANES)]
            tie_bias = jnp.where(lane_f + float(u * LANES) <= sig_rel, 0.0, NEG_MASK)
            parts.append(jnp.where(kk > tau_b, 0.0, jnp.where(kk == tau_b, tie_bias, NEG_MASK)))
        bias = jnp.concatenate(parts, axis=1)
        kt = k_ref[pl.ds(c0, ATT_TK), :]
        all_logits = [_dot_nt(qp_ref[n * G:(n + 1) * G].reshape(G * QB, LANES), kt)
                      for n in range(DSA_KV_HEADS)]
        for n in range(DSA_KV_HEADS):
            logits = all_logits[n]
            va = va_ref[n, pl.ds(c0, ATT_TK), :]
            for gq in range(G):
                h = n * G + gq
                lm = logits[gq * QB:(gq + 1) * QB] + bias
                tmax = lm[:, 0:LANES]
                for u in range(1, n_grp):
                    tmax = jnp.maximum(tmax, lm[:, u * LANES:(u + 1) * LANES])
                m_old = m_ref[h]
                m_new = jnp.maximum(m_old, jnp.max(tmax, axis=-1, keepdims=True))
                p = jnp.concatenate(
                    [_bf(jnp.exp2(lm[:, u * LANES:(u + 1) * LANES] - m_new)) for u in range(n_grp)], axis=1)
                m_ref[h] = m_new
                acc_ref[h] = jnp.exp2(m_old - m_new) * acc_ref[h] + _dot(p, va)

    def attn_body(j, carry):
        attn_tile(j)
        return carry
    lax.fori_loop(0, n_att, attn_body, 0)

    lane = lax.broadcasted_iota(jnp.int32, (QB, LANES), 1)
    g = _f32(g_ref[...])
    for pair in range(DSA_HEADS // 2):
        halves = []
        for pos in range(2):
            a = acc_ref[2 * pair + pos]
            halves.append(a * (1.0 / pltpu.roll(a, DSA_DH, 1)))
        o = jnp.where(lane < DSA_DH, halves[0], pltpu.roll(halves[1], DSA_DH, 1))
        sl = slice(pair * LANES, (pair + 1) * LANES)
        o_ref[:, sl] = _bf(o * _silu(g[:, sl]))


def _dsa_call(p, qp, kk, va, iqp, ik, w4):
    s = p.shape[0]
    topk = min(TOPK_MAX, s // 4)
    QB = DSA_QB
    assert IDX_TK == QB and s % ATT_TK == 0 and ATT_TK % SEL_LANES == 0 and s // LANES <= 256
    once = pl.Buffered(1)
    return pl.pallas_call(
        functools.partial(_dsa_kernel, topk=topk),
        grid=(s // QB,),
        in_specs=[pl.BlockSpec((DSA_HEADS, QB, LANES), lambda i: (0, i, 0)),
                  pl.BlockSpec((IDX_HEADS, QB, LANES), lambda i: (0, i, 0)),
                  pl.BlockSpec((QB, LANES), lambda i: (i, 0)),
                  _pspec(QB, "dsa_g", 512),
                  pl.BlockSpec((s, LANES), lambda i: (0, 0), pipeline_mode=once),
                  pl.BlockSpec((DSA_KV_HEADS, s, LANES), lambda i: (0, 0, 0), pipeline_mode=once),
                  pl.BlockSpec((s, LANES), lambda i: (0, 0), pipeline_mode=once)],
        out_specs=pl.BlockSpec((QB, 512), lambda i: (i, 0)),
        out_shape=jax.ShapeDtypeStruct((s, 512), jnp.bfloat16),
        scratch_shapes=[pltpu.VMEM((QB, s), jnp.float32),
                        pltpu.VMEM((QB, CAND_W), jnp.float32),
                        pltpu.VMEM((QB, LANES), jnp.int32),
                        pltpu.VMEM((QB, LANES), jnp.float32),
                        pltpu.VMEM((QB, LANES), jnp.float32),
                        pltpu.VMEM((QB, LANES), jnp.float32),
                        pltpu.VMEM((QB, LANES), jnp.float32),
                        pltpu.VMEM((QB, LANES), jnp.float32),
                        pltpu.VMEM((DSA_HEADS, QB, LANES), jnp.float32),
                        pltpu.VMEM((DSA_HEADS, QB, LANES), jnp.float32),
                        pltpu.SMEM((1,), jnp.int32)],
        compiler_params=_cparams(1, vmem_mb=56),
        name="dsa",
    )(qp, iqp, w4, p, kk, va, ik)


def _merge_kernel(x_ref, ret_ref, dsa_ref, gl_ref, m_ref, wr_ref, wd_ref, wg_ref, wo_ref,
                  post_ref, gate_ref, o_ref):
    d = D_MODEL
    y = (_sigmoid(_f32(m_ref[:, 0:d])) * _dot(ret_ref[...], wr_ref[...])
         + _sigmoid(_f32(m_ref[:, d:2 * d])) * _dot(dsa_ref[...], wd_ref[...])
         + _sigmoid(_f32(m_ref[:, 2 * d:3 * d])) * _dot(gl_ref[...], wg_ref[...]))
    y = _dot(_bf(y), wo_ref[...])
    yn = y * lax.rsqrt(jnp.mean(y * y, axis=-1, keepdims=True) + RMS_EPS) * post_ref[...]
    o_ref[...] = x_ref[...] + gate_ref[...] * yn


def _merge_call(x2, ret, dsa, gl, p, wr, wd, wg, wo, post, gate):
    s, d = x2.shape
    tm = min(512, s)
    rows = lambda w: pl.BlockSpec((tm, w), lambda i: (i, 0))
    whole = lambda a: pl.BlockSpec(a.shape, lambda i: (0, 0))
    return pl.pallas_call(
        _merge_kernel,
        grid=(s // tm,),
        in_specs=[rows(d), rows(512), rows(512), rows(512), _pspec(tm, "merge", 3072),
                  whole(wr), whole(wd), whole(wg), whole(wo), whole(post), whole(gate)],
        out_specs=rows(d),
        out_shape=jax.ShapeDtypeStruct((s, d), jnp.float32),
        compiler_params=_cparams(1),
        name="merge_out",
    )(x2, ret, dsa, gl, p, wr, wd, wg, wo, post, gate)


def _pack_w_in(w_in):
    depth, d, _ = w_in.shape
    w_in = w_in.astype(jnp.bfloat16)
    zeros = lambda n: jnp.zeros((depth, d, n), w_in.dtype)
    src = lambda name: w_in[:, :, _SRC[name][0]:_SRC[name][0] + _SRC[name][1]]
    pieces, at = [], 0
    for name, width in _PACK:
        assert at == PCOL[name]
        if name == "idx_kw":
            cols = [src("idx_k"), src("idx_w"), zeros(width - IDX_DH - IDX_HEADS)]
        elif name == "gla_a":
            cols = [src("gla_a"), zeros(width - GLA_RANK)]
        else:
            cols = [src(name)]
        pieces += cols
        at += width
    pieces.append(zeros(P_WIDTH - at))
    return jnp.concatenate(pieces, axis=-1).astype(jnp.bfloat16)


def kernel(x, c, positions, ada_w, ada_b, pre_norm, post_norm, w_in, gla_w_lr, gla_b_lr,
           w_br_ret, w_br_dsa, w_br_gla, w_out):
    b, s, d = x.shape
    assert b == 1 and d == D_MODEL
    depth = ada_w.shape[0]
    x2 = x.reshape(s, d)
    mod = _mod_call(jnp.broadcast_to(c, (8, d)), ada_w, ada_b.reshape(depth, 1, 3 * d))[:, 0:1, :]
    rcos, rsin, dcos, dsin = _tab_call(positions.reshape(s, 1))
    w_pack = _pack_w_in(w_in)
    wlr_pad = jnp.pad(gla_w_lr, ((0, 0), (0, LANES - GLA_RANK), (0, 0)))
    for l in range(depth):
        shift, scale, gate = mod[l, :, 0:d], mod[l, :, d:2 * d], mod[l, :, 2 * d:3 * d]
        p = _proj_call(x2, pre_norm[l][None, :], scale, shift, w_pack[l])
        ret = _ret_call(p, rcos, rsin)
        gl = _gla_call(p, wlr_pad[l], gla_b_lr[l][None, :])
        qp, kk, va, iqp, ik, w4 = _dprep_call(p, dcos, dsin)
        dsa = _dsa_call(p, qp, kk, va, iqp, ik, w4)
        x2 = _merge_call(x2, ret, dsa, gl, p, _bf(w_br_ret[l]), _bf(w_br_dsa[l]), _bf(w_br_gla[l]),
                         _bf(w_out[l]), post_norm[l][None, :], gate)
    return x2.reshape(b, s, d)
```

```python
import functools
import math

import jax
import jax.numpy as jnp
from jax import lax
from jax.experimental import pallas as pl
from jax.experimental.pallas import tpu as pltpu

D_MODEL = 1024
DEPTH = 4
RET_HEADS, RET_DK, RET_DV, RET_CHUNK, RET_THETA = 4, 64, 128, 128, 10000.0
DSA_HEADS, DSA_KV_HEADS, DSA_DH = 8, 2, 64
DSA_ROT = DSA_DH // 4
ROPE_THETA = 500000.0
IDX_HEADS, IDX_DH = 4, 64
TOPK_MAX = 256
GLA_HEADS, GLA_DK, GLA_DV, GLA_RANK, GLA_TAU, GLA_CHUNK = 4, 64, 128, 16, 16.0, 64
RMS_EPS = 1e-6
LANES = 128

_SRC = {}
_off = 0
for _name, _w in (("ret_q", 256), ("ret_k", 256), ("ret_v", 512), ("ret_g", 512),
                  ("dsa_q", 512), ("dsa_k", 128), ("dsa_v", 128), ("dsa_g", 512),
                  ("idx_q", 256), ("idx_k", 64), ("idx_w", 4),
                  ("gla_q", 256), ("gla_k", 256), ("gla_v", 512), ("gla_g", 512), ("gla_a", 16),
                  ("merge", 3072)):
    _SRC[_name] = (_off, _w)
    _off += _w
IN_WIDTH = _off

_PACK = (("merge", 3072), ("ret_q", 256), ("ret_k", 256), ("ret_v", 512), ("ret_g", 512),
         ("dsa_q", 512), ("dsa_g", 512), ("gla_v", 512), ("gla_g", 512),
         ("gla_q", 256), ("gla_k", 256), ("idx_q", 256),
         ("dsa_k", 128), ("dsa_v", 128), ("idx_kw", 128), ("gla_a", 128))
PCOL = {}
_off = 0
for _name, _w in _PACK:
    assert _off % min(_w, 1024) == 0
    PCOL[_name] = _off
    _off += _w
P_WIDTH = 8192
assert _off <= P_WIDTH

LOG2E = math.log2(math.e)
INT_MIN = -(2 ** 31)
F32_LOWEST = -3.4028234663852886e38
NEG_INF_KEY = -0x7F800000
NEG_INIT = -1e30
NEG_MASK = -2e30


def _cparams(n_axes, vmem_mb=48):
    return pltpu.CompilerParams(dimension_semantics=("arbitrary",) * n_axes,
                                vmem_limit_bytes=vmem_mb * 1024 * 1024)


def _bf(x):
    return x.astype(jnp.bfloat16)


def _f32(x):
    return x.astype(jnp.float32)


def _dot(a, b):
    return jnp.dot(a, b, preferred_element_type=jnp.float32)


def _dot_nt(a, b):
    return lax.dot_general(a, b, (((1,), (1,)), ((), ())), preferred_element_type=jnp.float32)


def _dot_tn(a, b):
    return lax.dot_general(a, b, (((0,), (0,)), ((), ())), preferred_element_type=jnp.float32)


def _split3(x):
    hi = _bf(x)
    r1 = x - hi.astype(jnp.float32)
    mid = _bf(r1)
    lo = _bf(r1 - mid.astype(jnp.float32))
    return hi, mid, lo


def _silu(x):
    return x * (1.0 / (1.0 + jnp.exp(-x)))


def _sigmoid(x):
    return 1.0 / (1.0 + jnp.exp(-x))


def _mod_kernel(c_ref, w_ref, b_ref, o_ref):
    c = c_ref[...]
    ca = _silu(c)
    acc = None
    for t in _split3(ca):
        for u in _split3(w_ref[0]):
            part = _dot(t, u)
            acc = part if acc is None else acc + part
    o_ref[0] = acc + b_ref[0]


def _mod_call(c8, ada_w, ada_b3):
    depth, d, n = ada_w.shape
    tn = 1024
    return pl.pallas_call(
        _mod_kernel,
        grid=(depth, n // tn),
        in_specs=[pl.BlockSpec((8, d), lambda l, j: (0, 0)),
                  pl.BlockSpec((1, d, tn), lambda l, j: (l, 0, j)),
                  pl.BlockSpec((1, 1, tn), lambda l, j: (l, 0, j))],
        out_specs=pl.BlockSpec((1, 8, tn), lambda l, j: (l, 0, j)),
        out_shape=jax.ShapeDtypeStruct((depth, 8, n), jnp.float32),
        compiler_params=_cparams(2),
        name="adaln_mod",
    )(c8, ada_w, ada_b3)


def _tab_kernel(pos_ref, rf_ref, rs_ref, df_ref, ds_ref, rc_o, rsn_o, dc_o, dsn_o):
    pos = pos_ref[...].astype(jnp.float32)
    ang = pos * rf_ref[...]
    rc_o[...] = jnp.cos(ang)
    rsn_o[...] = jnp.sin(ang) * rs_ref[...]
    ang = pos * df_ref[...]
    dc_o[...] = jnp.cos(ang)
    dsn_o[...] = jnp.sin(ang) * ds_ref[...]


def _rope_rows():
    half = RET_DK // 2
    f = RET_THETA ** (-jnp.arange(half, dtype=jnp.float32) * 2.0 / RET_DK)
    rf = jnp.tile(jnp.concatenate([f, f]), RET_HEADS)[None, :]
    rs = jnp.tile(jnp.concatenate([-jnp.ones(half), jnp.ones(half)]), RET_HEADS)[None, :].astype(jnp.float32)
    half = DSA_ROT // 2
    f = ROPE_THETA ** (-jnp.arange(half, dtype=jnp.float32) * 2.0 / DSA_ROT)
    z = jnp.zeros(DSA_DH - DSA_ROT, jnp.float32)
    df = jnp.tile(jnp.concatenate([f, f, z]), 2)[None, :]
    ds = jnp.tile(jnp.concatenate([-jnp.ones(half), jnp.ones(half), z]), 2)[None, :].astype(jnp.float32)
    return rf, rs, df, ds


def _tab_call(pos_col):
    s = pos_col.shape[0]
    tm = min(1024, s)
    rf, rs, df, ds = _rope_rows()
    row = lambda w: pl.BlockSpec((1, w), lambda i: (0, 0))
    out = lambda w: pl.BlockSpec((tm, w), lambda i: (i, 0))
    return pl.pallas_call(
        _tab_kernel,
        grid=(s // tm,),
        in_specs=[pl.BlockSpec((tm, 1), lambda i: (i, 0)), row(256), row(256), row(128), row(128)],
        out_specs=[out(256), out(256), out(128), out(128)],
        out_shape=[jax.ShapeDtypeStruct((s, 256), jnp.float32), jax.ShapeDtypeStruct((s, 256), jnp.float32),
                   jax.ShapeDtypeStruct((s, 128), jnp.float32), jax.ShapeDtypeStruct((s, 128), jnp.float32)],
        compiler_params=_cparams(1),
        name="rope_tables",
    )(pos_col, rf, rs, df, ds)


def _swap_halves(x, half, period):
    w = x.shape[-1]
    lane = lax.broadcasted_iota(jnp.int32, x.shape, x.ndim - 1) & (period - 1)
    up = pltpu.roll(x, w - half, x.ndim - 1)
    dn = pltpu.roll(x, half, x.ndim - 1)
    return jnp.where(lane < half, up, dn)


def _rope(x, cos, sin_signed, half, period):
    return x * cos + _swap_halves(x, half, period) * sin_signed


def _proj_kernel(x_ref, pre_ref, sc_ref, sh_ref, w_ref, o_ref):
    x = x_ref[...]
    xn = x * lax.rsqrt(jnp.mean(x * x, axis=-1, keepdims=True) + RMS_EPS)
    h = xn * pre_ref[...] * (1.0 + sc_ref[...]) + sh_ref[...]
    o_ref[...] = _bf(_dot(_bf(h), w_ref[...]))


def _proj_call(x2, pre, scale, shift, w_pack):
    s, d = x2.shape
    tm, tn = min(512, s), 2048
    vec = pl.BlockSpec((1, d), lambda j, i: (0, 0))
    return pl.pallas_call(
        _proj_kernel,
        grid=(P_WIDTH // tn, s // tm),
        in_specs=[pl.BlockSpec((tm, d), lambda j, i: (i, 0)), vec, vec, vec,
                  pl.BlockSpec((d, tn), lambda j, i: (0, j))],
        out_specs=pl.BlockSpec((tm, tn), lambda j, i: (i, j)),
        out_shape=jax.ShapeDtypeStruct((s, P_WIDTH), jnp.bfloat16),
        compiler_params=_cparams(2),
        name="in_proj",
    )(x2, pre, scale, shift, w_pack)


def _pspec(tm, name, width):
    blk = PCOL[name] // width
    assert PCOL[name] % width == 0
    return pl.BlockSpec((tm, width), lambda i: (i, blk))


def _ret_log_g(h):
    return math.log1p(-(2.0 ** (-5.0 - h)))


def _ret_kernel(q_ref, k_ref, v_ref, g_ref, cos_ref, sin_ref, o_ref,
                state_ref, decay_ref, qdec_ref, kend_ref, *, chunks):
    C = RET_CHUNK

    @pl.when(pl.program_id(0) == 0)
    def _():
        state_ref[...] = jnp.zeros_like(state_ref)
        ii = lax.broadcasted_iota(jnp.int32, (C, C), 0)
        jj = lax.broadcasted_iota(jnp.int32, (C, C), 1)
        rel = (ii - jj).astype(jnp.float32)
        row = lax.broadcasted_iota(jnp.int32, (C, RET_DK), 0).astype(jnp.float32)
        for h in range(RET_HEADS):
            lg = _ret_log_g(h)
            decay_ref[h] = jnp.where(rel >= 0, jnp.exp(lg * jnp.maximum(rel, 0.0)), 0.0)
            qdec_ref[:, h * RET_DK:(h + 1) * RET_DK] = jnp.exp((row + 1.0) * lg)
            kend_ref[:, h * RET_DK:(h + 1) * RET_DK] = jnp.exp((C - 1.0 - row) * lg)

    heads = [(slice(h * RET_DK, (h + 1) * RET_DK), slice(h * RET_DV, (h + 1) * RET_DV))
             for h in range(RET_HEADS)]
    chunk_rows = [slice(c * C, (c + 1) * C) for c in range(chunks)]
    qbs, kbs, qds, kds = [], [], [], []
    for rows in chunk_rows:
        cos, sin = cos_ref[rows, :], sin_ref[rows, :]
        q = _rope(_f32(q_ref[rows, :]), cos, sin, RET_DK // 2, RET_DK) * (RET_DK ** -0.5)
        k = _rope(_f32(k_ref[rows, :]), cos, sin, RET_DK // 2, RET_DK)
        qbs.append(_bf(q))
        kbs.append(_bf(k))
        qds.append(_bf(q * qdec_ref[...]))
        kds.append(_bf(k * kend_ref[...]))
    scores = [[_bf(_dot_nt(qbs[c][:, dk], kbs[c][:, dk]) * decay_ref[h]) for h, (dk, _) in enumerate(heads)]
              for c in range(chunks)]
    intra = [[_dot(scores[c][h], v_ref[chunk_rows[c], heads[h][1]]) for h in range(RET_HEADS)]
             for c in range(chunks)]
    kvs = [[_dot_tn(kds[c][:, heads[h][0]], v_ref[chunk_rows[c], heads[h][1]]) for h in range(RET_HEADS)]
           for c in range(chunks)]
    states = [state_ref[h] for h in range(RET_HEADS)]
    for c, rows in enumerate(chunk_rows):
        g = _f32(g_ref[rows, :])
        for h, (dk, dv) in enumerate(heads):
            o = intra[c][h] + _dot(qds[c][:, dk], _bf(states[h]))
            states[h] = math.exp(C * _ret_log_g(h)) * states[h] + kvs[c][h]
            o = o * lax.rsqrt(jnp.mean(o * o, axis=-1, keepdims=True) + RMS_EPS)
            o_ref[rows, dv] = _bf(o * _silu(g[:, dv]))
    for h in range(RET_HEADS):
        state_ref[h] = states[h]


def _ret_call(p, rcos, rsin):
    s = p.shape[0]
    chunks = 4
    tm = RET_CHUNK * chunks
    tab = pl.BlockSpec((tm, 256), lambda i: (i, 0))
    return pl.pallas_call(
        functools.partial(_ret_kernel, chunks=chunks),
        grid=(s // tm,),
        in_specs=[_pspec(tm, "ret_q", 256), _pspec(tm, "ret_k", 256), _pspec(tm, "ret_v", 512),
                  _pspec(tm, "ret_g", 512), tab, tab],
        out_specs=pl.BlockSpec((tm, 512), lambda i: (i, 0)),
        out_shape=jax.ShapeDtypeStruct((s, 512), jnp.bfloat16),
        scratch_shapes=[pltpu.VMEM((RET_HEADS, RET_DK, RET_DV), jnp.float32),
                        pltpu.VMEM((RET_HEADS, RET_CHUNK, RET_CHUNK), jnp.float32),
                        pltpu.VMEM((RET_CHUNK, RET_HEADS * RET_DK), jnp.float32),
                        pltpu.VMEM((RET_CHUNK, RET_HEADS * RET_DK), jnp.float32)],
        compiler_params=_cparams(1),
        name="retention",
    )(p, p, p, p, rcos, rsin)


def _gla_kernel(q_ref, k_ref, v_ref, g_ref, a_ref, wlr_ref, blr_ref, o_ref, state_ref, *, chunks):
    C = GLA_CHUNK

    @pl.when(pl.program_id(0) == 0)
    def _():
        state_ref[...] = jnp.zeros_like(state_ref)

    ii = lax.broadcasted_iota(jnp.int32, (C, C), 0)
    jj = lax.broadcasted_iota(jnp.int32, (C, C), 1)
    levels = [1 << b for b in range(C.bit_length() - 2, -1, -1)]
    parent = lambda x, s: lax.shift_right_logical(x, s.bit_length())
    level_masks = [jnp.logical_and(parent(ii, s) == parent(jj, s), (ii & s) > (jj & s)) for s in levels]
    diagonal = ii == jj
    sums = [jnp.where(jj <= ii, 1.0, 0.0)]
    sums += [jnp.where(jj <= lax.shift_left(parent(ii, s), s.bit_length()) + (s - 1), 1.0, 0.0) for s in levels]
    cum_rows = _bf(jnp.concatenate(sums, axis=0))
    wlr = wlr_ref[...]
    w_hi, w_mid, w_lo = _split3(wlr)

    a_hi, a_mid, a_lo = _split3(_f32(a_ref[...]))
    z = (_dot(a_hi, w_hi) + (_dot(a_hi, w_mid) + _dot(a_mid, w_hi))
         + (_dot(a_hi, w_lo) + _dot(a_mid, w_mid) + _dot(a_lo, w_hi))) + blr_ref[...]
    log_a_all = (jnp.minimum(z, 0.0) - jnp.log1p(jnp.exp(-jnp.abs(z)))) * (1.0 / GLA_TAU)

    heads = [(slice(h * GLA_DK, (h + 1) * GLA_DK), slice(h * GLA_DV, (h + 1) * GLA_DV))
             for h in range(GLA_HEADS)]
    chunk_rows = [slice(c * C, (c + 1) * C) for c in range(chunks)]
    sums_all = []
    for rows in chunk_rows:
        l_hi, l_mid, l_lo = _split3(log_a_all[rows, :])
        sums_all.append(_dot(cum_rows, l_hi) + _dot(cum_rows, l_mid) + _dot(cum_rows, l_lo))
    qls, kls, qgs, kds, e_lasts = [], [], [], [], []
    for rows, stacked in zip(chunk_rows, sums_all):
        bcum = stacked[0:C, :]
        b_last = bcum[C - 1:C, :]
        q = _f32(q_ref[rows, :]) * (GLA_DK ** -0.5)
        k = _f32(k_ref[rows, :])
        q_lv, k_lv = [_bf(q)], [_bf(k)]
        for n in range(len(levels)):
            base = stacked[(n + 1) * C:(n + 2) * C, :]
            q_lv.append(_bf(q * jnp.exp(jnp.minimum(bcum - base, 0.0))))
            k_lv.append(_bf(k * jnp.exp(jnp.minimum(base - bcum, 0.0))))
        qls.append(q_lv)
        kls.append(k_lv)
        qgs.append(_bf(q * jnp.exp(bcum)))
        kds.append(_bf(k * jnp.exp(b_last - bcum)))
        e_lasts.append(jnp.exp(b_last))
    pair_masks = [diagonal] + level_masks

    def chunk_attn(c, dk):
        total = None
        for mask, q_l, k_l in zip(pair_masks, qls[c], kls[c]):
            part = jnp.where(mask, _dot_nt(q_l[:, dk], k_l[:, dk]), 0.0)
            total = part if total is None else total + part
        return _bf(total)
    attns = [[chunk_attn(c, dk) for dk, _ in heads] for c in range(chunks)]
    intra = [[_dot(attns[c][h], v_ref[chunk_rows[c], heads[h][1]]) for h in range(GLA_HEADS)]
             for c in range(chunks)]
    kvs = [[_dot_tn(v_ref[chunk_rows[c], heads[h][1]], kds[c][:, heads[h][0]]) for h in range(GLA_HEADS)]
           for c in range(chunks)]
    states = [state_ref[h] for h in range(GLA_HEADS)]
    for c, rows in enumerate(chunk_rows):
        g = _f32(g_ref[rows, :])
        for h, (dk, dv) in enumerate(heads):
            o = intra[c][h] + _dot_nt(qgs[c][:, dk], _bf(states[h]))
            states[h] = e_lasts[c][:, dk] * states[h] + kvs[c][h]
            o = o * lax.rsqrt(jnp.mean(o * o, axis=-1, keepdims=True) + RMS_EPS)
            o_ref[rows, dv] = _bf(o * _silu(g[:, dv]))
    for h in range(GLA_HEADS):
        state_ref[h] = states[h]


def _gla_call(p, wlr_pad, blr):
    s = p.shape[0]
    chunks = 8
    tm = GLA_CHUNK * chunks
    return pl.pallas_call(
        functools.partial(_gla_kernel, chunks=chunks),
        grid=(s // tm,),
        in_specs=[_pspec(tm, "gla_q", 256), _pspec(tm, "gla_k", 256), _pspec(tm, "gla_v", 512),
                  _pspec(tm, "gla_g", 512), _pspec(tm, "gla_a", 128),
                  pl.BlockSpec((128, 256), lambda i: (0, 0)), pl.BlockSpec((1, 256), lambda i: (0, 0))],
        out_specs=pl.BlockSpec((tm, 512), lambda i: (i, 0)),
        out_shape=jax.ShapeDtypeStruct((s, 512), jnp.bfloat16),
        scratch_shapes=[pltpu.VMEM((GLA_HEADS, GLA_DV, GLA_DK), jnp.float32)],
        compiler_params=_cparams(1),
        name="gla",
    )(p, p, p, p, p, wlr_pad, blr)


def _place_head(pair, src_pos, dst_pos):
    lane = lax.broadcasted_iota(jnp.int32, pair.shape, 1)
    if src_pos != dst_pos:
        pair = pltpu.roll(pair, 64, 1)
    keep = (lane < 64) if dst_pos == 0 else (lane >= 64)
    return jnp.where(keep, pair, 0.0)


def _dprep_kernel(q_ref, k_ref, v_ref, iq_ref, ikw_ref, cos_ref, sin_ref,
                  qp_o, k_o, v_o, iqp_o, ik_o, w_o):
    cos, sin = cos_ref[...], sin_ref[...]
    half = DSA_ROT // 2
    cos4 = jnp.concatenate([cos] * 4, axis=1)
    sin4 = jnp.concatenate([sin] * 4, axis=1)
    q = _rope(_f32(q_ref[...]), cos4, sin4, half, DSA_DH) * (DSA_DH ** -0.5 * LOG2E)
    for h in range(DSA_HEADS):
        pair = q[:, (h // 2) * 128:(h // 2 + 1) * 128]
        qp_o[h] = _bf(_place_head(pair, h % 2, h // (DSA_HEADS // DSA_KV_HEADS)))
    k_o[...] = _bf(_rope(_f32(k_ref[...]), cos, sin, half, DSA_DH))
    v = _f32(v_ref[...])
    lane = lax.broadcasted_iota(jnp.int32, v.shape, 1)
    v_o[0] = _bf(jnp.where(lane < DSA_DH, v, 1.0))
    v_o[1] = _bf(jnp.where(lane < DSA_DH, pltpu.roll(v, DSA_DH, 1), 1.0))
    iq = _rope(_f32(iq_ref[...]), cos4[:, :256], sin4[:, :256], half, IDX_DH)
    for h in range(IDX_HEADS):
        pair = iq[:, (h // 2) * 128:(h // 2 + 1) * 128]
        iqp_o[h] = _bf(_place_head(pair, h % 2, 0))
    ikw = _f32(ikw_ref[...])
    lane = lax.broadcasted_iota(jnp.int32, ikw.shape, 1)
    ik_o[...] = _bf(jnp.where(lane < IDX_DH, _rope(ikw, cos, sin, half, DSA_DH), 0.0))
    wscale = (IDX_HEADS ** -0.5) * (IDX_DH ** -0.5)
    w_o[...] = pltpu.roll(ikw, 128 - IDX_DH, 1) * wscale


def _dprep_call(p, dcos, dsin):
    s = p.shape[0]
    tm = min(512, s)
    tab = pl.BlockSpec((tm, 128), lambda i: (i, 0))
    return pl.pallas_call(
        _dprep_kernel,
        grid=(s // tm,),
        in_specs=[_pspec(tm, "dsa_q", 512), _pspec(tm, "dsa_k", 128), _pspec(tm, "dsa_v", 128),
                  _pspec(tm, "idx_q", 256), _pspec(tm, "idx_kw", 128), tab, tab],
        out_specs=[pl.BlockSpec((DSA_HEADS, tm, 128), lambda i: (0, i, 0)), tab,
                   pl.BlockSpec((DSA_KV_HEADS, tm, 128), lambda i: (0, i, 0)),
                   pl.BlockSpec((IDX_HEADS, tm, 128), lambda i: (0, i, 0)), tab, tab],
        out_shape=[jax.ShapeDtypeStruct((DSA_HEADS, s, 128), jnp.bfloat16),
                   jax.ShapeDtypeStruct((s, 128), jnp.bfloat16),
                   jax.ShapeDtypeStruct((DSA_KV_HEADS, s, 128), jnp.bfloat16),
                   jax.ShapeDtypeStruct((IDX_HEADS, s, 128), jnp.bfloat16),
                   jax.ShapeDtypeStruct((s, 128), jnp.bfloat16),
                   jax.ShapeDtypeStruct((s, 128), jnp.float32)],
        compiler_params=_cparams(1),
        name="dsa_prep",
    )(p, p, p, p, p, dcos, dsin)


DSA_QB = 256
IDX_TK = 256
ATT_TK = 1024
SEL_ROWS = 128
SEL_LANES = 512
CAND_DEPTH = 10
CAND_SPLIT = 2
CAND_W = CAND_DEPTH * CAND_SPLIT * LANES
TIE_CHUNK = 8
NO_TIE_LIMIT = 1e9
assert CAND_W % SEL_LANES == 0 and 8 % CAND_SPLIT == 0


def _key_to_f32(key):
    bits = jnp.where(key >= 0, key, (0 - key) | INT_MIN)
    return lax.bitcast_convert_type(bits, jnp.float32)


def _count_lanes(keys_ref, r0, n_steps, cand, strict=False):
    def body(s, acc):
        c0 = s * SEL_LANES if isinstance(s, int) else pl.multiple_of(s * SEL_LANES, SEL_LANES)
        for u in range(SEL_LANES // LANES):
            kk = keys_ref[pl.ds(r0, SEL_ROWS), pl.ds(c0 + u * LANES, LANES)]
            acc = acc + jnp.where((kk > cand) if strict else (kk >= cand), 1.0, 0.0)
        return acc
    acc = jnp.zeros((SEL_ROWS, LANES), jnp.float32)
    if isinstance(n_steps, int):
        for s in range(n_steps):
            acc = body(s, acc)
        return acc
    return lax.fori_loop(0, n_steps, body, acc)


def _row_total(lane_counts):
    ones = jnp.ones((LANES, LANES), jnp.bfloat16)
    return _dot(_bf(lane_counts), ones)


def _f32_to_key(x):
    bits = lax.bitcast_convert_type(x, jnp.int32)
    return jnp.where(bits >= 0, bits, INT_MIN - bits)


def _search_window(cand_ref, lo_ref, final_ref):
    hi = lw = None
    for v in range(CAND_SPLIT):
        top = cand_ref[:, v * CAND_DEPTH * LANES:(v * CAND_DEPTH + 1) * LANES]
        hi = top if hi is None else jnp.maximum(hi, top)
        lw = top if lw is None else jnp.minimum(lw, top)
    k_hi = _f32_to_key(jnp.max(hi, axis=-1, keepdims=True))
    k_lw = _f32_to_key(jnp.min(lw, axis=-1, keepdims=True))
    nbits = 32 - lax.clz(k_hi ^ k_lw)
    nb = jnp.max(nbits.astype(jnp.float32)).astype(jnp.int32)
    low = lax.shift_left(jnp.int32(1), jnp.minimum(nb, 31)) - 1
    lo0 = jnp.where(nbits == 0, k_lw, jnp.where(nb >= 32, INT_MIN, k_lw & ~low))
    lo_ref[...] = jnp.broadcast_to(lo0, lo_ref.shape)
    final_ref[...] = jnp.broadcast_to(jnp.where(nbits == 0, 1.0, 0.0), final_ref.shape)
    return 32 - nb


def _radix_search(src_ref, n_steps, lo_ref, cnt_ref, topk, first_bit=None, final_ref=None, stage_ref=None):
    rows_total = lo_ref.shape[0]
    if first_bit is None:
        first_bit = jnp.int32(0)
        lo_ref[...] = jnp.full(lo_ref.shape, INT_MIN, jnp.int32)
        cnt_ref[...] = jnp.zeros(cnt_ref.shape, jnp.float32)
    else:
        cnt_ref[...] = jnp.full(cnt_ref.shape, NO_TIE_LIMIT, jnp.float32)

    if isinstance(n_steps, int):
        assert stage_ref is not None and rows_total == 2 * SEL_ROWS
        half_a, half_b = pl.ds(0, SEL_ROWS), pl.ds(SEL_ROWS, SEL_ROWS)

        def settle(rows, lane_counts, bit):
            cnt = _row_total(lane_counts)
            lo = lo_ref[rows, :]
            ok = cnt >= float(topk)
            if final_ref is not None:
                ok = jnp.logical_and(ok, final_ref[rows, :] <= 0.0)
            lo_ref[rows, :] = jnp.where(ok, lo + bit, lo)
            cnt_ref[rows, :] = jnp.where(ok, cnt, cnt_ref[rows, :])

        stage_ref[half_b, :] = jnp.zeros((SEL_ROWS, LANES), jnp.float32)

        def pass_body(b, carry):
            bit = lax.shift_left(jnp.int32(1), 31 - b)
            prev_bit = lax.shift_left(jnp.int32(1), jnp.minimum(32 - b, 31))
            staged = stage_ref[half_b, :]
            acc_a = _count_lanes(src_ref, 0, n_steps, _key_to_f32(lo_ref[half_a, :] + bit))
            settle(half_b, staged, prev_bit)
            acc_b = _count_lanes(src_ref, SEL_ROWS, n_steps, _key_to_f32(lo_ref[half_b, :] + bit))
            settle(half_a, acc_a, bit)
            stage_ref[half_b, :] = acc_b
            return carry
        lax.fori_loop(first_bit, 32, pass_body, 0)
        settle(half_b, stage_ref[half_b, :], jnp.int32(1))
        return

    def cond(c):
        b, pending = c
        return jnp.logical_and(b < 32, pending > 0)

    def body(c):
        b, _ = c
        bit = lax.shift_left(jnp.int32(1), 31 - b)
        off = None
        for rb in range(rows_total // SEL_ROWS):
            rows = pl.ds(rb * SEL_ROWS, SEL_ROWS)
            cand = lo_ref[rows, :] + bit
            cnt = _row_total(_count_lanes(src_ref, rb * SEL_ROWS, n_steps, _key_to_f32(cand)))
            ok = cnt >= float(topk)
            if final_ref is not None:
                ok = jnp.logical_and(ok, final_ref[rows, :] <= 0.0)
            lo_ref[rows, :] = jnp.where(ok, cand, lo_ref[rows, :])
            cnt = jnp.where(ok, cnt, cnt_ref[rows, :])
            cnt_ref[rows, :] = cnt
            miss = jnp.abs(cnt - float(topk))
            if final_ref is not None:
                miss = jnp.where(final_ref[rows, :] > 0.0, 0.0, miss)
            off = miss if off is None else jnp.maximum(off, miss)
        return b + 1, (jnp.max(off) > 0.0).astype(jnp.int32)
    lax.while_loop(cond, body, (first_bit, jnp.int32(1)))


def _lane_candidates(keys_ref, cand_ref, n_groups):
    n_stack = CAND_DEPTH * CAND_SPLIT

    def rg_body(rg, carry):
        r0 = pl.multiple_of(rg * 8, 8)

        def insert(st, c0, n_vregs):
            st = list(st)
            for u in range(n_vregs):
                x = keys_ref[pl.ds(r0, 8), pl.ds(c0 + u * LANES, LANES)]
                base = (u % CAND_SPLIT) * CAND_DEPTH
                for d in range(CAND_DEPTH):
                    cur = st[base + d]
                    st[base + d] = jnp.maximum(cur, x)
                    x = jnp.minimum(cur, x)
            return tuple(st)

        def two_groups(s, st):
            return insert(st, pl.multiple_of(s * 16 * LANES, 16 * LANES), 16)

        def one_group(s, st):
            return insert(st, pl.multiple_of(s * 8 * LANES, 8 * LANES), 8)
        init = tuple(jnp.full((8, LANES), -jnp.inf, jnp.float32) for _ in range(n_stack))
        n_pairs = lax.shift_right_logical(n_groups, 1)
        st = lax.fori_loop(0, n_pairs, two_groups, init)
        st = lax.fori_loop(2 * n_pairs, n_groups, one_group, st)
        for k in range(n_stack):
            cand_ref[pl.ds(r0, 8), k * LANES:(k + 1) * LANES] = st[k]
        return carry
    lax.fori_loop(0, cand_ref.shape[0] // 8, rg_body, 0)


def _ties_to_keep(src_ref, n_steps, lo_ref, cnt_ref, part_ref, need_ref, topk):
    rows_total = lo_ref.shape[0]
    surplus = jnp.where(lo_ref[...] > NEG_INF_KEY, cnt_ref[...] - float(topk), 0.0)
    need_ref[...] = jnp.full(need_ref.shape, NO_TIE_LIMIT, jnp.float32)

    @pl.when(jnp.max(surplus) > 0.0)
    def _():
        for rb in range(rows_total // SEL_ROWS):
            rows = pl.ds(rb * SEL_ROWS, SEL_ROWS)
            thr = _key_to_f32(jnp.maximum(lo_ref[rows, :], NEG_INF_KEY))
            part_ref[rows, :] = _count_lanes(src_ref, rb * SEL_ROWS, n_steps, thr, strict=True)
        above = _row_total(part_ref[...])
        need_ref[...] = jnp.where(surplus > 0.0, float(topk) - above, NO_TIE_LIMIT)


def _tie_index_bound(keys_ref, thr_ref, need_ref, seen_ref, sig_ref, n_chunks):
    kk_i = lax.broadcasted_iota(jnp.int32, (LANES, LANES), 0)
    jj_i = lax.broadcasted_iota(jnp.int32, (LANES, LANES), 1)
    tri = _bf(jnp.where(kk_i <= jj_i, 1.0, 0.0))
    seen_ref[...] = jnp.zeros(seen_ref.shape, jnp.float32)

    def cond(c):
        j, pending = c
        return jnp.logical_and(j < n_chunks, pending > 0)

    def body(c):
        j, _ = c
        thr, need = thr_ref[...], need_ref[...]
        seen, sig = seen_ref[...], sig_ref[...]
        starts = [pl.multiple_of((j * TIE_CHUNK + u) * LANES, LANES) for u in range(TIE_CHUNK)]
        ties = [_bf(jnp.where(keys_ref[:, pl.ds(c0, LANES)] == thr, 1.0, 0.0)) for c0 in starts]
        prefixes = [_dot(t, tri) for t in ties]
        totals = [_row_total(t) for t in ties]
        seens = [seen]
        for u in range(TIE_CHUNK):
            seens.append(seens[u] + totals[u])
        befores = [_row_total(jnp.where(seens[u] + prefixes[u] < need, 1.0, 0.0))
                   for u in range(TIE_CHUNK)]
        for u, c0 in enumerate(starts):
            here = jnp.where(seens[u] < need, jnp.where(seens[u + 1] >= need, 1.0, 0.0), 0.0)
            sig = jnp.where(here > 0.0, c0.astype(jnp.float32) + befores[u], sig)
        seen = seens[TIE_CHUNK]
        seen_ref[...] = seen
        sig_ref[...] = sig
        waiting = jnp.where(need < NO_TIE_LIMIT, jnp.where(seen < need, 1.0, 0.0), 0.0)
        return j + 1, (jnp.max(waiting) > 0.0).astype(jnp.int32)
    lax.while_loop(cond, body, (jnp.int32(0), jnp.int32(1)))


def _dsa_kernel(qp_ref, iqp_ref, w_ref, g_ref, k_ref, va_ref, ik_ref, o_ref,
                keys_ref, cand_ref, lo_ref, thr_ref, cnt_ref, part_ref, need_ref, sig_ref,
                m_ref, acc_ref, full_ref, *, topk):
    QB, TK = DSA_QB, IDX_TK
    G = DSA_HEADS // DSA_KV_HEADS
    i = pl.program_id(0)
    n_idx = i + 1
    n_att = (n_idx * IDX_TK + ATT_TK - 1) // ATT_TK
    n_sel = n_att * (ATT_TK // SEL_LANES)
    row_id = lax.broadcasted_iota(jnp.int32, (QB, TK), 0)
    col_id = lax.broadcasted_iota(jnp.int32, (QB, TK), 1)

    iq = iqp_ref[...].reshape(IDX_HEADS * QB, LANES)
    wb = [jnp.broadcast_to(w_ref[:, h:h + 1], (QB, TK)) for h in range(IDX_HEADS)]

    def score_tile(j, diag):
        c0 = pl.multiple_of(j * TK, TK)
        s_all = _dot_nt(iq, ik_ref[pl.ds(c0, TK), :])
        score = None
        for h in range(IDX_HEADS):
            term = wb[h] * jnp.maximum(s_all[h * QB:(h + 1) * QB], 0.0)
            score = term if score is None else score + term
        if diag:
            score = jnp.where(col_id <= row_id, score, -jnp.inf)
        keys_ref[:, pl.ds(c0, TK)] = score

    def score_body(j, carry):
        score_tile(j, False)
        return carry

    def score_quad(q, carry):
        for u in range(4):
            score_tile(q * 4 + u, False)
        return carry
    n_quads = lax.shift_right_logical(i, 2)
    lax.fori_loop(0, n_quads, score_quad, 0)
    lax.fori_loop(n_quads * 4, i, score_body, 0)
    score_tile(i, True)

    def blank_body(j, carry):
        c0 = pl.multiple_of(j * TK, TK)
        keys_ref[:, pl.ds(c0, TK)] = jnp.full((QB, TK), -jnp.inf, jnp.float32)
        return carry
    n_vis = n_idx * TK
    windows = [w for w in (CAND_W, 2 * CAND_W) if w <= keys_ref.shape[1]]
    att_end = n_att * (ATT_TK // TK)
    blank_end = att_end
    for w in reversed(windows):
        blank_end = jnp.where(n_vis <= w, jnp.maximum(att_end, w // TK), blank_end)
    lax.fori_loop(n_idx, blank_end, blank_body, 0)

    full_ref[0] = jnp.int32(1)

    @pl.when(n_vis > CAND_W)
    def _():
        _lane_candidates(keys_ref, cand_ref, n_att * (ATT_TK // (8 * LANES)))
        first_bit = _search_window(cand_ref, lo_ref, part_ref)
        _radix_search(cand_ref, CAND_W // SEL_LANES, lo_ref, cnt_ref, topk,
                      first_bit=first_bit, final_ref=part_ref, stage_ref=need_ref)
        thr = _key_to_f32(jnp.maximum(lo_ref[...], NEG_INF_KEY))
        exact = cnt_ref[...] == float(topk)
        lost = None
        for v in range(CAND_SPLIT):
            last = cand_ref[:, (v * CAND_DEPTH + CAND_DEPTH - 1) * LANES:(v * CAND_DEPTH + CAND_DEPTH) * LANES]
            hit = jnp.where(last > thr, 1.0, jnp.where(last == thr, jnp.where(exact, 1.0, 0.0), 0.0))
            lost = hit if lost is None else jnp.maximum(lost, hit)
        full_ref[0] = (jnp.max(lost) > 0.0).astype(jnp.int32)
        _ties_to_keep(cand_ref, CAND_W // SEL_LANES, lo_ref, cnt_ref, part_ref, need_ref, topk)

    @pl.when(full_ref[0] > 0)
    def _():
        def search_rows(n_steps):
            _radix_search(keys_ref, n_steps, lo_ref, cnt_ref, topk, stage_ref=need_ref)
            _ties_to_keep(keys_ref, n_steps, lo_ref, cnt_ref, part_ref, need_ref, topk)
        below = 0
        for w in windows:
            pl.when(jnp.logical_and(n_vis > below, n_vis <= w))(
                functools.partial(search_rows, w // SEL_LANES))
            below = w
        pl.when(n_vis > below)(functools.partial(search_rows, n_sel))

    tau = lo_ref[...]
    found = tau > NEG_INF_KEY
    thr_ref[...] = jnp.where(found, _key_to_f32(jnp.maximum(tau, NEG_INF_KEY)), F32_LOWEST)
    sig_ref[...] = jnp.full(sig_ref.shape, NO_TIE_LIMIT, jnp.float32)

    @pl.when(jnp.min(need_ref[...]) < NO_TIE_LIMIT)
    def _():
        _tie_index_bound(keys_ref, thr_ref, need_ref, part_ref, sig_ref, n_att * (ATT_TK // (TIE_CHUNK * LANES)))

    m_ref[...] = jnp.full(m_ref.shape, NEG_INIT, jnp.float32)
    acc_ref[...] = jnp.zeros(acc_ref.shape, jnp.float32)
    n_grp = ATT_TK // LANES
    lane_f = lax.broadcasted_iota(jnp.int32, (QB, LANES), 1).astype(jnp.float32)

    def attn_tile(j):
        c0 = pl.multiple_of(j * ATT_TK, ATT_TK)
        tau_b = thr_ref[...]
        sig_rel = sig_ref[...] - c0.astype(jnp.float32)
        parts = []
        for u in range(n_grp):
            kk = keys_ref[:, pl.ds(c0 + u * LANES, LANES)]
            tie_bias = jnp.where(lane_f + float(u * LANES) <= sig_rel, 0.0, NEG_MASK)
            parts.append(jnp.where(kk > tau_b, 0.0, jnp.where(kk == tau_b, tie_bias, NEG_MASK)))
        bias = jnp.concatenate(parts, axis=1)
        kt = k_ref[pl.ds(c0, ATT_TK), :]
        all_logits = [_dot_nt(qp_ref[n * G:(n + 1) * G].reshape(G * QB, LANES), kt)
                      for n in range(DSA_KV_HEADS)]
        for n in range(DSA_KV_HEADS):
            logits = all_logits[n]
            va = va_ref[n, pl.ds(c0, ATT_TK), :]
            for gq in range(G):
                h = n * G + gq
                lm = logits[gq * QB:(gq + 1) * QB] + bias
                tmax = lm[:, 0:LANES]
                for u in range(1, n_grp):
                    tmax = jnp.maximum(tmax, lm[:, u * LANES:(u + 1) * LANES])
                m_old = m_ref[h]
                m_new = jnp.maximum(m_old, jnp.max(tmax, axis=-1, keepdims=True))
                p = jnp.concatenate(
                    [_bf(jnp.exp2(lm[:, u * LANES:(u + 1) * LANES] - m_new)) for u in range(n_grp)], axis=1)
                m_ref[h] = m_new
                acc_ref[h] = jnp.exp2(m_old - m_new) * acc_ref[h] + _dot(p, va)

    def attn_body(j, carry):
        attn_tile(j)
        return carry
    lax.fori_loop(0, n_att, attn_body, 0)

    lane = lax.broadcasted_iota(jnp.int32, (QB, LANES), 1)
    g = _f32(g_ref[...])
    for pair in range(DSA_HEADS // 2):
        halves = []
        for pos in range(2):
            a = acc_ref[2 * pair + pos]
            halves.append(a * (1.0 / pltpu.roll(a, DSA_DH, 1)))
        o = jnp.where(lane < DSA_DH, halves[0], pltpu.roll(halves[1], DSA_DH, 1))
        sl = slice(pair * LANES, (pair + 1) * LANES)
        o_ref[:, sl] = _bf(o * _silu(g[:, sl]))


def _dsa_call(p, qp, kk, va, iqp, ik, w4):
    s = p.shape[0]
    topk = min(TOPK_MAX, s // 4)
    QB = DSA_QB
    assert IDX_TK == QB and s % ATT_TK == 0 and ATT_TK % SEL_LANES == 0 and s // LANES <= 256
    once = pl.Buffered(1)
    return pl.pallas_call(
        functools.partial(_dsa_kernel, topk=topk),
        grid=(s // QB,),
        in_specs=[pl.BlockSpec((DSA_HEADS, QB, LANES), lambda i: (0, i, 0)),
                  pl.BlockSpec((IDX_HEADS, QB, LANES), lambda i: (0, i, 0)),
                  pl.BlockSpec((QB, LANES), lambda i: (i, 0)),
                  _pspec(QB, "dsa_g", 512),
                  pl.BlockSpec((s, LANES), lambda i: (0, 0), pipeline_mode=once),
                  pl.BlockSpec((DSA_KV_HEADS, s, LANES), lambda i: (0, 0, 0), pipeline_mode=once),
                  pl.BlockSpec((s, LANES), lambda i: (0, 0), pipeline_mode=once)],
        out_specs=pl.BlockSpec((QB, 512), lambda i: (i, 0)),
        out_shape=jax.ShapeDtypeStruct((s, 512), jnp.bfloat16),
        scratch_shapes=[pltpu.VMEM((QB, s), jnp.float32),
                        pltpu.VMEM((QB, CAND_W), jnp.float32),
                        pltpu.VMEM((QB, LANES), jnp.int32),
                        pltpu.VMEM((QB, LANES), jnp.float32),
                        pltpu.VMEM((QB, LANES), jnp.float32),
                        pltpu.VMEM((QB, LANES), jnp.float32),
                        pltpu.VMEM((QB, LANES), jnp.float32),
                        pltpu.VMEM((QB, LANES), jnp.float32),
                        pltpu.VMEM((DSA_HEADS, QB, LANES), jnp.float32),
                        pltpu.VMEM((DSA_HEADS, QB, LANES), jnp.float32),
                        pltpu.SMEM((1,), jnp.int32)],
        compiler_params=_cparams(1, vmem_mb=56),
        name="dsa",
    )(qp, iqp, w4, p, kk, va, ik)


def _merge_kernel(x_ref, ret_ref, dsa_ref, gl_ref, m_ref, wr_ref, wd_ref, wg_ref, wo_ref,
                  post_ref, gate_ref, o_ref):
    d = D_MODEL
    y = (_sigmoid(_f32(m_ref[:, 0:d])) * _dot(ret_ref[...], wr_ref[...])
         + _sigmoid(_f32(m_ref[:, d:2 * d])) * _dot(dsa_ref[...], wd_ref[...])
         + _sigmoid(_f32(m_ref[:, 2 * d:3 * d])) * _dot(gl_ref[...], wg_ref[...]))
    y = _dot(_bf(y), wo_ref[...])
    yn = y * lax.rsqrt(jnp.mean(y * y, axis=-1, keepdims=True) + RMS_EPS) * post_ref[...]
    o_ref[...] = x_ref[...] + gate_ref[...] * yn


def _merge_call(x2, ret, dsa, gl, p, wr, wd, wg, wo, post, gate):
    s, d = x2.shape
    tm = min(512, s)
    rows = lambda w: pl.BlockSpec((tm, w), lambda i: (i, 0))
    whole = lambda a: pl.BlockSpec(a.shape, lambda i: (0, 0))
    return pl.pallas_call(
        _merge_kernel,
        grid=(s // tm,),
        in_specs=[rows(d), rows(512), rows(512), rows(512), _pspec(tm, "merge", 3072),
                  whole(wr), whole(wd), whole(wg), whole(wo), whole(post), whole(gate)],
        out_specs=rows(d),
        out_shape=jax.ShapeDtypeStruct((s, d), jnp.float32),
        compiler_params=_cparams(1),
        name="merge_out",
    )(x2, ret, dsa, gl, p, wr, wd, wg, wo, post, gate)


def _pack_w_in(w_in):
    depth, d, _ = w_in.shape
    w_in = w_in.astype(jnp.bfloat16)
    zeros = lambda n: jnp.zeros((depth, d, n), w_in.dtype)
    src = lambda name: w_in[:, :, _SRC[name][0]:_SRC[name][0] + _SRC[name][1]]
    pieces, at = [], 0
    for name, width in _PACK:
        assert at == PCOL[name]
        if name == "idx_kw":
            cols = [src("idx_k"), src("idx_w"), zeros(width - IDX_DH - IDX_HEADS)]
        elif name == "gla_a":
            cols = [src("gla_a"), zeros(width - GLA_RANK)]
        else:
            cols = [src(name)]
        pieces += cols
        at += width
    pieces.append(zeros(P_WIDTH - at))
    return jnp.concatenate(pieces, axis=-1).astype(jnp.bfloat16)


def kernel(x, c, positions, ada_w, ada_b, pre_norm, post_norm, w_in, gla_w_lr, gla_b_lr,
           w_br_ret, w_br_dsa, w_br_gla, w_out):
    b, s, d = x.shape
    assert b == 1 and d == D_MODEL
    depth = ada_w.shape[0]
    x2 = x.reshape(s, d)
    mod = _mod_call(jnp.broadcast_to(c, (8, d)), ada_w, ada_b.reshape(depth, 1, 3 * d))[:, 0:1, :]
    rcos, rsin, dcos, dsin = _tab_call(positions.reshape(s, 1))
    w_pack = _pack_w_in(w_in)
    wlr_pad = jnp.pad(gla_w_lr, ((0, 0), (0, LANES - GLA_RANK), (0, 0)))
    for l in range(depth):
        shift, scale, gate = mod[l, :, 0:d], mod[l, :, d:2 * d], mod[l, :, 2 * d:3 * d]
        p = _proj_call(x2, pre_norm[l][None, :], scale, shift, w_pack[l])
        ret = _ret_call(p, rcos, rsin)
        gl = _gla_call(p, wlr_pad[l], gla_b_lr[l][None, :])
        qp, kk, va, iqp, ik, w4 = _dprep_call(p, dcos, dsin)
        dsa = _dsa_call(p, qp, kk, va, iqp, ik, w4)
        x2 = _merge_call(x2, ret, dsa, gl, p, _bf(w_br_ret[l]), _bf(w_br_dsa[l]), _bf(w_br_gla[l]),
                         _bf(w_out[l]), post_norm[l][None, :], gate)
    return x2.reshape(b, s, d)
```

```python
import functools
import math

import jax
import jax.numpy as jnp
from jax import lax
from jax.experimental import pallas as pl
from jax.experimental.pallas import tpu as pltpu

D_MODEL = 1024
DEPTH = 4
RET_HEADS, RET_DK, RET_DV, RET_CHUNK, RET_THETA = 4, 64, 128, 128, 10000.0
DSA_HEADS, DSA_KV_HEADS, DSA_DH = 8, 2, 64
DSA_ROT = DSA_DH // 4
ROPE_THETA = 500000.0
IDX_HEADS, IDX_DH = 4, 64
TOPK_MAX = 256
GLA_HEADS, GLA_DK, GLA_DV, GLA_RANK, GLA_TAU, GLA_CHUNK = 4, 64, 128, 16, 16.0, 64
RMS_EPS = 1e-6
LANES = 128

_SRC = {}
_off = 0
for _name, _w in (("ret_q", 256), ("ret_k", 256), ("ret_v", 512), ("ret_g", 512),
                  ("dsa_q", 512), ("dsa_k", 128), ("dsa_v", 128), ("dsa_g", 512),
                  ("idx_q", 256), ("idx_k", 64), ("idx_w", 4),
                  ("gla_q", 256), ("gla_k", 256), ("gla_v", 512), ("gla_g", 512), ("gla_a", 16),
                  ("merge", 3072)):
    _SRC[_name] = (_off, _w)
    _off += _w
IN_WIDTH = _off

_PACK = (("merge", 3072), ("ret_q", 256), ("ret_k", 256), ("ret_v", 512), ("ret_g", 512),
         ("dsa_q", 512), ("dsa_g", 512), ("gla_v", 512), ("gla_g", 512),
         ("gla_q", 256), ("gla_k", 256), ("idx_q", 256),
         ("dsa_k", 128), ("dsa_v", 128), ("idx_kw", 128), ("gla_a", 128))
PCOL = {}
_off = 0
for _name, _w in _PACK:
    assert _off % min(_w, 1024) == 0
    PCOL[_name] = _off
    _off += _w
P_WIDTH = 8192
assert _off <= P_WIDTH

LOG2E = math.log2(math.e)
INT_MIN = -(2 ** 31)
F32_LOWEST = -3.4028234663852886e38
NEG_INF_KEY = -0x7F800000
NEG_INIT = -1e30
NEG_MASK = -2e30


def _cparams(n_axes, vmem_mb=48):
    return pltpu.CompilerParams(dimension_semantics=("arbitrary",) * n_axes,
                                vmem_limit_bytes=vmem_mb * 1024 * 1024)


def _bf(x):
    return x.astype(jnp.bfloat16)


def _f32(x):
    return x.astype(jnp.float32)


def _dot(a, b):
    return jnp.dot(a, b, preferred_element_type=jnp.float32)


def _dot_nt(a, b):
    return lax.dot_general(a, b, (((1,), (1,)), ((), ())), preferred_element_type=jnp.float32)


def _dot_tn(a, b):
    return lax.dot_general(a, b, (((0,), (0,)), ((), ())), preferred_element_type=jnp.float32)


def _split3(x):
    hi = _bf(x)
    r1 = x - hi.astype(jnp.float32)
    mid = _bf(r1)
    lo = _bf(r1 - mid.astype(jnp.float32))
    return hi, mid, lo


def _silu(x):
    return x * (1.0 / (1.0 + jnp.exp(-x)))


def _sigmoid(x):
    return 1.0 / (1.0 + jnp.exp(-x))


def _mod_kernel(c_ref, w_ref, b_ref, o_ref):
    c = c_ref[...]
    ca = _silu(c)
    acc = None
    for t in _split3(ca):
        for u in _split3(w_ref[0]):
            part = _dot(t, u)
            acc = part if acc is None else acc + part
    o_ref[0] = acc + b_ref[0]


def _mod_call(c8, ada_w, ada_b3):
    depth, d, n = ada_w.shape
    tn = 1024
    return pl.pallas_call(
        _mod_kernel,
        grid=(depth, n // tn),
        in_specs=[pl.BlockSpec((8, d), lambda l, j: (0, 0)),
                  pl.BlockSpec((1, d, tn), lambda l, j: (l, 0, j)),
                  pl.BlockSpec((1, 1, tn), lambda l, j: (l, 0, j))],
        out_specs=pl.BlockSpec((1, 8, tn), lambda l, j: (l, 0, j)),
        out_shape=jax.ShapeDtypeStruct((depth, 8, n), jnp.float32),
        compiler_params=_cparams(2),
        name="adaln_mod",
    )(c8, ada_w, ada_b3)


def _tab_kernel(pos_ref, rf_ref, rs_ref, df_ref, ds_ref, rc_o, rsn_o, dc_o, dsn_o):
    pos = pos_ref[...].astype(jnp.float32)
    ang = pos * rf_ref[...]
    rc_o[...] = jnp.cos(ang)
    rsn_o[...] = jnp.sin(ang) * rs_ref[...]
    ang = pos * df_ref[...]
    dc_o[...] = jnp.cos(ang)
    dsn_o[...] = jnp.sin(ang) * ds_ref[...]


def _rope_rows():
    half = RET_DK // 2
    f = RET_THETA ** (-jnp.arange(half, dtype=jnp.float32) * 2.0 / RET_DK)
    rf = jnp.tile(jnp.concatenate([f, f]), RET_HEADS)[None, :]
    rs = jnp.tile(jnp.concatenate([-jnp.ones(half), jnp.ones(half)]), RET_HEADS)[None, :].astype(jnp.float32)
    half = DSA_ROT // 2
    f = ROPE_THETA ** (-jnp.arange(half, dtype=jnp.float32) * 2.0 / DSA_ROT)
    z = jnp.zeros(DSA_DH - DSA_ROT, jnp.float32)
    df = jnp.tile(jnp.concatenate([f, f, z]), 2)[None, :]
    ds = jnp.tile(jnp.concatenate([-jnp.ones(half), jnp.ones(half), z]), 2)[None, :].astype(jnp.float32)
    return rf, rs, df, ds


def _tab_call(pos_col):
    s = pos_col.shape[0]
    tm = min(1024, s)
    rf, rs, df, ds = _rope_rows()
    row = lambda w: pl.BlockSpec((1, w), lambda i: (0, 0))
    out = lambda w: pl.BlockSpec((tm, w), lambda i: (i, 0))
    return pl.pallas_call(
        _tab_kernel,
        grid=(s // tm,),
        in_specs=[pl.BlockSpec((tm, 1), lambda i: (i, 0)), row(256), row(256), row(128), row(128)],
        out_specs=[out(256), out(256), out(128), out(128)],
        out_shape=[jax.ShapeDtypeStruct((s, 256), jnp.float32), jax.ShapeDtypeStruct((s, 256), jnp.float32),
                   jax.ShapeDtypeStruct((s, 128), jnp.float32), jax.ShapeDtypeStruct((s, 128), jnp.float32)],
        compiler_params=_cparams(1),
        name="rope_tables",
    )(pos_col, rf, rs, df, ds)


def _swap_halves(x, half, period):
    w = x.shape[-1]
    lane = lax.broadcasted_iota(jnp.int32, x.shape, x.ndim - 1) & (period - 1)
    up = pltpu.roll(x, w - half, x.ndim - 1)
    dn = pltpu.roll(x, half, x.ndim - 1)
    return jnp.where(lane < half, up, dn)


def _rope(x, cos, sin_signed, half, period):
    return x * cos + _swap_halves(x, half, period) * sin_signed


def _proj_kernel(x_ref, pre_ref, sc_ref, sh_ref, w_ref, o_ref):
    x = x_ref[...]
    xn = x * lax.rsqrt(jnp.mean(x * x, axis=-1, keepdims=True) + RMS_EPS)
    h = xn * pre_ref[...] * (1.0 + sc_ref[...]) + sh_ref[...]
    o_ref[...] = _bf(_dot(_bf(h), w_ref[...]))


def _proj_call(x2, pre, scale, shift, w_pack):
    s, d = x2.shape
    tm, tn = min(512, s), 2048
    vec = pl.BlockSpec((1, d), lambda j, i: (0, 0))
    return pl.pallas_call(
        _proj_kernel,
        grid=(P_WIDTH // tn, s // tm),
        in_specs=[pl.BlockSpec((tm, d), lambda j, i: (i, 0)), vec, vec, vec,
                  pl.BlockSpec((d, tn), lambda j, i: (0, j))],
        out_specs=pl.BlockSpec((tm, tn), lambda j, i: (i, j)),
        out_shape=jax.ShapeDtypeStruct((s, P_WIDTH), jnp.bfloat16),
        compiler_params=_cparams(2),
        name="in_proj",
    )(x2, pre, scale, shift, w_pack)


def _pspec(tm, name, width):
    blk = PCOL[name] // width
    assert PCOL[name] % width == 0
    return pl.BlockSpec((tm, width), lambda i: (i, blk))


def _ret_log_g(h):
    return math.log1p(-(2.0 ** (-5.0 - h)))


def _ret_kernel(q_ref, k_ref, v_ref, g_ref, cos_ref, sin_ref, o_ref,
                state_ref, decay_ref, qdec_ref, kend_ref, *, chunks):
    C = RET_CHUNK

    @pl.when(pl.program_id(0) == 0)
    def _():
        state_ref[...] = jnp.zeros_like(state_ref)
        ii = lax.broadcasted_iota(jnp.int32, (C, C), 0)
        jj = lax.broadcasted_iota(jnp.int32, (C, C), 1)
        rel = (ii - jj).astype(jnp.float32)
        row = lax.broadcasted_iota(jnp.int32, (C, RET_DK), 0).astype(jnp.float32)
        for h in range(RET_HEADS):
            lg = _ret_log_g(h)
            decay_ref[h] = jnp.where(rel >= 0, jnp.exp(lg * jnp.maximum(rel, 0.0)), 0.0)
            qdec_ref[:, h * RET_DK:(h + 1) * RET_DK] = jnp.exp((row + 1.0) * lg)
            kend_ref[:, h * RET_DK:(h + 1) * RET_DK] = jnp.exp((C - 1.0 - row) * lg)

    heads = [(slice(h * RET_DK, (h + 1) * RET_DK), slice(h * RET_DV, (h + 1) * RET_DV))
             for h in range(RET_HEADS)]
    chunk_rows = [slice(c * C, (c + 1) * C) for c in range(chunks)]
    qbs, kbs, qds, kds = [], [], [], []
    for rows in chunk_rows:
        cos, sin = cos_ref[rows, :], sin_ref[rows, :]
        q = _rope(_f32(q_ref[rows, :]), cos, sin, RET_DK // 2, RET_DK) * (RET_DK ** -0.5)
        k = _rope(_f32(k_ref[rows, :]), cos, sin, RET_DK // 2, RET_DK)
        qbs.append(_bf(q))
        kbs.append(_bf(k))
        qds.append(_bf(q * qdec_ref[...]))
        kds.append(_bf(k * kend_ref[...]))
    scores = [[_bf(_dot_nt(qbs[c][:, dk], kbs[c][:, dk]) * decay_ref[h]) for h, (dk, _) in enumerate(heads)]
              for c in range(chunks)]
    intra = [[_dot(scores[c][h], v_ref[chunk_rows[c], heads[h][1]]) for h in range(RET_HEADS)]
             for c in range(chunks)]
    kvs = [[_dot_tn(kds[c][:, heads[h][0]], v_ref[chunk_rows[c], heads[h][1]]) for h in range(RET_HEADS)]
           for c in range(chunks)]
    states = [state_ref[h] for h in range(RET_HEADS)]
    for c, rows in enumerate(chunk_rows):
        g = _f32(g_ref[rows, :])
        for h, (dk, dv) in enumerate(heads):
            o = intra[c][h] + _dot(qds[c][:, dk], _bf(states[h]))
            states[h] = math.exp(C * _ret_log_g(h)) * states[h] + kvs[c][h]
            o = o * lax.rsqrt(jnp.mean(o * o, axis=-1, keepdims=True) + RMS_EPS)
            o_ref[rows, dv] = _bf(o * _silu(g[:, dv]))
    for h in range(RET_HEADS):
        state_ref[h] = states[h]


def _ret_call(p, rcos, rsin):
    s = p.shape[0]
    chunks = 4
    tm = RET_CHUNK * chunks
    tab = pl.BlockSpec((tm, 256), lambda i: (i, 0))
    return pl.pallas_call(
        functools.partial(_ret_kernel, chunks=chunks),
        grid=(s // tm,),
        in_specs=[_pspec(tm, "ret_q", 256), _pspec(tm, "ret_k", 256), _pspec(tm, "ret_v", 512),
                  _pspec(tm, "ret_g", 512), tab, tab],
        out_specs=pl.BlockSpec((tm, 512), lambda i: (i, 0)),
        out_shape=jax.ShapeDtypeStruct((s, 512), jnp.bfloat16),
        scratch_shapes=[pltpu.VMEM((RET_HEADS, RET_DK, RET_DV), jnp.float32),
                        pltpu.VMEM((RET_HEADS, RET_CHUNK, RET_CHUNK), jnp.float32),
                        pltpu.VMEM((RET_CHUNK, RET_HEADS * RET_DK), jnp.float32),
                        pltpu.VMEM((RET_CHUNK, RET_HEADS * RET_DK), jnp.float32)],
        compiler_params=_cparams(1),
        name="retention",
    )(p, p, p, p, rcos, rsin)


def _gla_kernel(q_ref, k_ref, v_ref, g_ref, a_ref, wlr_ref, blr_ref, o_ref, state_ref, *, chunks):
    C = GLA_CHUNK

    @pl.when(pl.program_id(0) == 0)
    def _():
        state_ref[...] = jnp.zeros_like(state_ref)

    ii = lax.broadcasted_iota(jnp.int32, (C, C), 0)
    jj = lax.broadcasted_iota(jnp.int32, (C, C), 1)
    levels = [1 << b for b in range(C.bit_length() - 2, -1, -1)]
    parent = lambda x, s: lax.shift_right_logical(x, s.bit_length())
    level_masks = [jnp.logical_and(parent(ii, s) == parent(jj, s), (ii & s) > (jj & s)) for s in levels]
    diagonal = ii == jj
    sums = [jnp.where(jj <= ii, 1.0, 0.0)]
    sums += [jnp.where(jj <= lax.shift_left(parent(ii, s), s.bit_length()) + (s - 1), 1.0, 0.0) for s in levels]
    cum_rows = _bf(jnp.concatenate(sums, axis=0))
    wlr = wlr_ref[...]
    w_hi, w_mid, w_lo = _split3(wlr)

    a_hi, a_mid, a_lo = _split3(_f32(a_ref[...]))
    z = (_dot(a_hi, w_hi) + (_dot(a_hi, w_mid) + _dot(a_mid, w_hi))
         + (_dot(a_hi, w_lo) + _dot(a_mid, w_mid) + _dot(a_lo, w_hi))) + blr_ref[...]
    log_a_all = (jnp.minimum(z, 0.0) - jnp.log1p(jnp.exp(-jnp.abs(z)))) * (1.0 / GLA_TAU)

    heads = [(slice(h * GLA_DK, (h + 1) * GLA_DK), slice(h * GLA_DV, (h + 1) * GLA_DV))
             for h in range(GLA_HEADS)]
    chunk_rows = [slice(c * C, (c + 1) * C) for c in range(chunks)]
    sums_all = []
    for rows in chunk_rows:
        l_hi, l_mid, l_lo = _split3(log_a_all[rows, :])
        sums_all.append(_dot(cum_rows, l_hi) + _dot(cum_rows, l_mid) + _dot(cum_rows, l_lo))
    qls, kls, qgs, kds, e_lasts = [], [], [], [], []
    for rows, stacked in zip(chunk_rows, sums_all):
        bcum = stacked[0:C, :]
        b_last = bcum[C - 1:C, :]
        q = _f32(q_ref[rows, :]) * (GLA_DK ** -0.5)
        k = _f32(k_ref[rows, :])
        q_lv, k_lv = [_bf(q)], [_bf(k)]
        for n in range(len(levels)):
            base = stacked[(n + 1) * C:(n + 2) * C, :]
            q_lv.append(_bf(q * jnp.exp(jnp.minimum(bcum - base, 0.0))))
            k_lv.append(_bf(k * jnp.exp(jnp.minimum(base - bcum, 0.0))))
        qls.append(q_lv)
        kls.append(k_lv)
        qgs.append(_bf(q * jnp.exp(bcum)))
        kds.append(_bf(k * jnp.exp(b_last - bcum)))
        e_lasts.append(jnp.exp(b_last))
    pair_masks = [diagonal] + level_masks

    def chunk_attn(c, dk):
        total = None
        for mask, q_l, k_l in zip(pair_masks, qls[c], kls[c]):
            part = jnp.where(mask, _dot_nt(q_l[:, dk], k_l[:, dk]), 0.0)
            total = part if total is None else total + part
        return _bf(total)
    attns = [[chunk_attn(c, dk) for dk, _ in heads] for c in range(chunks)]
    intra = [[_dot(attns[c][h], v_ref[chunk_rows[c], heads[h][1]]) for h in range(GLA_HEADS)]
             for c in range(chunks)]
    kvs = [[_dot_tn(v_ref[chunk_rows[c], heads[h][1]], kds[c][:, heads[h][0]]) for h in range(GLA_HEADS)]
           for c in range(chunks)]
    states = [state_ref[h] for h in range(GLA_HEADS)]
    for c, rows in enumerate(chunk_rows):
        g = _f32(g_ref[rows, :])
        for h, (dk, dv) in enumerate(heads):
            o = intra[c][h] + _dot_nt(qgs[c][:, dk], _bf(states[h]))
            states[h] = e_lasts[c][:, dk] * states[h] + kvs[c][h]
            o = o * lax.rsqrt(jnp.mean(o * o, axis=-1, keepdims=True) + RMS_EPS)
            o_ref[rows, dv] = _bf(o * _silu(g[:, dv]))
    for h in range(GLA_HEADS):
        state_ref[h] = states[h]


def _gla_call(p, wlr_pad, blr):
    s = p.shape[0]
    chunks = 8
    tm = GLA_CHUNK * chunks
    return pl.pallas_call(
        functools.partial(_gla_kernel, chunks=chunks),
        grid=(s // tm,),
        in_specs=[_pspec(tm, "gla_q", 256), _pspec(tm, "gla_k", 256), _pspec(tm, "gla_v", 512),
                  _pspec(tm, "gla_g", 512), _pspec(tm, "gla_a", 128),
                  pl.BlockSpec((128, 256), lambda i: (0, 0)), pl.BlockSpec((1, 256), lambda i: (0, 0))],
        out_specs=pl.BlockSpec((tm, 512), lambda i: (i, 0)),
        out_shape=jax.ShapeDtypeStruct((s, 512), jnp.bfloat16),
        scratch_shapes=[pltpu.VMEM((GLA_HEADS, GLA_DV, GLA_DK), jnp.float32)],
        compiler_params=_cparams(1),
        name="gla",
    )(p, p, p, p, p, wlr_pad, blr)


def _place_head(pair, src_pos, dst_pos):
    lane = lax.broadcasted_iota(jnp.int32, pair.shape, 1)
    if src_pos != dst_pos:
        pair = pltpu.roll(pair, 64, 1)
    keep = (lane < 64) if dst_pos == 0 else (lane >= 64)
    return jnp.where(keep, pair, 0.0)


def _dprep_kernel(q_ref, k_ref, v_ref, iq_ref, ikw_ref, cos_ref, sin_ref,
                  qp_o, k_o, v_o, iqp_o, ik_o, w_o):
    cos, sin = cos_ref[...], sin_ref[...]
    half = DSA_ROT // 2
    cos4 = jnp.concatenate([cos] * 4, axis=1)
    sin4 = jnp.concatenate([sin] * 4, axis=1)
    q = _rope(_f32(q_ref[...]), cos4, sin4, half, DSA_DH) * (DSA_DH ** -0.5 * LOG2E)
    for h in range(DSA_HEADS):
        pair = q[:, (h // 2) * 128:(h // 2 + 1) * 128]
        qp_o[h] = _bf(_place_head(pair, h % 2, h // (DSA_HEADS // DSA_KV_HEADS)))
    k_o[...] = _bf(_rope(_f32(k_ref[...]), cos, sin, half, DSA_DH))
    v = _f32(v_ref[...])
    lane = lax.broadcasted_iota(jnp.int32, v.shape, 1)
    v_o[0] = _bf(jnp.where(lane < DSA_DH, v, 1.0))
    v_o[1] = _bf(jnp.where(lane < DSA_DH, pltpu.roll(v, DSA_DH, 1), 1.0))
    iq = _rope(_f32(iq_ref[...]), cos4[:, :256], sin4[:, :256], half, IDX_DH)
    for h in range(IDX_HEADS):
        pair = iq[:, (h // 2) * 128:(h // 2 + 1) * 128]
        iqp_o[h] = _bf(_place_head(pair, h % 2, 0))
    ikw = _f32(ikw_ref[...])
    lane = lax.broadcasted_iota(jnp.int32, ikw.shape, 1)
    ik_o[...] = _bf(jnp.where(lane < IDX_DH, _rope(ikw, cos, sin, half, DSA_DH), 0.0))
    wscale = (IDX_HEADS ** -0.5) * (IDX_DH ** -0.5)
    w_o[...] = pltpu.roll(ikw, 128 - IDX_DH, 1) * wscale


def _dprep_call(p, dcos, dsin):
    s = p.shape[0]
    tm = min(512, s)
    tab = pl.BlockSpec((tm, 128), lambda i: (i, 0))
    return pl.pallas_call(
        _dprep_kernel,
        grid=(s // tm,),
        in_specs=[_pspec(tm, "dsa_q", 512), _pspec(tm, "dsa_k", 128), _pspec(tm, "dsa_v", 128),
                  _pspec(tm, "idx_q", 256), _pspec(tm, "idx_kw", 128), tab, tab],
        out_specs=[pl.BlockSpec((DSA_HEADS, tm, 128), lambda i: (0, i, 0)), tab,
                   pl.BlockSpec((DSA_KV_HEADS, tm, 128), lambda i: (0, i, 0)),
                   pl.BlockSpec((IDX_HEADS, tm, 128), lambda i: (0, i, 0)), tab, tab],
        out_shape=[jax.ShapeDtypeStruct((DSA_HEADS, s, 128), jnp.bfloat16),
                   jax.ShapeDtypeStruct((s, 128), jnp.bfloat16),
                   jax.ShapeDtypeStruct((DSA_KV_HEADS, s, 128), jnp.bfloat16),
                   jax.ShapeDtypeStruct((IDX_HEADS, s, 128), jnp.bfloat16),
                   jax.ShapeDtypeStruct((s, 128), jnp.bfloat16),
                   jax.ShapeDtypeStruct((s, 128), jnp.float32)],
        compiler_params=_cparams(1),
        name="dsa_prep",
    )(p, p, p, p, p, dcos, dsin)


DSA_QB = 256
IDX_TK = 256
ATT_TK = 1024
SEL_ROWS = 128
SEL_LANES = 512
CAND_DEPTH = 10
CAND_SPLIT = 2
CAND_W = CAND_DEPTH * CAND_SPLIT * LANES
TIE_CHUNK = 8
NO_TIE_LIMIT = 1e9
assert CAND_W % SEL_LANES == 0 and 8 % CAND_SPLIT == 0


def _key_to_f32(key):
    bits = jnp.where(key >= 0, key, (0 - key) | INT_MIN)
    return lax.bitcast_convert_type(bits, jnp.float32)


def _count_lanes(keys_ref, r0, n_steps, cand, strict=False):
    def body(s, acc):
        c0 = s * SEL_LANES if isinstance(s, int) else pl.multiple_of(s * SEL_LANES, SEL_LANES)
        for u in range(SEL_LANES // LANES):
            kk = keys_ref[pl.ds(r0, SEL_ROWS), pl.ds(c0 + u * LANES, LANES)]
            acc = acc + jnp.where((kk > cand) if strict else (kk >= cand), 1.0, 0.0)
        return acc
    acc = jnp.zeros((SEL_ROWS, LANES), jnp.float32)
    if isinstance(n_steps, int):
        for s in range(n_steps):
            acc = body(s, acc)
        return acc
    return lax.fori_loop(0, n_steps, body, acc)


def _row_total(lane_counts):
    ones = jnp.ones((LANES, LANES), jnp.bfloat16)
    return _dot(_bf(lane_counts), ones)


def _f32_to_key(x):
    bits = lax.bitcast_convert_type(x, jnp.int32)
    return jnp.where(bits >= 0, bits, INT_MIN - bits)


def _search_window(cand_ref, lo_ref, final_ref):
    hi = lw = None
    for v in range(CAND_SPLIT):
        top = cand_ref[:, v * CAND_DEPTH * LANES:(v * CAND_DEPTH + 1) * LANES]
        hi = top if hi is None else jnp.maximum(hi, top)
        lw = top if lw is None else jnp.minimum(lw, top)
    k_hi = _f32_to_key(jnp.max(hi, axis=-1, keepdims=True))
    k_lw = _f32_to_key(jnp.min(lw, axis=-1, keepdims=True))
    nbits = 32 - lax.clz(k_hi ^ k_lw)
    nb = jnp.max(nbits.astype(jnp.float32)).astype(jnp.int32)
    low = lax.shift_left(jnp.int32(1), jnp.minimum(nb, 31)) - 1
    lo0 = jnp.where(nbits == 0, k_lw, jnp.where(nb >= 32, INT_MIN, k_lw & ~low))
    lo_ref[...] = jnp.broadcast_to(lo0, lo_ref.shape)
    final_ref[...] = jnp.broadcast_to(jnp.where(nbits == 0, 1.0, 0.0), final_ref.shape)
    return 32 - nb


def _radix_search(src_ref, n_steps, lo_ref, cnt_ref, topk, first_bit=None, final_ref=None, stage_ref=None):
    rows_total = lo_ref.shape[0]
    if first_bit is None:
        first_bit = jnp.int32(0)
        lo_ref[...] = jnp.full(lo_ref.shape, INT_MIN, jnp.int32)
        cnt_ref[...] = jnp.zeros(cnt_ref.shape, jnp.float32)
    else:
        cnt_ref[...] = jnp.full(cnt_ref.shape, NO_TIE_LIMIT, jnp.float32)

    if isinstance(n_steps, int):
        assert stage_ref is not None and rows_total == 2 * SEL_ROWS
        half_a, half_b = pl.ds(0, SEL_ROWS), pl.ds(SEL_ROWS, SEL_ROWS)

        def settle(rows, lane_counts, bit):
            cnt = _row_total(lane_counts)
            lo = lo_ref[rows, :]
            ok = cnt >= float(topk)
            if final_ref is not None:
                ok = jnp.logical_and(ok, final_ref[rows, :] <= 0.0)
            lo_ref[rows, :] = jnp.where(ok, lo + bit, lo)
            cnt_ref[rows, :] = jnp.where(ok, cnt, cnt_ref[rows, :])

        stage_ref[half_b, :] = jnp.zeros((SEL_ROWS, LANES), jnp.float32)

        def pass_body(b, carry):
            bit = lax.shift_left(jnp.int32(1), 31 - b)
            prev_bit = lax.shift_left(jnp.int32(1), jnp.minimum(32 - b, 31))
            staged = stage_ref[half_b, :]
            acc_a = _count_lanes(src_ref, 0, n_steps, _key_to_f32(lo_ref[half_a, :] + bit))
            settle(half_b, staged, prev_bit)
            acc_b = _count_lanes(src_ref, SEL_ROWS, n_steps, _key_to_f32(lo_ref[half_b, :] + bit))
            settle(half_a, acc_a, bit)
            stage_ref[half_b, :] = acc_b
            return carry
        lax.fori_loop(first_bit, 32, pass_body, 0)
        settle(half_b, stage_ref[half_b, :], jnp.int32(1))
        return

    def cond(c):
        b, pending = c
        return jnp.logical_and(b < 32, pending > 0)

    def body(c):
        b, _ = c
        bit = lax.shift_left(jnp.int32(1), 31 - b)
        off = None
        for rb in range(rows_total // SEL_ROWS):
            rows = pl.ds(rb * SEL_ROWS, SEL_ROWS)
            cand = lo_ref[rows, :] + bit
            cnt = _row_total(_count_lanes(src_ref, rb * SEL_ROWS, n_steps, _key_to_f32(cand)))
            ok = cnt >= float(topk)
            if final_ref is not None:
                ok = jnp.logical_and(ok, final_ref[rows, :] <= 0.0)
            lo_ref[rows, :] = jnp.where(ok, cand, lo_ref[rows, :])
            cnt = jnp.where(ok, cnt, cnt_ref[rows, :])
            cnt_ref[rows, :] = cnt
            miss = jnp.abs(cnt - float(topk))
            if final_ref is not None:
                miss = jnp.where(final_ref[rows, :] > 0.0, 0.0, miss)
            off = miss if off is None else jnp.maximum(off, miss)
        return b + 1, (jnp.max(off) > 0.0).astype(jnp.int32)
    lax.while_loop(cond, body, (first_bit, jnp.int32(1)))


def _lane_candidates(keys_ref, cand_ref, n_groups):
    n_stack = CAND_DEPTH * CAND_SPLIT

    def rg_body(rg, carry):
        r0 = pl.multiple_of(rg * 8, 8)

        def insert(st, c0, n_vregs):
            st = list(st)
            for u in range(n_vregs):
                x = keys_ref[pl.ds(r0, 8), pl.ds(c0 + u * LANES, LANES)]
                base = (u % CAND_SPLIT) * CAND_DEPTH
                for d in range(CAND_DEPTH):
                    cur = st[base + d]
                    st[base + d] = jnp.maximum(cur, x)
                    x = jnp.minimum(cur, x)
            return tuple(st)

        def two_groups(s, st):
            return insert(st, pl.multiple_of(s * 16 * LANES, 16 * LANES), 16)

        def one_group(s, st):
            return insert(st, pl.multiple_of(s * 8 * LANES, 8 * LANES), 8)
        init = tuple(jnp.full((8, LANES), -jnp.inf, jnp.float32) for _ in range(n_stack))
        n_pairs = lax.shift_right_logical(n_groups, 1)
        st = lax.fori_loop(0, n_pairs, two_groups, init)
        st = lax.fori_loop(2 * n_pairs, n_groups, one_group, st)
        for k in range(n_stack):
            cand_ref[pl.ds(r0, 8), k * LANES:(k + 1) * LANES] = st[k]
        return carry
    lax.fori_loop(0, cand_ref.shape[0] // 8, rg_body, 0)


def _ties_to_keep(src_ref, n_steps, lo_ref, cnt_ref, part_ref, need_ref, topk):
    rows_total = lo_ref.shape[0]
    surplus = jnp.where(lo_ref[...] > NEG_INF_KEY, cnt_ref[...] - float(topk), 0.0)
    need_ref[...] = jnp.full(need_ref.shape, NO_TIE_LIMIT, jnp.float32)

    @pl.when(jnp.max(surplus) > 0.0)
    def _():
        for rb in range(rows_total // SEL_ROWS):
            rows = pl.ds(rb * SEL_ROWS, SEL_ROWS)
            thr = _key_to_f32(jnp.maximum(lo_ref[rows, :], NEG_INF_KEY))
            part_ref[rows, :] = _count_lanes(src_ref, rb * SEL_ROWS, n_steps, thr, strict=True)
        above = _row_total(part_ref[...])
        need_ref[...] = jnp.where(surplus > 0.0, float(topk) - above, NO_TIE_LIMIT)


def _tie_index_bound(keys_ref, thr_ref, need_ref, seen_ref, sig_ref, n_chunks):
    kk_i = lax.broadcasted_iota(jnp.int32, (LANES, LANES), 0)
    jj_i = lax.broadcasted_iota(jnp.int32, (LANES, LANES), 1)
    tri = _bf(jnp.where(kk_i <= jj_i, 1.0, 0.0))
    seen_ref[...] = jnp.zeros(seen_ref.shape, jnp.float32)

    def cond(c):
        j, pending = c
        return jnp.logical_and(j < n_chunks, pending > 0)

    def body(c):
        j, _ = c
        thr, need = thr_ref[...], need_ref[...]
        seen, sig = seen_ref[...], sig_ref[...]
        starts = [pl.multiple_of((j * TIE_CHUNK + u) * LANES, LANES) for u in range(TIE_CHUNK)]
        ties = [_bf(jnp.where(keys_ref[:, pl.ds(c0, LANES)] == thr, 1.0, 0.0)) for c0 in starts]
        prefixes = [_dot(t, tri) for t in ties]
        totals = [_row_total(t) for t in ties]
        seens = [seen]
        for u in range(TIE_CHUNK):
            seens.append(seens[u] + totals[u])
        befores = [_row_total(jnp.where(seens[u] + prefixes[u] < need, 1.0, 0.0))
                   for u in range(TIE_CHUNK)]
        for u, c0 in enumerate(starts):
            here = jnp.where(seens[u] < need, jnp.where(seens[u + 1] >= need, 1.0, 0.0), 0.0)
            sig = jnp.where(here > 0.0, c0.astype(jnp.float32) + befores[u], sig)
        seen = seens[TIE_CHUNK]
        seen_ref[...] = seen
        sig_ref[...] = sig
        waiting = jnp.where(need < NO_TIE_LIMIT, jnp.where(seen < need, 1.0, 0.0), 0.0)
        return j + 1, (jnp.max(waiting) > 0.0).astype(jnp.int32)
    lax.while_loop(cond, body, (jnp.int32(0), jnp.int32(1)))


def _dsa_kernel(qp_ref, iqp_ref, w_ref, g_ref, k_ref, va_ref, ik_ref, o_ref,
                keys_ref, cand_ref, lo_ref, thr_ref, cnt_ref, part_ref, need_ref, sig_ref,
                m_ref, acc_ref, full_ref, *, topk):
    QB, TK = DSA_QB, IDX_TK
    G = DSA_HEADS // DSA_KV_HEADS
    i = pl.program_id(0)
    n_idx = i + 1
    n_att = (n_idx * IDX_TK + ATT_TK - 1) // ATT_TK
    n_sel = n_att * (ATT_TK // SEL_LANES)
    row_id = lax.broadcasted_iota(jnp.int32, (QB, TK), 0)
    col_id = lax.broadcasted_iota(jnp.int32, (QB, TK), 1)

    iq = iqp_ref[...].reshape(IDX_HEADS * QB, LANES)
    wb = [jnp.broadcast_to(w_ref[:, h:h + 1], (QB, TK)) for h in range(IDX_HEADS)]

    def score_tile(j, diag):
        c0 = pl.multiple_of(j * TK, TK)
        s_all = _dot_nt(iq, ik_ref[pl.ds(c0, TK), :])
        score = None
        for h in range(IDX_HEADS):
            term = wb[h] * jnp.maximum(s_all[h * QB:(h + 1) * QB], 0.0)
            score = term if score is None else score + term
        if diag:
            score = jnp.where(col_id <= row_id, score, -jnp.inf)
        keys_ref[:, pl.ds(c0, TK)] = score

    def score_body(j, carry):
        score_tile(j, False)
        return carry

    def score_group(width, first, q, carry):
        for u in range(width):
            score_tile(first + q * width + u, False)
        return carry
    n_eights = lax.shift_right_logical(i, 3)
    lax.fori_loop(0, n_eights, functools.partial(score_group, 8, 0), 0)
    done = n_eights * 8
    n_quads = lax.shift_right_logical(i - done, 2)
    lax.fori_loop(0, n_quads, functools.partial(score_group, 4, done), 0)
    lax.fori_loop(done + n_quads * 4, i, score_body, 0)
    score_tile(i, True)

    def blank_body(j, carry):
        c0 = pl.multiple_of(j * TK, TK)
        keys_ref[:, pl.ds(c0, TK)] = jnp.full((QB, TK), -jnp.inf, jnp.float32)
        return carry
    n_vis = n_idx * TK
    windows = [w for w in (CAND_W, 2 * CAND_W) if w <= keys_ref.shape[1]]
    att_end = n_att * (ATT_TK // TK)
    blank_end = att_end
    for w in reversed(windows):
        blank_end = jnp.where(n_vis <= w, jnp.maximum(att_end, w // TK), blank_end)
    lax.fori_loop(n_idx, blank_end, blank_body, 0)

    full_ref[0] = jnp.int32(1)

    @pl.when(n_vis > CAND_W)
    def _():
        _lane_candidates(keys_ref, cand_ref, n_att * (ATT_TK // (8 * LANES)))
        first_bit = _search_window(cand_ref, lo_ref, part_ref)
        _radix_search(cand_ref, CAND_W // SEL_LANES, lo_ref, cnt_ref, topk,
                      first_bit=first_bit, final_ref=part_ref, stage_ref=need_ref)
        thr = _key_to_f32(jnp.maximum(lo_ref[...], NEG_INF_KEY))
        exact = cnt_ref[...] == float(topk)
        lost = None
        for v in range(CAND_SPLIT):
            last = cand_ref[:, (v * CAND_DEPTH + CAND_DEPTH - 1) * LANES:(v * CAND_DEPTH + CAND_DEPTH) * LANES]
            hit = jnp.where(last > thr, 1.0, jnp.where(last == thr, jnp.where(exact, 1.0, 0.0), 0.0))
            lost = hit if lost is None else jnp.maximum(lost, hit)
        full_ref[0] = (jnp.max(lost) > 0.0).astype(jnp.int32)
        _ties_to_keep(cand_ref, CAND_W // SEL_LANES, lo_ref, cnt_ref, part_ref, need_ref, topk)

    @pl.when(full_ref[0] > 0)
    def _():
        def search_rows(n_steps):
            _radix_search(keys_ref, n_steps, lo_ref, cnt_ref, topk, stage_ref=need_ref)
            _ties_to_keep(keys_ref, n_steps, lo_ref, cnt_ref, part_ref, need_ref, topk)
        below = 0
        for w in windows:
            pl.when(jnp.logical_and(n_vis > below, n_vis <= w))(
                functools.partial(search_rows, w // SEL_LANES))
            below = w
        pl.when(n_vis > below)(functools.partial(search_rows, n_sel))

    tau = lo_ref[...]
    found = tau > NEG_INF_KEY
    thr_ref[...] = jnp.where(found, _key_to_f32(jnp.maximum(tau, NEG_INF_KEY)), F32_LOWEST)
    sig_ref[...] = jnp.full(sig_ref.shape, NO_TIE_LIMIT, jnp.float32)

    @pl.when(jnp.min(need_ref[...]) < NO_TIE_LIMIT)
    def _():
        _tie_index_bound(keys_ref, thr_ref, need_ref, part_ref, sig_ref, n_att * (ATT_TK // (TIE_CHUNK * LANES)))

    m_ref[...] = jnp.full(m_ref.shape, NEG_INIT, jnp.float32)
    acc_ref[...] = jnp.zeros(acc_ref.shape, jnp.float32)
    n_grp = ATT_TK // LANES
    lane_f = lax.broadcasted_iota(jnp.int32, (QB, LANES), 1).astype(jnp.float32)

    def attn_tile(j):
        c0 = pl.multiple_of(j * ATT_TK, ATT_TK)
        tau_b = thr_ref[...]
        sig_rel = sig_ref[...] - c0.astype(jnp.float32)
        parts = []
        for u in range(n_grp):
            kk = keys_ref[:, pl.ds(c0 + u * LANES, LANES)]
            tie_bias = jnp.where(lane_f + float(u * LANES) <= sig_rel, 0.0, NEG_MASK)
            parts.append(jnp.where(kk > tau_b, 0.0, jnp.where(kk == tau_b, tie_bias, NEG_MASK)))
        bias = jnp.concatenate(parts, axis=1)
        kt = k_ref[pl.ds(c0, ATT_TK), :]
        all_logits = [_dot_nt(qp_ref[n * G:(n + 1) * G].reshape(G * QB, LANES), kt)
                      for n in range(DSA_KV_HEADS)]
        for n in range(DSA_KV_HEADS):
            logits = all_logits[n]
            va = va_ref[n, pl.ds(c0, ATT_TK), :]
            for gq in range(G):
                h = n * G + gq
                lm = logits[gq * QB:(gq + 1) * QB] + bias
                tmax = lm[:, 0:LANES]
                for u in range(1, n_grp):
                    tmax = jnp.maximum(tmax, lm[:, u * LANES:(u + 1) * LANES])
                m_old = m_ref[h]
                m_new = jnp.maximum(m_old, jnp.max(tmax, axis=-1, keepdims=True))
                p = jnp.concatenate(
                    [_bf(jnp.exp2(lm[:, u * LANES:(u + 1) * LANES] - m_new)) for u in range(n_grp)], axis=1)
                m_ref[h] = m_new
                acc_ref[h] = jnp.exp2(m_old - m_new) * acc_ref[h] + _dot(p, va)

    def attn_body(j, carry):
        attn_tile(j)
        return carry
    lax.fori_loop(0, n_att, attn_body, 0)

    lane = lax.broadcasted_iota(jnp.int32, (QB, LANES), 1)
    g = _f32(g_ref[...])
    for pair in range(DSA_HEADS // 2):
        halves = []
        for pos in range(2):
            a = acc_ref[2 * pair + pos]
            halves.append(a * (1.0 / pltpu.roll(a, DSA_DH, 1)))
        o = jnp.where(lane < DSA_DH, halves[0], pltpu.roll(halves[1], DSA_DH, 1))
        sl = slice(pair * LANES, (pair + 1) * LANES)
        o_ref[:, sl] = _bf(o * _silu(g[:, sl]))


def _dsa_call(p, qp, kk, va, iqp, ik, w4):
    s = p.shape[0]
    topk = min(TOPK_MAX, s // 4)
    QB = DSA_QB
    assert IDX_TK == QB and s % ATT_TK == 0 and ATT_TK % SEL_LANES == 0 and s // LANES <= 256
    once = pl.Buffered(1)
    return pl.pallas_call(
        functools.partial(_dsa_kernel, topk=topk),
        grid=(s // QB,),
        in_specs=[pl.BlockSpec((DSA_HEADS, QB, LANES), lambda i: (0, i, 0)),
                  pl.BlockSpec((IDX_HEADS, QB, LANES), lambda i: (0, i, 0)),
                  pl.BlockSpec((QB, LANES), lambda i: (i, 0)),
                  _pspec(QB, "dsa_g", 512),
                  pl.BlockSpec((s, LANES), lambda i: (0, 0), pipeline_mode=once),
                  pl.BlockSpec((DSA_KV_HEADS, s, LANES), lambda i: (0, 0, 0), pipeline_mode=once),
                  pl.BlockSpec((s, LANES), lambda i: (0, 0), pipeline_mode=once)],
        out_specs=pl.BlockSpec((QB, 512), lambda i: (i, 0)),
        out_shape=jax.ShapeDtypeStruct((s, 512), jnp.bfloat16),
        scratch_shapes=[pltpu.VMEM((QB, s), jnp.float32),
                        pltpu.VMEM((QB, CAND_W), jnp.float32),
                        pltpu.VMEM((QB, LANES), jnp.int32),
                        pltpu.VMEM((QB, LANES), jnp.float32),
                        pltpu.VMEM((QB, LANES), jnp.float32),
                        pltpu.VMEM((QB, LANES), jnp.float32),
                        pltpu.VMEM((QB, LANES), jnp.float32),
                        pltpu.VMEM((QB, LANES), jnp.float32),
                        pltpu.VMEM((DSA_HEADS, QB, LANES), jnp.float32),
                        pltpu.VMEM((DSA_HEADS, QB, LANES), jnp.float32),
                        pltpu.SMEM((1,), jnp.int32)],
        compiler_params=_cparams(1, vmem_mb=56),
        name="dsa",
    )(qp, iqp, w4, p, kk, va, ik)


def _merge_kernel(x_ref, ret_ref, dsa_ref, gl_ref, m_ref, wr_ref, wd_ref, wg_ref, wo_ref,
                  post_ref, gate_ref, o_ref):
    d = D_MODEL
    y = (_sigmoid(_f32(m_ref[:, 0:d])) * _dot(ret_ref[...], wr_ref[...])
         + _sigmoid(_f32(m_ref[:, d:2 * d])) * _dot(dsa_ref[...], wd_ref[...])
         + _sigmoid(_f32(m_ref[:, 2 * d:3 * d])) * _dot(gl_ref[...], wg_ref[...]))
    y = _dot(_bf(y), wo_ref[...])
    yn = y * lax.rsqrt(jnp.mean(y * y, axis=-1, keepdims=True) + RMS_EPS) * post_ref[...]
    o_ref[...] = x_ref[...] + gate_ref[...] * yn


def _merge_call(x2, ret, dsa, gl, p, wr, wd, wg, wo, post, gate):
    s, d = x2.shape
    tm = min(512, s)
    rows = lambda w: pl.BlockSpec((tm, w), lambda i: (i, 0))
    whole = lambda a: pl.BlockSpec(a.shape, lambda i: (0, 0))
    return pl.pallas_call(
        _merge_kernel,
        grid=(s // tm,),
        in_specs=[rows(d), rows(512), rows(512), rows(512), _pspec(tm, "merge", 3072),
                  whole(wr), whole(wd), whole(wg), whole(wo), whole(post), whole(gate)],
        out_specs=rows(d),
        out_shape=jax.ShapeDtypeStruct((s, d), jnp.float32),
        compiler_params=_cparams(1),
        name="merge_out",
    )(x2, ret, dsa, gl, p, wr, wd, wg, wo, post, gate)


def _pack_w_in(w_in):
    depth, d, _ = w_in.shape
    w_in = w_in.astype(jnp.bfloat16)
    zeros = lambda n: jnp.zeros((depth, d, n), w_in.dtype)
    src = lambda name: w_in[:, :, _SRC[name][0]:_SRC[name][0] + _SRC[name][1]]
    pieces, at = [], 0
    for name, width in _PACK:
        assert at == PCOL[name]
        if name == "idx_kw":
            cols = [src("idx_k"), src("idx_w"), zeros(width - IDX_DH - IDX_HEADS)]
        elif name == "gla_a":
            cols = [src("gla_a"), zeros(width - GLA_RANK)]
        else:
            cols = [src(name)]
        pieces += cols
        at += width
    pieces.append(zeros(P_WIDTH - at))
    return jnp.concatenate(pieces, axis=-1).astype(jnp.bfloat16)


def kernel(x, c, positions, ada_w, ada_b, pre_norm, post_norm, w_in, gla_w_lr, gla_b_lr,
           w_br_ret, w_br_dsa, w_br_gla, w_out):
    b, s, d = x.shape
    assert b == 1 and d == D_MODEL
    depth = ada_w.shape[0]
    x2 = x.reshape(s, d)
    mod = _mod_call(jnp.broadcast_to(c, (8, d)), ada_w, ada_b.reshape(depth, 1, 3 * d))[:, 0:1, :]
    rcos, rsin, dcos, dsin = _tab_call(positions.reshape(s, 1))
    w_pack = _pack_w_in(w_in)
    wlr_pad = jnp.pad(gla_w_lr, ((0, 0), (0, LANES - GLA_RANK), (0, 0)))
    for l in range(depth):
        shift, scale, gate = mod[l, :, 0:d], mod[l, :, d:2 * d], mod[l, :, 2 * d:3 * d]
        p = _proj_call(x2, pre_norm[l][None, :], scale, shift, w_pack[l])
        ret = _ret_call(p, rcos, rsin)
        gl = _gla_call(p, wlr_pad[l], gla_b_lr[l][None, :])
        qp, kk, va, iqp, ik, w4 = _dprep_call(p, dcos, dsin)
        dsa = _dsa_call(p, qp, kk, va, iqp, ik, w4)
        x2 = _merge_call(x2, ret, dsa, gl, p, _bf(w_br_ret[l]), _bf(w_br_dsa[l]), _bf(w_br_gla[l]),
                         _bf(w_out[l]), post_norm[l][None, :], gate)
    return x2.reshape(b, s, d)
```

```python
import functools
import math

import jax
import jax.numpy as jnp
from jax import lax
from jax.experimental import pallas as pl
from jax.experimental.pallas import tpu as pltpu

D_MODEL = 1024
DEPTH = 4
RET_HEADS, RET_DK, RET_DV, RET_CHUNK, RET_THETA = 4, 64, 128, 128, 10000.0
DSA_HEADS, DSA_KV_HEADS, DSA_DH = 8, 2, 64
DSA_ROT = DSA_DH // 4
ROPE_THETA = 500000.0
IDX_HEADS, IDX_DH = 4, 64
TOPK_MAX = 256
GLA_HEADS, GLA_DK, GLA_DV, GLA_RANK, GLA_TAU, GLA_CHUNK = 4, 64, 128, 16, 16.0, 64
RMS_EPS = 1e-6
LANES = 128

_SRC = {}
_off = 0
for _name, _w in (("ret_q", 256), ("ret_k", 256), ("ret_v", 512), ("ret_g", 512),
                  ("dsa_q", 512), ("dsa_k", 128), ("dsa_v", 128), ("dsa_g", 512),
                  ("idx_q", 256), ("idx_k", 64), ("idx_w", 4),
                  ("gla_q", 256), ("gla_k", 256), ("gla_v", 512), ("gla_g", 512), ("gla_a", 16),
                  ("merge", 3072)):
    _SRC[_name] = (_off, _w)
    _off += _w
IN_WIDTH = _off

_PACK = (("merge", 3072), ("ret_q", 256), ("ret_k", 256), ("ret_v", 512), ("ret_g", 512),
         ("dsa_q", 512), ("dsa_g", 512), ("gla_v", 512), ("gla_g", 512),
         ("gla_q", 256), ("gla_k", 256), ("idx_q", 256),
         ("dsa_k", 128), ("dsa_v", 128), ("idx_kw", 128), ("gla_a", 128))
PCOL = {}
_off = 0
for _name, _w in _PACK:
    assert _off % min(_w, 1024) == 0
    PCOL[_name] = _off
    _off += _w
P_WIDTH = 8192
assert _off <= P_WIDTH

LOG2E = math.log2(math.e)
INT_MIN = -(2 ** 31)
F32_LOWEST = -3.4028234663852886e38
NEG_INF_KEY = -0x7F800000
NEG_INIT = -1e30
NEG_MASK = -2e30


def _cparams(n_axes, vmem_mb=48):
    return pltpu.CompilerParams(dimension_semantics=("arbitrary",) * n_axes,
                                vmem_limit_bytes=vmem_mb * 1024 * 1024)


def _bf(x):
    return x.astype(jnp.bfloat16)


def _f32(x):
    return x.astype(jnp.float32)


def _dot(a, b):
    return jnp.dot(a, b, preferred_element_type=jnp.float32)


def _dot_nt(a, b):
    return lax.dot_general(a, b, (((1,), (1,)), ((), ())), preferred_element_type=jnp.float32)


def _dot_tn(a, b):
    return lax.dot_general(a, b, (((0,), (0,)), ((), ())), preferred_element_type=jnp.float32)


def _split3(x):
    hi = _bf(x)
    r1 = x - hi.astype(jnp.float32)
    mid = _bf(r1)
    lo = _bf(r1 - mid.astype(jnp.float32))
    return hi, mid, lo


def _silu(x):
    return x * (1.0 / (1.0 + jnp.exp(-x)))


def _sigmoid(x):
    return 1.0 / (1.0 + jnp.exp(-x))


def _mod_kernel(c_ref, w_ref, b_ref, o_ref):
    c = c_ref[...]
    ca = _silu(c)
    acc = None
    for t in _split3(ca):
        for u in _split3(w_ref[0]):
            part = _dot(t, u)
            acc = part if acc is None else acc + part
    o_ref[0] = acc + b_ref[0]


def _mod_call(c8, ada_w, ada_b3):
    depth, d, n = ada_w.shape
    tn = 1024
    return pl.pallas_call(
        _mod_kernel,
        grid=(depth, n // tn),
        in_specs=[pl.BlockSpec((8, d), lambda l, j: (0, 0)),
                  pl.BlockSpec((1, d, tn), lambda l, j: (l, 0, j)),
                  pl.BlockSpec((1, 1, tn), lambda l, j: (l, 0, j))],
        out_specs=pl.BlockSpec((1, 8, tn), lambda l, j: (l, 0, j)),
        out_shape=jax.ShapeDtypeStruct((depth, 8, n), jnp.float32),
        compiler_params=_cparams(2),
        name="adaln_mod",
    )(c8, ada_w, ada_b3)


def _tab_kernel(pos_ref, rf_ref, rs_ref, df_ref, ds_ref, rc_o, rsn_o, dc_o, dsn_o):
    pos = pos_ref[...].astype(jnp.float32)
    ang = pos * rf_ref[...]
    rc_o[...] = jnp.cos(ang)
    rsn_o[...] = jnp.sin(ang) * rs_ref[...]
    ang = pos * df_ref[...]
    dc_o[...] = jnp.cos(ang)
    dsn_o[...] = jnp.sin(ang) * ds_ref[...]


def _rope_rows():
    half = RET_DK // 2
    f = RET_THETA ** (-jnp.arange(half, dtype=jnp.float32) * 2.0 / RET_DK)
    rf = jnp.tile(jnp.concatenate([f, f]), RET_HEADS)[None, :]
    rs = jnp.tile(jnp.concatenate([-jnp.ones(half), jnp.ones(half)]), RET_HEADS)[None, :].astype(jnp.float32)
    half = DSA_ROT // 2
    f = ROPE_THETA ** (-jnp.arange(half, dtype=jnp.float32) * 2.0 / DSA_ROT)
    z = jnp.zeros(DSA_DH - DSA_ROT, jnp.float32)
    df = jnp.tile(jnp.concatenate([f, f, z]), 2)[None, :]
    ds = jnp.tile(jnp.concatenate([-jnp.ones(half), jnp.ones(half), z]), 2)[None, :].astype(jnp.float32)
    return rf, rs, df, ds


def _tab_call(pos_col):
    s = pos_col.shape[0]
    tm = min(1024, s)
    rf, rs, df, ds = _rope_rows()
    row = lambda w: pl.BlockSpec((1, w), lambda i: (0, 0))
    out = lambda w: pl.BlockSpec((tm, w), lambda i: (i, 0))
    return pl.pallas_call(
        _tab_kernel,
        grid=(s // tm,),
        in_specs=[pl.BlockSpec((tm, 1), lambda i: (i, 0)), row(256), row(256), row(128), row(128)],
        out_specs=[out(256), out(256), out(128), out(128)],
        out_shape=[jax.ShapeDtypeStruct((s, 256), jnp.float32), jax.ShapeDtypeStruct((s, 256), jnp.float32),
                   jax.ShapeDtypeStruct((s, 128), jnp.float32), jax.ShapeDtypeStruct((s, 128), jnp.float32)],
        compiler_params=_cparams(1),
        name="rope_tables",
    )(pos_col, rf, rs, df, ds)


def _swap_halves(x, half, period):
    w = x.shape[-1]
    lane = lax.broadcasted_iota(jnp.int32, x.shape, x.ndim - 1) & (period - 1)
    up = pltpu.roll(x, w - half, x.ndim - 1)
    dn = pltpu.roll(x, half, x.ndim - 1)
    return jnp.where(lane < half, up, dn)


def _rope(x, cos, sin_signed, half, period):
    return x * cos + _swap_halves(x, half, period) * sin_signed


def _proj_kernel(x_ref, pre_ref, sc_ref, sh_ref, w_ref, o_ref):
    x = x_ref[...]
    xn = x * lax.rsqrt(jnp.mean(x * x, axis=-1, keepdims=True) + RMS_EPS)
    h = xn * pre_ref[...] * (1.0 + sc_ref[...]) + sh_ref[...]
    o_ref[...] = _bf(_dot(_bf(h), w_ref[...]))


def _proj_call(x2, pre, scale, shift, w_pack):
    s, d = x2.shape
    tm, tn = min(512, s), 2048
    vec = pl.BlockSpec((1, d), lambda j, i: (0, 0))
    return pl.pallas_call(
        _proj_kernel,
        grid=(P_WIDTH // tn, s // tm),
        in_specs=[pl.BlockSpec((tm, d), lambda j, i: (i, 0)), vec, vec, vec,
                  pl.BlockSpec((d, tn), lambda j, i: (0, j))],
        out_specs=pl.BlockSpec((tm, tn), lambda j, i: (i, j)),
        out_shape=jax.ShapeDtypeStruct((s, P_WIDTH), jnp.bfloat16),
        compiler_params=_cparams(2),
        name="in_proj",
    )(x2, pre, scale, shift, w_pack)


def _pspec(tm, name, width):
    blk = PCOL[name] // width
    assert PCOL[name] % width == 0
    return pl.BlockSpec((tm, width), lambda i: (i, blk))


def _ret_log_g(h):
    return math.log1p(-(2.0 ** (-5.0 - h)))


def _ret_kernel(q_ref, k_ref, v_ref, g_ref, cos_ref, sin_ref, o_ref,
                state_ref, decay_ref, qdec_ref, kend_ref, *, chunks):
    C = RET_CHUNK

    @pl.when(pl.program_id(0) == 0)
    def _():
        state_ref[...] = jnp.zeros_like(state_ref)
        ii = lax.broadcasted_iota(jnp.int32, (C, C), 0)
        jj = lax.broadcasted_iota(jnp.int32, (C, C), 1)
        rel = (ii - jj).astype(jnp.float32)
        row = lax.broadcasted_iota(jnp.int32, (C, RET_DK), 0).astype(jnp.float32)
        for h in range(RET_HEADS):
            lg = _ret_log_g(h)
            decay_ref[h] = jnp.where(rel >= 0, jnp.exp(lg * jnp.maximum(rel, 0.0)), 0.0)
            qdec_ref[:, h * RET_DK:(h + 1) * RET_DK] = jnp.exp((row + 1.0) * lg)
            kend_ref[:, h * RET_DK:(h + 1) * RET_DK] = jnp.exp((C - 1.0 - row) * lg)

    heads = [(slice(h * RET_DK, (h + 1) * RET_DK), slice(h * RET_DV, (h + 1) * RET_DV))
             for h in range(RET_HEADS)]
    chunk_rows = [slice(c * C, (c + 1) * C) for c in range(chunks)]
    qbs, kbs, qds, kds = [], [], [], []
    for rows in chunk_rows:
        cos, sin = cos_ref[rows, :], sin_ref[rows, :]
        q = _rope(_f32(q_ref[rows, :]), cos, sin, RET_DK // 2, RET_DK) * (RET_DK ** -0.5)
        k = _rope(_f32(k_ref[rows, :]), cos, sin, RET_DK // 2, RET_DK)
        qbs.append(_bf(q))
        kbs.append(_bf(k))
        qds.append(_bf(q * qdec_ref[...]))
        kds.append(_bf(k * kend_ref[...]))
    scores = [[_bf(_dot_nt(qbs[c][:, dk], kbs[c][:, dk]) * decay_ref[h]) for h, (dk, _) in enumerate(heads)]
              for c in range(chunks)]
    intra = [[_dot(scores[c][h], v_ref[chunk_rows[c], heads[h][1]]) for h in range(RET_HEADS)]
             for c in range(chunks)]
    kvs = [[_dot_tn(kds[c][:, heads[h][0]], v_ref[chunk_rows[c], heads[h][1]]) for h in range(RET_HEADS)]
           for c in range(chunks)]
    states = [state_ref[h] for h in range(RET_HEADS)]
    for c, rows in enumerate(chunk_rows):
        g = _f32(g_ref[rows, :])
        for h, (dk, dv) in enumerate(heads):
            o = intra[c][h] + _dot(qds[c][:, dk], _bf(states[h]))
            states[h] = math.exp(C * _ret_log_g(h)) * states[h] + kvs[c][h]
            o = o * lax.rsqrt(jnp.mean(o * o, axis=-1, keepdims=True) + RMS_EPS)
            o_ref[rows, dv] = _bf(o * _silu(g[:, dv]))
    for h in range(RET_HEADS):
        state_ref[h] = states[h]


def _ret_call(p, rcos, rsin):
    s = p.shape[0]
    chunks = 4
    tm = RET_CHUNK * chunks
    tab = pl.BlockSpec((tm, 256), lambda i: (i, 0))
    return pl.pallas_call(
        functools.partial(_ret_kernel, chunks=chunks),
        grid=(s // tm,),
        in_specs=[_pspec(tm, "ret_q", 256), _pspec(tm, "ret_k", 256), _pspec(tm, "ret_v", 512),
                  _pspec(tm, "ret_g", 512), tab, tab],
        out_specs=pl.BlockSpec((tm, 512), lambda i: (i, 0)),
        out_shape=jax.ShapeDtypeStruct((s, 512), jnp.bfloat16),
        scratch_shapes=[pltpu.VMEM((RET_HEADS, RET_DK, RET_DV), jnp.float32),
                        pltpu.VMEM((RET_HEADS, RET_CHUNK, RET_CHUNK), jnp.float32),
                        pltpu.VMEM((RET_CHUNK, RET_HEADS * RET_DK), jnp.float32),
                        pltpu.VMEM((RET_CHUNK, RET_HEADS * RET_DK), jnp.float32)],
        compiler_params=_cparams(1),
        name="retention",
    )(p, p, p, p, rcos, rsin)


def _gla_kernel(q_ref, k_ref, v_ref, g_ref, a_ref, wlr_ref, blr_ref, o_ref, state_ref, *, chunks):
    C = GLA_CHUNK

    @pl.when(pl.program_id(0) == 0)
    def _():
        state_ref[...] = jnp.zeros_like(state_ref)

    ii = lax.broadcasted_iota(jnp.int32, (C, C), 0)
    jj = lax.broadcasted_iota(jnp.int32, (C, C), 1)
    levels = [1 << b for b in range(C.bit_length() - 2, -1, -1)]
    parent = lambda x, s: lax.shift_right_logical(x, s.bit_length())
    level_masks = [jnp.logical_and(parent(ii, s) == parent(jj, s), (ii & s) > (jj & s)) for s in levels]
    diagonal = ii == jj
    sums = [jnp.where(jj <= ii, 1.0, 0.0)]
    sums += [jnp.where(jj <= lax.shift_left(parent(ii, s), s.bit_length()) + (s - 1), 1.0, 0.0) for s in levels]
    cum_rows = _bf(jnp.concatenate(sums, axis=0))
    wlr = wlr_ref[...]
    w_hi, w_mid, w_lo = _split3(wlr)

    a_hi, a_mid, a_lo = _split3(_f32(a_ref[...]))
    z = (_dot(a_hi, w_hi) + (_dot(a_hi, w_mid) + _dot(a_mid, w_hi))
         + (_dot(a_hi, w_lo) + _dot(a_mid, w_mid) + _dot(a_lo, w_hi))) + blr_ref[...]
    log_a_all = (jnp.minimum(z, 0.0) - jnp.log1p(jnp.exp(-jnp.abs(z)))) * (1.0 / GLA_TAU)

    heads = [(slice(h * GLA_DK, (h + 1) * GLA_DK), slice(h * GLA_DV, (h + 1) * GLA_DV))
             for h in range(GLA_HEADS)]
    chunk_rows = [slice(c * C, (c + 1) * C) for c in range(chunks)]
    sums_all = []
    for rows in chunk_rows:
        l_hi, l_mid, l_lo = _split3(log_a_all[rows, :])
        sums_all.append(_dot(cum_rows, l_hi) + _dot(cum_rows, l_mid) + _dot(cum_rows, l_lo))
    qls, kls, qgs, kds, e_lasts = [], [], [], [], []
    for rows, stacked in zip(chunk_rows, sums_all):
        bcum = stacked[0:C, :]
        b_last = bcum[C - 1:C, :]
        q = _f32(q_ref[rows, :]) * (GLA_DK ** -0.5)
        k = _f32(k_ref[rows, :])
        q_lv, k_lv = [_bf(q)], [_bf(k)]
        for n in range(len(levels)):
            base = stacked[(n + 1) * C:(n + 2) * C, :]
            q_lv.append(_bf(q * jnp.exp(jnp.minimum(bcum - base, 0.0))))
            k_lv.append(_bf(k * jnp.exp(jnp.minimum(base - bcum, 0.0))))
        qls.append(q_lv)
        kls.append(k_lv)
        qgs.append(_bf(q * jnp.exp(bcum)))
        kds.append(_bf(k * jnp.exp(b_last - bcum)))
        e_lasts.append(jnp.exp(b_last))
    pair_masks = [diagonal] + level_masks

    def chunk_attn(c, dk):
        total = None
        for mask, q_l, k_l in zip(pair_masks, qls[c], kls[c]):
            part = jnp.where(mask, _dot_nt(q_l[:, dk], k_l[:, dk]), 0.0)
            total = part if total is None else total + part
        return _bf(total)
    attns = [[chunk_attn(c, dk) for dk, _ in heads] for c in range(chunks)]
    intra = [[_dot(attns[c][h], v_ref[chunk_rows[c], heads[h][1]]) for h in range(GLA_HEADS)]
             for c in range(chunks)]
    kvs = [[_dot_tn(v_ref[chunk_rows[c], heads[h][1]], kds[c][:, heads[h][0]]) for h in range(GLA_HEADS)]
           for c in range(chunks)]
    states = [state_ref[h] for h in range(GLA_HEADS)]
    for c, rows in enumerate(chunk_rows):
        g = _f32(g_ref[rows, :])
        for h, (dk, dv) in enumerate(heads):
            o = intra[c][h] + _dot_nt(qgs[c][:, dk], _bf(states[h]))
            states[h] = e_lasts[c][:, dk] * states[h] + kvs[c][h]
            o = o * lax.rsqrt(jnp.mean(o * o, axis=-1, keepdims=True) + RMS_EPS)
            o_ref[rows, dv] = _bf(o * _silu(g[:, dv]))
    for h in range(GLA_HEADS):
        state_ref[h] = states[h]


def _gla_call(p, wlr_pad, blr):
    s = p.shape[0]
    chunks = 8
    tm = GLA_CHUNK * chunks
    return pl.pallas_call(
        functools.partial(_gla_kernel, chunks=chunks),
        grid=(s // tm,),
        in_specs=[_pspec(tm, "gla_q", 256), _pspec(tm, "gla_k", 256), _pspec(tm, "gla_v", 512),
                  _pspec(tm, "gla_g", 512), _pspec(tm, "gla_a", 128),
                  pl.BlockSpec((128, 256), lambda i: (0, 0)), pl.BlockSpec((1, 256), lambda i: (0, 0))],
        out_specs=pl.BlockSpec((tm, 512), lambda i: (i, 0)),
        out_shape=jax.ShapeDtypeStruct((s, 512), jnp.bfloat16),
        scratch_shapes=[pltpu.VMEM((GLA_HEADS, GLA_DV, GLA_DK), jnp.float32)],
        compiler_params=_cparams(1),
        name="gla",
    )(p, p, p, p, p, wlr_pad, blr)


def _place_head(pair, src_pos, dst_pos):
    lane = lax.broadcasted_iota(jnp.int32, pair.shape, 1)
    if src_pos != dst_pos:
        pair = pltpu.roll(pair, 64, 1)
    keep = (lane < 64) if dst_pos == 0 else (lane >= 64)
    return jnp.where(keep, pair, 0.0)


def _dprep_kernel(q_ref, k_ref, v_ref, iq_ref, ikw_ref, cos_ref, sin_ref,
                  qp_o, k_o, v_o, iqp_o, ik_o, w_o):
    cos, sin = cos_ref[...], sin_ref[...]
    half = DSA_ROT // 2
    cos4 = jnp.concatenate([cos] * 4, axis=1)
    sin4 = jnp.concatenate([sin] * 4, axis=1)
    q = _rope(_f32(q_ref[...]), cos4, sin4, half, DSA_DH) * (DSA_DH ** -0.5 * LOG2E)
    for h in range(DSA_HEADS):
        pair = q[:, (h // 2) * 128:(h // 2 + 1) * 128]
        qp_o[h] = _bf(_place_head(pair, h % 2, h // (DSA_HEADS // DSA_KV_HEADS)))
    k_o[...] = _bf(_rope(_f32(k_ref[...]), cos, sin, half, DSA_DH))
    v = _f32(v_ref[...])
    lane = lax.broadcasted_iota(jnp.int32, v.shape, 1)
    v_o[0] = _bf(jnp.where(lane < DSA_DH, v, 1.0))
    v_o[1] = _bf(jnp.where(lane < DSA_DH, pltpu.roll(v, DSA_DH, 1), 1.0))
    iq = _rope(_f32(iq_ref[...]), cos4[:, :256], sin4[:, :256], half, IDX_DH)
    for h in range(IDX_HEADS):
        pair = iq[:, (h // 2) * 128:(h // 2 + 1) * 128]
        iqp_o[h] = _bf(_place_head(pair, h % 2, 0))
    ikw = _f32(ikw_ref[...])
    lane = lax.broadcasted_iota(jnp.int32, ikw.shape, 1)
    ik_o[...] = _bf(jnp.where(lane < IDX_DH, _rope(ikw, cos, sin, half, DSA_DH), 0.0))
    wscale = (IDX_HEADS ** -0.5) * (IDX_DH ** -0.5)
    w_o[...] = pltpu.roll(ikw, 128 - IDX_DH, 1) * wscale


def _dprep_call(p, dcos, dsin):
    s = p.shape[0]
    tm = min(512, s)
    tab = pl.BlockSpec((tm, 128), lambda i: (i, 0))
    return pl.pallas_call(
        _dprep_kernel,
        grid=(s // tm,),
        in_specs=[_pspec(tm, "dsa_q", 512), _pspec(tm, "dsa_k", 128), _pspec(tm, "dsa_v", 128),
                  _pspec(tm, "idx_q", 256), _pspec(tm, "idx_kw", 128), tab, tab],
        out_specs=[pl.BlockSpec((DSA_HEADS, tm, 128), lambda i: (0, i, 0)), tab,
                   pl.BlockSpec((DSA_KV_HEADS, tm, 128), lambda i: (0, i, 0)),
                   pl.BlockSpec((IDX_HEADS, tm, 128), lambda i: (0, i, 0)), tab, tab],
        out_shape=[jax.ShapeDtypeStruct((DSA_HEADS, s, 128), jnp.bfloat16),
                   jax.ShapeDtypeStruct((s, 128), jnp.bfloat16),
                   jax.ShapeDtypeStruct((DSA_KV_HEADS, s, 128), jnp.bfloat16),
                   jax.ShapeDtypeStruct((IDX_HEADS, s, 128), jnp.bfloat16),
                   jax.ShapeDtypeStruct((s, 128), jnp.bfloat16),
                   jax.ShapeDtypeStruct((s, 128), jnp.float32)],
        compiler_params=_cparams(1),
        name="dsa_prep",
    )(p, p, p, p, p, dcos, dsin)


DSA_QB = 256
IDX_TK = 256
ATT_TK = 1024
SEL_ROWS = 128
SEL_LANES = 512
CAND_DEPTH = 10
CAND_SPLIT = 2
CAND_W = CAND_DEPTH * CAND_SPLIT * LANES
TIE_CHUNK = 8
NO_TIE_LIMIT = 1e9
assert CAND_W % SEL_LANES == 0 and 8 % CAND_SPLIT == 0


def _key_to_f32(key):
    bits = jnp.where(key >= 0, key, (0 - key) | INT_MIN)
    return lax.bitcast_convert_type(bits, jnp.float32)


def _count_lanes(keys_ref, r0, n_steps, cand, strict=False):
    def body(s, acc):
        c0 = s * SEL_LANES if isinstance(s, int) else pl.multiple_of(s * SEL_LANES, SEL_LANES)
        for u in range(SEL_LANES // LANES):
            kk = keys_ref[pl.ds(r0, SEL_ROWS), pl.ds(c0 + u * LANES, LANES)]
            acc = acc + jnp.where((kk > cand) if strict else (kk >= cand), 1.0, 0.0)
        return acc
    acc = jnp.zeros((SEL_ROWS, LANES), jnp.float32)
    if isinstance(n_steps, int):
        for s in range(n_steps):
            acc = body(s, acc)
        return acc
    return lax.fori_loop(0, n_steps, body, acc)


def _row_total(lane_counts):
    ones = jnp.ones((LANES, LANES), jnp.bfloat16)
    return _dot(_bf(lane_counts), ones)


def _f32_to_key(x):
    bits = lax.bitcast_convert_type(x, jnp.int32)
    return jnp.where(bits >= 0, bits, INT_MIN - bits)


def _search_window(cand_ref, lo_ref, final_ref):
    hi = lw = None
    for v in range(CAND_SPLIT):
        top = cand_ref[:, v * CAND_DEPTH * LANES:(v * CAND_DEPTH + 1) * LANES]
        hi = top if hi is None else jnp.maximum(hi, top)
        lw = top if lw is None else jnp.minimum(lw, top)
    k_hi = _f32_to_key(jnp.max(hi, axis=-1, keepdims=True))
    k_lw = _f32_to_key(jnp.min(lw, axis=-1, keepdims=True))
    nbits = 32 - lax.clz(k_hi ^ k_lw)
    nb = jnp.max(nbits.astype(jnp.float32)).astype(jnp.int32)
    low = lax.shift_left(jnp.int32(1), jnp.minimum(nb, 31)) - 1
    lo0 = jnp.where(nbits == 0, k_lw, jnp.where(nb >= 32, INT_MIN, k_lw & ~low))
    lo_ref[...] = jnp.broadcast_to(lo0, lo_ref.shape)
    final_ref[...] = jnp.broadcast_to(jnp.where(nbits == 0, 1.0, 0.0), final_ref.shape)
    return 32 - nb


def _radix_search(src_ref, n_steps, lo_ref, cnt_ref, topk, first_bit=None, final_ref=None, stage_ref=None):
    rows_total = lo_ref.shape[0]
    if first_bit is None:
        first_bit = jnp.int32(0)
        lo_ref[...] = jnp.full(lo_ref.shape, INT_MIN, jnp.int32)
        cnt_ref[...] = jnp.zeros(cnt_ref.shape, jnp.float32)
    else:
        cnt_ref[...] = jnp.full(cnt_ref.shape, NO_TIE_LIMIT, jnp.float32)

    if isinstance(n_steps, int):
        assert stage_ref is not None and rows_total == 2 * SEL_ROWS
        half_a, half_b = pl.ds(0, SEL_ROWS), pl.ds(SEL_ROWS, SEL_ROWS)

        def settle(rows, lane_counts, bit):
            cnt = _row_total(lane_counts)
            lo = lo_ref[rows, :]
            ok = cnt >= float(topk)
            if final_ref is not None:
                ok = jnp.logical_and(ok, final_ref[rows, :] <= 0.0)
            lo_ref[rows, :] = jnp.where(ok, lo + bit, lo)
            cnt_ref[rows, :] = jnp.where(ok, cnt, cnt_ref[rows, :])

        stage_ref[half_b, :] = jnp.zeros((SEL_ROWS, LANES), jnp.float32)

        def pass_body(b, carry):
            bit = lax.shift_left(jnp.int32(1), 31 - b)
            prev_bit = lax.shift_left(jnp.int32(1), jnp.minimum(32 - b, 31))
            staged = stage_ref[half_b, :]
            acc_a = _count_lanes(src_ref, 0, n_steps, _key_to_f32(lo_ref[half_a, :] + bit))
            settle(half_b, staged, prev_bit)
            acc_b = _count_lanes(src_ref, SEL_ROWS, n_steps, _key_to_f32(lo_ref[half_b, :] + bit))
            settle(half_a, acc_a, bit)
            stage_ref[half_b, :] = acc_b
            return carry
        lax.fori_loop(first_bit, 32, pass_body, 0)
        settle(half_b, stage_ref[half_b, :], jnp.int32(1))
        return

    def cond(c):
        b, pending = c
        return jnp.logical_and(b < 32, pending > 0)

    def body(c):
        b, _ = c
        bit = lax.shift_left(jnp.int32(1), 31 - b)
        off = None
        for rb in range(rows_total // SEL_ROWS):
            rows = pl.ds(rb * SEL_ROWS, SEL_ROWS)
            cand = lo_ref[rows, :] + bit
            cnt = _row_total(_count_lanes(src_ref, rb * SEL_ROWS, n_steps, _key_to_f32(cand)))
            ok = cnt >= float(topk)
            if final_ref is not None:
                ok = jnp.logical_and(ok, final_ref[rows, :] <= 0.0)
            lo_ref[rows, :] = jnp.where(ok, cand, lo_ref[rows, :])
            cnt = jnp.where(ok, cnt, cnt_ref[rows, :])
            cnt_ref[rows, :] = cnt
            miss = jnp.abs(cnt - float(topk))
            if final_ref is not None:
                miss = jnp.where(final_ref[rows, :] > 0.0, 0.0, miss)
            off = miss if off is None else jnp.maximum(off, miss)
        return b + 1, (jnp.max(off) > 0.0).astype(jnp.int32)
    lax.while_loop(cond, body, (first_bit, jnp.int32(1)))


def _lane_candidates(keys_ref, cand_ref, n_groups):
    n_stack = CAND_DEPTH * CAND_SPLIT

    def rg_body(rg, carry):
        r0 = pl.multiple_of(rg * 8, 8)

        def insert(st, c0, n_vregs):
            st = list(st)
            for u in range(n_vregs):
                x = keys_ref[pl.ds(r0, 8), pl.ds(c0 + u * LANES, LANES)]
                base = (u % CAND_SPLIT) * CAND_DEPTH
                for d in range(CAND_DEPTH):
                    cur = st[base + d]
                    st[base + d] = jnp.maximum(cur, x)
                    x = jnp.minimum(cur, x)
            return tuple(st)

        def two_groups(s, st):
            return insert(st, pl.multiple_of(s * 16 * LANES, 16 * LANES), 16)

        def one_group(s, st):
            return insert(st, pl.multiple_of(s * 8 * LANES, 8 * LANES), 8)
        init = tuple(jnp.full((8, LANES), -jnp.inf, jnp.float32) for _ in range(n_stack))
        n_pairs = lax.shift_right_logical(n_groups, 1)
        st = lax.fori_loop(0, n_pairs, two_groups, init)
        st = lax.fori_loop(2 * n_pairs, n_groups, one_group, st)
        for k in range(n_stack):
            cand_ref[pl.ds(r0, 8), k * LANES:(k + 1) * LANES] = st[k]
        return carry
    lax.fori_loop(0, cand_ref.shape[0] // 8, rg_body, 0)


def _ties_to_keep(src_ref, n_steps, lo_ref, cnt_ref, part_ref, need_ref, topk):
    rows_total = lo_ref.shape[0]
    surplus = jnp.where(lo_ref[...] > NEG_INF_KEY, cnt_ref[...] - float(topk), 0.0)
    need_ref[...] = jnp.full(need_ref.shape, NO_TIE_LIMIT, jnp.float32)

    @pl.when(jnp.max(surplus) > 0.0)
    def _():
        for rb in range(rows_total // SEL_ROWS):
            rows = pl.ds(rb * SEL_ROWS, SEL_ROWS)
            thr = _key_to_f32(jnp.maximum(lo_ref[rows, :], NEG_INF_KEY))
            part_ref[rows, :] = _count_lanes(src_ref, rb * SEL_ROWS, n_steps, thr, strict=True)
        above = _row_total(part_ref[...])
        need_ref[...] = jnp.where(surplus > 0.0, float(topk) - above, NO_TIE_LIMIT)


def _tie_index_bound(keys_ref, thr_ref, need_ref, seen_ref, sig_ref, n_chunks):
    kk_i = lax.broadcasted_iota(jnp.int32, (LANES, LANES), 0)
    jj_i = lax.broadcasted_iota(jnp.int32, (LANES, LANES), 1)
    tri = _bf(jnp.where(kk_i <= jj_i, 1.0, 0.0))
    seen_ref[...] = jnp.zeros(seen_ref.shape, jnp.float32)

    def cond(c):
        j, pending = c
        return jnp.logical_and(j < n_chunks, pending > 0)

    def body(c):
        j, _ = c
        thr, need = thr_ref[...], need_ref[...]
        seen, sig = seen_ref[...], sig_ref[...]
        starts = [pl.multiple_of((j * TIE_CHUNK + u) * LANES, LANES) for u in range(TIE_CHUNK)]
        ties = [_bf(jnp.where(keys_ref[:, pl.ds(c0, LANES)] == thr, 1.0, 0.0)) for c0 in starts]
        prefixes = [_dot(t, tri) for t in ties]
        totals = [_row_total(t) for t in ties]
        seens = [seen]
        for u in range(TIE_CHUNK):
            seens.append(seens[u] + totals[u])
        befores = [_row_total(jnp.where(seens[u] + prefixes[u] < need, 1.0, 0.0))
                   for u in range(TIE_CHUNK)]
        for u, c0 in enumerate(starts):
            here = jnp.where(seens[u] < need, jnp.where(seens[u + 1] >= need, 1.0, 0.0), 0.0)
            sig = jnp.where(here > 0.0, c0.astype(jnp.float32) + befores[u], sig)
        seen = seens[TIE_CHUNK]
        seen_ref[...] = seen
        sig_ref[...] = sig
        waiting = jnp.where(need < NO_TIE_LIMIT, jnp.where(seen < need, 1.0, 0.0), 0.0)
        return j + 1, (jnp.max(waiting) > 0.0).astype(jnp.int32)
    lax.while_loop(cond, body, (jnp.int32(0), jnp.int32(1)))


def _dsa_kernel(qp_ref, iqp_ref, w_ref, g_ref, k_ref, va_ref, ik_ref, o_ref,
                keys_ref, cand_ref, lo_ref, thr_ref, cnt_ref, part_ref, need_ref, sig_ref,
                m_ref, acc_ref, full_ref, *, topk):
    QB, TK = DSA_QB, IDX_TK
    G = DSA_HEADS // DSA_KV_HEADS
    i = pl.program_id(0)
    n_idx = i + 1
    n_att = (n_idx * IDX_TK + ATT_TK - 1) // ATT_TK
    n_sel = n_att * (ATT_TK // SEL_LANES)
    row_id = lax.broadcasted_iota(jnp.int32, (QB, TK), 0)
    col_id = lax.broadcasted_iota(jnp.int32, (QB, TK), 1)

    iq = iqp_ref[...].reshape(IDX_HEADS * QB, LANES)
    wb = [jnp.broadcast_to(w_ref[:, h:h + 1], (QB, TK)) for h in range(IDX_HEADS)]

    def score_tile(j, diag):
        c0 = pl.multiple_of(j * TK, TK)
        s_all = _dot_nt(iq, ik_ref[pl.ds(c0, TK), :])
        score = None
        for h in range(IDX_HEADS):
            term = wb[h] * jnp.maximum(s_all[h * QB:(h + 1) * QB], 0.0)
            score = term if score is None else score + term
        if diag:
            score = jnp.where(col_id <= row_id, score, -jnp.inf)
        keys_ref[:, pl.ds(c0, TK)] = score

    def score_body(j, carry):
        score_tile(j, False)
        return carry

    def score_group(width, first, q, carry):
        for u in range(width):
            score_tile(first + q * width + u, False)
        return carry
    n_eights = lax.shift_right_logical(i, 3)
    lax.fori_loop(0, n_eights, functools.partial(score_group, 8, 0), 0)
    done = n_eights * 8
    n_quads = lax.shift_right_logical(i - done, 2)
    lax.fori_loop(0, n_quads, functools.partial(score_group, 4, done), 0)
    lax.fori_loop(done + n_quads * 4, i, score_body, 0)
    score_tile(i, True)

    def blank_body(j, carry):
        c0 = pl.multiple_of(j * TK, TK)
        keys_ref[:, pl.ds(c0, TK)] = jnp.full((QB, TK), -jnp.inf, jnp.float32)
        return carry
    n_vis = n_idx * TK
    windows = [w for w in (CAND_W, 2 * CAND_W) if w <= keys_ref.shape[1]]
    att_end = n_att * (ATT_TK // TK)
    blank_end = att_end
    for w in reversed(windows):
        blank_end = jnp.where(n_vis <= w, jnp.maximum(att_end, w // TK), blank_end)
    lax.fori_loop(n_idx, blank_end, blank_body, 0)

    full_ref[0] = jnp.int32(1)

    @pl.when(n_vis > CAND_W)
    def _():
        _lane_candidates(keys_ref, cand_ref, n_att * (ATT_TK // (8 * LANES)))
        first_bit = _search_window(cand_ref, lo_ref, part_ref)
        _radix_search(cand_ref, CAND_W // SEL_LANES, lo_ref, cnt_ref, topk,
                      first_bit=first_bit, final_ref=part_ref, stage_ref=need_ref)
        thr = _key_to_f32(jnp.maximum(lo_ref[...], NEG_INF_KEY))
        exact = cnt_ref[...] == float(topk)
        lost = None
        for v in range(CAND_SPLIT):
            last = cand_ref[:, (v * CAND_DEPTH + CAND_DEPTH - 1) * LANES:(v * CAND_DEPTH + CAND_DEPTH) * LANES]
            hit = jnp.where(last > thr, 1.0, jnp.where(last == thr, jnp.where(exact, 1.0, 0.0), 0.0))
            lost = hit if lost is None else jnp.maximum(lost, hit)
        full_ref[0] = (jnp.max(lost) > 0.0).astype(jnp.int32)
        _ties_to_keep(cand_ref, CAND_W // SEL_LANES, lo_ref, cnt_ref, part_ref, need_ref, topk)

    @pl.when(full_ref[0] > 0)
    def _():
        def search_rows(n_steps):
            _radix_search(keys_ref, n_steps, lo_ref, cnt_ref, topk, stage_ref=need_ref)
            _ties_to_keep(keys_ref, n_steps, lo_ref, cnt_ref, part_ref, need_ref, topk)
        below = 0
        for w in windows:
            pl.when(jnp.logical_and(n_vis > below, n_vis <= w))(
                functools.partial(search_rows, w // SEL_LANES))
            below = w
        pl.when(n_vis > below)(functools.partial(search_rows, n_sel))

    tau = lo_ref[...]
    found = tau > NEG_INF_KEY
    thr_ref[...] = jnp.where(found, _key_to_f32(jnp.maximum(tau, NEG_INF_KEY)), F32_LOWEST)
    sig_ref[...] = jnp.full(sig_ref.shape, NO_TIE_LIMIT, jnp.float32)

    @pl.when(jnp.min(need_ref[...]) < NO_TIE_LIMIT)
    def _():
        _tie_index_bound(keys_ref, thr_ref, need_ref, part_ref, sig_ref, n_att * (ATT_TK // (TIE_CHUNK * LANES)))

    m_ref[...] = jnp.full(m_ref.shape, NEG_INIT, jnp.float32)
    acc_ref[...] = jnp.zeros(acc_ref.shape, jnp.float32)
    n_grp = ATT_TK // LANES
    lane_f = lax.broadcasted_iota(jnp.int32, (QB, LANES), 1).astype(jnp.float32)

    def attn_tile(j):
        c0 = pl.multiple_of(j * ATT_TK, ATT_TK)
        tau_b = thr_ref[...]
        sig_rel = sig_ref[...] - c0.astype(jnp.float32)
        parts = []
        for u in range(n_grp):
            kk = keys_ref[:, pl.ds(c0 + u * LANES, LANES)]
            tie_bias = jnp.where(lane_f + float(u * LANES) <= sig_rel, 0.0, NEG_MASK)
            parts.append(jnp.where(kk > tau_b, 0.0, jnp.where(kk == tau_b, tie_bias, NEG_MASK)))
        bias = jnp.concatenate(parts, axis=1)
        kt = k_ref[pl.ds(c0, ATT_TK), :]
        all_logits = [_dot_nt(qp_ref[n * G:(n + 1) * G].reshape(G * QB, LANES), kt)
                      for n in range(DSA_KV_HEADS)]
        for n in range(DSA_KV_HEADS):
            logits = all_logits[n]
            va = va_ref[n, pl.ds(c0, ATT_TK), :]
            for gq in range(G):
                h = n * G + gq
                lm = logits[gq * QB:(gq + 1) * QB] + bias
                tmax = lm[:, 0:LANES]
                for u in range(1, n_grp):
                    tmax = jnp.maximum(tmax, lm[:, u * LANES:(u + 1) * LANES])
                m_old = m_ref[h]
                m_new = jnp.maximum(m_old, jnp.max(tmax, axis=-1, keepdims=True))
                p = jnp.concatenate(
                    [_bf(jnp.exp2(lm[:, u * LANES:(u + 1) * LANES] - m_new)) for u in range(n_grp)], axis=1)
                m_ref[h] = m_new
                acc_ref[h] = jnp.exp2(m_old - m_new) * acc_ref[h] + _dot(p, va)

    def attn_body(j, carry):
        attn_tile(j)
        return carry
    lax.fori_loop(0, n_att, attn_body, 0)

    lane = lax.broadcasted_iota(jnp.int32, (QB, LANES), 1)
    g = _f32(g_ref[...])
    for pair in range(DSA_HEADS // 2):
        halves = []
        for pos in range(2):
            a = acc_ref[2 * pair + pos]
            halves.append(a * (1.0 / pltpu.roll(a, DSA_DH, 1)))
        o = jnp.where(lane < DSA_DH, halves[0], pltpu.roll(halves[1], DSA_DH, 1))
        sl = slice(pair * LANES, (pair + 1) * LANES)
        o_ref[:, sl] = _bf(o * _silu(g[:, sl]))


def _dsa_call(p, qp, kk, va, iqp, ik, w4):
    s = p.shape[0]
    topk = min(TOPK_MAX, s // 4)
    QB = DSA_QB
    assert IDX_TK == QB and s % ATT_TK == 0 and ATT_TK % SEL_LANES == 0 and s // LANES <= 256
    once = pl.Buffered(1)
    return pl.pallas_call(
        functools.partial(_dsa_kernel, topk=topk),
        grid=(s // QB,),
        in_specs=[pl.BlockSpec((DSA_HEADS, QB, LANES), lambda i: (0, i, 0)),
                  pl.BlockSpec((IDX_HEADS, QB, LANES), lambda i: (0, i, 0)),
                  pl.BlockSpec((QB, LANES), lambda i: (i, 0)),
                  _pspec(QB, "dsa_g", 512),
                  pl.BlockSpec((s, LANES), lambda i: (0, 0), pipeline_mode=once),
                  pl.BlockSpec((DSA_KV_HEADS, s, LANES), lambda i: (0, 0, 0), pipeline_mode=once),
                  pl.BlockSpec((s, LANES), lambda i: (0, 0), pipeline_mode=once)],
        out_specs=pl.BlockSpec((QB, 512), lambda i: (i, 0)),
        out_shape=jax.ShapeDtypeStruct((s, 512), jnp.bfloat16),
        scratch_shapes=[pltpu.VMEM((QB, s), jnp.float32),
                        pltpu.VMEM((QB, CAND_W), jnp.float32),
                        pltpu.VMEM((QB, LANES), jnp.int32),
                        pltpu.VMEM((QB, LANES), jnp.float32),
                        pltpu.VMEM((QB, LANES), jnp.float32),
                        pltpu.VMEM((QB, LANES), jnp.float32),
                        pltpu.VMEM((QB, LANES), jnp.float32),
                        pltpu.VMEM((QB, LANES), jnp.float32),
                        pltpu.VMEM((DSA_HEADS, QB, LANES), jnp.float32),
                        pltpu.VMEM((DSA_HEADS, QB, LANES), jnp.float32),
                        pltpu.SMEM((1,), jnp.int32)],
        compiler_params=_cparams(1, vmem_mb=56),
        name="dsa",
    )(qp, iqp, w4, p, kk, va, ik)


def _merge_kernel(x_ref, ret_ref, dsa_ref, gl_ref, m_ref, wr_ref, wd_ref, wg_ref, wo_ref,
                  post_ref, gate_ref, o_ref):
    d = D_MODEL
    y = (_sigmoid(_f32(m_ref[:, 0:d])) * _dot(ret_ref[...], wr_ref[...])
         + _sigmoid(_f32(m_ref[:, d:2 * d])) * _dot(dsa_ref[...], wd_ref[...])
         + _sigmoid(_f32(m_ref[:, 2 * d:3 * d])) * _dot(gl_ref[...], wg_ref[...]))
    y = _dot(_bf(y), wo_ref[...])
    yn = y * lax.rsqrt(jnp.mean(y * y, axis=-1, keepdims=True) + RMS_EPS) * post_ref[...]
    o_ref[...] = x_ref[...] + gate_ref[...] * yn


def _merge_call(x2, ret, dsa, gl, p, wr, wd, wg, wo, post, gate):
    s, d = x2.shape
    tm = min(512, s)
    rows = lambda w: pl.BlockSpec((tm, w), lambda i: (i, 0))
    whole = lambda a: pl.BlockSpec(a.shape, lambda i: (0, 0))
    return pl.pallas_call(
        _merge_kernel,
        grid=(s // tm,),
        in_specs=[rows(d), rows(512), rows(512), rows(512), _pspec(tm, "merge", 3072),
                  whole(wr), whole(wd), whole(wg), whole(wo), whole(post), whole(gate)],
        out_specs=rows(d),
        out_shape=jax.ShapeDtypeStruct((s, d), jnp.float32),
        compiler_params=_cparams(1),
        name="merge_out",
    )(x2, ret, dsa, gl, p, wr, wd, wg, wo, post, gate)


def _pack_w_in(w_in):
    depth, d, _ = w_in.shape
    zeros = lambda n: jnp.zeros((depth, d, n), jnp.bfloat16)
    src = lambda name: w_in[:, :, _SRC[name][0]:_SRC[name][0] + _SRC[name][1]].astype(jnp.bfloat16)
    pieces, at = [], 0
    for name, width in _PACK:
        assert at == PCOL[name]
        if name == "idx_kw":
            cols = [src("idx_k"), src("idx_w"), zeros(width - IDX_DH - IDX_HEADS)]
        elif name == "gla_a":
            cols = [src("gla_a"), zeros(width - GLA_RANK)]
        else:
            cols = [src(name)]
        pieces += cols
        at += width
    pieces.append(zeros(P_WIDTH - at))
    return jnp.concatenate(pieces, axis=-1).astype(jnp.bfloat16)


def kernel(x, c, positions, ada_w, ada_b, pre_norm, post_norm, w_in, gla_w_lr, gla_b_lr,
           w_br_ret, w_br_dsa, w_br_gla, w_out):
    b, s, d = x.shape
    assert b == 1 and d == D_MODEL
    depth = ada_w.shape[0]
    x2 = x.reshape(s, d)
    mod = _mod_call(jnp.broadcast_to(c, (8, d)), ada_w, ada_b.reshape(depth, 1, 3 * d))[:, 0:1, :]
    rcos, rsin, dcos, dsin = _tab_call(positions.reshape(s, 1))
    w_pack = _pack_w_in(w_in)
    wlr_pad = jnp.pad(gla_w_lr, ((0, 0), (0, LANES - GLA_RANK), (0, 0)))
    for l in range(depth):
        shift, scale, gate = mod[l, :, 0:d], mod[l, :, d:2 * d], mod[l, :, 2 * d:3 * d]
        p = _proj_call(x2, pre_norm[l][None, :], scale, shift, w_pack[l])
        ret = _ret_call(p, rcos, rsin)
        gl = _gla_call(p, wlr_pad[l], gla_b_lr[l][None, :])
        qp, kk, va, iqp, ik, w4 = _dprep_call(p, dcos, dsin)
        dsa = _dsa_call(p, qp, kk, va, iqp, ik, w4)
        x2 = _merge_call(x2, ret, dsa, gl, p, _bf(w_br_ret[l]), _bf(w_br_dsa[l]), _bf(w_br_gla[l]),
                         _bf(w_out[l]), post_norm[l][None, :], gate)
    return x2.reshape(b, s, d)
```

```python
import functools
import math

import jax
import jax.numpy as jnp
from jax import lax
from jax.experimental import pallas as pl
from jax.experimental.pallas import tpu as pltpu

D_MODEL = 1024
DEPTH = 4
RET_HEADS, RET_DK, RET_DV, RET_CHUNK, RET_THETA = 4, 64, 128, 128, 10000.0
DSA_HEADS, DSA_KV_HEADS, DSA_DH = 8, 2, 64
DSA_ROT = DSA_DH // 4
ROPE_THETA = 500000.0
IDX_HEADS, IDX_DH = 4, 64
TOPK_MAX = 256
GLA_HEADS, GLA_DK, GLA_DV, GLA_RANK, GLA_TAU, GLA_CHUNK = 4, 64, 128, 16, 16.0, 64
RMS_EPS = 1e-6
LANES = 128

_SRC = {}
_off = 0
for _name, _w in (("ret_q", 256), ("ret_k", 256), ("ret_v", 512), ("ret_g", 512),
                  ("dsa_q", 512), ("dsa_k", 128), ("dsa_v", 128), ("dsa_g", 512),
                  ("idx_q", 256), ("idx_k", 64), ("idx_w", 4),
                  ("gla_q", 256), ("gla_k", 256), ("gla_v", 512), ("gla_g", 512), ("gla_a", 16),
                  ("merge", 3072)):
    _SRC[_name] = (_off, _w)
    _off += _w
IN_WIDTH = _off

_PACK = (("merge", 3072), ("ret_q", 256), ("ret_k", 256), ("ret_v", 512), ("ret_g", 512),
         ("dsa_q", 512), ("dsa_g", 512), ("gla_v", 512), ("gla_g", 512),
         ("gla_q", 256), ("gla_k", 256), ("idx_q", 256),
         ("dsa_k", 128), ("dsa_v", 128), ("idx_kw", 128), ("gla_a", 128))
PCOL = {}
_off = 0
for _name, _w in _PACK:
    assert _off % min(_w, 1024) == 0
    PCOL[_name] = _off
    _off += _w
P_WIDTH = 8192
assert _off <= P_WIDTH

LOG2E = math.log2(math.e)
INT_MIN = -(2 ** 31)
F32_LOWEST = -3.4028234663852886e38
NEG_INF_KEY = -0x7F800000
NEG_INIT = -1e30
NEG_MASK = -2e30


def _cparams(n_axes, vmem_mb=48):
    return pltpu.CompilerParams(dimension_semantics=("arbitrary",) * n_axes,
                                vmem_limit_bytes=vmem_mb * 1024 * 1024)


def _bf(x):
    return x.astype(jnp.bfloat16)


def _f32(x):
    return x.astype(jnp.float32)


def _dot(a, b):
    return jnp.dot(a, b, preferred_element_type=jnp.float32)


def _dot_nt(a, b):
    return lax.dot_general(a, b, (((1,), (1,)), ((), ())), preferred_element_type=jnp.float32)


def _dot_tn(a, b):
    return lax.dot_general(a, b, (((0,), (0,)), ((), ())), preferred_element_type=jnp.float32)


def _split3(x):
    hi = _bf(x)
    r1 = x - hi.astype(jnp.float32)
    mid = _bf(r1)
    lo = _bf(r1 - mid.astype(jnp.float32))
    return hi, mid, lo


def _silu(x):
    return x * (1.0 / (1.0 + jnp.exp(-x)))


def _sigmoid(x):
    return 1.0 / (1.0 + jnp.exp(-x))


def _mod_kernel(c_ref, w_ref, b_ref, o_ref):
    c = c_ref[...]
    ca = _silu(c)
    acc = None
    for t in _split3(ca):
        for u in _split3(w_ref[0]):
            part = _dot(t, u)
            acc = part if acc is None else acc + part
    o_ref[0] = acc + b_ref[0]


def _mod_call(c8, ada_w, ada_b3):
    depth, d, n = ada_w.shape
    tn = 1024
    return pl.pallas_call(
        _mod_kernel,
        grid=(depth, n // tn),
        in_specs=[pl.BlockSpec((8, d), lambda l, j: (0, 0)),
                  pl.BlockSpec((1, d, tn), lambda l, j: (l, 0, j)),
                  pl.BlockSpec((1, 1, tn), lambda l, j: (l, 0, j))],
        out_specs=pl.BlockSpec((1, 8, tn), lambda l, j: (l, 0, j)),
        out_shape=jax.ShapeDtypeStruct((depth, 8, n), jnp.float32),
        compiler_params=_cparams(2),
        name="adaln_mod",
    )(c8, ada_w, ada_b3)


def _tab_kernel(pos_ref, f_ref, rs_ref, ds_ref, rc_o, rsn_o, dc_o, dsn_o):
    pos = pos_ref[...].astype(jnp.float32)
    ang = pos * f_ref[...]
    cs = (jnp.cos(ang), jnp.sin(ang))
    lane = lax.broadcasted_iota(jnp.int32, ang.shape, 1)
    n_ret, n_dsa = RET_DK // 2, DSA_ROT // 2

    def ret_table(t):
        t = jnp.where(lane < n_ret, t, 0.0)
        t = t + pltpu.roll(t, n_ret, 1)
        t = t + pltpu.roll(t, 2 * n_ret, 1)
        return jnp.concatenate([t, t], axis=1)

    def dsa_table(t):
        t = jnp.where(lane < n_dsa, pltpu.roll(t, LANES - n_ret, 1), 0.0)
        t = t + pltpu.roll(t, n_dsa, 1)
        return t + pltpu.roll(t, DSA_DH, 1)
    rc_o[...] = ret_table(cs[0])
    rsn_o[...] = ret_table(cs[1]) * rs_ref[...]
    dc_o[...] = jnp.where((lane & (DSA_DH - 1)) < DSA_ROT, dsa_table(cs[0]), 1.0)
    dsn_o[...] = dsa_table(cs[1]) * ds_ref[...]


def _rope_rows():
    half = RET_DK // 2
    f_ret = RET_THETA ** (-jnp.arange(half, dtype=jnp.float32) * 2.0 / RET_DK)
    rs = jnp.tile(jnp.concatenate([-jnp.ones(half), jnp.ones(half)]), RET_HEADS)[None, :].astype(jnp.float32)
    half = DSA_ROT // 2
    f_dsa = ROPE_THETA ** (-jnp.arange(half, dtype=jnp.float32) * 2.0 / DSA_ROT)
    z = jnp.zeros(DSA_DH - DSA_ROT, jnp.float32)
    ds = jnp.tile(jnp.concatenate([-jnp.ones(half), jnp.ones(half), z]), 2)[None, :].astype(jnp.float32)
    freqs = jnp.concatenate([f_ret, f_dsa, jnp.zeros(LANES - f_ret.size - f_dsa.size, jnp.float32)])[None, :]
    return freqs, rs, ds


def _tab_call(pos_col):
    s = pos_col.shape[0]
    tm = min(1024, s)
    freqs, rs, ds = _rope_rows()
    row = lambda w: pl.BlockSpec((1, w), lambda i: (0, 0))
    out = lambda w: pl.BlockSpec((tm, w), lambda i: (i, 0))
    return pl.pallas_call(
        _tab_kernel,
        grid=(s // tm,),
        in_specs=[pl.BlockSpec((tm, 1), lambda i: (i, 0)), row(128), row(256), row(128)],
        out_specs=[out(256), out(256), out(128), out(128)],
        out_shape=[jax.ShapeDtypeStruct((s, 256), jnp.float32), jax.ShapeDtypeStruct((s, 256), jnp.float32),
                   jax.ShapeDtypeStruct((s, 128), jnp.float32), jax.ShapeDtypeStruct((s, 128), jnp.float32)],
        compiler_params=_cparams(1),
        name="rope_tables",
    )(pos_col, freqs, rs, ds)


def _swap_halves(x, half, period):
    w = x.shape[-1]
    lane = lax.broadcasted_iota(jnp.int32, x.shape, x.ndim - 1) & (period - 1)
    up = pltpu.roll(x, w - half, x.ndim - 1)
    dn = pltpu.roll(x, half, x.ndim - 1)
    return jnp.where(lane < half, up, dn)


def _rope(x, cos, sin_signed, half, period):
    return x * cos + _swap_halves(x, half, period) * sin_signed


def _proj_kernel(x_ref, pre_ref, sc_ref, sh_ref, w_ref, o_ref):
    x = x_ref[...]
    xn = x * lax.rsqrt(jnp.mean(x * x, axis=-1, keepdims=True) + RMS_EPS)
    h = xn * pre_ref[...] * (1.0 + sc_ref[...]) + sh_ref[...]
    o_ref[...] = _bf(_dot(_bf(h), w_ref[...]))


def _proj_call(x2, pre, scale, shift, w_pack):
    s, d = x2.shape
    tm, tn = min(512, s), 2048
    vec = pl.BlockSpec((1, d), lambda j, i: (0, 0))
    return pl.pallas_call(
        _proj_kernel,
        grid=(P_WIDTH // tn, s // tm),
        in_specs=[pl.BlockSpec((tm, d), lambda j, i: (i, 0)), vec, vec, vec,
                  pl.BlockSpec((d, tn), lambda j, i: (0, j))],
        out_specs=pl.BlockSpec((tm, tn), lambda j, i: (i, j)),
        out_shape=jax.ShapeDtypeStruct((s, P_WIDTH), jnp.bfloat16),
        compiler_params=_cparams(2),
        name="in_proj",
    )(x2, pre, scale, shift, w_pack)


def _pspec(tm, name, width):
    blk = PCOL[name] // width
    assert PCOL[name] % width == 0
    return pl.BlockSpec((tm, width), lambda i: (i, blk))


def _ret_log_g(h):
    return math.log1p(-(2.0 ** (-5.0 - h)))


def _ret_kernel(q_ref, k_ref, v_ref, g_ref, cos_ref, sin_ref, o_ref,
                state_ref, decay_ref, qdec_ref, kend_ref, *, chunks):
    C = RET_CHUNK

    @pl.when(pl.program_id(0) == 0)
    def _():
        state_ref[...] = jnp.zeros_like(state_ref)
        ii = lax.broadcasted_iota(jnp.int32, (C, C), 0)
        jj = lax.broadcasted_iota(jnp.int32, (C, C), 1)
        rel = (ii - jj).astype(jnp.float32)
        row = lax.broadcasted_iota(jnp.int32, (C, RET_DK), 0).astype(jnp.float32)
        for h in range(RET_HEADS):
            lg = _ret_log_g(h)
            decay_ref[h] = jnp.where(rel >= 0, jnp.exp(lg * jnp.maximum(rel, 0.0)), 0.0)
            qdec_ref[:, h * RET_DK:(h + 1) * RET_DK] = jnp.exp((row + 1.0) * lg)
            kend_ref[:, h * RET_DK:(h + 1) * RET_DK] = jnp.exp((C - 1.0 - row) * lg)

    heads = [(slice(h * RET_DK, (h + 1) * RET_DK), slice(h * RET_DV, (h + 1) * RET_DV))
             for h in range(RET_HEADS)]
    chunk_rows = [slice(c * C, (c + 1) * C) for c in range(chunks)]
    qbs, kbs, qds, kds = [], [], [], []
    for rows in chunk_rows:
        cos, sin = cos_ref[rows, :], sin_ref[rows, :]
        q = _rope(_f32(q_ref[rows, :]), cos, sin, RET_DK // 2, RET_DK) * (RET_DK ** -0.5)
        k = _rope(_f32(k_ref[rows, :]), cos, sin, RET_DK // 2, RET_DK)
        qbs.append(_bf(q))
        kbs.append(_bf(k))
        qds.append(_bf(q * qdec_ref[...]))
        kds.append(_bf(k * kend_ref[...]))
    scores = [[_bf(_dot_nt(qbs[c][:, dk], kbs[c][:, dk]) * decay_ref[h]) for h, (dk, _) in enumerate(heads)]
              for c in range(chunks)]
    intra = [[_dot(scores[c][h], v_ref[chunk_rows[c], heads[h][1]]) for h in range(RET_HEADS)]
             for c in range(chunks)]
    kvs = [[_dot_tn(kds[c][:, heads[h][0]], v_ref[chunk_rows[c], heads[h][1]]) for h in range(RET_HEADS)]
           for c in range(chunks)]
    states = [state_ref[h] for h in range(RET_HEADS)]
    for c, rows in enumerate(chunk_rows):
        g = _f32(g_ref[rows, :])
        for h, (dk, dv) in enumerate(heads):
            o = intra[c][h] + _dot(qds[c][:, dk], _bf(states[h]))
            states[h] = math.exp(C * _ret_log_g(h)) * states[h] + kvs[c][h]
            o = o * lax.rsqrt(jnp.mean(o * o, axis=-1, keepdims=True) + RMS_EPS)
            o_ref[rows, dv] = _bf(o * _silu(g[:, dv]))
    for h in range(RET_HEADS):
        state_ref[h] = states[h]


def _ret_call(p, rcos, rsin):
    s = p.shape[0]
    chunks = 4
    tm = RET_CHUNK * chunks
    tab = pl.BlockSpec((tm, 256), lambda i: (i, 0))
    return pl.pallas_call(
        functools.partial(_ret_kernel, chunks=chunks),
        grid=(s // tm,),
        in_specs=[_pspec(tm, "ret_q", 256), _pspec(tm, "ret_k", 256), _pspec(tm, "ret_v", 512),
                  _pspec(tm, "ret_g", 512), tab, tab],
        out_specs=pl.BlockSpec((tm, 512), lambda i: (i, 0)),
        out_shape=jax.ShapeDtypeStruct((s, 512), jnp.bfloat16),
        scratch_shapes=[pltpu.VMEM((RET_HEADS, RET_DK, RET_DV), jnp.float32),
                        pltpu.VMEM((RET_HEADS, RET_CHUNK, RET_CHUNK), jnp.float32),
                        pltpu.VMEM((RET_CHUNK, RET_HEADS * RET_DK), jnp.float32),
                        pltpu.VMEM((RET_CHUNK, RET_HEADS * RET_DK), jnp.float32)],
        compiler_params=_cparams(1),
        name="retention",
    )(p, p, p, p, rcos, rsin)


def _gla_kernel(q_ref, k_ref, v_ref, g_ref, a_ref, wlr_ref, blr_ref, o_ref, state_ref, *, chunks):
    C = GLA_CHUNK

    @pl.when(pl.program_id(0) == 0)
    def _():
        state_ref[...] = jnp.zeros_like(state_ref)

    ii = lax.broadcasted_iota(jnp.int32, (C, C), 0)
    jj = lax.broadcasted_iota(jnp.int32, (C, C), 1)
    levels = [1 << b for b in range(C.bit_length() - 2, -1, -1)]
    parent = lambda x, s: lax.shift_right_logical(x, s.bit_length())
    level_masks = [jnp.logical_and(parent(ii, s) == parent(jj, s), (ii & s) > (jj & s)) for s in levels]
    diagonal = ii == jj
    sums = [jnp.where(jj <= ii, 1.0, 0.0)]
    sums += [jnp.where(jj <= lax.shift_left(parent(ii, s), s.bit_length()) + (s - 1), 1.0, 0.0) for s in levels]
    cum_rows = _bf(jnp.concatenate(sums, axis=0))
    wlr = wlr_ref[...]
    w_hi, w_mid, w_lo = _split3(wlr)

    a_hi, a_mid, a_lo = _split3(_f32(a_ref[...]))
    z = (_dot(a_hi, w_hi) + (_dot(a_hi, w_mid) + _dot(a_mid, w_hi))
         + (_dot(a_hi, w_lo) + _dot(a_mid, w_mid) + _dot(a_lo, w_hi))) + blr_ref[...]
    log_a_all = (jnp.minimum(z, 0.0) - jnp.log1p(jnp.exp(-jnp.abs(z)))) * (1.0 / GLA_TAU)

    heads = [(slice(h * GLA_DK, (h + 1) * GLA_DK), slice(h * GLA_DV, (h + 1) * GLA_DV))
             for h in range(GLA_HEADS)]
    chunk_rows = [slice(c * C, (c + 1) * C) for c in range(chunks)]
    sums_all = []
    for rows in chunk_rows:
        l_hi, l_mid, l_lo = _split3(log_a_all[rows, :])
        sums_all.append(_dot(cum_rows, l_hi) + _dot(cum_rows, l_mid) + _dot(cum_rows, l_lo))
    qls, kls, qgs, kds, e_lasts = [], [], [], [], []
    for rows, stacked in zip(chunk_rows, sums_all):
        bcum = stacked[0:C, :]
        b_last = bcum[C - 1:C, :]
        q = _f32(q_ref[rows, :]) * (GLA_DK ** -0.5)
        k = _f32(k_ref[rows, :])
        q_lv, k_lv = [_bf(q)], [_bf(k)]
        for n in range(len(levels)):
            base = stacked[(n + 1) * C:(n + 2) * C, :]
            q_lv.append(_bf(q * jnp.exp(jnp.minimum(bcum - base, 0.0))))
            k_lv.append(_bf(k * jnp.exp(jnp.minimum(base - bcum, 0.0))))
        qls.append(q_lv)
        kls.append(k_lv)
        qgs.append(_bf(q * jnp.exp(bcum)))
        kds.append(_bf(k * jnp.exp(b_last - bcum)))
        e_lasts.append(jnp.exp(b_last))
    pair_masks = [diagonal] + level_masks

    def chunk_attn(c, dk):
        total = None
        for mask, q_l, k_l in zip(pair_masks, qls[c], kls[c]):
            part = jnp.where(mask, _dot_nt(q_l[:, dk], k_l[:, dk]), 0.0)
            total = part if total is None else total + part
        return _bf(total)
    attns = [[chunk_attn(c, dk) for dk, _ in heads] for c in range(chunks)]
    intra = [[_dot(attns[c][h], v_ref[chunk_rows[c], heads[h][1]]) for h in range(GLA_HEADS)]
             for c in range(chunks)]
    kvs = [[_dot_tn(v_ref[chunk_rows[c], heads[h][1]], kds[c][:, heads[h][0]]) for h in range(GLA_HEADS)]
           for c in range(chunks)]
    states = [state_ref[h] for h in range(GLA_HEADS)]
    for c, rows in enumerate(chunk_rows):
        g = _f32(g_ref[rows, :])
        for h, (dk, dv) in enumerate(heads):
            o = intra[c][h] + _dot_nt(qgs[c][:, dk], _bf(states[h]))
            states[h] = e_lasts[c][:, dk] * states[h] + kvs[c][h]
            o = o * lax.rsqrt(jnp.mean(o * o, axis=-1, keepdims=True) + RMS_EPS)
            o_ref[rows, dv] = _bf(o * _silu(g[:, dv]))
    for h in range(GLA_HEADS):
        state_ref[h] = states[h]


def _gla_call(p, wlr_pad, blr):
    s = p.shape[0]
    chunks = 8
    tm = GLA_CHUNK * chunks
    return pl.pallas_call(
        functools.partial(_gla_kernel, chunks=chunks),
        grid=(s // tm,),
        in_specs=[_pspec(tm, "gla_q", 256), _pspec(tm, "gla_k", 256), _pspec(tm, "gla_v", 512),
                  _pspec(tm, "gla_g", 512), _pspec(tm, "gla_a", 128),
                  pl.BlockSpec((128, 256), lambda i: (0, 0)), pl.BlockSpec((1, 256), lambda i: (0, 0))],
        out_specs=pl.BlockSpec((tm, 512), lambda i: (i, 0)),
        out_shape=jax.ShapeDtypeStruct((s, 512), jnp.bfloat16),
        scratch_shapes=[pltpu.VMEM((GLA_HEADS, GLA_DV, GLA_DK), jnp.float32)],
        compiler_params=_cparams(1),
        name="gla",
    )(p, p, p, p, p, wlr_pad, blr)


def _place_head(pair, src_pos, dst_pos):
    lane = lax.broadcasted_iota(jnp.int32, pair.shape, 1)
    if src_pos != dst_pos:
        pair = pltpu.roll(pair, 64, 1)
    keep = (lane < 64) if dst_pos == 0 else (lane >= 64)
    return jnp.where(keep, pair, 0.0)


def _dprep_kernel(q_ref, k_ref, v_ref, iq_ref, ikw_ref, cos_ref, sin_ref,
                  qp_o, k_o, v_o, iqp_o, ik_o, w_o):
    cos, sin = cos_ref[...], sin_ref[...]
    half = DSA_ROT // 2
    cos4 = jnp.concatenate([cos] * 4, axis=1)
    sin4 = jnp.concatenate([sin] * 4, axis=1)
    q = _rope(_f32(q_ref[...]), cos4, sin4, half, DSA_DH) * (DSA_DH ** -0.5 * LOG2E)
    for h in range(DSA_HEADS):
        pair = q[:, (h // 2) * 128:(h // 2 + 1) * 128]
        qp_o[h] = _bf(_place_head(pair, h % 2, h // (DSA_HEADS // DSA_KV_HEADS)))
    k_o[...] = _bf(_rope(_f32(k_ref[...]), cos, sin, half, DSA_DH))
    v = _f32(v_ref[...])
    lane = lax.broadcasted_iota(jnp.int32, v.shape, 1)
    v_o[0] = _bf(jnp.where(lane < DSA_DH, v, 1.0))
    v_o[1] = _bf(jnp.where(lane < DSA_DH, pltpu.roll(v, DSA_DH, 1), 1.0))
    iq = _rope(_f32(iq_ref[...]), cos4[:, :256], sin4[:, :256], half, IDX_DH)
    for h in range(IDX_HEADS):
        pair = iq[:, (h // 2) * 128:(h // 2 + 1) * 128]
        iqp_o[h] = _bf(_place_head(pair, h % 2, 0))
    ikw = _f32(ikw_ref[...])
    lane = lax.broadcasted_iota(jnp.int32, ikw.shape, 1)
    ik_o[...] = _bf(jnp.where(lane < IDX_DH, _rope(ikw, cos, sin, half, DSA_DH), 0.0))
    wscale = (IDX_HEADS ** -0.5) * (IDX_DH ** -0.5)
    w_o[...] = pltpu.roll(ikw, 128 - IDX_DH, 1) * wscale


def _dprep_call(p, dcos, dsin):
    s = p.shape[0]
    tm = min(512, s)
    tab = pl.BlockSpec((tm, 128), lambda i: (i, 0))
    return pl.pallas_call(
        _dprep_kernel,
        grid=(s // tm,),
        in_specs=[_pspec(tm, "dsa_q", 512), _pspec(tm, "dsa_k", 128), _pspec(tm, "dsa_v", 128),
                  _pspec(tm, "idx_q", 256), _pspec(tm, "idx_kw", 128), tab, tab],
        out_specs=[pl.BlockSpec((DSA_HEADS, tm, 128), lambda i: (0, i, 0)), tab,
                   pl.BlockSpec((DSA_KV_HEADS, tm, 128), lambda i: (0, i, 0)),
                   pl.BlockSpec((IDX_HEADS, tm, 128), lambda i: (0, i, 0)), tab, tab],
        out_shape=[jax.ShapeDtypeStruct((DSA_HEADS, s, 128), jnp.bfloat16),
                   jax.ShapeDtypeStruct((s, 128), jnp.bfloat16),
                   jax.ShapeDtypeStruct((DSA_KV_HEADS, s, 128), jnp.bfloat16),
                   jax.ShapeDtypeStruct((IDX_HEADS, s, 128), jnp.bfloat16),
                   jax.ShapeDtypeStruct((s, 128), jnp.bfloat16),
                   jax.ShapeDtypeStruct((s, 128), jnp.float32)],
        compiler_params=_cparams(1),
        name="dsa_prep",
    )(p, p, p, p, p, dcos, dsin)


DSA_QB = 256
IDX_TK = 256
ATT_TK = 1024
SEL_ROWS = 128
SEL_LANES = 512
CAND_DEPTH = 10
CAND_SPLIT = 2
CAND_W = CAND_DEPTH * CAND_SPLIT * LANES
TIE_CHUNK = 8
NO_TIE_LIMIT = 1e9
assert CAND_W % SEL_LANES == 0 and 8 % CAND_SPLIT == 0


def _key_to_f32(key):
    bits = jnp.where(key >= 0, key, (0 - key) | INT_MIN)
    return lax.bitcast_convert_type(bits, jnp.float32)


def _count_lanes(keys_ref, r0, n_steps, cand, strict=False):
    def body(s, acc):
        c0 = s * SEL_LANES if isinstance(s, int) else pl.multiple_of(s * SEL_LANES, SEL_LANES)
        for u in range(SEL_LANES // LANES):
            kk = keys_ref[pl.ds(r0, SEL_ROWS), pl.ds(c0 + u * LANES, LANES)]
            acc = acc + jnp.where((kk > cand) if strict else (kk >= cand), 1.0, 0.0)
        return acc
    acc = jnp.zeros((SEL_ROWS, LANES), jnp.float32)
    if isinstance(n_steps, int):
        for s in range(n_steps):
            acc = body(s, acc)
        return acc
    return lax.fori_loop(0, n_steps, body, acc)


def _row_total(lane_counts):
    ones = jnp.ones((LANES, LANES), jnp.bfloat16)
    return _dot(_bf(lane_counts), ones)


def _f32_to_key(x):
    bits = lax.bitcast_convert_type(x, jnp.int32)
    return jnp.where(bits >= 0, bits, INT_MIN - bits)


def _search_window(cand_ref, lo_ref, final_ref):
    hi = lw = None
    for v in range(CAND_SPLIT):
        top = cand_ref[:, v * CAND_DEPTH * LANES:(v * CAND_DEPTH + 1) * LANES]
        hi = top if hi is None else jnp.maximum(hi, top)
        lw = top if lw is None else jnp.minimum(lw, top)
    k_hi = _f32_to_key(jnp.max(hi, axis=-1, keepdims=True))
    k_lw = _f32_to_key(jnp.min(lw, axis=-1, keepdims=True))
    nbits = 32 - lax.clz(k_hi ^ k_lw)
    nb = jnp.max(nbits.astype(jnp.float32)).astype(jnp.int32)
    low = lax.shift_left(jnp.int32(1), jnp.minimum(nb, 31)) - 1
    lo0 = jnp.where(nbits == 0, k_lw, jnp.where(nb >= 32, INT_MIN, k_lw & ~low))
    lo_ref[...] = jnp.broadcast_to(lo0, lo_ref.shape)
    final_ref[...] = jnp.broadcast_to(jnp.where(nbits == 0, 1.0, 0.0), final_ref.shape)
    return 32 - nb


def _radix_search(src_ref, n_steps, lo_ref, cnt_ref, topk, first_bit=None, final_ref=None, stage_ref=None):
    rows_total = lo_ref.shape[0]
    if first_bit is None:
        first_bit = jnp.int32(0)
        lo_ref[...] = jnp.full(lo_ref.shape, INT_MIN, jnp.int32)
        cnt_ref[...] = jnp.zeros(cnt_ref.shape, jnp.float32)
    else:
        cnt_ref[...] = jnp.full(cnt_ref.shape, NO_TIE_LIMIT, jnp.float32)

    if isinstance(n_steps, int):
        assert stage_ref is not None and rows_total == 2 * SEL_ROWS
        half_a, half_b = pl.ds(0, SEL_ROWS), pl.ds(SEL_ROWS, SEL_ROWS)

        def settle(rows, lane_counts, bit):
            cnt = _row_total(lane_counts)
            lo = lo_ref[rows, :]
            ok = cnt >= float(topk)
            if final_ref is not None:
                ok = jnp.logical_and(ok, final_ref[rows, :] <= 0.0)
            lo_ref[rows, :] = jnp.where(ok, lo + bit, lo)
            cnt_ref[rows, :] = jnp.where(ok, cnt, cnt_ref[rows, :])

        stage_ref[half_b, :] = jnp.zeros((SEL_ROWS, LANES), jnp.float32)

        def pass_body(b, carry):
            bit = lax.shift_left(jnp.int32(1), 31 - b)
            prev_bit = lax.shift_left(jnp.int32(1), jnp.minimum(32 - b, 31))
            staged = stage_ref[half_b, :]
            acc_a = _count_lanes(src_ref, 0, n_steps, _key_to_f32(lo_ref[half_a, :] + bit))
            settle(half_b, staged, prev_bit)
            acc_b = _count_lanes(src_ref, SEL_ROWS, n_steps, _key_to_f32(lo_ref[half_b, :] + bit))
            settle(half_a, acc_a, bit)
            stage_ref[half_b, :] = acc_b
            return carry
        lax.fori_loop(first_bit, 32, pass_body, 0)
        settle(half_b, stage_ref[half_b, :], jnp.int32(1))
        return

    def cond(c):
        b, pending = c
        return jnp.logical_and(b < 32, pending > 0)

    def body(c):
        b, _ = c
        bit = lax.shift_left(jnp.int32(1), 31 - b)
        off = None
        for rb in range(rows_total // SEL_ROWS):
            rows = pl.ds(rb * SEL_ROWS, SEL_ROWS)
            cand = lo_ref[rows, :] + bit
            cnt = _row_total(_count_lanes(src_ref, rb * SEL_ROWS, n_steps, _key_to_f32(cand)))
            ok = cnt >= float(topk)
            if final_ref is not None:
                ok = jnp.logical_and(ok, final_ref[rows, :] <= 0.0)
            lo_ref[rows, :] = jnp.where(ok, cand, lo_ref[rows, :])
            cnt = jnp.where(ok, cnt, cnt_ref[rows, :])
            cnt_ref[rows, :] = cnt
            miss = jnp.abs(cnt - float(topk))
            if final_ref is not None:
                miss = jnp.where(final_ref[rows, :] > 0.0, 0.0, miss)
            off = miss if off is None else jnp.maximum(off, miss)
        return b + 1, (jnp.max(off) > 0.0).astype(jnp.int32)
    lax.while_loop(cond, body, (first_bit, jnp.int32(1)))


def _lane_candidates(keys_ref, cand_ref, n_groups):
    n_stack = CAND_DEPTH * CAND_SPLIT

    def rg_body(rg, carry):
        r0 = pl.multiple_of(rg * 8, 8)

        def insert(st, c0, n_vregs):
            st = list(st)
            for u in range(n_vregs):
                x = keys_ref[pl.ds(r0, 8), pl.ds(c0 + u * LANES, LANES)]
                base = (u % CAND_SPLIT) * CAND_DEPTH
                for d in range(CAND_DEPTH):
                    cur = st[base + d]
                    st[base + d] = jnp.maximum(cur, x)
                    x = jnp.minimum(cur, x)
            return tuple(st)

        def two_groups(s, st):
            return insert(st, pl.multiple_of(s * 16 * LANES, 16 * LANES), 16)

        def one_group(s, st):
            return insert(st, pl.multiple_of(s * 8 * LANES, 8 * LANES), 8)
        init = tuple(jnp.full((8, LANES), -jnp.inf, jnp.float32) for _ in range(n_stack))
        n_pairs = lax.shift_right_logical(n_groups, 1)
        st = lax.fori_loop(0, n_pairs, two_groups, init)
        st = lax.fori_loop(2 * n_pairs, n_groups, one_group, st)
        for k in range(n_stack):
            cand_ref[pl.ds(r0, 8), k * LANES:(k + 1) * LANES] = st[k]
        return carry
    lax.fori_loop(0, cand_ref.shape[0] // 8, rg_body, 0)


def _ties_to_keep(src_ref, n_steps, lo_ref, cnt_ref, part_ref, need_ref, topk):
    rows_total = lo_ref.shape[0]
    surplus = jnp.where(lo_ref[...] > NEG_INF_KEY, cnt_ref[...] - float(topk), 0.0)
    need_ref[...] = jnp.full(need_ref.shape, NO_TIE_LIMIT, jnp.float32)

    @pl.when(jnp.max(surplus) > 0.0)
    def _():
        for rb in range(rows_total // SEL_ROWS):
            rows = pl.ds(rb * SEL_ROWS, SEL_ROWS)
            thr = _key_to_f32(jnp.maximum(lo_ref[rows, :], NEG_INF_KEY))
            part_ref[rows, :] = _count_lanes(src_ref, rb * SEL_ROWS, n_steps, thr, strict=True)
        above = _row_total(part_ref[...])
        need_ref[...] = jnp.where(surplus > 0.0, float(topk) - above, NO_TIE_LIMIT)


def _tie_index_bound(keys_ref, thr_ref, need_ref, seen_ref, sig_ref, n_chunks):
    kk_i = lax.broadcasted_iota(jnp.int32, (LANES, LANES), 0)
    jj_i = lax.broadcasted_iota(jnp.int32, (LANES, LANES), 1)
    tri = _bf(jnp.where(kk_i <= jj_i, 1.0, 0.0))
    seen_ref[...] = jnp.zeros(seen_ref.shape, jnp.float32)

    def cond(c):
        j, pending = c
        return jnp.logical_and(j < n_chunks, pending > 0)

    def body(c):
        j, _ = c
        thr, need = thr_ref[...], need_ref[...]
        seen, sig = seen_ref[...], sig_ref[...]
        starts = [pl.multiple_of((j * TIE_CHUNK + u) * LANES, LANES) for u in range(TIE_CHUNK)]
        ties = [_bf(jnp.where(keys_ref[:, pl.ds(c0, LANES)] == thr, 1.0, 0.0)) for c0 in starts]
        prefixes = [_dot(t, tri) for t in ties]
        totals = [_row_total(t) for t in ties]
        seens = [seen]
        for u in range(TIE_CHUNK):
            seens.append(seens[u] + totals[u])
        befores = [_row_total(jnp.where(seens[u] + prefixes[u] < need, 1.0, 0.0))
                   for u in range(TIE_CHUNK)]
        for u, c0 in enumerate(starts):
            here = jnp.where(seens[u] < need, jnp.where(seens[u + 1] >= need, 1.0, 0.0), 0.0)
            sig = jnp.where(here > 0.0, c0.astype(jnp.float32) + befores[u], sig)
        seen = seens[TIE_CHUNK]
        seen_ref[...] = seen
        sig_ref[...] = sig
        waiting = jnp.where(need < NO_TIE_LIMIT, jnp.where(seen < need, 1.0, 0.0), 0.0)
        return j + 1, (jnp.max(waiting) > 0.0).astype(jnp.int32)
    lax.while_loop(cond, body, (jnp.int32(0), jnp.int32(1)))


def _dsa_kernel(qp_ref, iqp_ref, w_ref, g_ref, k_ref, va_ref, ik_ref, o_ref,
                keys_ref, cand_ref, lo_ref, thr_ref, cnt_ref, part_ref, need_ref, sig_ref,
                m_ref, acc_ref, full_ref, *, topk):
    QB, TK = DSA_QB, IDX_TK
    G = DSA_HEADS // DSA_KV_HEADS
    i = pl.program_id(0)
    n_idx = i + 1
    n_att = (n_idx * IDX_TK + ATT_TK - 1) // ATT_TK
    n_sel = n_att * (ATT_TK // SEL_LANES)
    row_id = lax.broadcasted_iota(jnp.int32, (QB, TK), 0)
    col_id = lax.broadcasted_iota(jnp.int32, (QB, TK), 1)

    iq = iqp_ref[...].reshape(IDX_HEADS * QB, LANES)
    wb = [jnp.broadcast_to(w_ref[:, h:h + 1], (QB, TK)) for h in range(IDX_HEADS)]

    def score_tile(j, diag):
        c0 = pl.multiple_of(j * TK, TK)
        s_all = _dot_nt(iq, ik_ref[pl.ds(c0, TK), :])
        score = None
        for h in range(IDX_HEADS):
            term = wb[h] * jnp.maximum(s_all[h * QB:(h + 1) * QB], 0.0)
            score = term if score is None else score + term
        if diag:
            score = jnp.where(col_id <= row_id, score, -jnp.inf)
        keys_ref[:, pl.ds(c0, TK)] = score

    def score_body(j, carry):
        score_tile(j, False)
        return carry

    def score_group(width, first, q, carry):
        for u in range(width):
            score_tile(first + q * width + u, False)
        return carry
    n_eights = lax.shift_right_logical(i, 3)
    lax.fori_loop(0, n_eights, functools.partial(score_group, 8, 0), 0)
    done = n_eights * 8
    n_quads = lax.shift_right_logical(i - done, 2)
    lax.fori_loop(0, n_quads, functools.partial(score_group, 4, done), 0)
    lax.fori_loop(done + n_quads * 4, i, score_body, 0)
    score_tile(i, True)

    def blank_body(j, carry):
        c0 = pl.multiple_of(j * TK, TK)
        keys_ref[:, pl.ds(c0, TK)] = jnp.full((QB, TK), -jnp.inf, jnp.float32)
        return carry
    n_vis = n_idx * TK
    windows = [w for w in (CAND_W, 2 * CAND_W) if w <= keys_ref.shape[1]]
    att_end = n_att * (ATT_TK // TK)
    blank_end = att_end
    for w in reversed(windows):
        blank_end = jnp.where(n_vis <= w, jnp.maximum(att_end, w // TK), blank_end)
    lax.fori_loop(n_idx, blank_end, blank_body, 0)

    full_ref[0] = jnp.int32(1)

    @pl.when(n_vis > CAND_W)
    def _():
        _lane_candidates(keys_ref, cand_ref, n_att * (ATT_TK // (8 * LANES)))
        first_bit = _search_window(cand_ref, lo_ref, part_ref)
        _radix_search(cand_ref, CAND_W // SEL_LANES, lo_ref, cnt_ref, topk,
                      first_bit=first_bit, final_ref=part_ref, stage_ref=need_ref)
        thr = _key_to_f32(jnp.maximum(lo_ref[...], NEG_INF_KEY))
        exact = cnt_ref[...] == float(topk)
        lost = None
        for v in range(CAND_SPLIT):
            last = cand_ref[:, (v * CAND_DEPTH + CAND_DEPTH - 1) * LANES:(v * CAND_DEPTH + CAND_DEPTH) * LANES]
            hit = jnp.where(last > thr, 1.0, jnp.where(last == thr, jnp.where(exact, 1.0, 0.0), 0.0))
            lost = hit if lost is None else jnp.maximum(lost, hit)
        full_ref[0] = (jnp.max(lost) > 0.0).astype(jnp.int32)
        _ties_to_keep(cand_ref, CAND_W // SEL_LANES, lo_ref, cnt_ref, part_ref, need_ref, topk)

    @pl.when(full_ref[0] > 0)
    def _():
        def search_rows(n_steps):
            _radix_search(keys_ref, n_steps, lo_ref, cnt_ref, topk, stage_ref=need_ref)
            _ties_to_keep(keys_ref, n_steps, lo_ref, cnt_ref, part_ref, need_ref, topk)
        below = 0
        for w in windows:
            pl.when(jnp.logical_and(n_vis > below, n_vis <= w))(
                functools.partial(search_rows, w // SEL_LANES))
            below = w
        pl.when(n_vis > below)(functools.partial(search_rows, n_sel))

    tau = lo_ref[...]
    found = tau > NEG_INF_KEY
    thr_ref[...] = jnp.where(found, _key_to_f32(jnp.maximum(tau, NEG_INF_KEY)), F32_LOWEST)
    sig_ref[...] = jnp.full(sig_ref.shape, NO_TIE_LIMIT, jnp.float32)

    @pl.when(jnp.min(need_ref[...]) < NO_TIE_LIMIT)
    def _():
        _tie_index_bound(keys_ref, thr_ref, need_ref, part_ref, sig_ref, n_att * (ATT_TK // (TIE_CHUNK * LANES)))

    m_ref[...] = jnp.full(m_ref.shape, NEG_INIT, jnp.float32)
    acc_ref[...] = jnp.zeros(acc_ref.shape, jnp.float32)
    n_grp = ATT_TK // LANES
    lane_f = lax.broadcasted_iota(jnp.int32, (QB, LANES), 1).astype(jnp.float32)

    def attn_tile(j):
        c0 = pl.multiple_of(j * ATT_TK, ATT_TK)
        tau_b = thr_ref[...]
        sig_rel = sig_ref[...] - c0.astype(jnp.float32)
        parts = []
        for u in range(n_grp):
            kk = keys_ref[:, pl.ds(c0 + u * LANES, LANES)]
            tie_bias = jnp.where(lane_f + float(u * LANES) <= sig_rel, 0.0, NEG_MASK)
            parts.append(jnp.where(kk > tau_b, 0.0, jnp.where(kk == tau_b, tie_bias, NEG_MASK)))
        bias = jnp.concatenate(parts, axis=1)
        kt = k_ref[pl.ds(c0, ATT_TK), :]
        all_logits = [_dot_nt(qp_ref[n * G:(n + 1) * G].reshape(G * QB, LANES), kt)
                      for n in range(DSA_KV_HEADS)]
        for n in range(DSA_KV_HEADS):
            logits = all_logits[n]
            va = va_ref[n, pl.ds(c0, ATT_TK), :]
            for gq in range(G):
                h = n * G + gq
                lm = logits[gq * QB:(gq + 1) * QB] + bias
                tmax = lm[:, 0:LANES]
                for u in range(1, n_grp):
                    tmax = jnp.maximum(tmax, lm[:, u * LANES:(u + 1) * LANES])
                m_old = m_ref[h]
                m_new = jnp.maximum(m_old, jnp.max(tmax, axis=-1, keepdims=True))
                p = jnp.concatenate(
                    [_bf(jnp.exp2(lm[:, u * LANES:(u + 1) * LANES] - m_new)) for u in range(n_grp)], axis=1)
                m_ref[h] = m_new
                acc_ref[h] = jnp.exp2(m_old - m_new) * acc_ref[h] + _dot(p, va)

    def attn_body(j, carry):
        attn_tile(j)
        return carry
    lax.fori_loop(0, n_att, attn_body, 0)

    lane = lax.broadcasted_iota(jnp.int32, (QB, LANES), 1)
    g = _f32(g_ref[...])
    for pair in range(DSA_HEADS // 2):
        halves = []
        for pos in range(2):
            a = acc_ref[2 * pair + pos]
            halves.append(a * (1.0 / pltpu.roll(a, DSA_DH, 1)))
        o = jnp.where(lane < DSA_DH, halves[0], pltpu.roll(halves[1], DSA_DH, 1))
        sl = slice(pair * LANES, (pair + 1) * LANES)
        o_ref[:, sl] = _bf(o * _silu(g[:, sl]))


def _dsa_call(p, qp, kk, va, iqp, ik, w4):
    s = p.shape[0]
    topk = min(TOPK_MAX, s // 4)
    QB = DSA_QB
    assert IDX_TK == QB and s % ATT_TK == 0 and ATT_TK % SEL_LANES == 0 and s // LANES <= 256
    once = pl.Buffered(1)
    return pl.pallas_call(
        functools.partial(_dsa_kernel, topk=topk),
        grid=(s // QB,),
        in_specs=[pl.BlockSpec((DSA_HEADS, QB, LANES), lambda i: (0, i, 0)),
                  pl.BlockSpec((IDX_HEADS, QB, LANES), lambda i: (0, i, 0)),
                  pl.BlockSpec((QB, LANES), lambda i: (i, 0)),
                  _pspec(QB, "dsa_g", 512),
                  pl.BlockSpec((s, LANES), lambda i: (0, 0), pipeline_mode=once),
                  pl.BlockSpec((DSA_KV_HEADS, s, LANES), lambda i: (0, 0, 0), pipeline_mode=once),
                  pl.BlockSpec((s, LANES), lambda i: (0, 0), pipeline_mode=once)],
        out_specs=pl.BlockSpec((QB, 512), lambda i: (i, 0)),
        out_shape=jax.ShapeDtypeStruct((s, 512), jnp.bfloat16),
        scratch_shapes=[pltpu.VMEM((QB, s), jnp.float32),
                        pltpu.VMEM((QB, CAND_W), jnp.float32),
                        pltpu.VMEM((QB, LANES), jnp.int32),
                        pltpu.VMEM((QB, LANES), jnp.float32),
                        pltpu.VMEM((QB, LANES), jnp.float32),
                        pltpu.VMEM((QB, LANES), jnp.float32),
                        pltpu.VMEM((QB, LANES), jnp.float32),
                        pltpu.VMEM((QB, LANES), jnp.float32),
                        pltpu.VMEM((DSA_HEADS, QB, LANES), jnp.float32),
                        pltpu.VMEM((DSA_HEADS, QB, LANES), jnp.float32),
                        pltpu.SMEM((1,), jnp.int32)],
        compiler_params=_cparams(1, vmem_mb=56),
        name="dsa",
    )(qp, iqp, w4, p, kk, va, ik)


def _merge_kernel(x_ref, ret_ref, dsa_ref, gl_ref, m_ref, wr_ref, wd_ref, wg_ref, wo_ref,
                  post_ref, gate_ref, o_ref):
    d = D_MODEL
    y = (_sigmoid(_f32(m_ref[:, 0:d])) * _dot(ret_ref[...], wr_ref[...])
         + _sigmoid(_f32(m_ref[:, d:2 * d])) * _dot(dsa_ref[...], wd_ref[...])
         + _sigmoid(_f32(m_ref[:, 2 * d:3 * d])) * _dot(gl_ref[...], wg_ref[...]))
    y = _dot(_bf(y), wo_ref[...])
    yn = y * lax.rsqrt(jnp.mean(y * y, axis=-1, keepdims=True) + RMS_EPS) * post_ref[...]
    o_ref[...] = x_ref[...] + gate_ref[...] * yn


def _merge_call(x2, ret, dsa, gl, p, wr, wd, wg, wo, post, gate):
    s, d = x2.shape
    tm = min(512, s)
    rows = lambda w: pl.BlockSpec((tm, w), lambda i: (i, 0))
    whole = lambda a: pl.BlockSpec(a.shape, lambda i: (0, 0))
    return pl.pallas_call(
        _merge_kernel,
        grid=(s // tm,),
        in_specs=[rows(d), rows(512), rows(512), rows(512), _pspec(tm, "merge", 3072),
                  whole(wr), whole(wd), whole(wg), whole(wo), whole(post), whole(gate)],
        out_specs=rows(d),
        out_shape=jax.ShapeDtypeStruct((s, d), jnp.float32),
        compiler_params=_cparams(1),
        name="merge_out",
    )(x2, ret, dsa, gl, p, wr, wd, wg, wo, post, gate)


def _pack_w_in(w_in):
    depth, d, _ = w_in.shape
    w_in = w_in.astype(jnp.bfloat16)
    zeros = lambda n: jnp.zeros((depth, d, n), w_in.dtype)
    src = lambda name: w_in[:, :, _SRC[name][0]:_SRC[name][0] + _SRC[name][1]]
    pieces, at = [], 0
    for name, width in _PACK:
        assert at == PCOL[name]
        if name == "idx_kw":
            cols = [src("idx_k"), src("idx_w"), zeros(width - IDX_DH - IDX_HEADS)]
        elif name == "gla_a":
            cols = [src("gla_a"), zeros(width - GLA_RANK)]
        else:
            cols = [src(name)]
        pieces += cols
        at += width
    pieces.append(zeros(P_WIDTH - at))
    return jnp.concatenate(pieces, axis=-1).astype(jnp.bfloat16)


def kernel(x, c, positions, ada_w, ada_b, pre_norm, post_norm, w_in, gla_w_lr, gla_b_lr,
           w_br_ret, w_br_dsa, w_br_gla, w_out):
    b, s, d = x.shape
    assert b == 1 and d == D_MODEL
    depth = ada_w.shape[0]
    x2 = x.reshape(s, d)
    mod = _mod_call(jnp.broadcast_to(c, (8, d)), ada_w, ada_b.reshape(depth, 1, 3 * d))[:, 0:1, :]
    rcos, rsin, dcos, dsin = _tab_call(positions.reshape(s, 1))
    w_pack = _pack_w_in(w_in)
    wlr_pad = jnp.pad(gla_w_lr, ((0, 0), (0, LANES - GLA_RANK), (0, 0)))
    for l in range(depth):
        shift, scale, gate = mod[l, :, 0:d], mod[l, :, d:2 * d], mod[l, :, 2 * d:3 * d]
        p = _proj_call(x2, pre_norm[l][None, :], scale, shift, w_pack[l])
        ret = _ret_call(p, rcos, rsin)
        gl = _gla_call(p, wlr_pad[l], gla_b_lr[l][None, :])
        qp, kk, va, iqp, ik, w4 = _dprep_call(p, dcos, dsin)
        dsa = _dsa_call(p, qp, kk, va, iqp, ik, w4)
        x2 = _merge_call(x2, ret, dsa, gl, p, _bf(w_br_ret[l]), _bf(w_br_dsa[l]), _bf(w_br_gla[l]),
                         _bf(w_out[l]), post_norm[l][None, :], gate)
    return x2.reshape(b, s, d)
```
